```python
import jax, jax.numpy as jnp
from jax import lax
import numpy as np

D_MODEL = 2048
BATCH = 8
SEQ = 8192
DEPTH = 4

HEAD_DIM = 128
NORM_EPS = 1e-6
SB_HEADS = 8
SB_WIDTH = SB_HEADS * HEAD_DIM
SB_BLOCK = 128
CONV_WIDTH = 1024
CONV_KERNEL = 31
SGU_WIDTH = 1024
SGU_GROUPS = 8
SGU_CHUNK = 128
DIL_PATTERNS = ((128, 1), (512, 4), (2048, 16))
DIL_NGROUPS = 3
DIL_SLOTS = 8
DIL_WIDTH = DIL_SLOTS * HEAD_DIM
DIL_BLOCK = 128
N_BRANCH = 4
BRANCH_WIDTH = 1024
IN_SIZES = (SB_WIDTH, SB_WIDTH, SB_WIDTH, SB_WIDTH,
            CONV_WIDTH, CONV_WIDTH, CONV_WIDTH,
            SGU_WIDTH, SGU_WIDTH, SGU_WIDTH,
            DIL_NGROUPS * DIL_WIDTH, DIL_NGROUPS * DIL_WIDTH,
            DIL_WIDTH, DIL_WIDTH)
IN_WIDTH = 18432

kernel_name = "hybrid_sb_conv_sgu_dilated_block"


def rmsnorm(x, g):
    xf = x.astype(jnp.float32)
    y = xf * lax.rsqrt(jnp.mean(xf * xf, axis=-1, keepdims=True) + NORM_EPS)
    return (y * g.astype(jnp.float32)).astype(x.dtype)


def layernorm(x, g, b):
    xf = x.astype(jnp.float32)
    mu = jnp.mean(xf, axis=-1, keepdims=True)
    var = jnp.mean(jnp.square(xf - mu), axis=-1, keepdims=True)
    y = (xf - mu) * lax.rsqrt(var + NORM_EPS)
    return (y * g.astype(jnp.float32) + b.astype(jnp.float32)).astype(x.dtype)


def stick_breaking_attention(q, k, v):
    B, S, H, Dh = q.shape
    nb = S // SB_BLOCK
    scale = Dh ** -0.5
    qb = q.astype(jnp.float32).reshape(B, nb, SB_BLOCK, H, Dh).transpose(1, 0, 3, 2, 4)
    kf = k.astype(jnp.float32)
    vf = v.astype(jnp.float32)
    key_pos = jnp.arange(S)

    def block(args):
        qi, n = args
        z = jnp.einsum('bhqd,bkhd->bhqk', qi, kf) * scale
        q_pos = n * SB_BLOCK + jnp.arange(SB_BLOCK)
        causal = key_pos[None, :] < q_pos[:, None]
        log_beta = jax.nn.log_sigmoid(z)
        log_1mb = jnp.where(causal, jax.nn.log_sigmoid(-z), 0.0)
        after = lax.cumsum(log_1mb, axis=3, reverse=True) - log_1mb
        w = jnp.where(causal, jnp.exp(log_beta + after), 0.0)
        return jnp.einsum('bhqk,bkhd->bqhd', w, vf)

    out = lax.map(block, (qb, jnp.arange(nb)))
    return out.transpose(1, 0, 2, 3, 4).reshape(B, S, H * Dh)


def causal_depthwise_conv(x, w, b):
    K, C = w.shape
    xp = jnp.pad(x, ((0, 0), (K - 1, 0), (0, 0)))
    y = lax.conv_general_dilated(xp, w[:, None, :], window_strides=(1,), padding='VALID',
                                 dimension_numbers=('NWC', 'WIO', 'NWC'), feature_group_count=C)
    return y + b


def conformer_conv(glu_a, glu_b, conv_w, conv_b, ln_g, ln_b):
    g = glu_a * jax.nn.sigmoid(glu_b)
    c = causal_depthwise_conv(g, conv_w, conv_b)
    return jax.nn.silu(layernorm(c, ln_g, ln_b))


def spatial_gating(u, v, w_s, b_s, ln_g, ln_b):
    B, S, C = u.shape
    n = S // SGU_CHUNK
    G = SGU_GROUPS
    vn = layernorm(v, ln_g, ln_b).reshape(B, n, SGU_CHUNK, G, C // G)
    mask = jnp.tril(jnp.ones((SGU_CHUNK, SGU_CHUNK), dtype=bool))
    w = jnp.where(mask[None], w_s, jnp.zeros_like(w_s))
    z = jnp.einsum('gts,bnsgc->bntgc', w, vn) + b_s.T[None, None, :, :, None]
    return u * z.reshape(B, S, C)


def dilated_group_attention(q, k, v, window, dil):
    B, S, H, Dh = q.shape
    n_keys = window // dil
    L = S // dil
    N = B * dil
    T = DIL_BLOCK
    nb = -(-L // T)
    Lp = nb * T

    def to_classes(t):
        t = t.astype(jnp.float32).reshape(B, L, dil, H, Dh).transpose(0, 2, 1, 3, 4).reshape(N, L, H, Dh)
        t = jnp.pad(t, ((0, 0), (0, Lp - L), (0, 0), (0, 0)))
        return t.reshape(N, nb, T, H, Dh)

    def with_prev(t):
        prev = jnp.pad(t[:, :-1], ((0, 0), (1, 0), (0, 0), (0, 0), (0, 0)))
        return jnp.concatenate([prev, t], axis=2)

    qb = to_classes(q)
    kw = with_prev(to_classes(k))
    vw = with_prev(to_classes(v))
    s = jnp.einsum('nbqhd,nbkhd->nbhqk', qb, kw) * (Dh ** -0.5)
    a = jnp.arange(T)[:, None]
    c = jnp.arange(2 * T)[None, :]
    rel = T + a - c
    blk = jnp.arange(nb)[:, None, None]
    valid = (rel >= 0) & (rel <= n_keys) & ((blk > 0) | (c >= T))
    s = jnp.where(valid[None, :, None], s, -jnp.inf)
    m = jnp.max(s, axis=-1, keepdims=True)
    p = jnp.exp(s - m)
    den = jnp.sum(p, axis=-1, keepdims=True)
    o = jnp.einsum('nbhqk,nbkhd->nbqhd', p, vw) / den.transpose(0, 1, 3, 2, 4)
    lse = (m + jnp.log(den))[..., 0].transpose(0, 1, 3, 2)

    def from_classes(t):
        tail = t.shape[3:]
        t = t.reshape((N, Lp) + tail)[:, :L]
        t = jnp.moveaxis(t.reshape((B, dil, L) + tail), 1, 2)
        return t.reshape((B, S) + tail)

    return from_classes(o), from_classes(lse)


def dilated_mixture(q, k, v):
    outs, lses = [], []
    for g, (window, dil) in enumerate(DIL_PATTERNS):
        o, lse = dilated_group_attention(q[:, :, g], k[:, :, g], v, window, dil)
        outs.append(o)
        lses.append(lse)
    w = jax.nn.softmax(jnp.stack(lses, 0), axis=0)
    return jnp.einsum('gbsh,gbshd->bshd', w, jnp.stack(outs, 0))


def hybrid_layer(x, norm_g, w_in, conv_w, conv_b, conv_ln_g, conv_ln_b, sgu_ln_g, sgu_ln_b,
                 sgu_w, sgu_b, w_branch, w_gate, b_gate, w_out):
    B, S, D = x.shape
    h = rmsnorm(x, norm_g)
    proj = jnp.einsum('bsd,df->bsf', h, w_in)
    splits = [int(i) for i in np.cumsum(IN_SIZES)[:-1]]
    (a_q, a_k, a_v, a_g, b_a, b_b, b_g, c_u, c_v, c_g, d_q, d_k, d_v, d_g) = jnp.split(proj, splits, axis=-1)

    shp = (B, S, SB_HEADS, HEAD_DIM)
    ya = stick_breaking_attention(a_q.reshape(shp), a_k.reshape(shp), a_v.reshape(shp)).astype(x.dtype)
    ya = ya * jax.nn.silu(a_g)
    yb = conformer_conv(b_a, b_b, conv_w, conv_b, conv_ln_g, conv_ln_b) * jax.nn.silu(b_g)
    yc = spatial_gating(jax.nn.gelu(c_u), jax.nn.gelu(c_v), sgu_w, sgu_b, sgu_ln_g, sgu_ln_b) * jax.nn.silu(c_g)
    gshp = (B, S, DIL_NGROUPS, DIL_SLOTS, HEAD_DIM)
    yd = dilated_mixture(d_q.reshape(gshp), d_k.reshape(gshp), d_v.reshape(B, S, DIL_SLOTS, HEAD_DIM))
    yd = yd.reshape(B, S, DIL_WIDTH).astype(x.dtype) * jax.nn.silu(d_g)

    branches = jnp.stack([ya, yb, yc, yd], axis=2)
    yproj = jnp.einsum('bsnc,ncd->bsnd', branches, w_branch)
    gates = jax.nn.sigmoid(jnp.einsum('bsd,df->bsf', h, w_gate) + b_gate).reshape(B, S, N_BRANCH, D)
    merged = jnp.sum(gates * yproj, axis=2)
    return x + jnp.einsum('bsd,de->bse', merged, w_out)


def _fwd_setup_inputs(seed: int = 0) -> dict:
    key = jax.random.key(seed)
    ks = jax.random.split(key, 17)
    f32 = jnp.float32
    nrm = lambda k, shape, scale: jax.random.normal(k, shape, f32) * scale
    return {
        "x": nrm(ks[0], (BATCH, SEQ, D_MODEL), 1.0),
        "norm_g": 1.0 + nrm(ks[1], (DEPTH, D_MODEL), 0.01),
        "w_in": nrm(ks[2], (DEPTH, D_MODEL, IN_WIDTH), D_MODEL ** -0.5),
        "conv_w": nrm(ks[3], (DEPTH, CONV_KERNEL, CONV_WIDTH), CONV_KERNEL ** -0.5),
        "conv_b": nrm(ks[4], (DEPTH, CONV_WIDTH), 0.01),
        "conv_ln_g": 1.0 + nrm(ks[5], (DEPTH, CONV_WIDTH), 0.01),
        "conv_ln_b": nrm(ks[6], (DEPTH, CONV_WIDTH), 0.01),
        "sgu_ln_g": 1.0 + nrm(ks[7], (DEPTH, SGU_WIDTH), 0.01),
        "sgu_ln_b": nrm(ks[8], (DEPTH, SGU_WIDTH), 0.01),
        "sgu_w": nrm(ks[9], (DEPTH, SGU_GROUPS, SGU_CHUNK, SGU_CHUNK), SGU_CHUNK ** -0.5),
        "sgu_b": 1.0 + nrm(ks[10], (DEPTH, SGU_GROUPS, SGU_CHUNK), 0.01),
        "w_branch": nrm(ks[11], (DEPTH, N_BRANCH, BRANCH_WIDTH, D_MODEL), BRANCH_WIDTH ** -0.5),
        "w_gate": nrm(ks[12], (DEPTH, D_MODEL, N_BRANCH * D_MODEL), D_MODEL ** -0.5),
        "b_gate": nrm(ks[13], (DEPTH, N_BRANCH * D_MODEL), 0.01),
        "w_out": nrm(ks[14], (DEPTH, D_MODEL, D_MODEL), D_MODEL ** -0.5),
        "final_g": 1.0 + nrm(ks[15], (D_MODEL,), 0.01),
    }


def _fwd_reference(x, norm_g, w_in, conv_w, conv_b, conv_ln_g, conv_ln_b, sgu_ln_g, sgu_ln_b,
              sgu_w, sgu_b, w_branch, w_gate, b_gate, w_out, final_g):
    for l in range(DEPTH):
        x = hybrid_layer(x, norm_g[l], w_in[l], conv_w[l], conv_b[l], conv_ln_g[l], conv_ln_b[l],
                         sgu_ln_g[l], sgu_ln_b[l], sgu_w[l], sgu_b[l], w_branch[l], w_gate[l],
                         b_gate[l], w_out[l])
    return rmsnorm(x, final_g)


import jax as _jax
import jax.numpy as _jnp

TWIN_FORMAT = 'train_step'
FWD_PARAMS = ['x', 'norm_g', 'w_in', 'conv_w', 'conv_b', 'conv_ln_g', 'conv_ln_b', 'sgu_ln_g', 'sgu_ln_b', 'sgu_w', 'sgu_b', 'w_branch', 'w_gate', 'b_gate', 'w_out', 'final_g']
TWIN_WEIGHTS = ['norm_g', 'w_in', 'conv_w', 'conv_b', 'conv_ln_g', 'conv_ln_b', 'sgu_ln_g', 'sgu_ln_b', 'sgu_w', 'sgu_b', 'w_branch', 'w_gate', 'b_gate', 'w_out', 'final_g']
TWIN_DIFF_INPUT = 'x'
TWIN_INPUTS = ['x', 'norm_g', 'w_in', 'conv_w', 'conv_b', 'conv_ln_g', 'conv_ln_b', 'sgu_ln_g', 'sgu_ln_b', 'sgu_w', 'sgu_b', 'w_branch', 'w_gate', 'b_gate', 'w_out', 'final_g', 'loss_target', 'm_norm_g', 'm_w_in', 'm_conv_w', 'm_conv_b', 'm_conv_ln_g', 'm_conv_ln_b', 'm_sgu_ln_g', 'm_sgu_ln_b', 'm_sgu_w', 'm_sgu_b', 'm_w_branch', 'm_w_gate', 'm_b_gate', 'm_w_out', 'm_final_g', 'v_norm_g', 'v_w_in', 'v_conv_w', 'v_conv_b', 'v_conv_ln_g', 'v_conv_ln_b', 'v_sgu_ln_g', 'v_sgu_ln_b', 'v_sgu_w', 'v_sgu_b', 'v_w_branch', 'v_w_gate', 'v_b_gate', 'v_w_out', 'v_final_g']
TWIN_OUTPUTS = ['loss', 'grad_x', 'grad_norm_g', 'grad_w_in', 'grad_conv_w', 'grad_conv_b', 'grad_conv_ln_g', 'grad_conv_ln_b', 'grad_sgu_ln_g', 'grad_sgu_ln_b', 'grad_sgu_w', 'grad_sgu_b', 'grad_w_branch', 'grad_w_gate', 'grad_b_gate', 'grad_w_out', 'grad_final_g', 'delta_norm_g', 'delta_w_in', 'delta_conv_w', 'delta_conv_b', 'delta_conv_ln_g', 'delta_conv_ln_b', 'delta_sgu_ln_g', 'delta_sgu_ln_b', 'delta_sgu_w', 'delta_sgu_b', 'delta_w_branch', 'delta_w_gate', 'delta_b_gate', 'delta_w_out', 'delta_final_g', 'new_m_norm_g', 'new_m_w_in', 'new_m_conv_w', 'new_m_conv_b', 'new_m_conv_ln_g', 'new_m_conv_ln_b', 'new_m_sgu_ln_g', 'new_m_sgu_ln_b', 'new_m_sgu_w', 'new_m_sgu_b', 'new_m_w_branch', 'new_m_w_gate', 'new_m_b_gate', 'new_m_w_out', 'new_m_final_g', 'new_v_norm_g', 'new_v_w_in', 'new_v_conv_w', 'new_v_conv_b', 'new_v_conv_ln_g', 'new_v_conv_ln_b', 'new_v_sgu_ln_g', 'new_v_sgu_ln_b', 'new_v_sgu_w', 'new_v_sgu_b', 'new_v_w_branch', 'new_v_w_gate', 'new_v_b_gate', 'new_v_w_out', 'new_v_final_g']
TWIN_LEAF_KINDS = {'loss': 'loss', 'grad_x': 'grad_x', 'grad_norm_g': 'grad_w', 'grad_w_in': 'grad_w', 'grad_conv_w': 'grad_w', 'grad_conv_b': 'grad_w', 'grad_conv_ln_g': 'grad_w', 'grad_conv_ln_b': 'grad_w', 'grad_sgu_ln_g': 'grad_w', 'grad_sgu_ln_b': 'grad_w', 'grad_sgu_w': 'grad_w', 'grad_sgu_b': 'grad_w', 'grad_w_branch': 'grad_w', 'grad_w_gate': 'grad_w', 'grad_b_gate': 'grad_w', 'grad_w_out': 'grad_w', 'grad_final_g': 'grad_w', 'delta_norm_g': 'delta_w', 'delta_w_in': 'delta_w', 'delta_conv_w': 'delta_w', 'delta_conv_b': 'delta_w', 'delta_conv_ln_g': 'delta_w', 'delta_conv_ln_b': 'delta_w', 'delta_sgu_ln_g': 'delta_w', 'delta_sgu_ln_b': 'delta_w', 'delta_sgu_w': 'delta_w', 'delta_sgu_b': 'delta_w', 'delta_w_branch': 'delta_w', 'delta_w_gate': 'delta_w', 'delta_b_gate': 'delta_w', 'delta_w_out': 'delta_w', 'delta_final_g': 'delta_w', 'new_m_norm_g': 'new_m', 'new_m_w_in': 'new_m', 'new_m_conv_w': 'new_m', 'new_m_conv_b': 'new_m', 'new_m_conv_ln_g': 'new_m', 'new_m_conv_ln_b': 'new_m', 'new_m_sgu_ln_g': 'new_m', 'new_m_sgu_ln_b': 'new_m', 'new_m_sgu_w': 'new_m', 'new_m_sgu_b': 'new_m', 'new_m_w_branch': 'new_m', 'new_m_w_gate': 'new_m', 'new_m_b_gate': 'new_m', 'new_m_w_out': 'new_m', 'new_m_final_g': 'new_m', 'new_v_norm_g': 'new_v', 'new_v_w_in': 'new_v', 'new_v_conv_w': 'new_v', 'new_v_conv_b': 'new_v', 'new_v_conv_ln_g': 'new_v', 'new_v_conv_ln_b': 'new_v', 'new_v_sgu_ln_g': 'new_v', 'new_v_sgu_ln_b': 'new_v', 'new_v_sgu_w': 'new_v', 'new_v_sgu_b': 'new_v', 'new_v_w_branch': 'new_v', 'new_v_w_gate': 'new_v', 'new_v_b_gate': 'new_v', 'new_v_w_out': 'new_v', 'new_v_final_g': 'new_v'}


def _forward(args):
    return _fwd_reference(*[args[k] for k in FWD_PARAMS])


def _output_shape():
    def fwd():
        inp = _fwd_setup_inputs(0)
        return _fwd_reference(*[inp[k] for k in FWD_PARAMS])
    out = _jax.eval_shape(fwd)
    return out.shape, out.dtype

N_MICROBATCH = 1
ADAM_LR = 0.001
ADAM_B1 = 0.9
ADAM_B2 = 0.999
ADAM_EPS = 1e-08
ADAM_WD = 0.01
ADAM_STEP = 10
PER_EXAMPLE_BATCH_AXIS = {'x': 0, 'loss_target': 0}
SHARED_INPUTS = []
_WEIGHT_DTYPES = {'norm_g': _jnp.float32, 'w_in': _jnp.float32, 'conv_w': _jnp.float32, 'conv_b': _jnp.float32, 'conv_ln_g': _jnp.float32, 'conv_ln_b': _jnp.float32, 'sgu_ln_g': _jnp.float32, 'sgu_ln_b': _jnp.float32, 'sgu_w': _jnp.float32, 'sgu_b': _jnp.float32, 'w_branch': _jnp.float32, 'w_gate': _jnp.float32, 'b_gate': _jnp.float32, 'w_out': _jnp.float32, 'final_g': _jnp.float32}
MOMENT_SCALE = {'norm_g': 7.763468e-02, 'w_in': 2.523584e-02, 'conv_w': 3.480157e-02, 'conv_b': 7.551621e-02, 'conv_ln_g': 4.032255e-02, 'conv_ln_b': 3.505959e-02, 'sgu_ln_g': 2.548082e-02, 'sgu_ln_b': 2.575616e-02, 'sgu_w': 2.595392e-02, 'sgu_b': 3.676343e-02, 'w_branch': 2.390198e-02, 'w_gate': 9.340959e-03, 'b_gate': 9.265815e-03, 'w_out': 4.779566e-02, 'final_g': 3.194572e+01}


def _to_microbatches(a, axis):
    t = _jnp.moveaxis(a, axis, 0)
    t = t.reshape((N_MICROBATCH, t.shape[0] // N_MICROBATCH) + t.shape[1:])
    return _jnp.moveaxis(t, 1, axis + 1)


def setup_inputs(seed: int = 0) -> dict:
    inp = _fwd_setup_inputs(seed)
    key = _jax.random.fold_in(_jax.random.key(seed), 7919)
    shape, _ = _output_shape()
    out = dict(inp)
    out["loss_target"] = _jax.random.normal(_jax.random.fold_in(key, 0), shape, _jnp.float32)
    for i, name in enumerate(TWIN_WEIGHTS):
        w = inp[name].astype(_jnp.float32)
        if MOMENT_SCALE is None:
            s = _jnp.sqrt(_jnp.mean(_jnp.square(w)) + 1e-30)
        else:
            s = MOMENT_SCALE[name]
        km, kv = _jax.random.split(_jax.random.fold_in(key, i + 1))
        out[name] = w
        out["m_" + name] = s * _jax.random.normal(km, w.shape, _jnp.float32)
        out["v_" + name] = (s * s) * _jax.random.uniform(kv, w.shape, _jnp.float32, 0.5, 1.5)
    if N_MICROBATCH > 1:
        for name, axis in PER_EXAMPLE_BATCH_AXIS.items():
            out[name] = _to_microbatches(out[name], axis)
    return {'x': out['x'], 'norm_g': out['norm_g'], 'w_in': out['w_in'], 'conv_w': out['conv_w'], 'conv_b': out['conv_b'], 'conv_ln_g': out['conv_ln_g'], 'conv_ln_b': out['conv_ln_b'], 'sgu_ln_g': out['sgu_ln_g'], 'sgu_ln_b': out['sgu_ln_b'], 'sgu_w': out['sgu_w'], 'sgu_b': out['sgu_b'], 'w_branch': out['w_branch'], 'w_gate': out['w_gate'], 'b_gate': out['b_gate'], 'w_out': out['w_out'], 'final_g': out['final_g'], 'loss_target': out['loss_target'], 'm_norm_g': out['m_norm_g'], 'm_w_in': out['m_w_in'], 'm_conv_w': out['m_conv_w'], 'm_conv_b': out['m_conv_b'], 'm_conv_ln_g': out['m_conv_ln_g'], 'm_conv_ln_b': out['m_conv_ln_b'], 'm_sgu_ln_g': out['m_sgu_ln_g'], 'm_sgu_ln_b': out['m_sgu_ln_b'], 'm_sgu_w': out['m_sgu_w'], 'm_sgu_b': out['m_sgu_b'], 'm_w_branch': out['m_w_branch'], 'm_w_gate': out['m_w_gate'], 'm_b_gate': out['m_b_gate'], 'm_w_out': out['m_w_out'], 'm_final_g': out['m_final_g'], 'v_norm_g': out['v_norm_g'], 'v_w_in': out['v_w_in'], 'v_conv_w': out['v_conv_w'], 'v_conv_b': out['v_conv_b'], 'v_conv_ln_g': out['v_conv_ln_g'], 'v_conv_ln_b': out['v_conv_ln_b'], 'v_sgu_ln_g': out['v_sgu_ln_g'], 'v_sgu_ln_b': out['v_sgu_ln_b'], 'v_sgu_w': out['v_sgu_w'], 'v_sgu_b': out['v_sgu_b'], 'v_w_branch': out['v_w_branch'], 'v_w_gate': out['v_w_gate'], 'v_b_gate': out['v_b_gate'], 'v_w_out': out['v_w_out'], 'v_final_g': out['v_final_g']}


def _loss(weights, diff, rest, loss_target):
    with _jax.named_scope("forward"):
        args = {**rest, TWIN_DIFF_INPUT: diff, **{k: w.astype(_WEIGHT_DTYPES[k]) for k, w in weights.items()}}
        y = _forward(args)
    with _jax.named_scope("loss_head"):
        err = _jnp.square(y.astype(_jnp.float32) - loss_target)
        return 0.5 * _jnp.sum(_jnp.mean(err, axis=-1)) if err.ndim else 0.5 * err


def _adamw(w, g, m, v):
    m = ADAM_B1 * m + (1.0 - ADAM_B1) * g
    v = ADAM_B2 * v + (1.0 - ADAM_B2) * _jnp.square(g)
    m_hat = m / (1.0 - ADAM_B1 ** ADAM_STEP)
    v_hat = v / (1.0 - ADAM_B2 ** ADAM_STEP)
    delta = -ADAM_LR * (m_hat / (_jnp.sqrt(v_hat) + ADAM_EPS) + ADAM_WD * w)
    return delta, m, v


def reference(x, norm_g, w_in, conv_w, conv_b, conv_ln_g, conv_ln_b, sgu_ln_g, sgu_ln_b, sgu_w, sgu_b, w_branch, w_gate, b_gate, w_out, final_g, loss_target, m_norm_g, m_w_in, m_conv_w, m_conv_b, m_conv_ln_g, m_conv_ln_b, m_sgu_ln_g, m_sgu_ln_b, m_sgu_w, m_sgu_b, m_w_branch, m_w_gate, m_b_gate, m_w_out, m_final_g, v_norm_g, v_w_in, v_conv_w, v_conv_b, v_conv_ln_g, v_conv_ln_b, v_sgu_ln_g, v_sgu_ln_b, v_sgu_w, v_sgu_b, v_w_branch, v_w_gate, v_b_gate, v_w_out, v_final_g):
    given = dict(x=x, norm_g=norm_g, w_in=w_in, conv_w=conv_w, conv_b=conv_b, conv_ln_g=conv_ln_g, conv_ln_b=conv_ln_b, sgu_ln_g=sgu_ln_g, sgu_ln_b=sgu_ln_b, sgu_w=sgu_w, sgu_b=sgu_b, w_branch=w_branch, w_gate=w_gate, b_gate=b_gate, w_out=w_out, final_g=final_g, loss_target=loss_target, m_norm_g=m_norm_g, m_w_in=m_w_in, m_conv_w=m_conv_w, m_conv_b=m_conv_b, m_conv_ln_g=m_conv_ln_g, m_conv_ln_b=m_conv_ln_b, m_sgu_ln_g=m_sgu_ln_g, m_sgu_ln_b=m_sgu_ln_b, m_sgu_w=m_sgu_w, m_sgu_b=m_sgu_b, m_w_branch=m_w_branch, m_w_gate=m_w_gate, m_b_gate=m_b_gate, m_w_out=m_w_out, m_final_g=m_final_g, v_norm_g=v_norm_g, v_w_in=v_w_in, v_conv_w=v_conv_w, v_conv_b=v_conv_b, v_conv_ln_g=v_conv_ln_g, v_conv_ln_b=v_conv_ln_b, v_sgu_ln_g=v_sgu_ln_g, v_sgu_ln_b=v_sgu_ln_b, v_sgu_w=v_sgu_w, v_sgu_b=v_sgu_b, v_w_branch=v_w_branch, v_w_gate=v_w_gate, v_b_gate=v_b_gate, v_w_out=v_w_out, v_final_g=v_final_g)
    weights = {n: given[n] for n in TWIN_WEIGHTS}
    shared = {n: given[n] for n in SHARED_INPUTS}
    per_example = {n: given[n] for n in ['x']}
    grad_fn = _jax.value_and_grad(_loss, argnums=(0, 1))

    def one_microbatch(ex, loss_target):
        ex = dict(ex)
        diff = ex.pop(TWIN_DIFF_INPUT)
        return grad_fn(weights, diff, {**shared, **ex}, loss_target)

    if N_MICROBATCH == 1:
        loss, (grad_w, grad_x) = one_microbatch(per_example, given["loss_target"])
    else:
        def body(carry, xs):
            loss_sum, grad_sum = carry
            l_k, (gw_k, gx_k) = one_microbatch(xs[0], xs[1])
            with _jax.named_scope("update"):
                return (loss_sum + l_k, _jax.tree.map(_jnp.add, grad_sum, gw_k)), gx_k

        init = (_jnp.zeros((), _jnp.float32), _jax.tree.map(_jnp.zeros_like, weights))
        (loss, grad_w), grad_x = _jax.lax.scan(body, init, (per_example, given["loss_target"]))
    with _jax.named_scope("update"):
        delta_w, new_m, new_v = {}, {}, {}
        for n in TWIN_WEIGHTS:
            delta_w[n], new_m[n], new_v[n] = _adamw(weights[n], grad_w[n], given["m_" + n], given["v_" + n])
    return (loss, grad_x, *[grad_w[n] for n in TWIN_WEIGHTS], *[delta_w[n] for n in TWIN_WEIGHTS],
            *[new_m[n] for n in TWIN_WEIGHTS], *[new_v[n] for n in TWIN_WEIGHTS])
```

```python
import functools

import jax
import jax.numpy as jnp
from jax import lax
from jax.experimental import pallas as pl
from jax.experimental.pallas import tpu as pltpu

F32 = jnp.float32
BF16 = jnp.bfloat16
LANES = 128
SUBLANES = 8
CONV_HALO = 32
NORM_EPS = 1e-6
NEG = -1e30
N_DEV = 8
N_CHIP = 4
DIL_PATTERNS = ((128, 1), (512, 4), (2048, 16))

ADAM_LR = 0.001
ADAM_B1 = 0.9
ADAM_B2 = 0.999
ADAM_EPS = 1e-08
ADAM_WD = 0.01
ADAM_STEP = 10

MESH = pl.DeviceIdType.MESH
NN = (((1,), (0,)), ((), ()))
NT = (((1,), (1,)), ((), ()))
TN = (((0,), (0,)), ((), ()))
VMEM_LIMIT = 52 << 20


def _sds(shape, dtype):
    return jax.ShapeDtypeStruct(tuple(shape), dtype)


def _cp(*sem):
    return pltpu.CompilerParams(dimension_semantics=tuple(sem), vmem_limit_bytes=VMEM_LIMIT)


def _pick(n, target, quantum=LANES):
    if n <= target:
        return n
    t = (target // quantum) * quantum
    while t >= quantum:
        if n % t == 0:
            return t
        t -= quantum
    return n


def _sigmoid(x):
    return 1.0 / (1.0 + jnp.exp(-x))


def _silu(x):
    return x * _sigmoid(x)


def _dsilu(x):
    s = _sigmoid(x)
    return s * (1.0 + x * (1.0 - s))


_GELU_K = 0.7978845608028654
_GELU_A = 0.044715


def _gelu(x):
    return 0.5 * x * (1.0 + jnp.tanh(_GELU_K * (x + _GELU_A * x * x * x)))


def _dgelu(x):
    t = jnp.tanh(_GELU_K * (x + _GELU_A * x * x * x))
    return 0.5 * (1.0 + t) + 0.5 * x * (1.0 - t * t) * _GELU_K * (1.0 + 3.0 * _GELU_A * x * x)


def _ln_stats(v):
    mu = jnp.mean(v, axis=-1, keepdims=True)
    d = v - mu
    var = jnp.mean(d * d, axis=-1, keepdims=True)
    r = lax.rsqrt(var + NORM_EPS)
    return d * r, r


def _ln_bwd(dxh, xh, r):
    return r * (dxh - jnp.mean(dxh, axis=-1, keepdims=True) - xh * jnp.mean(dxh * xh, axis=-1, keepdims=True))


def _acc_rows(ref, row, val, first):
    @pl.when(first)
    def _():
        ref[...] = jnp.zeros_like(ref)
    ref[row:row + 1, :] += val


def _mm(name, a, b, *, grid, kaxis, dims, a_spec, b_spec, acc_shape, out_shape, out_specs,
        epilogue, extras=(), extra_specs=(), sem):
    nk = grid[kaxis]
    ne = len(extras)

    def body(*refs):
        a_ref, b_ref = refs[0], refs[1]
        ex = refs[2:2 + ne]
        outs = refs[2 + ne:-1]
        acc = refs[-1]
        k = pl.program_id(kaxis)

        @pl.when(k == 0)
        def _():
            acc[...] = jnp.zeros_like(acc)

        acc[...] += lax.dot_general(a_ref[...].astype(BF16), b_ref[...].astype(BF16), dims,
                                    preferred_element_type=F32)

        @pl.when(k == nk - 1)
        def _():
            epilogue(acc[...], ex, outs)

    return pl.pallas_call(
        body, name=name, grid=grid, in_specs=[a_spec, b_spec, *extra_specs], out_specs=out_specs,
        out_shape=out_shape, scratch_shapes=[pltpu.VMEM(acc_shape, F32)], compiler_params=_cp(*sem),
    )(a, b, *extras)


def _store(dtype):
    def ep(acc, ex, outs):
        outs[0][...] = acc.astype(dtype)
    return ep


def _mm_nn(name, a, b, out_dtype, epilogue=None, extras=(), extra_specs=(), tm=1024, tn=1024):
    M, K = a.shape
    N = b.shape[1]
    tm, tn = _pick(M, tm, SUBLANES), _pick(N, tn)
    return _mm(name, a, b, grid=(M // tm, N // tn, 1), kaxis=2, dims=NN,
               a_spec=pl.BlockSpec((tm, K), lambda i, j, k: (i, 0)),
               b_spec=pl.BlockSpec((K, tn), lambda i, j, k: (0, j)),
               acc_shape=(tm, tn), out_shape=_sds((M, N), out_dtype),
               out_specs=pl.BlockSpec((tm, tn), lambda i, j, k: (i, j)),
               epilogue=epilogue or _store(out_dtype), extras=extras, extra_specs=extra_specs,
               sem=("parallel", "parallel", "arbitrary"))


def _mm_nt(name, a, b, out_dtype, addend=None, tm=1024, tn=1024, tk=1024):
    M, K = a.shape
    N = b.shape[0]
    tm, tn, tk = _pick(M, tm, SUBLANES), _pick(N, tn), _pick(K, tk)
    extras, extra_specs = (), ()
    if addend is not None:
        extras = (addend,)
        extra_specs = (pl.BlockSpec((tm, tn), lambda i, j, k: (i, j)),)

    def ep(acc, ex, outs):
        if ex:
            acc = acc + ex[0][...]
        outs[0][...] = acc.astype(out_dtype)

    return _mm(name, a, b, grid=(M // tm, N // tn, K // tk), kaxis=2, dims=NT,
               a_spec=pl.BlockSpec((tm, tk), lambda i, j, k: (i, k)),
               b_spec=pl.BlockSpec((tn, tk), lambda i, j, k: (j, k)),
               acc_shape=(tm, tn), out_shape=_sds((M, N), out_dtype),
               out_specs=pl.BlockSpec((tm, tn), lambda i, j, k: (i, j)),
               epilogue=ep, extras=extras, extra_specs=extra_specs,
               sem=("parallel", "parallel", "arbitrary"))


def _mm_tn(name, a, b, tm=1024, tn=1024, tk=1024):
    K, M = a.shape
    N = b.shape[1]
    tm, tn, tk = _pick(M, tm), _pick(N, tn), _pick(K, tk, SUBLANES)
    return _mm(name, a, b, grid=(M // tm, N // tn, K // tk), kaxis=2, dims=TN,
               a_spec=pl.BlockSpec((tk, tm), lambda i, j, k: (k, i)),
               b_spec=pl.BlockSpec((tk, tn), lambda i, j, k: (k, j)),
               acc_shape=(tm, tn), out_shape=_sds((M, N), F32),
               out_specs=pl.BlockSpec((tm, tn), lambda i, j, k: (i, j)),
               epilogue=_store(F32), sem=("parallel", "parallel", "arbitrary"))


def _rmsnorm_fwd(x, g_row):
    T, D = x.shape
    tm = _pick(T, 512, SUBLANES)

    def body(x_ref, g_ref, h_ref):
        xv = x_ref[...]
        r = lax.rsqrt(jnp.mean(xv * xv, axis=-1, keepdims=True) + NORM_EPS)
        h_ref[...] = (xv * r * g_ref[...]).astype(BF16)

    row = pl.BlockSpec((tm, D), lambda i: (i, 0))
    return pl.pallas_call(body, name="rmsnorm_fwd", grid=(T // tm,),
                          in_specs=[row, pl.BlockSpec((1, D), lambda i: (0, 0))], out_specs=row,
                          out_shape=_sds((T, D), BF16), compiler_params=_cp("parallel"))(x, g_row)


def _rmsnorm_bwd(x, dh, dout, g_row):
    T, D = x.shape
    tm = _pick(T, 256, SUBLANES)

    def body(x_ref, dh_ref, do_ref, g_ref, dx_ref, dg_ref):
        xv = x_ref[...]
        r = lax.rsqrt(jnp.mean(xv * xv, axis=-1, keepdims=True) + NORM_EPS)
        xh = xv * r
        dhv = dh_ref[...]
        dxh = dhv * g_ref[...]
        dx_ref[...] = do_ref[...] + r * (dxh - xh * jnp.mean(dxh * xh, axis=-1, keepdims=True))
        _acc_rows(dg_ref, 0, jnp.sum(dhv * xh, axis=0, keepdims=True), pl.program_id(0) == 0)

    row = pl.BlockSpec((tm, D), lambda i: (i, 0))
    return pl.pallas_call(body, name="rmsnorm_bwd", grid=(T // tm,),
                          in_specs=[row, row, row, pl.BlockSpec((1, D), lambda i: (0, 0))],
                          out_specs=[row, pl.BlockSpec((SUBLANES, D), lambda i: (0, 0))],
                          out_shape=[_sds((T, D), F32), _sds((SUBLANES, D), F32)],
                          compiler_params=_cp("arbitrary"))(x, dh, dout, g_row)


def _loss_head(x, target, g_row):
    T, D = x.shape
    tm = _pick(T, 256, SUBLANES)

    def body(x_ref, t_ref, g_ref, dx_ref, dg_ref, loss_ref):
        first = pl.program_id(0) == 0
        xv = x_ref[...]
        g = g_ref[...]
        r = lax.rsqrt(jnp.mean(xv * xv, axis=-1, keepdims=True) + NORM_EPS)
        xh = xv * r
        err = xh * g - t_ref[...]
        part = 0.5 * jnp.sum(jnp.mean(err * err, axis=-1, keepdims=True), axis=0, keepdims=True)

        @pl.when(first)
        def _():
            loss_ref[...] = jnp.zeros_like(loss_ref)

        loss_ref[...] += jnp.broadcast_to(part, loss_ref.shape)
        dy = err / D
        _acc_rows(dg_ref, 0, jnp.sum(dy * xh, axis=0, keepdims=True), first)
        dxh = dy * g
        dx_ref[...] = r * (dxh - xh * jnp.mean(dxh * xh, axis=-1, keepdims=True))

    row = pl.BlockSpec((tm, D), lambda i: (i, 0))
    return pl.pallas_call(body, name="loss_head", grid=(T // tm,),
                          in_specs=[row, row, pl.BlockSpec((1, D), lambda i: (0, 0))],
                          out_specs=[row, pl.BlockSpec((SUBLANES, D), lambda i: (0, 0)),
                                     pl.BlockSpec((SUBLANES, LANES), lambda i: (0, 0))],
                          out_shape=[_sds((T, D), F32), _sds((SUBLANES, D), F32), _sds((SUBLANES, LANES), F32)],
                          compiler_params=_cp("arbitrary"))(x, target, g_row)


def _tri(cmp):
    r = lax.broadcasted_iota(jnp.int32, (LANES, LANES), 0)
    c = lax.broadcasted_iota(jnp.int32, (LANES, LANES), 1)
    return cmp(r, c).astype(BF16)


def _scan_mm(x, tri):
    hi = x.astype(BF16)
    lo = (x - hi.astype(F32)).astype(BF16)
    return (jnp.dot(hi, tri, preferred_element_type=F32) + jnp.dot(lo, tri, preferred_element_type=F32))


def _sb_tile(qs, kb, row, key, c_after, tri_after):
    z = lax.dot_general(qs, kb, NT, preferred_element_type=F32)
    lb = jnp.minimum(z, 0.0) - jnp.log(1.0 + jnp.exp(-jnp.abs(z)))
    causal = key < row
    l1m = jnp.where(causal, lb - z, 0.0)
    after = c_after + _scan_mm(l1m, tri_after)
    w = jnp.where(causal, jnp.exp(lb + after), 0.0)
    return lb, l1m, w, causal


def _sb_fwd(proj, W):
    T = proj.shape[0]
    H = W // LANES
    assert T // LANES <= LANES
    tq = _pick(T, 256, LANES)
    sub = tq // LANES
    scale = LANES ** -0.5

    def body(q_ref, k_ref, v_ref, g_ref, o_ref, y_ref, c_ref, run):
        i = pl.program_id(1)
        qs = (q_ref[...] * scale).astype(BF16)
        row = i * tq + lax.broadcasted_iota(jnp.int32, (tq, LANES), 0)
        col = lax.broadcasted_iota(jnp.int32, (tq, LANES), 1)
        tri = _tri(lambda r, c: r > c)
        nkb = (i + 1) * sub
        o_ref[...] = jnp.zeros_like(o_ref)
        c_ref[...] = jnp.zeros_like(c_ref)
        run[...] = jnp.zeros_like(run)

        def step(jj, carry):
            j = nkb - 1 - jj
            off = pl.multiple_of(j * LANES, LANES)
            kb = k_ref[pl.ds(off, LANES), :].astype(BF16)
            vb = v_ref[pl.ds(off, LANES), :].astype(BF16)
            c_after = run[...]
            _, l1m, w, _ = _sb_tile(qs, kb, row, col + off, c_after, tri)
            o_ref[...] += jnp.dot(w.astype(BF16), vb, preferred_element_type=F32)
            c_ref[...] = jnp.where(col == j, c_after, c_ref[...])
            run[...] = c_after + jnp.sum(l1m, axis=1, keepdims=True)
            return carry

        lax.fori_loop(0, nkb, step, 0)
        y_ref[...] = (o_ref[...] * _silu(g_ref[...])).astype(BF16)

    qspec = lambda c0: pl.BlockSpec((tq, LANES), lambda h, i: (i, c0 + h))
    kvspec = lambda c0: pl.BlockSpec((T, LANES), lambda h, i: (0, c0 + h))
    out = pl.BlockSpec((tq, LANES), lambda h, i: (i, h))
    return pl.pallas_call(body, name="sb_fwd", grid=(H, T // tq),
                          in_specs=[qspec(0), kvspec(H), kvspec(2 * H), qspec(3 * H)], out_specs=[out, out, out],
                          out_shape=[_sds((T, W), F32), _sds((T, W), BF16), _sds((T, W), F32)],
                          scratch_shapes=[pltpu.VMEM((tq, LANES), F32)],
                          compiler_params=_cp("parallel", "arbitrary"))(proj, proj, proj, proj)


def _sb_bwd(proj, o, carries, dy, W):
    T = proj.shape[0]
    H = W // LANES
    tq = _pick(T, 256, LANES)
    sub = tq // LANES
    scale = LANES ** -0.5

    def body(q_ref, k_ref, v_ref, g_ref, o_ref, c_ref, dy_ref, dq_ref, dk_ref, dv_ref, dg_ref, run):
        i = pl.program_id(1)

        @pl.when(i == 0)
        def _():
            dk_ref[...] = jnp.zeros_like(dk_ref)
            dv_ref[...] = jnp.zeros_like(dv_ref)

        gv = g_ref[...]
        dyv = dy_ref[...]
        dg_ref[...] = (dyv * o_ref[...] * _dsilu(gv)).astype(BF16)
        dob = (dyv * _silu(gv)).astype(BF16)
        qs = (q_ref[...] * scale).astype(BF16)
        row = i * tq + lax.broadcasted_iota(jnp.int32, (tq, LANES), 0)
        col = lax.broadcasted_iota(jnp.int32, (tq, LANES), 1)
        tri_after = _tri(lambda r, c: r > c)
        tri_before = _tri(lambda r, c: r < c)
        dq_ref[...] = jnp.zeros_like(dq_ref)
        run[...] = jnp.zeros_like(run)

        def step(j, carry):
            off = pl.multiple_of(j * LANES, LANES)
            kb = k_ref[pl.ds(off, LANES), :].astype(BF16)
            vb = v_ref[pl.ds(off, LANES), :].astype(BF16)
            c_after = jnp.sum(jnp.where(col == j, c_ref[...], 0.0), axis=1, keepdims=True)
            lb, _, w, causal = _sb_tile(qs, kb, row, col + off, c_after, tri_after)
            gw = w * lax.dot_general(dob, vb, NT, preferred_element_type=F32)
            gpre = run[...]
            gbefore = gpre + _scan_mm(gw, tri_before)
            beta = jnp.exp(lb)
            dz = jnp.where(causal, gw * (1.0 - beta) - gbefore * beta, 0.0).astype(BF16)
            dq_ref[...] += jnp.dot(dz, kb, preferred_element_type=F32)
            dk_ref[pl.ds(off, LANES), :] += lax.dot_general(dz, qs, TN, preferred_element_type=F32)
            dv_ref[pl.ds(off, LANES), :] += lax.dot_general(w.astype(BF16), dob, TN, preferred_element_type=F32)
            run[...] = gpre + jnp.sum(gw, axis=1, keepdims=True)
            return carry

        lax.fori_loop(0, (i + 1) * sub, step, 0)
        dq_ref[...] = dq_ref[...] * scale

    qspec = lambda c0: pl.BlockSpec((tq, LANES), lambda h, i: (i, c0 + h))
    kvspec = lambda c0: pl.BlockSpec((T, LANES), lambda h, i: (0, c0 + h))
    blk = pl.BlockSpec((tq, LANES), lambda h, i: (i, h))
    full = pl.BlockSpec((T, LANES), lambda h, i: (0, h))
    return pl.pallas_call(body, name="sb_bwd", grid=(H, T // tq),
                          in_specs=[qspec(0), kvspec(H), kvspec(2 * H), qspec(3 * H), blk, blk, blk],
                          out_specs=[blk, full, full, blk],
                          out_shape=[_sds((T, W), F32), _sds((T, W), F32), _sds((T, W), F32), _sds((T, W), BF16)],
                          scratch_shapes=[pltpu.VMEM((tq, LANES), F32)],
                          compiler_params=_cp("parallel", "arbitrary"))(proj, proj, proj, proj, o, carries, dy)


def _conv_specs(T, W, tm, col_a):
    per = tm // CONV_HALO
    cur = lambda c: pl.BlockSpec((tm, W), lambda i: (i, c))
    prev = lambda c: pl.BlockSpec((CONV_HALO, W), lambda i: (jnp.maximum(i * per - 1, 0), c))
    return cur, prev


def _conv_fwd(proj, conv_w, K, conv_b, ln_g, ln_b, W):
    T = proj.shape[0]
    tm = _pick(T, 256, CONV_HALO)
    lead = CONV_HALO - (K - 1)
    cur, prev = _conv_specs(T, W, tm, 4)

    def body(a_ref, b_ref, ah_ref, bh_ref, g_ref, w_ref, cb_ref, lg_ref, lb_ref, c_ref, y_ref, glu):
        i = pl.program_id(0)
        glu[0:CONV_HALO, :] = jnp.where(i > 0, ah_ref[...] * _sigmoid(bh_ref[...]), 0.0)
        glu[CONV_HALO:, :] = a_ref[...] * _sigmoid(b_ref[...])
        c = jnp.broadcast_to(cb_ref[...], (tm, W))
        for k in range(K):
            c = c + w_ref[k:k + 1, :] * glu[lead + k:lead + k + tm, :]
        c_ref[...] = c
        xh, _ = _ln_stats(c)
        y_ref[...] = (_silu(xh * lg_ref[...] + lb_ref[...]) * _silu(g_ref[...])).astype(BF16)

    vec = pl.BlockSpec((1, W), lambda i: (0, 0))
    row = pl.BlockSpec((tm, W), lambda i: (i, 0))
    return pl.pallas_call(body, name="conv_fwd", grid=(T // tm,),
                          in_specs=[cur(4), cur(5), prev(4), prev(5), cur(6),
                                    pl.BlockSpec((CONV_HALO, W), lambda i: (0, 0)), vec, vec, vec],
                          out_specs=[row, row], out_shape=[_sds((T, W), F32), _sds((T, W), BF16)],
                          scratch_shapes=[pltpu.VMEM((tm + CONV_HALO, W), F32)],
                          compiler_params=_cp("parallel"))(proj, proj, proj, proj, proj, conv_w, conv_b, ln_g, ln_b)


def _conv_bwd_ln(proj, c, dy, ln_g, ln_b, W):
    T = proj.shape[0]
    tm = _pick(T, 256, SUBLANES)

    def body(g_ref, c_ref, dy_ref, lg_ref, lb_ref, dc_ref, dg_ref, st_ref):
        first = pl.program_id(0) == 0
        gv = g_ref[...]
        dyv = dy_ref[...]
        xh, r = _ln_stats(c_ref[...])
        lg = lg_ref[...]
        ln = xh * lg + lb_ref[...]
        dg_ref[...] = (dyv * _silu(ln) * _dsilu(gv)).astype(BF16)
        dln = dyv * _silu(gv) * _dsilu(ln)
        dc = _ln_bwd(dln * lg, xh, r)
        dc_ref[...] = dc
        _acc_rows(st_ref, 0, jnp.sum(dln * xh, axis=0, keepdims=True), first)
        st_ref[1:2, :] += jnp.sum(dln, axis=0, keepdims=True)
        st_ref[2:3, :] += jnp.sum(dc, axis=0, keepdims=True)

    row = pl.BlockSpec((tm, W), lambda i: (i, 0))
    vec = pl.BlockSpec((1, W), lambda i: (0, 0))
    return pl.pallas_call(body, name="conv_bwd_ln", grid=(T // tm,),
                          in_specs=[pl.BlockSpec((tm, W), lambda i: (i, 6)), row, row, vec, vec],
                          out_specs=[row, row, pl.BlockSpec((SUBLANES, W), lambda i: (0, 0))],
                          out_shape=[_sds((T, W), F32), _sds((T, W), BF16), _sds((SUBLANES, W), F32)],
                          compiler_params=_cp("arbitrary"))(proj, c, dy, ln_g, ln_b)


def _conv_bwd_taps(proj, dc, conv_w, K, W):
    T = proj.shape[0]
    tm = _pick(T, 256, CONV_HALO)
    lead = CONV_HALO - (K - 1)
    per = tm // CONV_HALO
    nblk = T // tm
    cur, prev = _conv_specs(T, W, tm, 4)

    def body(a_ref, b_ref, ah_ref, bh_ref, dc_ref, dcn_ref, w_ref, da_ref, db_ref, dw_ref, glu, dcs):
        i = pl.program_id(0)
        av = a_ref[...]
        sb = _sigmoid(b_ref[...])
        glu[0:CONV_HALO, :] = jnp.where(i > 0, ah_ref[...] * _sigmoid(bh_ref[...]), 0.0)
        glu[CONV_HALO:, :] = av * sb
        dcv = dc_ref[...]
        dcs[0:tm, :] = dcv
        dcs[tm:, :] = jnp.where(i < nblk - 1, dcn_ref[...], 0.0)

        @pl.when(i == 0)
        def _():
            dw_ref[...] = jnp.zeros_like(dw_ref)

        dglu = jnp.zeros((tm, W), F32)
        for k in range(K):
            dglu = dglu + w_ref[k:k + 1, :] * dcs[K - 1 - k:K - 1 - k + tm, :]
            dw_ref[k:k + 1, :] += jnp.sum(dcv * glu[lead + k:lead + k + tm, :], axis=0, keepdims=True)
        da_ref[...] = (dglu * sb).astype(BF16)
        db_ref[...] = (dglu * av * sb * (1.0 - sb)).astype(BF16)

    row = pl.BlockSpec((tm, W), lambda i: (i, 0))
    nxt = pl.BlockSpec((CONV_HALO, W), lambda i: (jnp.minimum((i + 1) * per, T // CONV_HALO - 1), 0))
    return pl.pallas_call(body, name="conv_bwd_taps", grid=(nblk,),
                          in_specs=[cur(4), cur(5), prev(4), prev(5), row, nxt,
                                    pl.BlockSpec((CONV_HALO, W), lambda i: (0, 0))],
                          out_specs=[row, row, pl.BlockSpec((CONV_HALO, W), lambda i: (0, 0))],
                          out_shape=[_sds((T, W), BF16), _sds((T, W), BF16), _sds((CONV_HALO, W), F32)],
                          scratch_shapes=[pltpu.VMEM((tm + CONV_HALO, W), F32), pltpu.VMEM((tm + CONV_HALO, W), F32)],
                          compiler_params=_cp("arbitrary"))(proj, proj, proj, proj, dc, dc, conv_w)


def _sgu_common(cu, cv, lg, lb, w_ref, bt_ref, z_scr, G, nch):
    u = _gelu(cu)
    xh, r = _ln_stats(_gelu(cv))
    vn = (xh * lg + lb).astype(BF16)
    rr = lax.broadcasted_iota(jnp.int32, (LANES, LANES), 0)
    cc = lax.broadcasted_iota(jnp.int32, (LANES, LANES), 1)
    tril = rr >= cc
    wts = [jnp.where(tril, w_ref[g], 0.0).astype(BF16) for g in range(G)]
    for ch in range(nch):
        rs = slice(ch * LANES, (ch + 1) * LANES)
        for g in range(G):
            cs = slice(g * LANES, (g + 1) * LANES)
            z_scr[rs, cs] = jnp.dot(wts[g], vn[rs, cs], preferred_element_type=F32) + bt_ref[:, g:g + 1]
    return u, xh, r, vn, wts, tril


def _sgu_fwd(proj, sgu_w, sgu_bt, ln_g, ln_b, W):
    T = proj.shape[0]
    G = W // LANES
    tm = _pick(T, 256, LANES)
    nch = tm // LANES

    def body(u_ref, v_ref, g_ref, w_ref, bt_ref, lg_ref, lb_ref, y_ref, z_scr):
        u, *_ = _sgu_common(u_ref[...], v_ref[...], lg_ref[...], lb_ref[...], w_ref, bt_ref, z_scr, G, nch)
        y_ref[...] = (u * z_scr[...] * _silu(g_ref[...])).astype(BF16)

    cur = lambda c: pl.BlockSpec((tm, W), lambda i: (i, c))
    vec = pl.BlockSpec((1, W), lambda i: (0, 0))
    return pl.pallas_call(body, name="sgu_fwd", grid=(T // tm,),
                          in_specs=[cur(7), cur(8), cur(9), pl.BlockSpec((G, LANES, LANES), lambda i: (0, 0, 0)),
                                    pl.BlockSpec((LANES, G), lambda i: (0, 0)), vec, vec],
                          out_specs=pl.BlockSpec((tm, W), lambda i: (i, 0)), out_shape=_sds((T, W), BF16),
                          scratch_shapes=[pltpu.VMEM((tm, W), F32)],
                          compiler_params=_cp("parallel"))(proj, proj, proj, sgu_w, sgu_bt, ln_g, ln_b)


def _sgu_bwd(proj, dy, sgu_w, sgu_bt, ln_g, ln_b, W):
    T = proj.shape[0]
    G = W // LANES
    tm = _pick(T, 256, LANES)
    nch = tm // LANES

    def body(u_ref, v_ref, g_ref, dy_ref, w_ref, bt_ref, lg_ref, lb_ref,
             du_ref, dv_ref, dg_ref, dw_ref, dbt_ref, st_ref, z_scr, dvn_scr):
        first = pl.program_id(0) == 0
        cu, cv, gv, dyv = u_ref[...], v_ref[...], g_ref[...], dy_ref[...]
        lg = lg_ref[...]
        u, xh, r, vn, wts, tril = _sgu_common(cu, cv, lg, lb_ref[...], w_ref, bt_ref, z_scr, G, nch)
        z = z_scr[...]
        sg = _silu(gv)
        dg_ref[...] = (dyv * u * z * _dsilu(gv)).astype(BF16)
        du_ref[...] = (dyv * z * sg * _dgelu(cu)).astype(BF16)
        dz = dyv * u * sg
        dzb = dz.astype(BF16)

        @pl.when(first)
        def _():
            dw_ref[...] = jnp.zeros_like(dw_ref)
            dbt_ref[...] = jnp.zeros_like(dbt_ref)

        for g in range(G):
            cs = slice(g * LANES, (g + 1) * LANES)
            dwg = jnp.zeros((LANES, LANES), F32)
            dbg = jnp.zeros((LANES, 1), F32)
            for ch in range(nch):
                rs = slice(ch * LANES, (ch + 1) * LANES)
                dwg = dwg + lax.dot_general(dzb[rs, cs], vn[rs, cs], NT, preferred_element_type=F32)
                dbg = dbg + jnp.sum(dz[rs, cs], axis=1, keepdims=True)
                dvn_scr[rs, cs] = lax.dot_general(wts[g], dzb[rs, cs], TN, preferred_element_type=F32)
            dw_ref[g] += jnp.where(tril, dwg, 0.0)
            dbt_ref[:, g:g + 1] += dbg
        dvn = dvn_scr[...]
        _acc_rows(st_ref, 0, jnp.sum(dvn * xh, axis=0, keepdims=True), first)
        st_ref[1:2, :] += jnp.sum(dvn, axis=0, keepdims=True)
        dv_ref[...] = (_ln_bwd(dvn * lg, xh, r) * _dgelu(cv)).astype(BF16)

    cur = lambda c: pl.BlockSpec((tm, W), lambda i: (i, c))
    row = pl.BlockSpec((tm, W), lambda i: (i, 0))
    vec = pl.BlockSpec((1, W), lambda i: (0, 0))
    wspec = pl.BlockSpec((G, LANES, LANES), lambda i: (0, 0, 0))
    bspec = pl.BlockSpec((LANES, G), lambda i: (0, 0))
    return pl.pallas_call(body, name="sgu_bwd", grid=(T // tm,),
                          in_specs=[cur(7), cur(8), cur(9), row, wspec, bspec, vec, vec],
                          out_specs=[row, row, row, wspec, bspec, pl.BlockSpec((SUBLANES, W), lambda i: (0, 0))],
                          out_shape=[_sds((T, W), BF16)] * 3 + [_sds((G, LANES, LANES), F32), _sds((LANES, G), F32),
                                                                 _sds((SUBLANES, W), F32)],
                          scratch_shapes=[pltpu.VMEM((tm, W), F32), pltpu.VMEM((tm, W), F32)],
                          compiler_params=_cp("arbitrary"))(proj, proj, proj, dy, sgu_w, sgu_bt, ln_g, ln_b)


def _class_rows(r, dil):
    return pl.ds(r, LANES, stride=dil) if dil > 1 else pl.ds(0, LANES)


def _dil_masks():
    a = lax.broadcasted_iota(jnp.int32, (LANES, LANES), 0)
    c = lax.broadcasted_iota(jnp.int32, (LANES, LANES), 1)
    return c <= a, c >= a


def _dil_fwd_group(proj, W, gi, dil):
    T = proj.shape[0]
    H = W // LANES
    R = LANES * dil
    nsb = T // R
    scale = LANES ** -0.5
    cq, ck, cv = (10 + gi) * H, (13 + gi) * H, 16 * H

    def body(q_ref, kc_ref, kp_ref, vc_ref, vp_ref, o_ref, l_ref):
        b = pl.program_id(1)
        m_cur, m_prev = _dil_masks()
        m_prev = m_prev & (b > 0)

        def cls(r, carry):
            sl = _class_rows(r, dil)
            q = (q_ref[sl, :] * scale).astype(BF16)
            sc = lax.dot_general(q, kc_ref[sl, :].astype(BF16), NT, preferred_element_type=F32)
            sp = lax.dot_general(q, kp_ref[sl, :].astype(BF16), NT, preferred_element_type=F32)
            sc = jnp.where(m_cur, sc, NEG)
            sp = jnp.where(m_prev, sp, NEG)
            m = jnp.maximum(jnp.max(sc, axis=1, keepdims=True), jnp.max(sp, axis=1, keepdims=True))
            pc = jnp.exp(sc - m)
            pp = jnp.exp(sp - m)
            den = jnp.sum(pc, axis=1, keepdims=True) + jnp.sum(pp, axis=1, keepdims=True)
            pv = (jnp.dot(pc.astype(BF16), vc_ref[sl, :].astype(BF16), preferred_element_type=F32)
                  + jnp.dot(pp.astype(BF16), vp_ref[sl, :].astype(BF16), preferred_element_type=F32))
            o_ref[sl, :] = pv / den
            l_ref[sl, :] = jnp.broadcast_to(m + jnp.log(den), (LANES, LANES))
            return carry

        lax.fori_loop(0, dil, cls, 0)

    cur = lambda c0: pl.BlockSpec((R, LANES), lambda h, b: (b, c0 + h))
    prv = lambda c0: pl.BlockSpec((R, LANES), lambda h, b: (jnp.maximum(b - 1, 0), c0 + h))
    out = pl.BlockSpec((R, LANES), lambda h, b: (b, h))
    return pl.pallas_call(body, name=f"dil_fwd_g{gi}", grid=(H, nsb),
                          in_specs=[cur(cq), cur(ck), prv(ck), cur(cv), prv(cv)], out_specs=[out, out],
                          out_shape=[_sds((T, W), F32), _sds((T, W), F32)],
                          compiler_params=_cp("parallel", "parallel"))(proj, proj, proj, proj, proj)


def _dil_combine(proj, os_, ls_, W):
    T = proj.shape[0]
    tm = _pick(T, 256, SUBLANES)

    def body(g_ref, o0, o1, o2, l0, l1, l2, od_ref, lse_ref, y_ref):
        a0, a1, a2 = l0[...], l1[...], l2[...]
        m = jnp.maximum(jnp.maximum(a0, a1), a2)
        e0, e1, e2 = jnp.exp(a0 - m), jnp.exp(a1 - m), jnp.exp(a2 - m)
        s = e0 + e1 + e2
        od = (e0 / s) * o0[...] + (e1 / s) * o1[...] + (e2 / s) * o2[...]
        od_ref[...] = od
        lse_ref[...] = m + jnp.log(s)
        y_ref[...] = (od * _silu(g_ref[...])).astype(BF16)

    row = pl.BlockSpec((tm, W), lambda i: (i, 0))
    return pl.pallas_call(body, name="dil_combine", grid=(T // tm,),
                          in_specs=[pl.BlockSpec((tm, W), lambda i: (i, 17))] + [row] * 6, out_specs=[row, row, row],
                          out_shape=[_sds((T, W), F32), _sds((T, W), F32), _sds((T, W), BF16)],
                          compiler_params=_cp("parallel"))(proj, *os_, *ls_)


def _dil_bwd_pre(proj, od, dy, W):
    T = proj.shape[0]
    H = W // LANES
    tm = _pick(T, 512, SUBLANES)

    def body(g_ref, od_ref, dy_ref, do_ref, dl_ref, dg_ref):
        gv, odv, dyv = g_ref[...], od_ref[...], dy_ref[...]
        do = dyv * _silu(gv)
        do_ref[...] = do
        dl_ref[...] = jnp.broadcast_to(jnp.sum(do * odv, axis=1, keepdims=True), (tm, LANES))
        dg_ref[...] = (dyv * odv * _dsilu(gv)).astype(BF16)

    blk = pl.BlockSpec((tm, LANES), lambda i, h: (i, h))
    return pl.pallas_call(body, name="dil_bwd_pre", grid=(T // tm, H),
                          in_specs=[pl.BlockSpec((tm, LANES), lambda i, h: (i, 17 * H + h)), blk, blk],
                          out_specs=[blk, blk, blk],
                          out_shape=[_sds((T, W), F32), _sds((T, W), F32), _sds((T, W), BF16)],
                          compiler_params=_cp("parallel", "parallel"))(proj, od, dy)


def _dil_bwd_group(proj, do, lse, delta, W, gi, dil):
    T = proj.shape[0]
    H = W // LANES
    R = LANES * dil
    nsb = T // R
    scale = LANES ** -0.5
    cq, ck, cv = (10 + gi) * H, (13 + gi) * H, 16 * H

    def body(qc_ref, qn_ref, kc_ref, kp_ref, vc_ref, vp_ref, doc_ref, don_ref, lc_ref, ln_ref, dc_ref, dn_ref,
             dq_ref, dk_ref, dv_ref):
        b = pl.program_id(1)
        m_cur, m_prev = _dil_masks()
        m_cp = m_prev & (b > 0)
        m_nc = m_prev & (b < nsb - 1)

        def probs(q, k, mask, l):
            s = lax.dot_general(q, k, NT, preferred_element_type=F32)
            return jnp.exp(jnp.where(mask, s - l, NEG))

        def cls(r, carry):
            sl = _class_rows(r, dil)
            q_c = (qc_ref[sl, :] * scale).astype(BF16)
            q_n = (qn_ref[sl, :] * scale).astype(BF16)
            k_c, k_p = kc_ref[sl, :].astype(BF16), kp_ref[sl, :].astype(BF16)
            v_c, v_p = vc_ref[sl, :].astype(BF16), vp_ref[sl, :].astype(BF16)
            do_c, do_n = doc_ref[sl, :].astype(BF16), don_ref[sl, :].astype(BF16)
            l_c, l_n = lc_ref[sl, :], ln_ref[sl, :]
            d_c, d_n = dc_ref[sl, :], dn_ref[sl, :]
            p_cc = probs(q_c, k_c, m_cur, l_c)
            p_cp = probs(q_c, k_p, m_cp, l_c)
            p_nc = probs(q_n, k_c, m_nc, l_n)
            ds_cc = (p_cc * (lax.dot_general(do_c, v_c, NT, preferred_element_type=F32) - d_c)).astype(BF16)
            ds_cp = (p_cp * (lax.dot_general(do_c, v_p, NT, preferred_element_type=F32) - d_c)).astype(BF16)
            ds_nc = (p_nc * (lax.dot_general(do_n, v_c, NT, preferred_element_type=F32) - d_n)).astype(BF16)
            dq = (jnp.dot(ds_cc, k_c, preferred_element_type=F32) + jnp.dot(ds_cp, k_p, preferred_element_type=F32))
            dk = (lax.dot_general(ds_cc, q_c, TN, preferred_element_type=F32)
                  + lax.dot_general(ds_nc, q_n, TN, preferred_element_type=F32))
            dv = (lax.dot_general(p_cc.astype(BF16), do_c, TN, preferred_element_type=F32)
                  + lax.dot_general(p_nc.astype(BF16), do_n, TN, preferred_element_type=F32))
            dq_ref[sl, :] = dq * scale
            dk_ref[sl, :] = dk
            dv_ref[sl, :] = dv
            return carry

        lax.fori_loop(0, dil, cls, 0)

    cur = lambda c0: pl.BlockSpec((R, LANES), lambda h, b: (b, c0 + h))
    prv = lambda c0: pl.BlockSpec((R, LANES), lambda h, b: (jnp.maximum(b - 1, 0), c0 + h))
    nxt = lambda c0: pl.BlockSpec((R, LANES), lambda h, b: (jnp.minimum(b + 1, nsb - 1), c0 + h))
    out = pl.BlockSpec((R, LANES), lambda h, b: (b, h))
    return pl.pallas_call(body, name=f"dil_bwd_g{gi}", grid=(H, nsb),
                          in_specs=[cur(cq), nxt(cq), cur(ck), prv(ck), cur(cv), prv(cv),
                                    cur(0), nxt(0), cur(0), nxt(0), cur(0), nxt(0)],
                          out_specs=[out, out, out], out_shape=[_sds((T, W), F32)] * 3,
                          compiler_params=_cp("parallel", "parallel"))(
        proj, proj, proj, proj, proj, proj, do, do, lse, lse, delta, delta)


def _sum3_bf16(a, b, c):
    T, W = a.shape
    tm = _pick(T, 512, SUBLANES)

    def body(a_ref, b_ref, c_ref, o_ref):
        o_ref[...] = (a_ref[...] + b_ref[...] + c_ref[...]).astype(BF16)

    row = pl.BlockSpec((tm, W), lambda i: (i, 0))
    return pl.pallas_call(body, name="dil_dv_sum", grid=(T // tm,), in_specs=[row, row, row], out_specs=row,
                          out_shape=_sds((T, W), BF16), compiler_params=_cp("parallel"))(a, b, c)


def _to_bf16(name, parts):
    T, W = parts[0].shape
    n = len(parts)
    tm = _pick(T, 512, SUBLANES)

    def body(*refs):
        o_ref = refs[n]
        for p in range(n):
            o_ref[:, p * W:(p + 1) * W] = refs[p][...].astype(BF16)

    row = pl.BlockSpec((tm, W), lambda i: (i, 0))
    return pl.pallas_call(body, name=name, grid=(T // tm,), in_specs=[row] * n,
                          out_specs=pl.BlockSpec((tm, n * W), lambda i: (i, 0)),
                          out_shape=_sds((T, n * W), BF16), compiler_params=_cp("parallel"))(*parts)


def _branch_merge(ys, w_br, gates):
    NB, T, W = ys.shape
    D = w_br.shape[2]
    tm, tn = _pick(T, 1024, SUBLANES), _pick(D, 1024)
    nj = D // tn

    def body(y_ref, w_ref, g_ref, yp_ref, m_ref, acc):
        n = pl.program_id(2)
        yp = jnp.dot(y_ref[...], w_ref[...], preferred_element_type=F32)
        yp_ref[...] = yp

        @pl.when(n == 0)
        def _():
            acc[...] = jnp.zeros_like(acc)

        acc[...] += g_ref[...] * yp

        @pl.when(n == NB - 1)
        def _():
            m_ref[...] = acc[...].astype(BF16)

    return pl.pallas_call(
        body, name="branch_merge", grid=(T // tm, nj, NB),
        in_specs=[pl.BlockSpec((None, tm, W), lambda i, j, n: (n, i, 0)),
                  pl.BlockSpec((None, W, tn), lambda i, j, n: (n, 0, j)),
                  pl.BlockSpec((tm, tn), lambda i, j, n: (i, n * nj + j))],
        out_specs=[pl.BlockSpec((None, tm, tn), lambda i, j, n: (n, i, j)),
                   pl.BlockSpec((tm, tn), lambda i, j, n: (i, j))],
        out_shape=[_sds((NB, T, D), F32), _sds((T, D), BF16)],
        scratch_shapes=[pltpu.VMEM((tm, tn), F32)],
        compiler_params=_cp("parallel", "parallel", "arbitrary"))(ys, w_br, gates)


def _merge_bwd(dmerged, gates, yproj):
    NB, T, D = yproj.shape
    tm = _pick(T, 256, SUBLANES)

    def body(dm_ref, g_ref, yp_ref, dyp_ref, dz_ref, db_ref):
        dm, g = dm_ref[...], g_ref[...]
        dyp_ref[...] = (dm * g).astype(BF16)
        dz = dm * yp_ref[...] * g * (1.0 - g)
        dz_ref[...] = dz.astype(BF16)
        _acc_rows(db_ref, 0, jnp.sum(dz, axis=0, keepdims=True), pl.program_id(1) == 0)

    return pl.pallas_call(
        body, name="merge_bwd", grid=(NB, T // tm),
        in_specs=[pl.BlockSpec((tm, D), lambda n, i: (i, 0)), pl.BlockSpec((tm, D), lambda n, i: (i, n)),
                  pl.BlockSpec((None, tm, D), lambda n, i: (n, i, 0))],
        out_specs=[pl.BlockSpec((None, tm, D), lambda n, i: (n, i, 0)), pl.BlockSpec((tm, D), lambda n, i: (i, n)),
                   pl.BlockSpec((SUBLANES, D), lambda n, i: (0, n))],
        out_shape=[_sds((NB, T, D), BF16), _sds((T, NB * D), BF16), _sds((SUBLANES, NB * D), F32)],
        compiler_params=_cp("parallel", "arbitrary"))(dmerged, gates, yproj)


def _branch_bwd_dy(dyp, w_br):
    NB, T, D = dyp.shape
    W = w_br.shape[1]
    tm = _pick(T, 1024, SUBLANES)

    def body(a_ref, w_ref, o_ref):
        o_ref[...] = lax.dot_general(a_ref[...], w_ref[...], NT, preferred_element_type=F32)

    return pl.pallas_call(body, name="branch_bwd_dy", grid=(NB, T // tm),
                          in_specs=[pl.BlockSpec((None, tm, D), lambda n, i: (n, i, 0)),
                                    pl.BlockSpec((None, W, D), lambda n, i: (n, 0, 0))],
                          out_specs=pl.BlockSpec((None, tm, W), lambda n, i: (n, i, 0)),
                          out_shape=_sds((NB, T, W), F32), compiler_params=_cp("parallel", "parallel"))(dyp, w_br)


def _branch_bwd_dw(ys, dyp):
    NB, T, W = ys.shape
    D = dyp.shape[2]
    tm, tn, tk = _pick(W, 1024), _pick(D, 1024), _pick(T, 1024, SUBLANES)
    return _mm("branch_bwd_dw", ys, dyp, grid=(NB, W // tm, D // tn, T // tk), kaxis=3, dims=TN,
               a_spec=pl.BlockSpec((None, tk, tm), lambda n, i, j, k: (n, k, i)),
               b_spec=pl.BlockSpec((None, tk, tn), lambda n, i, j, k: (n, k, j)),
               acc_shape=(tm, tn), out_shape=_sds((NB, W, D), F32),
               out_specs=pl.BlockSpec((None, tm, tn), lambda n, i, j, k: (n, i, j)),
               epilogue=_store(F32), sem=("parallel", "parallel", "parallel", "arbitrary"))


def _layer_fwd(x, wl, sp):
    W = sp["conv_b"].shape[-1]
    D = x.shape[1]
    h = _rmsnorm_fwd(x, sp["norm_g"])
    proj = _mm_nn("proj", h, wl["w_in"], F32, tn=768)

    def gate_ep(acc, ex, outs):
        outs[0][...] = _sigmoid(acc + ex[0][...])

    tn_g = _pick(4 * D, 1024)
    gates = _mm_nn("gates", h, wl["w_gate"], F32, epilogue=gate_ep, extras=(sp["b_gate"],),
                   extra_specs=(pl.BlockSpec((1, tn_g), lambda i, j, k: (0, j)),), tn=tn_g)
    oa, ya, sb_carries = _sb_fwd(proj, W)
    cpre, yb = _conv_fwd(proj, wl["conv_w"], wl["taps"], sp["conv_b"], sp["conv_ln_g"], sp["conv_ln_b"], W)
    yc = _sgu_fwd(proj, sp["sgu_w"], sp["sgu_bt"], sp["sgu_ln_g"], sp["sgu_ln_b"], W)
    os_, ls_ = zip(*[_dil_fwd_group(proj, W, gi, dil) for gi, (_, dil) in enumerate(DIL_PATTERNS)])
    od, lse, yd = _dil_combine(proj, os_, ls_, W)
    ys = jnp.stack([ya, yb, yc, yd])
    yproj, merged = _branch_merge(ys, wl["w_br"], gates)

    def res_ep(acc, ex, outs):
        outs[0][...] = ex[0][...] + acc

    tm_o, tn_o = _pick(x.shape[0], 1024, SUBLANES), _pick(D, 1024)
    xn = _mm_nn("out_proj", merged, wl["w_out"], F32, epilogue=res_ep, extras=(x,),
                extra_specs=(pl.BlockSpec((tm_o, tn_o), lambda i, j, k: (i, j)),), tm=tm_o, tn=tn_o)
    saved = dict(x=x, h=h, proj=proj, gates=gates, oa=oa, sb_carries=sb_carries, cpre=cpre, od=od, lse=lse, ys=ys, yproj=yproj, merged=merged)
    return xn, saved


def _layer_bwd(dout, sv, wl, sp):
    W = sp["conv_b"].shape[-1]
    proj = sv["proj"]
    dmerged = _mm_nt("out_proj_bwd_dx", dout, wl["w_out"], F32, tm=512)
    g_w_out = _mm_tn("out_proj_bwd_dw", sv["merged"], dout)
    dyp, dzg, db_gate = _merge_bwd(dmerged, sv["gates"], sv["yproj"])
    dy = _branch_bwd_dy(dyp, wl["w_br"])
    g_w_br = _branch_bwd_dw(sv["ys"], dyp)
    g_w_gate = _mm_tn("gate_bwd_dw", sv["h"], dzg)

    a_dq, a_dk, a_dv, a_dg = _sb_bwd(proj, sv["oa"], sv["sb_carries"], dy[0], W)
    a_qkv = _to_bf16("sb_bwd_cast", [a_dq, a_dk, a_dv])

    dc, b_dg, conv_stats = _conv_bwd_ln(proj, sv["cpre"], dy[1], sp["conv_ln_g"], sp["conv_ln_b"], W)
    b_da, b_db, g_conv_w = _conv_bwd_taps(proj, dc, wl["conv_w"], wl["taps"], W)

    c_du, c_dv, c_dg, g_sgu_w, g_sgu_bt, sgu_stats = _sgu_bwd(proj, dy[2], sp["sgu_w"], sp["sgu_bt"],
                                                              sp["sgu_ln_g"], sp["sgu_ln_b"], W)

    do, delta, d_dg = _dil_bwd_pre(proj, sv["od"], dy[3], W)
    dqs, dks, dvs = zip(*[_dil_bwd_group(proj, do, sv["lse"], delta, W, gi, dil)
                          for gi, (_, dil) in enumerate(DIL_PATTERNS)])
    d_qk = _to_bf16("dil_bwd_cast", [*dqs, *dks])
    d_dv = _sum3_bf16(*dvs)

    dproj = jnp.concatenate([a_qkv, a_dg, b_da, b_db, b_dg, c_du, c_dv, c_dg, d_qk, d_dv, d_dg], axis=1)
    g_w_in = _mm_tn("proj_bwd_dw", sv["h"], dproj, tn=768)
    dh = _mm_nt("gate_bwd_dh", dzg, wl["w_gate"], F32)
    dh = _mm_nt("proj_bwd_dh", dproj, wl["w_in"], F32, addend=dh)
    dx, dnorm = _rmsnorm_bwd(sv["x"], dh, dout, sp["norm_g"])

    K = wl["taps"]
    small = dict(norm_g=dnorm[0], conv_w=g_conv_w[:K], conv_b=conv_stats[2], conv_ln_g=conv_stats[0],
                 conv_ln_b=conv_stats[1], sgu_ln_g=sgu_stats[0], sgu_ln_b=sgu_stats[1], sgu_w=g_sgu_w,
                 sgu_b=g_sgu_bt.T, b_gate=db_gate[0])
    big = dict(w_in=g_w_in, w_gate=g_w_gate, w_br=g_w_br, w_out=g_w_out)
    return dx, big, small


HBM = pl.BlockSpec(memory_space=pl.ANY)


def _mesh_pos():
    return lax.axis_index("x"), lax.axis_index("y"), lax.axis_index("c")


def _other_chips(x, y):
    return [(1 - x, y), (x, 1 - y), (1 - x, 1 - y)]


def _shard_of(ref, axis, size, index):
    idx = [slice(None)] * len(ref.shape)
    idx[axis] = pl.ds(index * size, size)
    return ref.at[tuple(idx)]


def _all_gather(name, shards, axes):
    n = len(shards)
    out_shape = []
    for s, ax in zip(shards, axes):
        shp = list(s.shape)
        shp[ax] *= N_DEV
        out_shape.append(_sds(shp, s.dtype))

    def body(*refs):
        ins, outs = refs[:n], refs[n:2 * n]
        send, recv, lsem = refs[2 * n:]
        x, y, c = _mesh_pos()
        me, sib = (x, y, c), (x, y, 1 - c)
        chips = _other_chips(x, y)
        dev = lambda px, py, pc: 4 * px + 2 * py + pc

        def blk(a, d):
            return _shard_of(outs[a], axes[a], ins[a].shape[axes[a]], d)

        def cp(a, k, d, to, src=None):
            return pltpu.make_async_remote_copy(src_ref=blk(a, d) if src is None else src, dst_ref=blk(a, d),
                                                send_sem=send.at[a * 7 + k], recv_sem=recv.at[a * 7 + k],
                                                device_id=to, device_id_type=MESH)

        own = [pltpu.make_async_copy(ins[a], blk(a, dev(*me)), lsem.at[a]) for a in range(n)]
        for o in own:
            o.start()
        first = []
        for a in range(n):
            first.append(cp(a, 0, dev(*me), sib, src=ins[a]))
            first += [cp(a, 1 + j, dev(*me), (*ch, c), src=ins[a]) for j, ch in enumerate(chips)]
        for f in first:
            f.start()
        passed = []
        for j, ch in enumerate(chips):
            for a in range(n):
                cp(a, 1 + j, dev(*ch, c), me).wait_recv()
                p = cp(a, 4 + j, dev(*ch, c), sib)
                p.start()
                passed.append(p)
        for a in range(n):
            cp(a, 0, dev(*sib), me).wait_recv()
            for j, ch in enumerate(chips):
                cp(a, 4 + j, dev(*ch, 1 - c), me).wait_recv()
        for f in first + passed:
            f.wait_send()
        for o in own:
            o.wait()

    return pl.pallas_call(body, name=name, in_specs=[HBM] * n, out_specs=[HBM] * n, out_shape=out_shape,
                          scratch_shapes=[pltpu.SemaphoreType.DMA((7 * n,)), pltpu.SemaphoreType.DMA((7 * n,)),
                                          pltpu.SemaphoreType.DMA((n,))])(*shards)


def _rs_pair(name, grads, axes):
    n = len(grads)
    sizes = [g.shape[ax] // N_DEV for g, ax in zip(grads, axes)]
    out_shape = []
    for g, ax, sz in zip(grads, axes, sizes):
        shp = list(g.shape)
        shp[ax] = sz
        out_shape.append(_sds([N_CHIP] + shp, g.dtype))

    def body(*refs):
        ins, outs = refs[:n], refs[n:2 * n]
        send, recv = refs[2 * n:]
        x, y, c = _mesh_pos()
        cps = []
        for a in range(n):
            for k in range(N_CHIP):
                cps.append(pltpu.make_async_remote_copy(
                    src_ref=_shard_of(ins[a], axes[a], sizes[a], 2 * k + (1 - c)), dst_ref=outs[a].at[k],
                    send_sem=send.at[a * N_CHIP + k], recv_sem=recv.at[a * N_CHIP + k],
                    device_id=(x, y, 1 - c), device_id_type=MESH))
        for cp in cps:
            cp.start()
        for cp in cps:
            cp.wait()

    return pl.pallas_call(body, name=name, in_specs=[HBM] * n, out_specs=[HBM] * n, out_shape=out_shape,
                          scratch_shapes=[pltpu.SemaphoreType.DMA((N_CHIP * n,)),
                                          pltpu.SemaphoreType.DMA((N_CHIP * n,))])(*grads)


def _pair_sum(name, grad, recv, axis, core):
    R, C = grad.shape
    _, r, cw = recv.shape
    tr = _pick(r, 512, SUBLANES)
    nr = r // tr

    def body(c_ref, g_ref, r_ref, o_ref):
        o_ref[...] = g_ref[...] + r_ref[...]

    if axis == 1:
        gspec = pl.BlockSpec((tr, cw), lambda k, i, c_ref: (i, 2 * k + c_ref[0]))
    else:
        gspec = pl.BlockSpec((tr, cw), lambda k, i, c_ref: ((2 * k + c_ref[0]) * nr + i, 0))
    part = pl.BlockSpec((None, tr, cw), lambda k, i, c_ref: (k, i, 0))
    return pl.pallas_call(
        body, name=name, out_shape=_sds((N_CHIP, r, cw), F32),
        grid_spec=pltpu.PrefetchScalarGridSpec(num_scalar_prefetch=1, grid=(N_CHIP, nr), in_specs=[gspec, part],
                                               out_specs=part),
        compiler_params=_cp("parallel", "parallel"))(core, grad, recv)


def _rs_chips(name, parts):
    n = len(parts)

    def body(*refs):
        ins, outs = refs[:n], refs[n:2 * n]
        send, recv, lsem = refs[2 * n:]
        x, y, c = _mesh_pos()
        mine = 2 * x + y
        own = [pltpu.make_async_copy(ins[a].at[mine], outs[a].at[mine], lsem.at[a]) for a in range(n)]
        for o in own:
            o.start()
        cps = []
        for a in range(n):
            for j, (px, py) in enumerate(_other_chips(x, y)):
                cps.append(pltpu.make_async_remote_copy(
                    src_ref=ins[a].at[2 * px + py], dst_ref=outs[a].at[mine],
                    send_sem=send.at[a * 3 + j], recv_sem=recv.at[a * 3 + j],
                    device_id=(px, py, c), device_id_type=MESH))
        for cp in cps:
            cp.start()
        for a in range(n):
            for j, (px, py) in enumerate(_other_chips(x, y)):
                pltpu.make_async_remote_copy(
                    src_ref=ins[a].at[mine], dst_ref=outs[a].at[2 * px + py],
                    send_sem=send.at[a * 3 + j], recv_sem=recv.at[a * 3 + j],
                    device_id=(x, y, c), device_id_type=MESH).wait_recv()
        for cp in cps:
            cp.wait_send()
        for o in own:
            o.wait()

    return pl.pallas_call(body, name=name, in_specs=[HBM] * n, out_specs=[HBM] * n,
                          out_shape=[_sds(p.shape, p.dtype) for p in parts],
                          scratch_shapes=[pltpu.SemaphoreType.DMA((3 * n,)), pltpu.SemaphoreType.DMA((3 * n,)),
                                          pltpu.SemaphoreType.DMA((n,))])(*parts)


def _adamw_sum(name, parts, w, m, v):
    P, R, C = parts.shape
    tr = _pick(R, max(SUBLANES, (1 << 19) // C // SUBLANES * SUBLANES), SUBLANES)

    def body(p_ref, w_ref, m_ref, v_ref, g_ref, d_ref, mo_ref, vo_ref):
        g = p_ref[0]
        for k in range(1, P):
            g = g + p_ref[k]
        mn = ADAM_B1 * m_ref[...] + (1.0 - ADAM_B1) * g
        vn = ADAM_B2 * v_ref[...] + (1.0 - ADAM_B2) * (g * g)
        m_hat = mn / (1.0 - ADAM_B1 ** ADAM_STEP)
        v_hat = vn / (1.0 - ADAM_B2 ** ADAM_STEP)
        g_ref[...] = g
        d_ref[...] = -ADAM_LR * (m_hat / (jnp.sqrt(v_hat) + ADAM_EPS) + ADAM_WD * w_ref[...])
        mo_ref[...] = mn
        vo_ref[...] = vn

    row = pl.BlockSpec((tr, C), lambda i: (i, 0))
    return pl.pallas_call(body, name=name, grid=(R // tr,),
                          in_specs=[pl.BlockSpec((P, tr, C), lambda i: (0, i, 0)), row, row, row],
                          out_specs=[row] * 4, out_shape=[_sds((R, C), F32)] * 4,
                          compiler_params=_cp("parallel"))(parts, w, m, v)


def _rows128(a, pad_rows=SUBLANES):
    flat = a.reshape(-1, LANES)
    pad = (-flat.shape[0]) % pad_rows
    return jnp.pad(flat, ((0, pad), (0, 0))) if pad else flat


SMALL = ("norm_g", "conv_b", "conv_ln_g", "conv_ln_b", "sgu_ln_g", "sgu_ln_b", "sgu_w", "sgu_b", "b_gate", "final_g")


def kernel(x, norm_g, w_in, conv_w, conv_b, conv_ln_g, conv_ln_b, sgu_ln_g, sgu_ln_b, sgu_w, sgu_b, w_branch, w_gate, b_gate, w_out, final_g, loss_target, m_norm_g, m_w_in, m_conv_w, m_conv_b, m_conv_ln_g, m_conv_ln_b, m_sgu_ln_g, m_sgu_ln_b, m_sgu_w, m_sgu_b, m_w_branch, m_w_gate, m_b_gate, m_w_out, m_final_g, v_norm_g, v_w_in, v_conv_w, v_conv_b, v_conv_ln_g, v_conv_ln_b, v_sgu_ln_g, v_sgu_ln_b, v_sgu_w, v_sgu_b, v_w_branch, v_w_gate, v_b_gate, v_w_out, v_final_g):
    L, D = norm_g.shape
    W = conv_b.shape[1]
    taps = conv_w.shape[1]
    weights = dict(norm_g=norm_g, w_in=w_in, conv_w=conv_w, conv_b=conv_b, conv_ln_g=conv_ln_g, conv_ln_b=conv_ln_b,
                   sgu_ln_g=sgu_ln_g, sgu_ln_b=sgu_ln_b, sgu_w=sgu_w, sgu_b=sgu_b, w_branch=w_branch, w_gate=w_gate,
                   b_gate=b_gate, w_out=w_out, final_g=final_g)
    mom_m = dict(norm_g=m_norm_g, w_in=m_w_in, conv_w=m_conv_w, conv_b=m_conv_b, conv_ln_g=m_conv_ln_g,
                 conv_ln_b=m_conv_ln_b, sgu_ln_g=m_sgu_ln_g, sgu_ln_b=m_sgu_ln_b, sgu_w=m_sgu_w, sgu_b=m_sgu_b,
                 w_branch=m_w_branch, w_gate=m_w_gate, b_gate=m_b_gate, w_out=m_w_out, final_g=m_final_g)
    mom_v = dict(norm_g=v_norm_g, w_in=v_w_in, conv_w=v_conv_w, conv_b=v_conv_b, conv_ln_g=v_conv_ln_g,
                 conv_ln_b=v_conv_ln_b, sgu_ln_g=v_sgu_ln_g, sgu_ln_b=v_sgu_ln_b, sgu_w=v_sgu_w, sgu_b=v_sgu_b,
                 w_branch=v_w_branch, w_gate=v_w_gate, b_gate=v_b_gate, w_out=v_w_out, final_g=v_final_g)
    core = lax.axis_index("c").astype(jnp.int32).reshape(1)
    me = 4 * lax.axis_index("x") + 2 * lax.axis_index("y") + lax.axis_index("c")

    xs = x[0]
    saved, gathered, smalls = [], [], []
    for l in range(L):
        w_in_f, w_gate_f, w_br_f, w_out_f, conv_w_f = _all_gather(
            "gather_weights",
            [w_in[l].astype(BF16), w_gate[l].astype(BF16), w_branch[l].astype(BF16), w_out[l].astype(BF16),
             jnp.pad(conv_w[l], ((0, CONV_HALO - taps), (0, 0)))],
            [1, 1, 2, 0, 1])
        wl = dict(w_in=w_in_f, w_gate=w_gate_f, w_br=w_br_f, w_out=w_out_f, conv_w=conv_w_f, taps=taps)
        sp = dict(norm_g=norm_g[l][None], conv_b=conv_b[l][None], conv_ln_g=conv_ln_g[l][None],
                  conv_ln_b=conv_ln_b[l][None], sgu_ln_g=sgu_ln_g[l][None], sgu_ln_b=sgu_ln_b[l][None],
                  sgu_w=sgu_w[l], sgu_bt=sgu_b[l].T, b_gate=b_gate[l][None])
        xs, sv = _layer_fwd(xs, wl, sp)
        saved.append(sv)
        gathered.append(wl)
        smalls.append(sp)

    dx, d_final, loss_part = _loss_head(xs, loss_target[0], final_g[None])
    loss = lax.psum(loss_part[0, 0], ("x", "y", "c"))

    big_names = ("w_in", "w_gate", "w_branch", "w_out")
    big_axes = (1, 1, 1, 0)
    outs = {}
    small_grads = []
    for l in reversed(range(L)):
        dx, big, small = _layer_bwd(dx, saved[l], gathered[l], smalls[l])
        small_grads.append(small)
        g2d = [big["w_in"], big["w_gate"], big["w_br"].reshape(-1, D), big["w_out"]]
        recv = _rs_pair("reduce_pair", g2d, big_axes)
        parts = [_pair_sum("pair_sum_" + nm, g, r, ax, core) for nm, g, r, ax in zip(big_names, g2d, recv, big_axes)]
        total = _rs_chips("reduce_chips", parts)
        for nm, tot in zip(big_names, total):
            shard2d = tot.shape[1:]
            res = _adamw_sum("adamw_" + nm, tot, weights[nm][l].reshape(shard2d), mom_m[nm][l].reshape(shard2d),
                             mom_v[nm][l].reshape(shard2d))
            outs.setdefault(nm, []).append(res)
    small_grads.reverse()
    big_out = {nm: [jnp.stack([outs[nm][L - 1 - l][q] for l in range(L)]).reshape(weights[nm].shape) for q in range(4)]
               for nm in big_names}

    sg = {nm: jnp.stack([small_grads[l][nm] for l in range(L)]) for nm in SMALL[:-1]}
    sg["final_g"] = d_final[0]
    conv_w_full = jnp.stack([small_grads[l]["conv_w"] for l in range(L)])
    segs = [_rows128(sg[nm]) for nm in SMALL] + [_rows128(conv_w_full)]
    offs = [0]
    for s in segs:
        offs.append(offs[-1] + s.shape[0])
    pack = jnp.concatenate(segs, axis=0)
    (allp,) = _all_gather("gather_small_grads", [pack[None]], [0])

    def packed(src):
        return jnp.concatenate([_rows128(src[nm]) for nm in SMALL] + [jnp.zeros_like(segs[-1])], axis=0)

    s_g, s_d, s_m, s_v = _adamw_sum("adamw_small", allp, packed(weights), packed(mom_m), packed(mom_v))

    def unpack(buf, i, like):
        n = like.size // LANES
        return buf[offs[i]:offs[i] + n].reshape(like.shape)

    small_out = {nm: [unpack(b, i, weights[nm]) for b in (s_g, s_d, s_m, s_v)] for i, nm in enumerate(SMALL)}
    Wc = conv_w.shape[2]
    cw_sum = lax.dynamic_slice_in_dim(unpack(s_g, len(SMALL), conv_w_full), me * Wc, Wc, axis=2)
    cshape = (L * conv_w.shape[1], Wc)
    conv_out = [o.reshape(conv_w.shape) for o in _adamw_sum(
        "adamw_conv_w", cw_sum.reshape((1,) + cshape), conv_w.reshape(cshape), m_conv_w.reshape(cshape),
        v_conv_w.reshape(cshape))]

    order = ["norm_g", "w_in", "conv_w", "conv_b", "conv_ln_g", "conv_ln_b", "sgu_ln_g", "sgu_ln_b", "sgu_w", "sgu_b",
             "w_branch", "w_gate", "b_gate", "w_out", "final_g"]
    table = dict(small_out)
    table.update(big_out)
    table["conv_w"] = conv_out
    result = [loss, dx[None]]
    for q in range(4):
        result += [table[nm][q] for nm in order]
    return tuple(result)
```

```python
import functools

import jax
import jax.numpy as jnp
from jax import lax
from jax.experimental import pallas as pl
from jax.experimental.pallas import tpu as pltpu

F32 = jnp.float32
BF16 = jnp.bfloat16
LANES = 128
SUBLANES = 8
CONV_HALO = 32
SB_KEY_BLOCK = 512
NORM_EPS = 1e-6
NEG = -1e30
N_DEV = 8
N_CHIP = 4
DIL_PATTERNS = ((128, 1), (512, 4), (2048, 16))

ADAM_LR = 0.001
ADAM_B1 = 0.9
ADAM_B2 = 0.999
ADAM_EPS = 1e-08
ADAM_WD = 0.01
ADAM_STEP = 10

MESH = pl.DeviceIdType.MESH
NN = (((1,), (0,)), ((), ()))
NT = (((1,), (1,)), ((), ()))
TN = (((0,), (0,)), ((), ()))
VMEM_LIMIT = 52 << 20


def _sds(shape, dtype):
    return jax.ShapeDtypeStruct(tuple(shape), dtype)


def _cp(*sem):
    return pltpu.CompilerParams(dimension_semantics=tuple(sem), vmem_limit_bytes=VMEM_LIMIT)


def _pick(n, target, quantum=LANES):
    if n <= target:
        return n
    t = (target // quantum) * quantum
    while t >= quantum:
        if n % t == 0:
            return t
        t -= quantum
    return n


def _sigmoid(x):
    return 1.0 / (1.0 + jnp.exp(-x))


def _silu(x):
    return x * _sigmoid(x)


def _dsilu(x):
    s = _sigmoid(x)
    return s * (1.0 + x * (1.0 - s))


_GELU_K = 0.7978845608028654
_GELU_A = 0.044715


def _gelu(x):
    return 0.5 * x * (1.0 + jnp.tanh(_GELU_K * (x + _GELU_A * x * x * x)))


def _dgelu(x):
    t = jnp.tanh(_GELU_K * (x + _GELU_A * x * x * x))
    return 0.5 * (1.0 + t) + 0.5 * x * (1.0 - t * t) * _GELU_K * (1.0 + 3.0 * _GELU_A * x * x)


def _ln_stats(v):
    mu = jnp.mean(v, axis=-1, keepdims=True)
    d = v - mu
    var = jnp.mean(d * d, axis=-1, keepdims=True)
    r = lax.rsqrt(var + NORM_EPS)
    return d * r, r


def _ln_bwd(dxh, xh, r):
    return r * (dxh - jnp.mean(dxh, axis=-1, keepdims=True) - xh * jnp.mean(dxh * xh, axis=-1, keepdims=True))


def _acc_rows(ref, row, val, first):
    @pl.when(first)
    def _():
        ref[...] = jnp.zeros_like(ref)
    ref[row:row + 1, :] += val


def _mm(name, a, b, *, grid, kaxis, dims, a_spec, b_spec, acc_shape, out_shape, out_specs,
        epilogue, extras=(), extra_specs=(), sem):
    nk = grid[kaxis]
    ne = len(extras)

    def body(*refs):
        a_ref, b_ref = refs[0], refs[1]
        ex = refs[2:2 + ne]
        outs = refs[2 + ne:-1]
        acc = refs[-1]
        k = pl.program_id(kaxis)

        @pl.when(k == 0)
        def _():
            acc[...] = jnp.zeros_like(acc)

        acc[...] += lax.dot_general(a_ref[...].astype(BF16), b_ref[...].astype(BF16), dims,
                                    preferred_element_type=F32)

        @pl.when(k == nk - 1)
        def _():
            epilogue(acc[...], ex, outs)

    return pl.pallas_call(
        body, name=name, grid=grid, in_specs=[a_spec, b_spec, *extra_specs], out_specs=out_specs,
        out_shape=out_shape, scratch_shapes=[pltpu.VMEM(acc_shape, F32)], compiler_params=_cp(*sem),
    )(a, b, *extras)


def _store(dtype):
    def ep(acc, ex, outs):
        outs[0][...] = acc.astype(dtype)
    return ep


def _mm_nn(name, a, b, out_dtype, epilogue=None, extras=(), extra_specs=(), tm=1024, tn=1024):
    M, K = a.shape
    N = b.shape[1]
    tm, tn = _pick(M, tm, SUBLANES), _pick(N, tn)
    return _mm(name, a, b, grid=(M // tm, N // tn, 1), kaxis=2, dims=NN,
               a_spec=pl.BlockSpec((tm, K), lambda i, j, k: (i, 0)),
               b_spec=pl.BlockSpec((K, tn), lambda i, j, k: (0, j)),
               acc_shape=(tm, tn), out_shape=_sds((M, N), out_dtype),
               out_specs=pl.BlockSpec((tm, tn), lambda i, j, k: (i, j)),
               epilogue=epilogue or _store(out_dtype), extras=extras, extra_specs=extra_specs,
               sem=("parallel", "parallel", "arbitrary"))


def _mm_nt(name, a, b, out_dtype, addend=None, tm=1024, tn=1024, tk=1024):
    M, K = a.shape
    N = b.shape[0]
    tm, tn, tk = _pick(M, tm, SUBLANES), _pick(N, tn), _pick(K, tk)
    extras, extra_specs = (), ()
    if addend is not None:
        extras = (addend,)
        extra_specs = (pl.BlockSpec((tm, tn), lambda i, j, k: (i, j)),)

    def ep(acc, ex, outs):
        if ex:
            acc = acc + ex[0][...]
        outs[0][...] = acc.astype(out_dtype)

    return _mm(name, a, b, grid=(M // tm, N // tn, K // tk), kaxis=2, dims=NT,
               a_spec=pl.BlockSpec((tm, tk), lambda i, j, k: (i, k)),
               b_spec=pl.BlockSpec((tn, tk), lambda i, j, k: (j, k)),
               acc_shape=(tm, tn), out_shape=_sds((M, N), out_dtype),
               out_specs=pl.BlockSpec((tm, tn), lambda i, j, k: (i, j)),
               epilogue=ep, extras=extras, extra_specs=extra_specs,
               sem=("parallel", "parallel", "arbitrary"))


def _mm_tn(name, a, b, tm=1024, tn=1024, tk=1024):
    K, M = a.shape
    N = b.shape[1]
    tm, tn, tk = _pick(M, tm), _pick(N, tn), _pick(K, tk, SUBLANES)
    return _mm(name, a, b, grid=(M // tm, N // tn, K // tk), kaxis=2, dims=TN,
               a_spec=pl.BlockSpec((tk, tm), lambda i, j, k: (k, i)),
               b_spec=pl.BlockSpec((tk, tn), lambda i, j, k: (k, j)),
               acc_shape=(tm, tn), out_shape=_sds((M, N), F32),
               out_specs=pl.BlockSpec((tm, tn), lambda i, j, k: (i, j)),
               epilogue=_store(F32), sem=("parallel", "parallel", "arbitrary"))


def _rmsnorm_fwd(x, g_row):
    T, D = x.shape
    tm = _pick(T, 512, SUBLANES)

    def body(x_ref, g_ref, h_ref):
        xv = x_ref[...]
        r = lax.rsqrt(jnp.mean(xv * xv, axis=-1, keepdims=True) + NORM_EPS)
        h_ref[...] = (xv * r * g_ref[...]).astype(BF16)

    row = pl.BlockSpec((tm, D), lambda i: (i, 0))
    return pl.pallas_call(body, name="rmsnorm_fwd", grid=(T // tm,),
                          in_specs=[row, pl.BlockSpec((1, D), lambda i: (0, 0))], out_specs=row,
                          out_shape=_sds((T, D), BF16), compiler_params=_cp("parallel"))(x, g_row)


def _rmsnorm_bwd(x, dh, dout, g_row):
    T, D = x.shape
    tm = _pick(T, 256, SUBLANES)

    def body(x_ref, dh_ref, do_ref, g_ref, dx_ref, dg_ref):
        xv = x_ref[...]
        r = lax.rsqrt(jnp.mean(xv * xv, axis=-1, keepdims=True) + NORM_EPS)
        xh = xv * r
        dhv = dh_ref[...]
        dxh = dhv * g_ref[...]
        dx_ref[...] = do_ref[...] + r * (dxh - xh * jnp.mean(dxh * xh, axis=-1, keepdims=True))
        _acc_rows(dg_ref, 0, jnp.sum(dhv * xh, axis=0, keepdims=True), pl.program_id(0) == 0)

    row = pl.BlockSpec((tm, D), lambda i: (i, 0))
    return pl.pallas_call(body, name="rmsnorm_bwd", grid=(T // tm,),
                          in_specs=[row, row, row, pl.BlockSpec((1, D), lambda i: (0, 0))],
                          out_specs=[row, pl.BlockSpec((SUBLANES, D), lambda i: (0, 0))],
                          out_shape=[_sds((T, D), F32), _sds((SUBLANES, D), F32)],
                          compiler_params=_cp("arbitrary"))(x, dh, dout, g_row)


def _loss_head(x, target, g_row):
    T, D = x.shape
    tm = _pick(T, 256, SUBLANES)

    def body(x_ref, t_ref, g_ref, dx_ref, dg_ref, loss_ref):
        first = pl.program_id(0) == 0
        xv = x_ref[...]
        g = g_ref[...]
        r = lax.rsqrt(jnp.mean(xv * xv, axis=-1, keepdims=True) + NORM_EPS)
        xh = xv * r
        err = xh * g - t_ref[...]
        part = 0.5 * jnp.sum(jnp.mean(err * err, axis=-1, keepdims=True), axis=0, keepdims=True)

        @pl.when(first)
        def _():
            loss_ref[...] = jnp.zeros_like(loss_ref)

        loss_ref[...] += jnp.broadcast_to(part, loss_ref.shape)
        dy = err / D
        _acc_rows(dg_ref, 0, jnp.sum(dy * xh, axis=0, keepdims=True), first)
        dxh = dy * g
        dx_ref[...] = r * (dxh - xh * jnp.mean(dxh * xh, axis=-1, keepdims=True))

    row = pl.BlockSpec((tm, D), lambda i: (i, 0))
    return pl.pallas_call(body, name="loss_head", grid=(T // tm,),
                          in_specs=[row, row, pl.BlockSpec((1, D), lambda i: (0, 0))],
                          out_specs=[row, pl.BlockSpec((SUBLANES, D), lambda i: (0, 0)),
                                     pl.BlockSpec((SUBLANES, LANES), lambda i: (0, 0))],
                          out_shape=[_sds((T, D), F32), _sds((SUBLANES, D), F32), _sds((SUBLANES, LANES), F32)],
                          compiler_params=_cp("arbitrary"))(x, target, g_row)


def _tri(cmp):
    r = lax.broadcasted_iota(jnp.int32, (LANES, LANES), 0)
    c = lax.broadcasted_iota(jnp.int32, (LANES, LANES), 1)
    return cmp(r, c).astype(BF16)


def _scan_mm(x, tri):
    hi = x.astype(BF16)
    lo = (x - hi.astype(F32)).astype(BF16)
    return (jnp.dot(hi, tri, preferred_element_type=F32) + jnp.dot(lo, tri, preferred_element_type=F32))


def _sb_scores(qs, kb, causal):
    z = lax.dot_general(qs, kb, NT, preferred_element_type=F32)
    lb = jnp.minimum(z, 0.0) - jnp.log(1.0 + jnp.exp(-jnp.abs(z)))
    l1 = lb - z
    return lb, (l1 if causal is None else jnp.where(causal, l1, 0.0))


def _masked(causal, x):
    return x if causal is None else jnp.where(causal, x, 0.0)


def _lanes(x, s):
    return x[:, s * LANES:(s + 1) * LANES]


def _sb_fwd(proj, W):
    T = proj.shape[0]
    H = W // LANES
    assert T // LANES <= LANES
    tq = _pick(T, 256, LANES)
    kblk = _pick(T, SB_KEY_BLOCK, LANES)
    assert kblk % tq == 0
    nsub = kblk // LANES
    scale = LANES ** -0.5

    def body(q_ref, k_ref, v_ref, g_ref, o_ref, y_ref, c_ref, run):
        i = pl.program_id(1)
        qs = (q_ref[...] * scale).astype(BF16)
        row = i * tq + lax.broadcasted_iota(jnp.int32, (tq, kblk), 0)
        key = lax.broadcasted_iota(jnp.int32, (tq, kblk), 1)
        col = lax.broadcasted_iota(jnp.int32, (tq, LANES), 1)
        tri = _tri(lambda r, c: r > c)
        nkb = ((i + 1) * tq + kblk - 1) // kblk
        o_ref[...] = jnp.zeros_like(o_ref)
        c_ref[...] = jnp.zeros_like(c_ref)
        run[...] = jnp.zeros_like(run)

        def make_step(diagonal):
            def step(jj, carry):
                j = nkb - 1 - jj
                off = pl.multiple_of(j * kblk, kblk)
                kb = k_ref[pl.ds(off, kblk), :].astype(BF16)
                vb = v_ref[pl.ds(off, kblk), :].astype(BF16)
                causal = (key + off < row) if diagonal else None
                lb, l1m = _sb_scores(qs, kb, causal)
                c_after = run[...]
                cs = c_ref[...]
                after = [None] * nsub
                for s in reversed(range(nsub)):
                    part = _lanes(l1m, s)
                    after[s] = c_after + _scan_mm(part, tri)
                    cs = jnp.where(col == j * nsub + s, c_after, cs)
                    c_after = c_after + jnp.sum(part, axis=1, keepdims=True)
                run[...] = c_after
                c_ref[...] = cs
                w = _masked(causal, jnp.exp(lb + jnp.concatenate(after, axis=1)))
                o_ref[...] += jnp.dot(w.astype(BF16), vb, preferred_element_type=F32)
                return carry
            return step

        make_step(True)(0, 0)
        lax.fori_loop(1, nkb, make_step(False), 0)
        y_ref[...] = (o_ref[...] * _silu(g_ref[...])).astype(BF16)

    qspec = lambda c0: pl.BlockSpec((tq, LANES), lambda h, i: (i, c0 + h))
    kvspec = lambda c0: pl.BlockSpec((T, LANES), lambda h, i: (0, c0 + h))
    out = pl.BlockSpec((tq, LANES), lambda h, i: (i, h))
    return pl.pallas_call(body, name="sb_fwd", grid=(H, T // tq),
                          in_specs=[qspec(0), kvspec(H), kvspec(2 * H), qspec(3 * H)], out_specs=[out, out, out],
                          out_shape=[_sds((T, W), F32), _sds((T, W), BF16), _sds((T, W), F32)],
                          scratch_shapes=[pltpu.VMEM((tq, LANES), F32)],
                          compiler_params=_cp("parallel", "arbitrary"))(proj, proj, proj, proj)


def _sb_bwd(proj, o, carries, dy, W):
    T = proj.shape[0]
    H = W // LANES
    tq = _pick(T, 256, LANES)
    kblk = _pick(T, SB_KEY_BLOCK, LANES)
    assert kblk % tq == 0
    nsub = kblk // LANES
    scale = LANES ** -0.5

    def body(q_ref, k_ref, v_ref, g_ref, o_ref, c_ref, dy_ref, dq_ref, dk_ref, dv_ref, dg_ref, run):
        i = pl.program_id(1)

        @pl.when(i == 0)
        def _():
            dk_ref[...] = jnp.zeros_like(dk_ref)
            dv_ref[...] = jnp.zeros_like(dv_ref)

        gv = g_ref[...]
        dyv = dy_ref[...]
        dg_ref[...] = (dyv * o_ref[...] * _dsilu(gv)).astype(BF16)
        dob = (dyv * _silu(gv)).astype(BF16)
        qs = (q_ref[...] * scale).astype(BF16)
        row = i * tq + lax.broadcasted_iota(jnp.int32, (tq, kblk), 0)
        key = lax.broadcasted_iota(jnp.int32, (tq, kblk), 1)
        col = lax.broadcasted_iota(jnp.int32, (tq, LANES), 1)
        tri_after = _tri(lambda r, c: r > c)
        tri_before = _tri(lambda r, c: r < c)
        dq_ref[...] = jnp.zeros_like(dq_ref)
        run[...] = jnp.zeros_like(run)

        def make_step(diagonal):
            def step(j, carry):
                off = pl.multiple_of(j * kblk, kblk)
                kb = k_ref[pl.ds(off, kblk), :].astype(BF16)
                vb = v_ref[pl.ds(off, kblk), :].astype(BF16)
                causal = (key + off < row) if diagonal else None
                lb, l1m = _sb_scores(qs, kb, causal)
                cs = c_ref[...]
                after = [jnp.sum(jnp.where(col == j * nsub + s, cs, 0.0), axis=1, keepdims=True)
                         + _scan_mm(_lanes(l1m, s), tri_after) for s in range(nsub)]
                w = _masked(causal, jnp.exp(lb + jnp.concatenate(after, axis=1)))
                gw = w * lax.dot_general(dob, vb, NT, preferred_element_type=F32)
                gpre = run[...]
                before = [None] * nsub
                for s in range(nsub):
                    part = _lanes(gw, s)
                    before[s] = gpre + _scan_mm(part, tri_before)
                    gpre = gpre + jnp.sum(part, axis=1, keepdims=True)
                run[...] = gpre
                beta = jnp.exp(lb)
                dz = _masked(causal, gw * (1.0 - beta) - jnp.concatenate(before, axis=1) * beta).astype(BF16)
                dq_ref[...] += jnp.dot(dz, kb, preferred_element_type=F32)
                dk_ref[pl.ds(off, kblk), :] += lax.dot_general(dz, qs, TN, preferred_element_type=F32)
                dv_ref[pl.ds(off, kblk), :] += lax.dot_general(w.astype(BF16), dob, TN, preferred_element_type=F32)
                return carry
            return step

        last = ((i + 1) * tq + kblk - 1) // kblk - 1
        lax.fori_loop(0, last, make_step(False), 0)
        make_step(True)(last, 0)
        dq_ref[...] = dq_ref[...] * scale

    qspec = lambda c0: pl.BlockSpec((tq, LANES), lambda h, i: (i, c0 + h))
    kvspec = lambda c0: pl.BlockSpec((T, LANES), lambda h, i: (0, c0 + h))
    blk = pl.BlockSpec((tq, LANES), lambda h, i: (i, h))
    full = pl.BlockSpec((T, LANES), lambda h, i: (0, h))
    return pl.pallas_call(body, name="sb_bwd", grid=(H, T // tq),
                          in_specs=[qspec(0), kvspec(H), kvspec(2 * H), qspec(3 * H), blk, blk, blk],
                          out_specs=[blk, full, full, blk],
                          out_shape=[_sds((T, W), F32), _sds((T, W), F32), _sds((T, W), F32), _sds((T, W), BF16)],
                          scratch_shapes=[pltpu.VMEM((tq, LANES), F32)],
                          compiler_params=_cp("parallel", "arbitrary"))(proj, proj, proj, proj, o, carries, dy)


def _conv_specs(T, W, tm, col_a):
    per = tm // CONV_HALO
    cur = lambda c: pl.BlockSpec((tm, W), lambda i: (i, c))
    prev = lambda c: pl.BlockSpec((CONV_HALO, W), lambda i: (jnp.maximum(i * per - 1, 0), c))
    return cur, prev


def _conv_fwd(proj, conv_w, K, conv_b, ln_g, ln_b, W):
    T = proj.shape[0]
    tm = _pick(T, 256, CONV_HALO)
    lead = CONV_HALO - (K - 1)
    cur, prev = _conv_specs(T, W, tm, 4)

    def body(a_ref, b_ref, ah_ref, bh_ref, g_ref, w_ref, cb_ref, lg_ref, lb_ref, c_ref, y_ref, glu):
        i = pl.program_id(0)
        glu[0:CONV_HALO, :] = jnp.where(i > 0, ah_ref[...] * _sigmoid(bh_ref[...]), 0.0)
        glu[CONV_HALO:, :] = a_ref[...] * _sigmoid(b_ref[...])
        c = jnp.broadcast_to(cb_ref[...], (tm, W))
        for k in range(K):
            c = c + w_ref[k:k + 1, :] * glu[lead + k:lead + k + tm, :]
        c_ref[...] = c
        xh, _ = _ln_stats(c)
        y_ref[...] = (_silu(xh * lg_ref[...] + lb_ref[...]) * _silu(g_ref[...])).astype(BF16)

    vec = pl.BlockSpec((1, W), lambda i: (0, 0))
    row = pl.BlockSpec((tm, W), lambda i: (i, 0))
    return pl.pallas_call(body, name="conv_fwd", grid=(T // tm,),
                          in_specs=[cur(4), cur(5), prev(4), prev(5), cur(6),
                                    pl.BlockSpec((CONV_HALO, W), lambda i: (0, 0)), vec, vec, vec],
                          out_specs=[row, row], out_shape=[_sds((T, W), F32), _sds((T, W), BF16)],
                          scratch_shapes=[pltpu.VMEM((tm + CONV_HALO, W), F32)],
                          compiler_params=_cp("parallel"))(proj, proj, proj, proj, proj, conv_w, conv_b, ln_g, ln_b)


def _conv_bwd_ln(proj, c, dy, ln_g, ln_b, W):
    T = proj.shape[0]
    tm = _pick(T, 256, SUBLANES)

    def body(g_ref, c_ref, dy_ref, lg_ref, lb_ref, dc_ref, dg_ref, st_ref):
        first = pl.program_id(0) == 0
        gv = g_ref[...]
        dyv = dy_ref[...]
        xh, r = _ln_stats(c_ref[...])
        lg = lg_ref[...]
        ln = xh * lg + lb_ref[...]
        dg_ref[...] = (dyv * _silu(ln) * _dsilu(gv)).astype(BF16)
        dln = dyv * _silu(gv) * _dsilu(ln)
        dc = _ln_bwd(dln * lg, xh, r)
        dc_ref[...] = dc
        _acc_rows(st_ref, 0, jnp.sum(dln * xh, axis=0, keepdims=True), first)
        st_ref[1:2, :] += jnp.sum(dln, axis=0, keepdims=True)
        st_ref[2:3, :] += jnp.sum(dc, axis=0, keepdims=True)

    row = pl.BlockSpec((tm, W), lambda i: (i, 0))
    vec = pl.BlockSpec((1, W), lambda i: (0, 0))
    return pl.pallas_call(body, name="conv_bwd_ln", grid=(T // tm,),
                          in_specs=[pl.BlockSpec((tm, W), lambda i: (i, 6)), row, row, vec, vec],
                          out_specs=[row, row, pl.BlockSpec((SUBLANES, W), lambda i: (0, 0))],
                          out_shape=[_sds((T, W), F32), _sds((T, W), BF16), _sds((SUBLANES, W), F32)],
                          compiler_params=_cp("arbitrary"))(proj, c, dy, ln_g, ln_b)


def _conv_bwd_taps(proj, dc, conv_w, K, W):
    T = proj.shape[0]
    tm = _pick(T, 256, CONV_HALO)
    lead = CONV_HALO - (K - 1)
    per = tm // CONV_HALO
    nblk = T // tm
    cur, prev = _conv_specs(T, W, tm, 4)

    def body(a_ref, b_ref, ah_ref, bh_ref, dc_ref, dcn_ref, w_ref, da_ref, db_ref, dw_ref, glu, dcs):
        i = pl.program_id(0)
        av = a_ref[...]
        sb = _sigmoid(b_ref[...])
        glu[0:CONV_HALO, :] = jnp.where(i > 0, ah_ref[...] * _sigmoid(bh_ref[...]), 0.0)
        glu[CONV_HALO:, :] = av * sb
        dcv = dc_ref[...]
        dcs[0:tm, :] = dcv
        dcs[tm:, :] = jnp.where(i < nblk - 1, dcn_ref[...], 0.0)

        @pl.when(i == 0)
        def _():
            dw_ref[...] = jnp.zeros_like(dw_ref)

        dglu = jnp.zeros((tm, W), F32)
        for k in range(K):
            dglu = dglu + w_ref[k:k + 1, :] * dcs[K - 1 - k:K - 1 - k + tm, :]
            dw_ref[k:k + 1, :] += jnp.sum(dcv * glu[lead + k:lead + k + tm, :], axis=0, keepdims=True)
        da_ref[...] = (dglu * sb).astype(BF16)
        db_ref[...] = (dglu * av * sb * (1.0 - sb)).astype(BF16)

    row = pl.BlockSpec((tm, W), lambda i: (i, 0))
    nxt = pl.BlockSpec((CONV_HALO, W), lambda i: (jnp.minimum((i + 1) * per, T // CONV_HALO - 1), 0))
    return pl.pallas_call(body, name="conv_bwd_taps", grid=(nblk,),
                          in_specs=[cur(4), cur(5), prev(4), prev(5), row, nxt,
                                    pl.BlockSpec((CONV_HALO, W), lambda i: (0, 0))],
                          out_specs=[row, row, pl.BlockSpec((CONV_HALO, W), lambda i: (0, 0))],
                          out_shape=[_sds((T, W), BF16), _sds((T, W), BF16), _sds((CONV_HALO, W), F32)],
                          scratch_shapes=[pltpu.VMEM((tm + CONV_HALO, W), F32), pltpu.VMEM((tm + CONV_HALO, W), F32)],
                          compiler_params=_cp("arbitrary"))(proj, proj, proj, proj, dc, dc, conv_w)


def _sgu_common(cu, cv, lg, lb, w_ref, bt_ref, z_scr, G, nch):
    u = _gelu(cu)
    xh, r = _ln_stats(_gelu(cv))
    vn = (xh * lg + lb).astype(BF16)
    rr = lax.broadcasted_iota(jnp.int32, (LANES, LANES), 0)
    cc = lax.broadcasted_iota(jnp.int32, (LANES, LANES), 1)
    tril = rr >= cc
    wts = [jnp.where(tril, w_ref[g], 0.0).astype(BF16) for g in range(G)]
    for ch in range(nch):
        rs = slice(ch * LANES, (ch + 1) * LANES)
        for g in range(G):
            cs = slice(g * LANES, (g + 1) * LANES)
            z_scr[rs, cs] = jnp.dot(wts[g], vn[rs, cs], preferred_element_type=F32) + bt_ref[:, g:g + 1]
    return u, xh, r, vn, wts, tril


def _sgu_fwd(proj, sgu_w, sgu_bt, ln_g, ln_b, W):
    T = proj.shape[0]
    G = W // LANES
    tm = _pick(T, 256, LANES)
    nch = tm // LANES

    def body(u_ref, v_ref, g_ref, w_ref, bt_ref, lg_ref, lb_ref, y_ref, z_scr):
        u, *_ = _sgu_common(u_ref[...], v_ref[...], lg_ref[...], lb_ref[...], w_ref, bt_ref, z_scr, G, nch)
        y_ref[...] = (u * z_scr[...] * _silu(g_ref[...])).astype(BF16)

    cur = lambda c: pl.BlockSpec((tm, W), lambda i: (i, c))
    vec = pl.BlockSpec((1, W), lambda i: (0, 0))
    return pl.pallas_call(body, name="sgu_fwd", grid=(T // tm,),
                          in_specs=[cur(7), cur(8), cur(9), pl.BlockSpec((G, LANES, LANES), lambda i: (0, 0, 0)),
                                    pl.BlockSpec((LANES, G), lambda i: (0, 0)), vec, vec],
                          out_specs=pl.BlockSpec((tm, W), lambda i: (i, 0)), out_shape=_sds((T, W), BF16),
                          scratch_shapes=[pltpu.VMEM((tm, W), F32)],
                          compiler_params=_cp("parallel"))(proj, proj, proj, sgu_w, sgu_bt, ln_g, ln_b)


def _sgu_bwd(proj, dy, sgu_w, sgu_bt, ln_g, ln_b, W):
    T = proj.shape[0]
    G = W // LANES
    tm = _pick(T, 256, LANES)
    nch = tm // LANES

    def body(u_ref, v_ref, g_ref, dy_ref, w_ref, bt_ref, lg_ref, lb_ref,
             du_ref, dv_ref, dg_ref, dw_ref, dbt_ref, st_ref, z_scr, dvn_scr):
        first = pl.program_id(0) == 0
        cu, cv, gv, dyv = u_ref[...], v_ref[...], g_ref[...], dy_ref[...]
        lg = lg_ref[...]
        u, xh, r, vn, wts, tril = _sgu_common(cu, cv, lg, lb_ref[...], w_ref, bt_ref, z_scr, G, nch)
        z = z_scr[...]
        sg = _silu(gv)
        dg_ref[...] = (dyv * u * z * _dsilu(gv)).astype(BF16)
        du_ref[...] = (dyv * z * sg * _dgelu(cu)).astype(BF16)
        dz = dyv * u * sg
        dzb = dz.astype(BF16)

        @pl.when(first)
        def _():
            dw_ref[...] = jnp.zeros_like(dw_ref)
            dbt_ref[...] = jnp.zeros_like(dbt_ref)

        for g in range(G):
            cs = slice(g * LANES, (g + 1) * LANES)
            dwg = jnp.zeros((LANES, LANES), F32)
            dbg = jnp.zeros((LANES, 1), F32)
            for ch in range(nch):
                rs = slice(ch * LANES, (ch + 1) * LANES)
                dwg = dwg + lax.dot_general(dzb[rs, cs], vn[rs, cs], NT, preferred_element_type=F32)
                dbg = dbg + jnp.sum(dz[rs, cs], axis=1, keepdims=True)
                dvn_scr[rs, cs] = lax.dot_general(wts[g], dzb[rs, cs], TN, preferred_element_type=F32)
            dw_ref[g] += jnp.where(tril, dwg, 0.0)
            dbt_ref[:, g:g + 1] += dbg
        dvn = dvn_scr[...]
        _acc_rows(st_ref, 0, jnp.sum(dvn * xh, axis=0, keepdims=True), first)
        st_ref[1:2, :] += jnp.sum(dvn, axis=0, keepdims=True)
        dv_ref[...] = (_ln_bwd(dvn * lg, xh, r) * _dgelu(cv)).astype(BF16)

    cur = lambda c: pl.BlockSpec((tm, W), lambda i: (i, c))
    row = pl.BlockSpec((tm, W), lambda i: (i, 0))
    vec = pl.BlockSpec((1, W), lambda i: (0, 0))
    wspec = pl.BlockSpec((G, LANES, LANES), lambda i: (0, 0, 0))
    bspec = pl.BlockSpec((LANES, G), lambda i: (0, 0))
    return pl.pallas_call(body, name="sgu_bwd", grid=(T // tm,),
                          in_specs=[cur(7), cur(8), cur(9), row, wspec, bspec, vec, vec],
                          out_specs=[row, row, row, wspec, bspec, pl.BlockSpec((SUBLANES, W), lambda i: (0, 0))],
                          out_shape=[_sds((T, W), BF16)] * 3 + [_sds((G, LANES, LANES), F32), _sds((LANES, G), F32),
                                                                 _sds((SUBLANES, W), F32)],
                          scratch_shapes=[pltpu.VMEM((tm, W), F32), pltpu.VMEM((tm, W), F32)],
                          compiler_params=_cp("arbitrary"))(proj, proj, proj, dy, sgu_w, sgu_bt, ln_g, ln_b)


def _class_rows(r, dil):
    return pl.ds(r, LANES, stride=dil) if dil > 1 else pl.ds(0, LANES)


def _dil_masks():
    a = lax.broadcasted_iota(jnp.int32, (LANES, LANES), 0)
    c = lax.broadcasted_iota(jnp.int32, (LANES, LANES), 1)
    return c <= a, c >= a


def _dil_fwd_group(proj, W, gi, dil):
    T = proj.shape[0]
    H = W // LANES
    R = LANES * dil
    nsb = T // R
    scale = LANES ** -0.5
    cq, ck, cv = (10 + gi) * H, (13 + gi) * H, 16 * H

    def body(q_ref, kc_ref, kp_ref, vc_ref, vp_ref, o_ref, l_ref):
        b = pl.program_id(1)
        m_cur, m_prev = _dil_masks()
        m_prev = m_prev & (b > 0)

        def cls(r, carry):
            sl = _class_rows(r, dil)
            q = (q_ref[sl, :] * scale).astype(BF16)
            sc = lax.dot_general(q, kc_ref[sl, :].astype(BF16), NT, preferred_element_type=F32)
            sp = lax.dot_general(q, kp_ref[sl, :].astype(BF16), NT, preferred_element_type=F32)
            sc = jnp.where(m_cur, sc, NEG)
            sp = jnp.where(m_prev, sp, NEG)
            m = jnp.maximum(jnp.max(sc, axis=1, keepdims=True), jnp.max(sp, axis=1, keepdims=True))
            pc = jnp.exp(sc - m)
            pp = jnp.exp(sp - m)
            den = jnp.sum(pc, axis=1, keepdims=True) + jnp.sum(pp, axis=1, keepdims=True)
            pv = (jnp.dot(pc.astype(BF16), vc_ref[sl, :].astype(BF16), preferred_element_type=F32)
                  + jnp.dot(pp.astype(BF16), vp_ref[sl, :].astype(BF16), preferred_element_type=F32))
            o_ref[sl, :] = pv / den
            l_ref[sl, :] = jnp.broadcast_to(m + jnp.log(den), (LANES, LANES))
            return carry

        lax.fori_loop(0, dil, cls, 0)

    cur = lambda c0: pl.BlockSpec((R, LANES), lambda h, b: (b, c0 + h))
    prv = lambda c0: pl.BlockSpec((R, LANES), lambda h, b: (jnp.maximum(b - 1, 0), c0 + h))
    out = pl.BlockSpec((R, LANES), lambda h, b: (b, h))
    return pl.pallas_call(body, name=f"dil_fwd_g{gi}", grid=(H, nsb),
                          in_specs=[cur(cq), cur(ck), prv(ck), cur(cv), prv(cv)], out_specs=[out, out],
                          out_shape=[_sds((T, W), F32), _sds((T, W), F32)],
                          compiler_params=_cp("parallel", "parallel"))(proj, proj, proj, proj, proj)


def _dil_combine(proj, os_, ls_, W):
    T = proj.shape[0]
    tm = _pick(T, 256, SUBLANES)

    def body(g_ref, o0, o1, o2, l0, l1, l2, od_ref, lse_ref, y_ref):
        a0, a1, a2 = l0[...], l1[...], l2[...]
        m = jnp.maximum(jnp.maximum(a0, a1), a2)
        e0, e1, e2 = jnp.exp(a0 - m), jnp.exp(a1 - m), jnp.exp(a2 - m)
        s = e0 + e1 + e2
        od = (e0 / s) * o0[...] + (e1 / s) * o1[...] + (e2 / s) * o2[...]
        od_ref[...] = od
        lse_ref[...] = m + jnp.log(s)
        y_ref[...] = (od * _silu(g_ref[...])).astype(BF16)

    row = pl.BlockSpec((tm, W), lambda i: (i, 0))
    return pl.pallas_call(body, name="dil_combine", grid=(T // tm,),
                          in_specs=[pl.BlockSpec((tm, W), lambda i: (i, 17))] + [row] * 6, out_specs=[row, row, row],
                          out_shape=[_sds((T, W), F32), _sds((T, W), F32), _sds((T, W), BF16)],
                          compiler_params=_cp("parallel"))(proj, *os_, *ls_)


def _dil_bwd_pre(proj, od, dy, W):
    T = proj.shape[0]
    H = W // LANES
    tm = _pick(T, 512, SUBLANES)

    def body(g_ref, od_ref, dy_ref, do_ref, dl_ref, dg_ref):
        gv, odv, dyv = g_ref[...], od_ref[...], dy_ref[...]
        do = dyv * _silu(gv)
        do_ref[...] = do
        dl_ref[...] = jnp.broadcast_to(jnp.sum(do * odv, axis=1, keepdims=True), (tm, LANES))
        dg_ref[...] = (dyv * odv * _dsilu(gv)).astype(BF16)

    blk = pl.BlockSpec((tm, LANES), lambda i, h: (i, h))
    return pl.pallas_call(body, name="dil_bwd_pre", grid=(T // tm, H),
                          in_specs=[pl.BlockSpec((tm, LANES), lambda i, h: (i, 17 * H + h)), blk, blk],
                          out_specs=[blk, blk, blk],
                          out_shape=[_sds((T, W), F32), _sds((T, W), F32), _sds((T, W), BF16)],
                          compiler_params=_cp("parallel", "parallel"))(proj, od, dy)


def _dil_bwd_group(proj, do, lse, delta, W, gi, dil):
    T = proj.shape[0]
    H = W // LANES
    R = LANES * dil
    nsb = T // R
    scale = LANES ** -0.5
    cq, ck, cv = (10 + gi) * H, (13 + gi) * H, 16 * H

    def body(qc_ref, qn_ref, kc_ref, kp_ref, vc_ref, vp_ref, doc_ref, don_ref, lc_ref, ln_ref, dc_ref, dn_ref,
             dq_ref, dk_ref, dv_ref):
        b = pl.program_id(1)
        m_cur, m_prev = _dil_masks()
        m_cp = m_prev & (b > 0)
        m_nc = m_prev & (b < nsb - 1)

        def probs(q, k, mask, l):
            s = lax.dot_general(q, k, NT, preferred_element_type=F32)
            return jnp.exp(jnp.where(mask, s - l, NEG))

        def cls(r, carry):
            sl = _class_rows(r, dil)
            q_c = (qc_ref[sl, :] * scale).astype(BF16)
            q_n = (qn_ref[sl, :] * scale).astype(BF16)
            k_c, k_p = kc_ref[sl, :].astype(BF16), kp_ref[sl, :].astype(BF16)
            v_c, v_p = vc_ref[sl, :].astype(BF16), vp_ref[sl, :].astype(BF16)
            do_c, do_n = doc_ref[sl, :].astype(BF16), don_ref[sl, :].astype(BF16)
            l_c, l_n = lc_ref[sl, :], ln_ref[sl, :]
            d_c, d_n = dc_ref[sl, :], dn_ref[sl, :]
            p_cc = probs(q_c, k_c, m_cur, l_c)
            p_cp = probs(q_c, k_p, m_cp, l_c)
            p_nc = probs(q_n, k_c, m_nc, l_n)
            ds_cc = (p_cc * (lax.dot_general(do_c, v_c, NT, preferred_element_type=F32) - d_c)).astype(BF16)
            ds_cp = (p_cp * (lax.dot_general(do_c, v_p, NT, preferred_element_type=F32) - d_c)).astype(BF16)
            ds_nc = (p_nc * (lax.dot_general(do_n, v_c, NT, preferred_element_type=F32) - d_n)).astype(BF16)
            dq = (jnp.dot(ds_cc, k_c, preferred_element_type=F32) + jnp.dot(ds_cp, k_p, preferred_element_type=F32))
            dk = (lax.dot_general(ds_cc, q_c, TN, preferred_element_type=F32)
                  + lax.dot_general(ds_nc, q_n, TN, preferred_element_type=F32))
            dv = (lax.dot_general(p_cc.astype(BF16), do_c, TN, preferred_element_type=F32)
                  + lax.dot_general(p_nc.astype(BF16), do_n, TN, preferred_element_type=F32))
            dq_ref[sl, :] = dq * scale
            dk_ref[sl, :] = dk
            dv_ref[sl, :] = dv
            return carry

        lax.fori_loop(0, dil, cls, 0)

    cur = lambda c0: pl.BlockSpec((R, LANES), lambda h, b: (b, c0 + h))
    prv = lambda c0: pl.BlockSpec((R, LANES), lambda h, b: (jnp.maximum(b - 1, 0), c0 + h))
    nxt = lambda c0: pl.BlockSpec((R, LANES), lambda h, b: (jnp.minimum(b + 1, nsb - 1), c0 + h))
    out = pl.BlockSpec((R, LANES), lambda h, b: (b, h))
    return pl.pallas_call(body, name=f"dil_bwd_g{gi}", grid=(H, nsb),
                          in_specs=[cur(cq), nxt(cq), cur(ck), prv(ck), cur(cv), prv(cv),
                                    cur(0), nxt(0), cur(0), nxt(0), cur(0), nxt(0)],
                          out_specs=[out, out, out], out_shape=[_sds((T, W), F32)] * 3,
                          compiler_params=_cp("parallel", "parallel"))(
        proj, proj, proj, proj, proj, proj, do, do, lse, lse, delta, delta)


def _sum3_bf16(a, b, c):
    T, W = a.shape
    tm = _pick(T, 512, SUBLANES)

    def body(a_ref, b_ref, c_ref, o_ref):
        o_ref[...] = (a_ref[...] + b_ref[...] + c_ref[...]).astype(BF16)

    row = pl.BlockSpec((tm, W), lambda i: (i, 0))
    return pl.pallas_call(body, name="dil_dv_sum", grid=(T // tm,), in_specs=[row, row, row], out_specs=row,
                          out_shape=_sds((T, W), BF16), compiler_params=_cp("parallel"))(a, b, c)


def _to_bf16(name, parts):
    T, W = parts[0].shape
    n = len(parts)
    tm = _pick(T, 512, SUBLANES)

    def body(*refs):
        o_ref = refs[n]
        for p in range(n):
            o_ref[:, p * W:(p + 1) * W] = refs[p][...].astype(BF16)

    row = pl.BlockSpec((tm, W), lambda i: (i, 0))
    return pl.pallas_call(body, name=name, grid=(T // tm,), in_specs=[row] * n,
                          out_specs=pl.BlockSpec((tm, n * W), lambda i: (i, 0)),
                          out_shape=_sds((T, n * W), BF16), compiler_params=_cp("parallel"))(*parts)


def _branch_merge(ys, w_br, gates):
    NB, T, W = ys.shape
    D = w_br.shape[2]
    tm, tn = _pick(T, 1024, SUBLANES), _pick(D, 1024)
    nj = D // tn

    def body(y_ref, w_ref, g_ref, yp_ref, m_ref, acc):
        n = pl.program_id(2)
        yp = jnp.dot(y_ref[...], w_ref[...], preferred_element_type=F32)
        yp_ref[...] = yp

        @pl.when(n == 0)
        def _():
            acc[...] = jnp.zeros_like(acc)

        acc[...] += g_ref[...] * yp

        @pl.when(n == NB - 1)
        def _():
            m_ref[...] = acc[...].astype(BF16)

    return pl.pallas_call(
        body, name="branch_merge", grid=(T // tm, nj, NB),
        in_specs=[pl.BlockSpec((None, tm, W), lambda i, j, n: (n, i, 0)),
                  pl.BlockSpec((None, W, tn), lambda i, j, n: (n, 0, j)),
                  pl.BlockSpec((tm, tn), lambda i, j, n: (i, n * nj + j))],
        out_specs=[pl.BlockSpec((None, tm, tn), lambda i, j, n: (n, i, j)),
                   pl.BlockSpec((tm, tn), lambda i, j, n: (i, j))],
        out_shape=[_sds((NB, T, D), F32), _sds((T, D), BF16)],
        scratch_shapes=[pltpu.VMEM((tm, tn), F32)],
        compiler_params=_cp("parallel", "parallel", "arbitrary"))(ys, w_br, gates)


def _merge_bwd(dmerged, gates, yproj):
    NB, T, D = yproj.shape
    tm = _pick(T, 256, SUBLANES)

    def body(dm_ref, g_ref, yp_ref, dyp_ref, dz_ref, db_ref):
        dm, g = dm_ref[...], g_ref[...]
        dyp_ref[...] = (dm * g).astype(BF16)
        dz = dm * yp_ref[...] * g * (1.0 - g)
        dz_ref[...] = dz.astype(BF16)
        _acc_rows(db_ref, 0, jnp.sum(dz, axis=0, keepdims=True), pl.program_id(1) == 0)

    return pl.pallas_call(
        body, name="merge_bwd", grid=(NB, T // tm),
        in_specs=[pl.BlockSpec((tm, D), lambda n, i: (i, 0)), pl.BlockSpec((tm, D), lambda n, i: (i, n)),
                  pl.BlockSpec((None, tm, D), lambda n, i: (n, i, 0))],
        out_specs=[pl.BlockSpec((None, tm, D), lambda n, i: (n, i, 0)), pl.BlockSpec((tm, D), lambda n, i: (i, n)),
                   pl.BlockSpec((SUBLANES, D), lambda n, i: (0, n))],
        out_shape=[_sds((NB, T, D), BF16), _sds((T, NB * D), BF16), _sds((SUBLANES, NB * D), F32)],
        compiler_params=_cp("parallel", "arbitrary"))(dmerged, gates, yproj)


def _branch_bwd_dy(dyp, w_br):
    NB, T, D = dyp.shape
    W = w_br.shape[1]
    tm = _pick(T, 1024, SUBLANES)

    def body(a_ref, w_ref, o_ref):
        o_ref[...] = lax.dot_general(a_ref[...], w_ref[...], NT, preferred_element_type=F32)

    return pl.pallas_call(body, name="branch_bwd_dy", grid=(NB, T // tm),
                          in_specs=[pl.BlockSpec((None, tm, D), lambda n, i: (n, i, 0)),
                                    pl.BlockSpec((None, W, D), lambda n, i: (n, 0, 0))],
                          out_specs=pl.BlockSpec((None, tm, W), lambda n, i: (n, i, 0)),
                          out_shape=_sds((NB, T, W), F32), compiler_params=_cp("parallel", "parallel"))(dyp, w_br)


def _branch_bwd_dw(ys, dyp):
    NB, T, W = ys.shape
    D = dyp.shape[2]
    tm, tn, tk = _pick(W, 1024), _pick(D, 1024), _pick(T, 1024, SUBLANES)
    return _mm("branch_bwd_dw", ys, dyp, grid=(NB, W // tm, D // tn, T // tk), kaxis=3, dims=TN,
               a_spec=pl.BlockSpec((None, tk, tm), lambda n, i, j, k: (n, k, i)),
               b_spec=pl.BlockSpec((None, tk, tn), lambda n, i, j, k: (n, k, j)),
               acc_shape=(tm, tn), out_shape=_sds((NB, W, D), F32),
               out_specs=pl.BlockSpec((None, tm, tn), lambda n, i, j, k: (n, i, j)),
               epilogue=_store(F32), sem=("parallel", "parallel", "parallel", "arbitrary"))


def _layer_fwd(x, wl, sp):
    W = sp["conv_b"].shape[-1]
    D = x.shape[1]
    h = _rmsnorm_fwd(x, sp["norm_g"])
    proj = _mm_nn("proj", h, wl["w_in"], F32, tn=768)

    def gate_ep(acc, ex, outs):
        outs[0][...] = _sigmoid(acc + ex[0][...])

    tn_g = _pick(4 * D, 1024)
    gates = _mm_nn("gates", h, wl["w_gate"], F32, epilogue=gate_ep, extras=(sp["b_gate"],),
                   extra_specs=(pl.BlockSpec((1, tn_g), lambda i, j, k: (0, j)),), tn=tn_g)
    oa, ya, sb_carries = _sb_fwd(proj, W)
    cpre, yb = _conv_fwd(proj, wl["conv_w"], wl["taps"], sp["conv_b"], sp["conv_ln_g"], sp["conv_ln_b"], W)
    yc = _sgu_fwd(proj, sp["sgu_w"], sp["sgu_bt"], sp["sgu_ln_g"], sp["sgu_ln_b"], W)
    os_, ls_ = zip(*[_dil_fwd_group(proj, W, gi, dil) for gi, (_, dil) in enumerate(DIL_PATTERNS)])
    od, lse, yd = _dil_combine(proj, os_, ls_, W)
    ys = jnp.stack([ya, yb, yc, yd])
    yproj, merged = _branch_merge(ys, wl["w_br"], gates)

    def res_ep(acc, ex, outs):
        outs[0][...] = ex[0][...] + acc

    tm_o, tn_o = _pick(x.shape[0], 1024, SUBLANES), _pick(D, 1024)
    xn = _mm_nn("out_proj", merged, wl["w_out"], F32, epilogue=res_ep, extras=(x,),
                extra_specs=(pl.BlockSpec((tm_o, tn_o), lambda i, j, k: (i, j)),), tm=tm_o, tn=tn_o)
    saved = dict(x=x, h=h, proj=proj, gates=gates, oa=oa, sb_carries=sb_carries, cpre=cpre, od=od, lse=lse, ys=ys, yproj=yproj, merged=merged)
    return xn, saved


def _layer_bwd(dout, sv, wl, sp):
    W = sp["conv_b"].shape[-1]
    proj = sv["proj"]
    dmerged = _mm_nt("out_proj_bwd_dx", dout, wl["w_out"], F32, tm=512)
    g_w_out = _mm_tn("out_proj_bwd_dw", sv["merged"], dout)
    dyp, dzg, db_gate = _merge_bwd(dmerged, sv["gates"], sv["yproj"])
    dy = _branch_bwd_dy(dyp, wl["w_br"])
    g_w_br = _branch_bwd_dw(sv["ys"], dyp)
    g_w_gate = _mm_tn("gate_bwd_dw", sv["h"], dzg)

    a_dq, a_dk, a_dv, a_dg = _sb_bwd(proj, sv["oa"], sv["sb_carries"], dy[0], W)
    a_qkv = _to_bf16("sb_bwd_cast", [a_dq, a_dk, a_dv])

    dc, b_dg, conv_stats = _conv_bwd_ln(proj, sv["cpre"], dy[1], sp["conv_ln_g"], sp["conv_ln_b"], W)
    b_da, b_db, g_conv_w = _conv_bwd_taps(proj, dc, wl["conv_w"], wl["taps"], W)

    c_du, c_dv, c_dg, g_sgu_w, g_sgu_bt, sgu_stats = _sgu_bwd(proj, dy[2], sp["sgu_w"], sp["sgu_bt"],
                                                              sp["sgu_ln_g"], sp["sgu_ln_b"], W)

    do, delta, d_dg = _dil_bwd_pre(proj, sv["od"], dy[3], W)
    dqs, dks, dvs = zip(*[_dil_bwd_group(proj, do, sv["lse"], delta, W, gi, dil)
                          for gi, (_, dil) in enumerate(DIL_PATTERNS)])
    d_qk = _to_bf16("dil_bwd_cast", [*dqs, *dks])
    d_dv = _sum3_bf16(*dvs)

    dproj = jnp.concatenate([a_qkv, a_dg, b_da, b_db, b_dg, c_du, c_dv, c_dg, d_qk, d_dv, d_dg], axis=1)
    g_w_in = _mm_tn("proj_bwd_dw", sv["h"], dproj, tn=768)
    dh = _mm_nt("gate_bwd_dh", dzg, wl["w_gate"], F32)
    dh = _mm_nt("proj_bwd_dh", dproj, wl["w_in"], F32, addend=dh)
    dx, dnorm = _rmsnorm_bwd(sv["x"], dh, dout, sp["norm_g"])

    K = wl["taps"]
    small = dict(norm_g=dnorm[0], conv_w=g_conv_w[:K], conv_b=conv_stats[2], conv_ln_g=conv_stats[0],
                 conv_ln_b=conv_stats[1], sgu_ln_g=sgu_stats[0], sgu_ln_b=sgu_stats[1], sgu_w=g_sgu_w,
                 sgu_b=g_sgu_bt.T, b_gate=db_gate[0])
    big = dict(w_in=g_w_in, w_gate=g_w_gate, w_br=g_w_br, w_out=g_w_out)
    return dx, big, small


HBM = pl.BlockSpec(memory_space=pl.ANY)


def _mesh_pos():
    return lax.axis_index("x"), lax.axis_index("y"), lax.axis_index("c")


def _other_chips(x, y):
    return [(1 - x, y), (x, 1 - y), (1 - x, 1 - y)]


def _shard_of(ref, axis, size, index):
    idx = [slice(None)] * len(ref.shape)
    idx[axis] = pl.ds(index * size, size)
    return ref.at[tuple(idx)]


def _all_gather(name, shards, axes):
    n = len(shards)
    out_shape = []
    for s, ax in zip(shards, axes):
        shp = list(s.shape)
        shp[ax] *= N_DEV
        out_shape.append(_sds(shp, s.dtype))

    def body(*refs):
        ins, outs = refs[:n], refs[n:2 * n]
        send, recv, lsem = refs[2 * n:]
        x, y, c = _mesh_pos()
        me, sib = (x, y, c), (x, y, 1 - c)
        chips = _other_chips(x, y)
        dev = lambda px, py, pc: 4 * px + 2 * py + pc

        def blk(a, d):
            return _shard_of(outs[a], axes[a], ins[a].shape[axes[a]], d)

        def cp(a, k, d, to, src=None):
            return pltpu.make_async_remote_copy(src_ref=blk(a, d) if src is None else src, dst_ref=blk(a, d),
                                                send_sem=send.at[a * 7 + k], recv_sem=recv.at[a * 7 + k],
                                                device_id=to, device_id_type=MESH)

        own = [pltpu.make_async_copy(ins[a], blk(a, dev(*me)), lsem.at[a]) for a in range(n)]
        for o in own:
            o.start()
        first = []
        for a in range(n):
            first.append(cp(a, 0, dev(*me), sib, src=ins[a]))
            first += [cp(a, 1 + j, dev(*me), (*ch, c), src=ins[a]) for j, ch in enumerate(chips)]
        for f in first:
            f.start()
        passed = []
        for j, ch in enumerate(chips):
            for a in range(n):
                cp(a, 1 + j, dev(*ch, c), me).wait_recv()
                p = cp(a, 4 + j, dev(*ch, c), sib)
                p.start()
                passed.append(p)
        for a in range(n):
            cp(a, 0, dev(*sib), me).wait_recv()
            for j, ch in enumerate(chips):
                cp(a, 4 + j, dev(*ch, 1 - c), me).wait_recv()
        for f in first + passed:
            f.wait_send()
        for o in own:
            o.wait()

    return pl.pallas_call(body, name=name, in_specs=[HBM] * n, out_specs=[HBM] * n, out_shape=out_shape,
                          scratch_shapes=[pltpu.SemaphoreType.DMA((7 * n,)), pltpu.SemaphoreType.DMA((7 * n,)),
                                          pltpu.SemaphoreType.DMA((n,))])(*shards)


def _rs_pair(name, grads, axes):
    n = len(grads)
    sizes = [g.shape[ax] // N_DEV for g, ax in zip(grads, axes)]
    out_shape = []
    for g, ax, sz in zip(grads, axes, sizes):
        shp = list(g.shape)
        shp[ax] = sz
        out_shape.append(_sds([N_CHIP] + shp, g.dtype))

    def body(*refs):
        ins, outs = refs[:n], refs[n:2 * n]
        send, recv = refs[2 * n:]
        x, y, c = _mesh_pos()
        cps = []
        for a in range(n):
            for k in range(N_CHIP):
                cps.append(pltpu.make_async_remote_copy(
                    src_ref=_shard_of(ins[a], axes[a], sizes[a], 2 * k + (1 - c)), dst_ref=outs[a].at[k],
                    send_sem=send.at[a * N_CHIP + k], recv_sem=recv.at[a * N_CHIP + k],
                    device_id=(x, y, 1 - c), device_id_type=MESH))
        for cp in cps:
            cp.start()
        for cp in cps:
            cp.wait()

    return pl.pallas_call(body, name=name, in_specs=[HBM] * n, out_specs=[HBM] * n, out_shape=out_shape,
                          scratch_shapes=[pltpu.SemaphoreType.DMA((N_CHIP * n,)),
                                          pltpu.SemaphoreType.DMA((N_CHIP * n,))])(*grads)


def _pair_sum(name, grad, recv, axis, core):
    R, C = grad.shape
    _, r, cw = recv.shape
    tr = _pick(r, 512, SUBLANES)
    nr = r // tr

    def body(c_ref, g_ref, r_ref, o_ref):
        o_ref[...] = g_ref[...] + r_ref[...]

    if axis == 1:
        gspec = pl.BlockSpec((tr, cw), lambda k, i, c_ref: (i, 2 * k + c_ref[0]))
    else:
        gspec = pl.BlockSpec((tr, cw), lambda k, i, c_ref: ((2 * k + c_ref[0]) * nr + i, 0))
    part = pl.BlockSpec((None, tr, cw), lambda k, i, c_ref: (k, i, 0))
    return pl.pallas_call(
        body, name=name, out_shape=_sds((N_CHIP, r, cw), F32),
        grid_spec=pltpu.PrefetchScalarGridSpec(num_scalar_prefetch=1, grid=(N_CHIP, nr), in_specs=[gspec, part],
                                               out_specs=part),
        compiler_params=_cp("parallel", "parallel"))(core, grad, recv)


def _rs_chips(name, parts):
    n = len(parts)

    def body(*refs):
        ins, outs = refs[:n], refs[n:2 * n]
        send, recv, lsem = refs[2 * n:]
        x, y, c = _mesh_pos()
        mine = 2 * x + y
        own = [pltpu.make_async_copy(ins[a].at[mine], outs[a].at[mine], lsem.at[a]) for a in range(n)]
        for o in own:
            o.start()
        cps = []
        for a in range(n):
            for j, (px, py) in enumerate(_other_chips(x, y)):
                cps.append(pltpu.make_async_remote_copy(
                    src_ref=ins[a].at[2 * px + py], dst_ref=outs[a].at[mine],
                    send_sem=send.at[a * 3 + j], recv_sem=recv.at[a * 3 + j],
                    device_id=(px, py, c), device_id_type=MESH))
        for cp in cps:
            cp.start()
        for a in range(n):
            for j, (px, py) in enumerate(_other_chips(x, y)):
                pltpu.make_async_remote_copy(
                    src_ref=ins[a].at[mine], dst_ref=outs[a].at[2 * px + py],
                    send_sem=send.at[a * 3 + j], recv_sem=recv.at[a * 3 + j],
                    device_id=(x, y, c), device_id_type=MESH).wait_recv()
        for cp in cps:
            cp.wait_send()
        for o in own:
            o.wait()

    return pl.pallas_call(body, name=name, in_specs=[HBM] * n, out_specs=[HBM] * n,
                          out_shape=[_sds(p.shape, p.dtype) for p in parts],
                          scratch_shapes=[pltpu.SemaphoreType.DMA((3 * n,)), pltpu.SemaphoreType.DMA((3 * n,)),
                                          pltpu.SemaphoreType.DMA((n,))])(*parts)


def _adamw_sum(name, parts, w, m, v):
    P, R, C = parts.shape
    tr = _pick(R, max(SUBLANES, (1 << 19) // C // SUBLANES * SUBLANES), SUBLANES)

    def body(p_ref, w_ref, m_ref, v_ref, g_ref, d_ref, mo_ref, vo_ref):
        g = p_ref[0]
        for k in range(1, P):
            g = g + p_ref[k]
        mn = ADAM_B1 * m_ref[...] + (1.0 - ADAM_B1) * g
        vn = ADAM_B2 * v_ref[...] + (1.0 - ADAM_B2) * (g * g)
        m_hat = mn / (1.0 - ADAM_B1 ** ADAM_STEP)
        v_hat = vn / (1.0 - ADAM_B2 ** ADAM_STEP)
        g_ref[...] = g
        d_ref[...] = -ADAM_LR * (m_hat / (jnp.sqrt(v_hat) + ADAM_EPS) + ADAM_WD * w_ref[...])
        mo_ref[...] = mn
        vo_ref[...] = vn

    row = pl.BlockSpec((tr, C), lambda i: (i, 0))
    return pl.pallas_call(body, name=name, grid=(R // tr,),
                          in_specs=[pl.BlockSpec((P, tr, C), lambda i: (0, i, 0)), row, row, row],
                          out_specs=[row] * 4, out_shape=[_sds((R, C), F32)] * 4,
                          compiler_params=_cp("parallel"))(parts, w, m, v)


def _rows128(a, pad_rows=SUBLANES):
    flat = a.reshape(-1, LANES)
    pad = (-flat.shape[0]) % pad_rows
    return jnp.pad(flat, ((0, pad), (0, 0))) if pad else flat


SMALL = ("norm_g", "conv_b", "conv_ln_g", "conv_ln_b", "sgu_ln_g", "sgu_ln_b", "sgu_w", "sgu_b", "b_gate", "final_g")


def kernel(x, norm_g, w_in, conv_w, conv_b, conv_ln_g, conv_ln_b, sgu_ln_g, sgu_ln_b, sgu_w, sgu_b, w_branch, w_gate, b_gate, w_out, final_g, loss_target, m_norm_g, m_w_in, m_conv_w, m_conv_b, m_conv_ln_g, m_conv_ln_b, m_sgu_ln_g, m_sgu_ln_b, m_sgu_w, m_sgu_b, m_w_branch, m_w_gate, m_b_gate, m_w_out, m_final_g, v_norm_g, v_w_in, v_conv_w, v_conv_b, v_conv_ln_g, v_conv_ln_b, v_sgu_ln_g, v_sgu_ln_b, v_sgu_w, v_sgu_b, v_w_branch, v_w_gate, v_b_gate, v_w_out, v_final_g):
    L, D = norm_g.shape
    W = conv_b.shape[1]
    taps = conv_w.shape[1]
    weights = dict(norm_g=norm_g, w_in=w_in, conv_w=conv_w, conv_b=conv_b, conv_ln_g=conv_ln_g, conv_ln_b=conv_ln_b,
                   sgu_ln_g=sgu_ln_g, sgu_ln_b=sgu_ln_b, sgu_w=sgu_w, sgu_b=sgu_b, w_branch=w_branch, w_gate=w_gate,
                   b_gate=b_gate, w_out=w_out, final_g=final_g)
    mom_m = dict(norm_g=m_norm_g, w_in=m_w_in, conv_w=m_conv_w, conv_b=m_conv_b, conv_ln_g=m_conv_ln_g,
                 conv_ln_b=m_conv_ln_b, sgu_ln_g=m_sgu_ln_g, sgu_ln_b=m_sgu_ln_b, sgu_w=m_sgu_w, sgu_b=m_sgu_b,
                 w_branch=m_w_branch, w_gate=m_w_gate, b_gate=m_b_gate, w_out=m_w_out, final_g=m_final_g)
    mom_v = dict(norm_g=v_norm_g, w_in=v_w_in, conv_w=v_conv_w, conv_b=v_conv_b, conv_ln_g=v_conv_ln_g,
                 conv_ln_b=v_conv_ln_b, sgu_ln_g=v_sgu_ln_g, sgu_ln_b=v_sgu_ln_b, sgu_w=v_sgu_w, sgu_b=v_sgu_b,
                 w_branch=v_w_branch, w_gate=v_w_gate, b_gate=v_b_gate, w_out=v_w_out, final_g=v_final_g)
    core = lax.axis_index("c").astype(jnp.int32).reshape(1)
    me = 4 * lax.axis_index("x") + 2 * lax.axis_index("y") + lax.axis_index("c")

    xs = x[0]
    saved, gathered, smalls = [], [], []
    for l in range(L):
        w_in_f, w_gate_f, w_br_f, w_out_f, conv_w_f = _all_gather(
            "gather_weights",
            [w_in[l].astype(BF16), w_gate[l].astype(BF16), w_branch[l].astype(BF16), w_out[l].astype(BF16),
             jnp.pad(conv_w[l], ((0, CONV_HALO - taps), (0, 0)))],
            [1, 1, 2, 0, 1])
        wl = dict(w_in=w_in_f, w_gate=w_gate_f, w_br=w_br_f, w_out=w_out_f, conv_w=conv_w_f, taps=taps)
        sp = dict(norm_g=norm_g[l][None], conv_b=conv_b[l][None], conv_ln_g=conv_ln_g[l][None],
                  conv_ln_b=conv_ln_b[l][None], sgu_ln_g=sgu_ln_g[l][None], sgu_ln_b=sgu_ln_b[l][None],
                  sgu_w=sgu_w[l], sgu_bt=sgu_b[l].T, b_gate=b_gate[l][None])
        xs, sv = _layer_fwd(xs, wl, sp)
        saved.append(sv)
        gathered.append(wl)
        smalls.append(sp)

    dx, d_final, loss_part = _loss_head(xs, loss_target[0], final_g[None])
    loss = lax.psum(loss_part[0, 0], ("x", "y", "c"))

    big_names = ("w_in", "w_gate", "w_branch", "w_out")
    big_axes = (1, 1, 1, 0)
    outs = {}
    small_grads = []
    for l in reversed(range(L)):
        dx, big, small = _layer_bwd(dx, saved[l], gathered[l], smalls[l])
        small_grads.append(small)
        g2d = [big["w_in"], big["w_gate"], big["w_br"].reshape(-1, D), big["w_out"]]
        recv = _rs_pair("reduce_pair", g2d, big_axes)
        parts = [_pair_sum("pair_sum_" + nm, g, r, ax, core) for nm, g, r, ax in zip(big_names, g2d, recv, big_axes)]
        total = _rs_chips("reduce_chips", parts)
        for nm, tot in zip(big_names, total):
            shard2d = tot.shape[1:]
            res = _adamw_sum("adamw_" + nm, tot, weights[nm][l].reshape(shard2d), mom_m[nm][l].reshape(shard2d),
                             mom_v[nm][l].reshape(shard2d))
            outs.setdefault(nm, []).append(res)
    small_grads.reverse()
    big_out = {nm: [jnp.stack([outs[nm][L - 1 - l][q] for l in range(L)]).reshape(weights[nm].shape) for q in range(4)]
               for nm in big_names}

    sg = {nm: jnp.stack([small_grads[l][nm] for l in range(L)]) for nm in SMALL[:-1]}
    sg["final_g"] = d_final[0]
    conv_w_full = jnp.stack([small_grads[l]["conv_w"] for l in range(L)])
    segs = [_rows128(sg[nm]) for nm in SMALL] + [_rows128(conv_w_full)]
    offs = [0]
    for s in segs:
        offs.append(offs[-1] + s.shape[0])
    pack = jnp.concatenate(segs, axis=0)
    (allp,) = _all_gather("gather_small_grads", [pack[None]], [0])

    def packed(src):
        return jnp.concatenate([_rows128(src[nm]) for nm in SMALL] + [jnp.zeros_like(segs[-1])], axis=0)

    s_g, s_d, s_m, s_v = _adamw_sum("adamw_small", allp, packed(weights), packed(mom_m), packed(mom_v))

    def unpack(buf, i, like):
        n = like.size // LANES
        return buf[offs[i]:offs[i] + n].reshape(like.shape)

    small_out = {nm: [unpack(b, i, weights[nm]) for b in (s_g, s_d, s_m, s_v)] for i, nm in enumerate(SMALL)}
    Wc = conv_w.shape[2]
    cw_sum = lax.dynamic_slice_in_dim(unpack(s_g, len(SMALL), conv_w_full), me * Wc, Wc, axis=2)
    cshape = (L * conv_w.shape[1], Wc)
    conv_out = [o.reshape(conv_w.shape) for o in _adamw_sum(
        "adamw_conv_w", cw_sum.reshape((1,) + cshape), conv_w.reshape(cshape), m_conv_w.reshape(cshape),
        v_conv_w.reshape(cshape))]

    order = ["norm_g", "w_in", "conv_w", "conv_b", "conv_ln_g", "conv_ln_b", "sgu_ln_g", "sgu_ln_b", "sgu_w", "sgu_b",
             "w_branch", "w_gate", "b_gate", "w_out", "final_g"]
    table = dict(small_out)
    table.update(big_out)
    table["conv_w"] = conv_out
    result = [loss, dx[None]]
    for q in range(4):
        result += [table[nm][q] for nm in order]
    return tuple(result)
```

```python
import functools

import jax
import jax.numpy as jnp
from jax import lax
from jax.experimental import pallas as pl
from jax.experimental.pallas import tpu as pltpu

F32 = jnp.float32
BF16 = jnp.bfloat16
LANES = 128
SUBLANES = 8
CONV_HALO = 32
SB_KEY_BLOCK = 512
DIL_STEP_ROWS = 2048
DIL_UNROLL = 4
NORM_EPS = 1e-6
NEG = -1e30
N_DEV = 8
N_CHIP = 4
DIL_PATTERNS = ((128, 1), (512, 4), (2048, 16))

ADAM_LR = 0.001
ADAM_B1 = 0.9
ADAM_B2 = 0.999
ADAM_EPS = 1e-08
ADAM_WD = 0.01
ADAM_STEP = 10

MESH = pl.DeviceIdType.MESH
NN = (((1,), (0,)), ((), ()))
NT = (((1,), (1,)), ((), ()))
TN = (((0,), (0,)), ((), ()))
VMEM_LIMIT = 52 << 20


def _sds(shape, dtype):
    return jax.ShapeDtypeStruct(tuple(shape), dtype)


def _cp(*sem):
    return pltpu.CompilerParams(dimension_semantics=tuple(sem), vmem_limit_bytes=VMEM_LIMIT)


def _pick(n, target, quantum=LANES):
    if n <= target:
        return n
    t = (target // quantum) * quantum
    while t >= quantum:
        if n % t == 0:
            return t
        t -= quantum
    return n


def _sigmoid(x):
    return 1.0 / (1.0 + jnp.exp(-x))


def _silu(x):
    return x * _sigmoid(x)


def _dsilu(x):
    s = _sigmoid(x)
    return s * (1.0 + x * (1.0 - s))


_GELU_K = 0.7978845608028654
_GELU_A = 0.044715


def _gelu(x):
    return 0.5 * x * (1.0 + jnp.tanh(_GELU_K * (x + _GELU_A * x * x * x)))


def _dgelu(x):
    t = jnp.tanh(_GELU_K * (x + _GELU_A * x * x * x))
    return 0.5 * (1.0 + t) + 0.5 * x * (1.0 - t * t) * _GELU_K * (1.0 + 3.0 * _GELU_A * x * x)


def _ln_stats(v):
    mu = jnp.mean(v, axis=-1, keepdims=True)
    d = v - mu
    var = jnp.mean(d * d, axis=-1, keepdims=True)
    r = lax.rsqrt(var + NORM_EPS)
    return d * r, r


def _ln_bwd(dxh, xh, r):
    return r * (dxh - jnp.mean(dxh, axis=-1, keepdims=True) - xh * jnp.mean(dxh * xh, axis=-1, keepdims=True))


def _acc_rows(ref, row, val, first):
    @pl.when(first)
    def _():
        ref[...] = jnp.zeros_like(ref)
    ref[row:row + 1, :] += val


def _mm(name, a, b, *, grid, kaxis, dims, a_spec, b_spec, acc_shape, out_shape, out_specs,
        epilogue, extras=(), extra_specs=(), sem):
    nk = grid[kaxis]
    ne = len(extras)

    def body(*refs):
        a_ref, b_ref = refs[0], refs[1]
        ex = refs[2:2 + ne]
        outs = refs[2 + ne:-1]
        acc = refs[-1]
        k = pl.program_id(kaxis)

        @pl.when(k == 0)
        def _():
            acc[...] = jnp.zeros_like(acc)

        acc[...] += lax.dot_general(a_ref[...].astype(BF16), b_ref[...].astype(BF16), dims,
                                    preferred_element_type=F32)

        @pl.when(k == nk - 1)
        def _():
            epilogue(acc[...], ex, outs)

    return pl.pallas_call(
        body, name=name, grid=grid, in_specs=[a_spec, b_spec, *extra_specs], out_specs=out_specs,
        out_shape=out_shape, scratch_shapes=[pltpu.VMEM(acc_shape, F32)], compiler_params=_cp(*sem),
    )(a, b, *extras)


def _store(dtype):
    def ep(acc, ex, outs):
        outs[0][...] = acc.astype(dtype)
    return ep


def _mm_nn(name, a, b, out_dtype, epilogue=None, extras=(), extra_specs=(), tm=1024, tn=1024):
    M, K = a.shape
    N = b.shape[1]
    tm, tn = _pick(M, tm, SUBLANES), _pick(N, tn)
    return _mm(name, a, b, grid=(M // tm, N // tn, 1), kaxis=2, dims=NN,
               a_spec=pl.BlockSpec((tm, K), lambda i, j, k: (i, 0)),
               b_spec=pl.BlockSpec((K, tn), lambda i, j, k: (0, j)),
               acc_shape=(tm, tn), out_shape=_sds((M, N), out_dtype),
               out_specs=pl.BlockSpec((tm, tn), lambda i, j, k: (i, j)),
               epilogue=epilogue or _store(out_dtype), extras=extras, extra_specs=extra_specs,
               sem=("parallel", "parallel", "arbitrary"))


def _mm_nt(name, a, b, out_dtype, addend=None, tm=1024, tn=1024, tk=1024):
    M, K = a.shape
    N = b.shape[0]
    tm, tn, tk = _pick(M, tm, SUBLANES), _pick(N, tn), _pick(K, tk)
    extras, extra_specs = (), ()
    if addend is not None:
        extras = (addend,)
        extra_specs = (pl.BlockSpec((tm, tn), lambda i, j, k: (i, j)),)

    def ep(acc, ex, outs):
        if ex:
            acc = acc + ex[0][...]
        outs[0][...] = acc.astype(out_dtype)

    return _mm(name, a, b, grid=(M // tm, N // tn, K // tk), kaxis=2, dims=NT,
               a_spec=pl.BlockSpec((tm, tk), lambda i, j, k: (i, k)),
               b_spec=pl.BlockSpec((tn, tk), lambda i, j, k: (j, k)),
               acc_shape=(tm, tn), out_shape=_sds((M, N), out_dtype),
               out_specs=pl.BlockSpec((tm, tn), lambda i, j, k: (i, j)),
               epilogue=ep, extras=extras, extra_specs=extra_specs,
               sem=("parallel", "parallel", "arbitrary"))


def _mm_tn(name, a, b, tm=1024, tn=1024, tk=1024):
    K, M = a.shape
    N = b.shape[1]
    tm, tn, tk = _pick(M, tm), _pick(N, tn), _pick(K, tk, SUBLANES)
    return _mm(name, a, b, grid=(M // tm, N // tn, K // tk), kaxis=2, dims=TN,
               a_spec=pl.BlockSpec((tk, tm), lambda i, j, k: (k, i)),
               b_spec=pl.BlockSpec((tk, tn), lambda i, j, k: (k, j)),
               acc_shape=(tm, tn), out_shape=_sds((M, N), F32),
               out_specs=pl.BlockSpec((tm, tn), lambda i, j, k: (i, j)),
               epilogue=_store(F32), sem=("parallel", "parallel", "arbitrary"))


def _rmsnorm_fwd(x, g_row):
    T, D = x.shape
    tm = _pick(T, 512, SUBLANES)

    def body(x_ref, g_ref, h_ref):
        xv = x_ref[...]
        r = lax.rsqrt(jnp.mean(xv * xv, axis=-1, keepdims=True) + NORM_EPS)
        h_ref[...] = (xv * r * g_ref[...]).astype(BF16)

    row = pl.BlockSpec((tm, D), lambda i: (i, 0))
    return pl.pallas_call(body, name="rmsnorm_fwd", grid=(T // tm,),
                          in_specs=[row, pl.BlockSpec((1, D), lambda i: (0, 0))], out_specs=row,
                          out_shape=_sds((T, D), BF16), compiler_params=_cp("parallel"))(x, g_row)


def _rmsnorm_bwd(x, dh, dout, g_row):
    T, D = x.shape
    tm = _pick(T, 256, SUBLANES)

    def body(x_ref, dh_ref, do_ref, g_ref, dx_ref, dg_ref):
        xv = x_ref[...]
        r = lax.rsqrt(jnp.mean(xv * xv, axis=-1, keepdims=True) + NORM_EPS)
        xh = xv * r
        dhv = dh_ref[...]
        dxh = dhv * g_ref[...]
        dx_ref[...] = do_ref[...] + r * (dxh - xh * jnp.mean(dxh * xh, axis=-1, keepdims=True))
        _acc_rows(dg_ref, 0, jnp.sum(dhv * xh, axis=0, keepdims=True), pl.program_id(0) == 0)

    row = pl.BlockSpec((tm, D), lambda i: (i, 0))
    return pl.pallas_call(body, name="rmsnorm_bwd", grid=(T // tm,),
                          in_specs=[row, row, row, pl.BlockSpec((1, D), lambda i: (0, 0))],
                          out_specs=[row, pl.BlockSpec((SUBLANES, D), lambda i: (0, 0))],
                          out_shape=[_sds((T, D), F32), _sds((SUBLANES, D), F32)],
                          compiler_params=_cp("arbitrary"))(x, dh, dout, g_row)


def _loss_head(x, target, g_row):
    T, D = x.shape
    tm = _pick(T, 256, SUBLANES)

    def body(x_ref, t_ref, g_ref, dx_ref, dg_ref, loss_ref):
        first = pl.program_id(0) == 0
        xv = x_ref[...]
        g = g_ref[...]
        r = lax.rsqrt(jnp.mean(xv * xv, axis=-1, keepdims=True) + NORM_EPS)
        xh = xv * r
        err = xh * g - t_ref[...]
        part = 0.5 * jnp.sum(jnp.mean(err * err, axis=-1, keepdims=True), axis=0, keepdims=True)

        @pl.when(first)
        def _():
            loss_ref[...] = jnp.zeros_like(loss_ref)

        loss_ref[...] += jnp.broadcast_to(part, loss_ref.shape)
        dy = err / D
        _acc_rows(dg_ref, 0, jnp.sum(dy * xh, axis=0, keepdims=True), first)
        dxh = dy * g
        dx_ref[...] = r * (dxh - xh * jnp.mean(dxh * xh, axis=-1, keepdims=True))

    row = pl.BlockSpec((tm, D), lambda i: (i, 0))
    return pl.pallas_call(body, name="loss_head", grid=(T // tm,),
                          in_specs=[row, row, pl.BlockSpec((1, D), lambda i: (0, 0))],
                          out_specs=[row, pl.BlockSpec((SUBLANES, D), lambda i: (0, 0)),
                                     pl.BlockSpec((SUBLANES, LANES), lambda i: (0, 0))],
                          out_shape=[_sds((T, D), F32), _sds((SUBLANES, D), F32), _sds((SUBLANES, LANES), F32)],
                          compiler_params=_cp("arbitrary"))(x, target, g_row)


def _tri(cmp):
    r = lax.broadcasted_iota(jnp.int32, (LANES, LANES), 0)
    c = lax.broadcasted_iota(jnp.int32, (LANES, LANES), 1)
    return cmp(r, c).astype(BF16)


def _scan_mm(x, tri):
    hi = x.astype(BF16)
    lo = (x - hi.astype(F32)).astype(BF16)
    return (jnp.dot(hi, tri, preferred_element_type=F32) + jnp.dot(lo, tri, preferred_element_type=F32))


def _sb_scores(qs, kb, causal):
    z = lax.dot_general(qs, kb, NT, preferred_element_type=F32)
    lb = jnp.minimum(z, 0.0) - jnp.log(1.0 + jnp.exp(-jnp.abs(z)))
    l1 = lb - z
    return lb, (l1 if causal is None else jnp.where(causal, l1, 0.0))


def _masked(causal, x):
    return x if causal is None else jnp.where(causal, x, 0.0)


def _lanes(x, s):
    return x[:, s * LANES:(s + 1) * LANES]


def _sb_fwd(proj, W):
    T = proj.shape[0]
    H = W // LANES
    assert T // LANES <= LANES
    tq = _pick(T, 256, LANES)
    kblk = _pick(T, SB_KEY_BLOCK, LANES)
    assert kblk % tq == 0
    nsub = kblk // LANES
    scale = LANES ** -0.5

    def body(q_ref, k_ref, v_ref, g_ref, o_ref, y_ref, c_ref, run):
        i = pl.program_id(1)
        qs = (q_ref[...] * scale).astype(BF16)
        row = i * tq + lax.broadcasted_iota(jnp.int32, (tq, kblk), 0)
        key = lax.broadcasted_iota(jnp.int32, (tq, kblk), 1)
        col = lax.broadcasted_iota(jnp.int32, (tq, LANES), 1)
        tri = _tri(lambda r, c: r > c)
        nkb = ((i + 1) * tq + kblk - 1) // kblk
        o_ref[...] = jnp.zeros_like(o_ref)
        c_ref[...] = jnp.zeros_like(c_ref)
        run[...] = jnp.zeros_like(run)

        def make_step(diagonal):
            def step(jj, carry):
                j = nkb - 1 - jj
                off = pl.multiple_of(j * kblk, kblk)
                kb = k_ref[pl.ds(off, kblk), :].astype(BF16)
                vb = v_ref[pl.ds(off, kblk), :].astype(BF16)
                causal = (key + off < row) if diagonal else None
                lb, l1m = _sb_scores(qs, kb, causal)
                c_after = run[...]
                cs = c_ref[...]
                after = [None] * nsub
                for s in reversed(range(nsub)):
                    part = _lanes(l1m, s)
                    after[s] = c_after + _scan_mm(part, tri)
                    cs = jnp.where(col == j * nsub + s, c_after, cs)
                    c_after = c_after + jnp.sum(part, axis=1, keepdims=True)
                run[...] = c_after
                c_ref[...] = cs
                w = _masked(causal, jnp.exp(lb + jnp.concatenate(after, axis=1)))
                o_ref[...] += jnp.dot(w.astype(BF16), vb, preferred_element_type=F32)
                return carry
            return step

        make_step(True)(0, 0)
        lax.fori_loop(1, nkb, make_step(False), 0)
        y_ref[...] = (o_ref[...] * _silu(g_ref[...])).astype(BF16)

    qspec = lambda c0: pl.BlockSpec((tq, LANES), lambda h, i: (i, c0 + h))
    kvspec = lambda c0: pl.BlockSpec((T, LANES), lambda h, i: (0, c0 + h))
    out = pl.BlockSpec((tq, LANES), lambda h, i: (i, h))
    return pl.pallas_call(body, name="sb_fwd", grid=(H, T // tq),
                          in_specs=[qspec(0), kvspec(H), kvspec(2 * H), qspec(3 * H)], out_specs=[out, out, out],
                          out_shape=[_sds((T, W), F32), _sds((T, W), BF16), _sds((T, W), F32)],
                          scratch_shapes=[pltpu.VMEM((tq, LANES), F32)],
                          compiler_params=_cp("parallel", "arbitrary"))(proj, proj, proj, proj)


def _sb_bwd(proj, o, carries, dy, W):
    T = proj.shape[0]
    H = W // LANES
    tq = _pick(T, 256, LANES)
    kblk = _pick(T, SB_KEY_BLOCK, LANES)
    assert kblk % tq == 0
    nsub = kblk // LANES
    scale = LANES ** -0.5

    def body(q_ref, k_ref, v_ref, g_ref, o_ref, c_ref, dy_ref, dq_ref, dk_ref, dv_ref, dg_ref, run):
        i = pl.program_id(1)

        @pl.when(i == 0)
        def _():
            dk_ref[...] = jnp.zeros_like(dk_ref)
            dv_ref[...] = jnp.zeros_like(dv_ref)

        gv = g_ref[...]
        dyv = dy_ref[...]
        dg_ref[...] = (dyv * o_ref[...] * _dsilu(gv)).astype(BF16)
        dob = (dyv * _silu(gv)).astype(BF16)
        qs = (q_ref[...] * scale).astype(BF16)
        row = i * tq + lax.broadcasted_iota(jnp.int32, (tq, kblk), 0)
        key = lax.broadcasted_iota(jnp.int32, (tq, kblk), 1)
        col = lax.broadcasted_iota(jnp.int32, (tq, LANES), 1)
        tri_after = _tri(lambda r, c: r > c)
        tri_before = _tri(lambda r, c: r < c)
        dq_ref[...] = jnp.zeros_like(dq_ref)
        run[...] = jnp.zeros_like(run)

        def make_step(diagonal):
            def step(j, carry):
                off = pl.multiple_of(j * kblk, kblk)
                kb = k_ref[pl.ds(off, kblk), :].astype(BF16)
                vb = v_ref[pl.ds(off, kblk), :].astype(BF16)
                causal = (key + off < row) if diagonal else None
                lb, l1m = _sb_scores(qs, kb, causal)
                cs = c_ref[...]
                after = [jnp.sum(jnp.where(col == j * nsub + s, cs, 0.0), axis=1, keepdims=True)
                         + _scan_mm(_lanes(l1m, s), tri_after) for s in range(nsub)]
                w = _masked(causal, jnp.exp(lb + jnp.concatenate(after, axis=1)))
                gw = w * lax.dot_general(dob, vb, NT, preferred_element_type=F32)
                gpre = run[...]
                before = [None] * nsub
                for s in range(nsub):
                    part = _lanes(gw, s)
                    before[s] = gpre + _scan_mm(part, tri_before)
                    gpre = gpre + jnp.sum(part, axis=1, keepdims=True)
                run[...] = gpre
                beta = jnp.exp(lb)
                dz = _masked(causal, gw * (1.0 - beta) - jnp.concatenate(before, axis=1) * beta).astype(BF16)
                dq_ref[...] += jnp.dot(dz, kb, preferred_element_type=F32)
                dk_ref[pl.ds(off, kblk), :] += lax.dot_general(dz, qs, TN, preferred_element_type=F32)
                dv_ref[pl.ds(off, kblk), :] += lax.dot_general(w.astype(BF16), dob, TN, preferred_element_type=F32)
                return carry
            return step

        last = ((i + 1) * tq + kblk - 1) // kblk - 1
        lax.fori_loop(0, last, make_step(False), 0)
        make_step(True)(last, 0)
        dq_ref[...] = dq_ref[...] * scale

    qspec = lambda c0: pl.BlockSpec((tq, LANES), lambda h, i: (i, c0 + h))
    kvspec = lambda c0: pl.BlockSpec((T, LANES), lambda h, i: (0, c0 + h))
    blk = pl.BlockSpec((tq, LANES), lambda h, i: (i, h))
    full = pl.BlockSpec((T, LANES), lambda h, i: (0, h))
    return pl.pallas_call(body, name="sb_bwd", grid=(H, T // tq),
                          in_specs=[qspec(0), kvspec(H), kvspec(2 * H), qspec(3 * H), blk, blk,
                                    pl.BlockSpec((None, tq, LANES), lambda h, i: (0, i, h))],
                          out_specs=[blk, full, full, blk],
                          out_shape=[_sds((T, W), F32), _sds((T, W), F32), _sds((T, W), F32), _sds((T, W), BF16)],
                          scratch_shapes=[pltpu.VMEM((tq, LANES), F32)],
                          compiler_params=_cp("parallel", "arbitrary"))(proj, proj, proj, proj, o, carries, dy)


def _conv_specs(T, W, tm, col_a):
    per = tm // CONV_HALO
    cur = lambda c: pl.BlockSpec((tm, W), lambda i: (i, c))
    prev = lambda c: pl.BlockSpec((CONV_HALO, W), lambda i: (jnp.maximum(i * per - 1, 0), c))
    return cur, prev


def _conv_fwd(proj, conv_w, K, conv_b, ln_g, ln_b, W):
    T = proj.shape[0]
    tm = _pick(T, 256, CONV_HALO)
    lead = CONV_HALO - (K - 1)
    cur, prev = _conv_specs(T, W, tm, 4)

    def body(a_ref, b_ref, ah_ref, bh_ref, g_ref, w_ref, cb_ref, lg_ref, lb_ref, c_ref, y_ref, glu):
        i = pl.program_id(0)
        glu[0:CONV_HALO, :] = jnp.where(i > 0, ah_ref[...] * _sigmoid(bh_ref[...]), 0.0)
        glu[CONV_HALO:, :] = a_ref[...] * _sigmoid(b_ref[...])
        c = jnp.broadcast_to(cb_ref[...], (tm, W))
        for k in range(K):
            c = c + w_ref[k:k + 1, :] * glu[lead + k:lead + k + tm, :]
        c_ref[...] = c
        xh, _ = _ln_stats(c)
        y_ref[...] = (_silu(xh * lg_ref[...] + lb_ref[...]) * _silu(g_ref[...])).astype(BF16)

    vec = pl.BlockSpec((1, W), lambda i: (0, 0))
    row = pl.BlockSpec((tm, W), lambda i: (i, 0))
    return pl.pallas_call(body, name="conv_fwd", grid=(T // tm,),
                          in_specs=[cur(4), cur(5), prev(4), prev(5), cur(6),
                                    pl.BlockSpec((CONV_HALO, W), lambda i: (0, 0)), vec, vec, vec],
                          out_specs=[row, row], out_shape=[_sds((T, W), F32), _sds((T, W), BF16)],
                          scratch_shapes=[pltpu.VMEM((tm + CONV_HALO, W), F32)],
                          compiler_params=_cp("parallel"))(proj, proj, proj, proj, proj, conv_w, conv_b, ln_g, ln_b)


def _conv_bwd_ln(proj, c, dy, ln_g, ln_b, W):
    T = proj.shape[0]
    tm = _pick(T, 256, SUBLANES)

    def body(g_ref, c_ref, dy_ref, lg_ref, lb_ref, dc_ref, dg_ref, st_ref):
        first = pl.program_id(0) == 0
        gv = g_ref[...]
        dyv = dy_ref[...]
        xh, r = _ln_stats(c_ref[...])
        lg = lg_ref[...]
        ln = xh * lg + lb_ref[...]
        dg_ref[...] = (dyv * _silu(ln) * _dsilu(gv)).astype(BF16)
        dln = dyv * _silu(gv) * _dsilu(ln)
        dc = _ln_bwd(dln * lg, xh, r)
        dc_ref[...] = dc
        _acc_rows(st_ref, 0, jnp.sum(dln * xh, axis=0, keepdims=True), first)
        st_ref[1:2, :] += jnp.sum(dln, axis=0, keepdims=True)
        st_ref[2:3, :] += jnp.sum(dc, axis=0, keepdims=True)

    row = pl.BlockSpec((tm, W), lambda i: (i, 0))
    vec = pl.BlockSpec((1, W), lambda i: (0, 0))
    return pl.pallas_call(body, name="conv_bwd_ln", grid=(T // tm,),
                          in_specs=[pl.BlockSpec((tm, W), lambda i: (i, 6)), row,
                                    pl.BlockSpec((None, tm, W), lambda i: (1, i, 0)), vec, vec],
                          out_specs=[row, row, pl.BlockSpec((SUBLANES, W), lambda i: (0, 0))],
                          out_shape=[_sds((T, W), F32), _sds((T, W), BF16), _sds((SUBLANES, W), F32)],
                          compiler_params=_cp("arbitrary"))(proj, c, dy, ln_g, ln_b)


def _conv_bwd_taps(proj, dc, conv_w, K, W):
    T = proj.shape[0]
    tm = _pick(T, 256, CONV_HALO)
    lead = CONV_HALO - (K - 1)
    per = tm // CONV_HALO
    nblk = T // tm
    cur, prev = _conv_specs(T, W, tm, 4)

    def body(a_ref, b_ref, ah_ref, bh_ref, dc_ref, dcn_ref, w_ref, da_ref, db_ref, dw_ref, glu, dcs):
        i = pl.program_id(0)
        av = a_ref[...]
        sb = _sigmoid(b_ref[...])
        glu[0:CONV_HALO, :] = jnp.where(i > 0, ah_ref[...] * _sigmoid(bh_ref[...]), 0.0)
        glu[CONV_HALO:, :] = av * sb
        dcv = dc_ref[...]
        dcs[0:tm, :] = dcv
        dcs[tm:, :] = jnp.where(i < nblk - 1, dcn_ref[...], 0.0)

        @pl.when(i == 0)
        def _():
            dw_ref[...] = jnp.zeros_like(dw_ref)

        dglu = jnp.zeros((tm, W), F32)
        for k in range(K):
            dglu = dglu + w_ref[k:k + 1, :] * dcs[K - 1 - k:K - 1 - k + tm, :]
            dw_ref[k:k + 1, :] += jnp.sum(dcv * glu[lead + k:lead + k + tm, :], axis=0, keepdims=True)
        da_ref[...] = (dglu * sb).astype(BF16)
        db_ref[...] = (dglu * av * sb * (1.0 - sb)).astype(BF16)

    row = pl.BlockSpec((tm, W), lambda i: (i, 0))
    nxt = pl.BlockSpec((CONV_HALO, W), lambda i: (jnp.minimum((i + 1) * per, T // CONV_HALO - 1), 0))
    return pl.pallas_call(body, name="conv_bwd_taps", grid=(nblk,),
                          in_specs=[cur(4), cur(5), prev(4), prev(5), row, nxt,
                                    pl.BlockSpec((CONV_HALO, W), lambda i: (0, 0))],
                          out_specs=[row, row, pl.BlockSpec((CONV_HALO, W), lambda i: (0, 0))],
                          out_shape=[_sds((T, W), BF16), _sds((T, W), BF16), _sds((CONV_HALO, W), F32)],
                          scratch_shapes=[pltpu.VMEM((tm + CONV_HALO, W), F32), pltpu.VMEM((tm + CONV_HALO, W), F32)],
                          compiler_params=_cp("arbitrary"))(proj, proj, proj, proj, dc, dc, conv_w)


def _sgu_common(cu, cv, lg, lb, w_ref, bt_ref, z_scr, G, nch):
    u = _gelu(cu)
    xh, r = _ln_stats(_gelu(cv))
    vn = (xh * lg + lb).astype(BF16)
    rr = lax.broadcasted_iota(jnp.int32, (LANES, LANES), 0)
    cc = lax.broadcasted_iota(jnp.int32, (LANES, LANES), 1)
    tril = rr >= cc
    wts = [jnp.where(tril, w_ref[g], 0.0).astype(BF16) for g in range(G)]
    for ch in range(nch):
        rs = slice(ch * LANES, (ch + 1) * LANES)
        for g in range(G):
            cs = slice(g * LANES, (g + 1) * LANES)
            z_scr[rs, cs] = jnp.dot(wts[g], vn[rs, cs], preferred_element_type=F32) + bt_ref[:, g:g + 1]
    return u, xh, r, vn, wts, tril


def _sgu_fwd(proj, sgu_w, sgu_bt, ln_g, ln_b, W):
    T = proj.shape[0]
    G = W // LANES
    tm = _pick(T, 256, LANES)
    nch = tm // LANES

    def body(u_ref, v_ref, g_ref, w_ref, bt_ref, lg_ref, lb_ref, y_ref, z_scr):
        u, *_ = _sgu_common(u_ref[...], v_ref[...], lg_ref[...], lb_ref[...], w_ref, bt_ref, z_scr, G, nch)
        y_ref[...] = (u * z_scr[...] * _silu(g_ref[...])).astype(BF16)

    cur = lambda c: pl.BlockSpec((tm, W), lambda i: (i, c))
    vec = pl.BlockSpec((1, W), lambda i: (0, 0))
    return pl.pallas_call(body, name="sgu_fwd", grid=(T // tm,),
                          in_specs=[cur(7), cur(8), cur(9), pl.BlockSpec((G, LANES, LANES), lambda i: (0, 0, 0)),
                                    pl.BlockSpec((LANES, G), lambda i: (0, 0)), vec, vec],
                          out_specs=pl.BlockSpec((tm, W), lambda i: (i, 0)), out_shape=_sds((T, W), BF16),
                          scratch_shapes=[pltpu.VMEM((tm, W), F32)],
                          compiler_params=_cp("parallel"))(proj, proj, proj, sgu_w, sgu_bt, ln_g, ln_b)


def _sgu_bwd(proj, dy, sgu_w, sgu_bt, ln_g, ln_b, W):
    T = proj.shape[0]
    G = W // LANES
    tm = _pick(T, 256, LANES)
    nch = tm // LANES

    def body(u_ref, v_ref, g_ref, dy_ref, w_ref, bt_ref, lg_ref, lb_ref,
             du_ref, dv_ref, dg_ref, dw_ref, dbt_ref, st_ref, z_scr, dvn_scr):
        first = pl.program_id(0) == 0
        cu, cv, gv, dyv = u_ref[...], v_ref[...], g_ref[...], dy_ref[...]
        lg = lg_ref[...]
        u, xh, r, vn, wts, tril = _sgu_common(cu, cv, lg, lb_ref[...], w_ref, bt_ref, z_scr, G, nch)
        z = z_scr[...]
        sg = _silu(gv)
        dg_ref[...] = (dyv * u * z * _dsilu(gv)).astype(BF16)
        du_ref[...] = (dyv * z * sg * _dgelu(cu)).astype(BF16)
        dz = dyv * u * sg
        dzb = dz.astype(BF16)

        @pl.when(first)
        def _():
            dw_ref[...] = jnp.zeros_like(dw_ref)
            dbt_ref[...] = jnp.zeros_like(dbt_ref)

        for g in range(G):
            cs = slice(g * LANES, (g + 1) * LANES)
            dwg = jnp.zeros((LANES, LANES), F32)
            dbg = jnp.zeros((LANES, 1), F32)
            for ch in range(nch):
                rs = slice(ch * LANES, (ch + 1) * LANES)
                dwg = dwg + lax.dot_general(dzb[rs, cs], vn[rs, cs], NT, preferred_element_type=F32)
                dbg = dbg + jnp.sum(dz[rs, cs], axis=1, keepdims=True)
                dvn_scr[rs, cs] = lax.dot_general(wts[g], dzb[rs, cs], TN, preferred_element_type=F32)
            dw_ref[g] += jnp.where(tril, dwg, 0.0)
            dbt_ref[:, g:g + 1] += dbg
        dvn = dvn_scr[...]
        _acc_rows(st_ref, 0, jnp.sum(dvn * xh, axis=0, keepdims=True), first)
        st_ref[1:2, :] += jnp.sum(dvn, axis=0, keepdims=True)
        dv_ref[...] = (_ln_bwd(dvn * lg, xh, r) * _dgelu(cv)).astype(BF16)

    cur = lambda c: pl.BlockSpec((tm, W), lambda i: (i, c))
    row = pl.BlockSpec((tm, W), lambda i: (i, 0))
    vec = pl.BlockSpec((1, W), lambda i: (0, 0))
    wspec = pl.BlockSpec((G, LANES, LANES), lambda i: (0, 0, 0))
    bspec = pl.BlockSpec((LANES, G), lambda i: (0, 0))
    return pl.pallas_call(body, name="sgu_bwd", grid=(T // tm,),
                          in_specs=[cur(7), cur(8), cur(9), pl.BlockSpec((None, tm, W), lambda i: (2, i, 0)),
                                    wspec, bspec, vec, vec],
                          out_specs=[row, row, row, wspec, bspec, pl.BlockSpec((SUBLANES, W), lambda i: (0, 0))],
                          out_shape=[_sds((T, W), BF16)] * 3 + [_sds((G, LANES, LANES), F32), _sds((LANES, G), F32),
                                                                 _sds((SUBLANES, W), F32)],
                          scratch_shapes=[pltpu.VMEM((tm, W), F32), pltpu.VMEM((tm, W), F32)],
                          compiler_params=_cp("arbitrary"))(proj, proj, proj, dy, sgu_w, sgu_bt, ln_g, ln_b)


def _rows(start, dil):
    return pl.ds(start, LANES, stride=dil) if dil > 1 else pl.ds(start, LANES)


def _dil_masks():
    a = lax.broadcasted_iota(jnp.int32, (LANES, LANES), 0)
    c = lax.broadcasted_iota(jnp.int32, (LANES, LANES), 1)
    return c <= a, c >= a


def _dil_geometry(T, dil):
    sbr = LANES * dil
    nb = max(1, min(DIL_STEP_ROWS, T) // sbr)
    return sbr, nb, T // (sbr * nb)


def _for_units(nb, dil, unit):
    for blk in range(nb):
        if dil <= DIL_UNROLL:
            for r in range(dil):
                unit(blk, r)
        else:
            def chunk(it, carry):
                for u in range(DIL_UNROLL):
                    unit(blk, it * DIL_UNROLL + u)
                return carry
            lax.fori_loop(0, dil // DIL_UNROLL, chunk, 0)


def _dil_fwd_group(proj, W, gi, dil):
    T = proj.shape[0]
    H = W // LANES
    sbr, nb, nsteps = _dil_geometry(T, dil)
    scale = LANES ** -0.5
    cq, ck, cv = (10 + gi) * H, (13 + gi) * H, 16 * H

    def body(q_ref, kc_ref, kp_ref, vc_ref, vp_ref, o_ref, l_ref):
        b = pl.program_id(1)
        m_cur, m_prev = _dil_masks()
        m_first = m_prev & (b > 0)

        def unit(blk, r):
            sl = _rows(blk * sbr + r, dil)
            if blk == 0:
                kp, vp, mp = kp_ref[_rows(r, dil), :], vp_ref[_rows(r, dil), :], m_first
            else:
                sp_ = _rows((blk - 1) * sbr + r, dil)
                kp, vp, mp = kc_ref[sp_, :], vc_ref[sp_, :], m_prev
            q = (q_ref[sl, :] * scale).astype(BF16)
            sc = lax.dot_general(q, kc_ref[sl, :].astype(BF16), NT, preferred_element_type=F32)
            sp = lax.dot_general(q, kp.astype(BF16), NT, preferred_element_type=F32)
            sc = jnp.where(m_cur, sc, NEG)
            sp = jnp.where(mp, sp, NEG)
            m = jnp.maximum(jnp.max(sc, axis=1, keepdims=True), jnp.max(sp, axis=1, keepdims=True))
            pc = jnp.exp(sc - m)
            pp = jnp.exp(sp - m)
            den = jnp.sum(pc, axis=1, keepdims=True) + jnp.sum(pp, axis=1, keepdims=True)
            pv = (jnp.dot(pc.astype(BF16), vc_ref[sl, :].astype(BF16), preferred_element_type=F32)
                  + jnp.dot(pp.astype(BF16), vp.astype(BF16), preferred_element_type=F32))
            o_ref[sl, :] = pv / den
            l_ref[sl, :] = jnp.broadcast_to(m + jnp.log(den), (LANES, LANES))

        _for_units(nb, dil, unit)

    cur = lambda c0: pl.BlockSpec((sbr * nb, LANES), lambda h, b: (b, c0 + h))
    prv = lambda c0: pl.BlockSpec((sbr, LANES), lambda h, b: (jnp.maximum(b * nb - 1, 0), c0 + h))
    out = pl.BlockSpec((sbr * nb, LANES), lambda h, b: (b, h))
    return pl.pallas_call(body, name=f"dil_fwd_g{gi}", grid=(H, nsteps),
                          in_specs=[cur(cq), cur(ck), prv(ck), cur(cv), prv(cv)], out_specs=[out, out],
                          out_shape=[_sds((T, W), F32), _sds((T, W), F32)],
                          compiler_params=_cp("parallel", "parallel"))(proj, proj, proj, proj, proj)


def _dil_combine(proj, os_, ls_, W):
    T = proj.shape[0]
    tm = _pick(T, 256, SUBLANES)

    def body(g_ref, o0, o1, o2, l0, l1, l2, od_ref, lse_ref, y_ref):
        a0, a1, a2 = l0[...], l1[...], l2[...]
        m = jnp.maximum(jnp.maximum(a0, a1), a2)
        e0, e1, e2 = jnp.exp(a0 - m), jnp.exp(a1 - m), jnp.exp(a2 - m)
        s = e0 + e1 + e2
        od = (e0 / s) * o0[...] + (e1 / s) * o1[...] + (e2 / s) * o2[...]
        od_ref[...] = od
        lse_ref[...] = m + jnp.log(s)
        y_ref[...] = (od * _silu(g_ref[...])).astype(BF16)

    row = pl.BlockSpec((tm, W), lambda i: (i, 0))
    return pl.pallas_call(body, name="dil_combine", grid=(T // tm,),
                          in_specs=[pl.BlockSpec((tm, W), lambda i: (i, 17))] + [row] * 6, out_specs=[row, row, row],
                          out_shape=[_sds((T, W), F32), _sds((T, W), F32), _sds((T, W), BF16)],
                          compiler_params=_cp("parallel"))(proj, *os_, *ls_)


def _dil_bwd_pre(proj, od, dy, W):
    T = proj.shape[0]
    H = W // LANES
    tm = _pick(T, 512, SUBLANES)

    def body(g_ref, od_ref, dy_ref, do_ref, dl_ref, dg_ref):
        gv, odv, dyv = g_ref[...], od_ref[...], dy_ref[...]
        do = dyv * _silu(gv)
        do_ref[...] = do
        dl_ref[...] = jnp.broadcast_to(jnp.sum(do * odv, axis=1, keepdims=True), (tm, LANES))
        dg_ref[...] = (dyv * odv * _dsilu(gv)).astype(BF16)

    blk = pl.BlockSpec((tm, LANES), lambda i, h: (i, h))
    return pl.pallas_call(body, name="dil_bwd_pre", grid=(T // tm, H),
                          in_specs=[pl.BlockSpec((tm, LANES), lambda i, h: (i, 17 * H + h)), blk,
                                    pl.BlockSpec((None, tm, LANES), lambda i, h: (3, i, h))],
                          out_specs=[blk, blk, blk],
                          out_shape=[_sds((T, W), F32), _sds((T, W), F32), _sds((T, W), BF16)],
                          compiler_params=_cp("parallel", "parallel"))(proj, od, dy)


def _dil_bwd_group(proj, do, lse, delta, W, gi, dil):
    T = proj.shape[0]
    H = W // LANES
    sbr, nb, nsteps = _dil_geometry(T, dil)
    scale = LANES ** -0.5
    cq, ck, cv = (10 + gi) * H, (13 + gi) * H, 16 * H

    def body(qc_ref, qn_ref, kc_ref, kp_ref, vc_ref, vp_ref, doc_ref, don_ref, lc_ref, ln_ref, dc_ref, dn_ref,
             dq_ref, dk_ref, dv_ref):
        b = pl.program_id(1)
        m_cur, m_prev = _dil_masks()
        m_first = m_prev & (b > 0)
        m_last = m_prev & (b < nsteps - 1)

        def probs(q, k, mask, l):
            s = lax.dot_general(q, k, NT, preferred_element_type=F32)
            return jnp.exp(jnp.where(mask, s - l, NEG))

        def unit(blk, r):
            sl = _rows(blk * sbr + r, dil)
            if blk == 0:
                edge = _rows(r, dil)
                k_p, v_p, m_cp = kp_ref[edge, :], vp_ref[edge, :], m_first
            else:
                sp_ = _rows((blk - 1) * sbr + r, dil)
                k_p, v_p, m_cp = kc_ref[sp_, :], vc_ref[sp_, :], m_prev
            if blk == nb - 1:
                edge = _rows(r, dil)
                q_n, do_n, l_n, d_n, m_nc = qn_ref[edge, :], don_ref[edge, :], ln_ref[edge, :], dn_ref[edge, :], m_last
            else:
                sn_ = _rows((blk + 1) * sbr + r, dil)
                q_n, do_n, l_n, d_n, m_nc = qc_ref[sn_, :], doc_ref[sn_, :], lc_ref[sn_, :], dc_ref[sn_, :], m_prev
            q_c = (qc_ref[sl, :] * scale).astype(BF16)
            q_n = (q_n * scale).astype(BF16)
            k_c, k_p = kc_ref[sl, :].astype(BF16), k_p.astype(BF16)
            v_c, v_p = vc_ref[sl, :].astype(BF16), v_p.astype(BF16)
            do_c, do_n = doc_ref[sl, :].astype(BF16), do_n.astype(BF16)
            l_c, d_c = lc_ref[sl, :], dc_ref[sl, :]
            p_cc = probs(q_c, k_c, m_cur, l_c)
            p_cp = probs(q_c, k_p, m_cp, l_c)
            p_nc = probs(q_n, k_c, m_nc, l_n)
            ds_cc = (p_cc * (lax.dot_general(do_c, v_c, NT, preferred_element_type=F32) - d_c)).astype(BF16)
            ds_cp = (p_cp * (lax.dot_general(do_c, v_p, NT, preferred_element_type=F32) - d_c)).astype(BF16)
            ds_nc = (p_nc * (lax.dot_general(do_n, v_c, NT, preferred_element_type=F32) - d_n)).astype(BF16)
            dq = (jnp.dot(ds_cc, k_c, preferred_element_type=F32) + jnp.dot(ds_cp, k_p, preferred_element_type=F32))
            dk = (lax.dot_general(ds_cc, q_c, TN, preferred_element_type=F32)
                  + lax.dot_general(ds_nc, q_n, TN, preferred_element_type=F32))
            dv = (lax.dot_general(p_cc.astype(BF16), do_c, TN, preferred_element_type=F32)
                  + lax.dot_general(p_nc.astype(BF16), do_n, TN, preferred_element_type=F32))
            dq_ref[sl, :] = dq * scale
            dk_ref[sl, :] = dk
            dv_ref[sl, :] = dv

        _for_units(nb, dil, unit)

    cur = lambda c0: pl.BlockSpec((sbr * nb, LANES), lambda h, b: (b, c0 + h))
    prv = lambda c0: pl.BlockSpec((sbr, LANES), lambda h, b: (jnp.maximum(b * nb - 1, 0), c0 + h))
    nxt = lambda c0: pl.BlockSpec((sbr, LANES), lambda h, b: (jnp.minimum((b + 1) * nb, T // sbr - 1), c0 + h))
    out = pl.BlockSpec((sbr * nb, LANES), lambda h, b: (b, h))
    return pl.pallas_call(body, name=f"dil_bwd_g{gi}", grid=(H, nsteps),
                          in_specs=[cur(cq), nxt(cq), cur(ck), prv(ck), cur(cv), prv(cv),
                                    cur(0), nxt(0), cur(0), nxt(0), cur(0), nxt(0)],
                          out_specs=[out, out, out], out_shape=[_sds((T, W), F32)] * 3,
                          compiler_params=_cp("parallel", "parallel"))(
        proj, proj, proj, proj, proj, proj, do, do, lse, lse, delta, delta)


def _sum3_bf16(a, b, c):
    T, W = a.shape
    tm = _pick(T, 512, SUBLANES)

    def body(a_ref, b_ref, c_ref, o_ref):
        o_ref[...] = (a_ref[...] + b_ref[...] + c_ref[...]).astype(BF16)

    row = pl.BlockSpec((tm, W), lambda i: (i, 0))
    return pl.pallas_call(body, name="dil_dv_sum", grid=(T // tm,), in_specs=[row, row, row], out_specs=row,
                          out_shape=_sds((T, W), BF16), compiler_params=_cp("parallel"))(a, b, c)


def _to_bf16(name, parts):
    T, W = parts[0].shape
    n = len(parts)
    tm = _pick(T, 512, SUBLANES)

    def body(*refs):
        o_ref = refs[n]
        for p in range(n):
            o_ref[:, p * W:(p + 1) * W] = refs[p][...].astype(BF16)

    row = pl.BlockSpec((tm, W), lambda i: (i, 0))
    return pl.pallas_call(body, name=name, grid=(T // tm,), in_specs=[row] * n,
                          out_specs=pl.BlockSpec((tm, n * W), lambda i: (i, 0)),
                          out_shape=_sds((T, n * W), BF16), compiler_params=_cp("parallel"))(*parts)


def _branch_merge(ys, w_br, gates):
    NB, T, W = ys.shape
    D = w_br.shape[2]
    tm, tn = _pick(T, 1024, SUBLANES), _pick(D, 1024)
    nj = D // tn

    def body(y_ref, w_ref, g_ref, yp_ref, m_ref, acc):
        n = pl.program_id(2)
        yp = jnp.dot(y_ref[...], w_ref[...], preferred_element_type=F32)
        yp_ref[...] = yp

        @pl.when(n == 0)
        def _():
            acc[...] = jnp.zeros_like(acc)

        acc[...] += g_ref[...] * yp

        @pl.when(n == NB - 1)
        def _():
            m_ref[...] = acc[...].astype(BF16)

    return pl.pallas_call(
        body, name="branch_merge", grid=(T // tm, nj, NB),
        in_specs=[pl.BlockSpec((None, tm, W), lambda i, j, n: (n, i, 0)),
                  pl.BlockSpec((None, W, tn), lambda i, j, n: (n, 0, j)),
                  pl.BlockSpec((tm, tn), lambda i, j, n: (i, n * nj + j))],
        out_specs=[pl.BlockSpec((None, tm, tn), lambda i, j, n: (n, i, j)),
                   pl.BlockSpec((tm, tn), lambda i, j, n: (i, j))],
        out_shape=[_sds((NB, T, D), F32), _sds((T, D), BF16)],
        scratch_shapes=[pltpu.VMEM((tm, tn), F32)],
        compiler_params=_cp("parallel", "parallel", "arbitrary"))(ys, w_br, gates)


def _merge_bwd(dmerged, gates, yproj):
    NB, T, D = yproj.shape
    tm = _pick(T, 256, SUBLANES)

    def body(dm_ref, g_ref, yp_ref, dyp_ref, dz_ref, db_ref):
        dm, g = dm_ref[...], g_ref[...]
        dyp_ref[...] = (dm * g).astype(BF16)
        dz = dm * yp_ref[...] * g * (1.0 - g)
        dz_ref[...] = dz.astype(BF16)
        _acc_rows(db_ref, 0, jnp.sum(dz, axis=0, keepdims=True), pl.program_id(1) == 0)

    return pl.pallas_call(
        body, name="merge_bwd", grid=(NB, T // tm),
        in_specs=[pl.BlockSpec((tm, D), lambda n, i: (i, 0)), pl.BlockSpec((tm, D), lambda n, i: (i, n)),
                  pl.BlockSpec((None, tm, D), lambda n, i: (n, i, 0))],
        out_specs=[pl.BlockSpec((None, tm, D), lambda n, i: (n, i, 0)), pl.BlockSpec((tm, D), lambda n, i: (i, n)),
                   pl.BlockSpec((SUBLANES, D), lambda n, i: (0, n))],
        out_shape=[_sds((NB, T, D), BF16), _sds((T, NB * D), BF16), _sds((SUBLANES, NB * D), F32)],
        compiler_params=_cp("parallel", "arbitrary"))(dmerged, gates, yproj)


def _branch_bwd_dy(dyp, w_br):
    NB, T, D = dyp.shape
    W = w_br.shape[1]
    tm = _pick(T, 1024, SUBLANES)

    def body(a_ref, w_ref, o_ref):
        o_ref[...] = lax.dot_general(a_ref[...], w_ref[...], NT, preferred_element_type=F32)

    return pl.pallas_call(body, name="branch_bwd_dy", grid=(NB, T // tm),
                          in_specs=[pl.BlockSpec((None, tm, D), lambda n, i: (n, i, 0)),
                                    pl.BlockSpec((None, W, D), lambda n, i: (n, 0, 0))],
                          out_specs=pl.BlockSpec((None, tm, W), lambda n, i: (n, i, 0)),
                          out_shape=_sds((NB, T, W), F32), compiler_params=_cp("parallel", "parallel"))(dyp, w_br)


def _branch_bwd_dw(ys, dyp):
    NB, T, W = ys.shape
    D = dyp.shape[2]
    tm, tn, tk = _pick(W, 1024), _pick(D, 1024), _pick(T, 1024, SUBLANES)
    return _mm("branch_bwd_dw", ys, dyp, grid=(NB, W // tm, D // tn, T // tk), kaxis=3, dims=TN,
               a_spec=pl.BlockSpec((None, tk, tm), lambda n, i, j, k: (n, k, i)),
               b_spec=pl.BlockSpec((None, tk, tn), lambda n, i, j, k: (n, k, j)),
               acc_shape=(tm, tn), out_shape=_sds((NB, W, D), F32),
               out_specs=pl.BlockSpec((None, tm, tn), lambda n, i, j, k: (n, i, j)),
               epilogue=_store(F32), sem=("parallel", "parallel", "parallel", "arbitrary"))


def _layer_fwd(x, wl, sp):
    W = sp["conv_b"].shape[-1]
    D = x.shape[1]
    h = _rmsnorm_fwd(x, sp["norm_g"])
    proj = _mm_nn("proj", h, wl["w_in"], F32, tn=768)

    def gate_ep(acc, ex, outs):
        outs[0][...] = _sigmoid(acc + ex[0][...])

    tn_g = _pick(4 * D, 1024)
    gates = _mm_nn("gates", h, wl["w_gate"], F32, epilogue=gate_ep, extras=(sp["b_gate"],),
                   extra_specs=(pl.BlockSpec((1, tn_g), lambda i, j, k: (0, j)),), tn=tn_g)
    oa, ya, sb_carries = _sb_fwd(proj, W)
    cpre, yb = _conv_fwd(proj, wl["conv_w"], wl["taps"], sp["conv_b"], sp["conv_ln_g"], sp["conv_ln_b"], W)
    yc = _sgu_fwd(proj, sp["sgu_w"], sp["sgu_bt"], sp["sgu_ln_g"], sp["sgu_ln_b"], W)
    os_, ls_ = zip(*[_dil_fwd_group(proj, W, gi, dil) for gi, (_, dil) in enumerate(DIL_PATTERNS)])
    od, lse, yd = _dil_combine(proj, os_, ls_, W)
    ys = jnp.stack([ya, yb, yc, yd])
    yproj, merged = _branch_merge(ys, wl["w_br"], gates)

    def res_ep(acc, ex, outs):
        outs[0][...] = ex[0][...] + acc

    tm_o, tn_o = _pick(x.shape[0], 1024, SUBLANES), _pick(D, 1024)
    xn = _mm_nn("out_proj", merged, wl["w_out"], F32, epilogue=res_ep, extras=(x,),
                extra_specs=(pl.BlockSpec((tm_o, tn_o), lambda i, j, k: (i, j)),), tm=tm_o, tn=tn_o)
    saved = dict(x=x, h=h, proj=proj, gates=gates, oa=oa, sb_carries=sb_carries, cpre=cpre, od=od, lse=lse, ys=ys, yproj=yproj, merged=merged)
    return xn, saved


def _layer_bwd(dout, sv, wl, sp):
    W = sp["conv_b"].shape[-1]
    proj = sv["proj"]
    dmerged = _mm_nt("out_proj_bwd_dx", dout, wl["w_out"], F32, tm=512)
    g_w_out = _mm_tn("out_proj_bwd_dw", sv["merged"], dout)
    dyp, dzg, db_gate = _merge_bwd(dmerged, sv["gates"], sv["yproj"])
    dy = _branch_bwd_dy(dyp, wl["w_br"])
    g_w_br = _branch_bwd_dw(sv["ys"], dyp)
    g_w_gate = _mm_tn("gate_bwd_dw", sv["h"], dzg)

    a_dq, a_dk, a_dv, a_dg = _sb_bwd(proj, sv["oa"], sv["sb_carries"], dy, W)
    a_qkv = _to_bf16("sb_bwd_cast", [a_dq, a_dk, a_dv])

    dc, b_dg, conv_stats = _conv_bwd_ln(proj, sv["cpre"], dy, sp["conv_ln_g"], sp["conv_ln_b"], W)
    b_da, b_db, g_conv_w = _conv_bwd_taps(proj, dc, wl["conv_w"], wl["taps"], W)

    c_du, c_dv, c_dg, g_sgu_w, g_sgu_bt, sgu_stats = _sgu_bwd(proj, dy, sp["sgu_w"], sp["sgu_bt"],
                                                              sp["sgu_ln_g"], sp["sgu_ln_b"], W)

    do, delta, d_dg = _dil_bwd_pre(proj, sv["od"], dy, W)
    dqs, dks, dvs = zip(*[_dil_bwd_group(proj, do, sv["lse"], delta, W, gi, dil)
                          for gi, (_, dil) in enumerate(DIL_PATTERNS)])
    d_qk = _to_bf16("dil_bwd_cast", [*dqs, *dks])
    d_dv = _sum3_bf16(*dvs)

    dproj = jnp.concatenate([a_qkv, a_dg, b_da, b_db, b_dg, c_du, c_dv, c_dg, d_qk, d_dv, d_dg], axis=1)
    g_w_in = _mm_tn("proj_bwd_dw", sv["h"], dproj, tn=768)
    dh = _mm_nt("gate_bwd_dh", dzg, wl["w_gate"], F32)
    dh = _mm_nt("proj_bwd_dh", dproj, wl["w_in"], F32, addend=dh)
    dx, dnorm = _rmsnorm_bwd(sv["x"], dh, dout, sp["norm_g"])

    K = wl["taps"]
    small = dict(norm_g=dnorm[0], conv_w=g_conv_w[:K], conv_b=conv_stats[2], conv_ln_g=conv_stats[0],
                 conv_ln_b=conv_stats[1], sgu_ln_g=sgu_stats[0], sgu_ln_b=sgu_stats[1], sgu_w=g_sgu_w,
                 sgu_b=g_sgu_bt.T, b_gate=db_gate[0])
    big = dict(w_in=g_w_in, w_gate=g_w_gate, w_br=g_w_br, w_out=g_w_out)
    return dx, big, small


HBM = pl.BlockSpec(memory_space=pl.ANY)


def _mesh_pos():
    return lax.axis_index("x"), lax.axis_index("y"), lax.axis_index("c")


def _other_chips(x, y):
    return [(1 - x, y), (x, 1 - y), (1 - x, 1 - y)]


def _shard_of(ref, axis, size, index):
    idx = [slice(None)] * len(ref.shape)
    idx[axis] = pl.ds(index * size, size)
    return ref.at[tuple(idx)]


def _all_gather(name, shards, axes):
    n = len(shards)
    out_shape = []
    for s, ax in zip(shards, axes):
        shp = list(s.shape)
        shp[ax] *= N_DEV
        out_shape.append(_sds(shp, s.dtype))

    def body(*refs):
        ins, outs = refs[:n], refs[n:2 * n]
        send, recv, lsem = refs[2 * n:]
        x, y, c = _mesh_pos()
        me, sib = (x, y, c), (x, y, 1 - c)
        chips = _other_chips(x, y)
        dev = lambda px, py, pc: 4 * px + 2 * py + pc

        def blk(a, d):
            return _shard_of(outs[a], axes[a], ins[a].shape[axes[a]], d)

        def cp(a, k, d, to, src=None):
            return pltpu.make_async_remote_copy(src_ref=blk(a, d) if src is None else src, dst_ref=blk(a, d),
                                                send_sem=send.at[a * 7 + k], recv_sem=recv.at[a * 7 + k],
                                                device_id=to, device_id_type=MESH)

        own = [pltpu.make_async_copy(ins[a], blk(a, dev(*me)), lsem.at[a]) for a in range(n)]
        for o in own:
            o.start()
        first = []
        for a in range(n):
            first.append(cp(a, 0, dev(*me), sib, src=ins[a]))
            first += [cp(a, 1 + j, dev(*me), (*ch, c), src=ins[a]) for j, ch in enumerate(chips)]
        for f in first:
            f.start()
        passed = []
        for j, ch in enumerate(chips):
            for a in range(n):
                cp(a, 1 + j, dev(*ch, c), me).wait_recv()
                p = cp(a, 4 + j, dev(*ch, c), sib)
                p.start()
                passed.append(p)
        for a in range(n):
            cp(a, 0, dev(*sib), me).wait_recv()
            for j, ch in enumerate(chips):
                cp(a, 4 + j, dev(*ch, 1 - c), me).wait_recv()
        for f in first + passed:
            f.wait_send()
        for o in own:
            o.wait()

    return pl.pallas_call(body, name=name, in_specs=[HBM] * n, out_specs=[HBM] * n, out_shape=out_shape,
                          scratch_shapes=[pltpu.SemaphoreType.DMA((7 * n,)), pltpu.SemaphoreType.DMA((7 * n,)),
                                          pltpu.SemaphoreType.DMA((n,))])(*shards)


def _rs_pair(name, grads, axes):
    n = len(grads)
    sizes = [g.shape[ax] // N_DEV for g, ax in zip(grads, axes)]
    out_shape = []
    for g, ax, sz in zip(grads, axes, sizes):
        shp = list(g.shape)
        shp[ax] = sz
        out_shape.append(_sds([N_CHIP] + shp, g.dtype))

    def body(*refs):
        ins, outs = refs[:n], refs[n:2 * n]
        send, recv = refs[2 * n:]
        x, y, c = _mesh_pos()
        cps = []
        for a in range(n):
            for k in range(N_CHIP):
                cps.append(pltpu.make_async_remote_copy(
                    src_ref=_shard_of(ins[a], axes[a], sizes[a], 2 * k + (1 - c)), dst_ref=outs[a].at[k],
                    send_sem=send.at[a * N_CHIP + k], recv_sem=recv.at[a * N_CHIP + k],
                    device_id=(x, y, 1 - c), device_id_type=MESH))
        for cp in cps:
            cp.start()
        for cp in cps:
            cp.wait()

    return pl.pallas_call(body, name=name, in_specs=[HBM] * n, out_specs=[HBM] * n, out_shape=out_shape,
                          scratch_shapes=[pltpu.SemaphoreType.DMA((N_CHIP * n,)),
                                          pltpu.SemaphoreType.DMA((N_CHIP * n,))])(*grads)


def _pair_sum(name, grad, recv, axis, core):
    R, C = grad.shape
    _, r, cw = recv.shape
    tr = _pick(r, 512, SUBLANES)
    nr = r // tr

    def body(c_ref, g_ref, r_ref, o_ref):
        o_ref[...] = (g_ref[...] + r_ref[...]).astype(BF16)

    if axis == 1:
        gspec = pl.BlockSpec((tr, cw), lambda k, i, c_ref: (i, 2 * k + c_ref[0]))
    else:
        gspec = pl.BlockSpec((tr, cw), lambda k, i, c_ref: ((2 * k + c_ref[0]) * nr + i, 0))
    part = pl.BlockSpec((None, tr, cw), lambda k, i, c_ref: (k, i, 0))
    return pl.pallas_call(
        body, name=name, out_shape=_sds((N_CHIP, r, cw), BF16),
        grid_spec=pltpu.PrefetchScalarGridSpec(num_scalar_prefetch=1, grid=(N_CHIP, nr), in_specs=[gspec, part],
                                               out_specs=part),
        compiler_params=_cp("parallel", "parallel"))(core, grad, recv)


def _rs_chips(name, parts):
    n = len(parts)

    def body(*refs):
        ins, outs = refs[:n], refs[n:2 * n]
        send, recv, lsem = refs[2 * n:]
        x, y, c = _mesh_pos()
        mine = 2 * x + y
        own = [pltpu.make_async_copy(ins[a].at[mine], outs[a].at[mine], lsem.at[a]) for a in range(n)]
        for o in own:
            o.start()
        cps = []
        for a in range(n):
            for j, (px, py) in enumerate(_other_chips(x, y)):
                cps.append(pltpu.make_async_remote_copy(
                    src_ref=ins[a].at[2 * px + py], dst_ref=outs[a].at[mine],
                    send_sem=send.at[a * 3 + j], recv_sem=recv.at[a * 3 + j],
                    device_id=(px, py, c), device_id_type=MESH))
        for cp in cps:
            cp.start()
        for a in range(n):
            for j, (px, py) in enumerate(_other_chips(x, y)):
                pltpu.make_async_remote_copy(
                    src_ref=ins[a].at[mine], dst_ref=outs[a].at[2 * px + py],
                    send_sem=send.at[a * 3 + j], recv_sem=recv.at[a * 3 + j],
                    device_id=(x, y, c), device_id_type=MESH).wait_recv()
        for cp in cps:
            cp.wait_send()
        for o in own:
            o.wait()

    return pl.pallas_call(body, name=name, in_specs=[HBM] * n, out_specs=[HBM] * n,
                          out_shape=[_sds(p.shape, p.dtype) for p in parts],
                          scratch_shapes=[pltpu.SemaphoreType.DMA((3 * n,)), pltpu.SemaphoreType.DMA((3 * n,)),
                                          pltpu.SemaphoreType.DMA((n,))])(*parts)


def _adamw_sum(name, parts, w, m, v):
    P, R, C = parts.shape
    tr = _pick(R, max(SUBLANES, (1 << 19) // C // SUBLANES * SUBLANES), SUBLANES)

    def body(p_ref, w_ref, m_ref, v_ref, g_ref, d_ref, mo_ref, vo_ref):
        g = p_ref[0].astype(F32)
        for k in range(1, P):
            g = g + p_ref[k].astype(F32)
        mn = ADAM_B1 * m_ref[...] + (1.0 - ADAM_B1) * g
        vn = ADAM_B2 * v_ref[...] + (1.0 - ADAM_B2) * (g * g)
        m_hat = mn / (1.0 - ADAM_B1 ** ADAM_STEP)
        v_hat = vn / (1.0 - ADAM_B2 ** ADAM_STEP)
        g_ref[...] = g
        d_ref[...] = -ADAM_LR * (m_hat / (jnp.sqrt(v_hat) + ADAM_EPS) + ADAM_WD * w_ref[...])
        mo_ref[...] = mn
        vo_ref[...] = vn

    row = pl.BlockSpec((tr, C), lambda i: (i, 0))
    return pl.pallas_call(body, name=name, grid=(R // tr,),
                          in_specs=[pl.BlockSpec((P, tr, C), lambda i: (0, i, 0)), row, row, row],
                          out_specs=[row] * 4, out_shape=[_sds((R, C), F32)] * 4,
                          compiler_params=_cp("parallel"))(parts, w, m, v)


def _rows128(a, pad_rows=SUBLANES):
    flat = a.reshape(-1, LANES)
    pad = (-flat.shape[0]) % pad_rows
    return jnp.pad(flat, ((0, pad), (0, 0))) if pad else flat


SMALL = ("norm_g", "conv_b", "conv_ln_g", "conv_ln_b", "sgu_ln_g", "sgu_ln_b", "sgu_w", "sgu_b", "b_gate", "final_g")


def kernel(x, norm_g, w_in, conv_w, conv_b, conv_ln_g, conv_ln_b, sgu_ln_g, sgu_ln_b, sgu_w, sgu_b, w_branch, w_gate, b_gate, w_out, final_g, loss_target, m_norm_g, m_w_in, m_conv_w, m_conv_b, m_conv_ln_g, m_conv_ln_b, m_sgu_ln_g, m_sgu_ln_b, m_sgu_w, m_sgu_b, m_w_branch, m_w_gate, m_b_gate, m_w_out, m_final_g, v_norm_g, v_w_in, v_conv_w, v_conv_b, v_conv_ln_g, v_conv_ln_b, v_sgu_ln_g, v_sgu_ln_b, v_sgu_w, v_sgu_b, v_w_branch, v_w_gate, v_b_gate, v_w_out, v_final_g):
    L, D = norm_g.shape
    W = conv_b.shape[1]
    taps = conv_w.shape[1]
    weights = dict(norm_g=norm_g, w_in=w_in, conv_w=conv_w, conv_b=conv_b, conv_ln_g=conv_ln_g, conv_ln_b=conv_ln_b,
                   sgu_ln_g=sgu_ln_g, sgu_ln_b=sgu_ln_b, sgu_w=sgu_w, sgu_b=sgu_b, w_branch=w_branch, w_gate=w_gate,
                   b_gate=b_gate, w_out=w_out, final_g=final_g)
    mom_m = dict(norm_g=m_norm_g, w_in=m_w_in, conv_w=m_conv_w, conv_b=m_conv_b, conv_ln_g=m_conv_ln_g,
                 conv_ln_b=m_conv_ln_b, sgu_ln_g=m_sgu_ln_g, sgu_ln_b=m_sgu_ln_b, sgu_w=m_sgu_w, sgu_b=m_sgu_b,
                 w_branch=m_w_branch, w_gate=m_w_gate, b_gate=m_b_gate, w_out=m_w_out, final_g=m_final_g)
    mom_v = dict(norm_g=v_norm_g, w_in=v_w_in, conv_w=v_conv_w, conv_b=v_conv_b, conv_ln_g=v_conv_ln_g,
                 conv_ln_b=v_conv_ln_b, sgu_ln_g=v_sgu_ln_g, sgu_ln_b=v_sgu_ln_b, sgu_w=v_sgu_w, sgu_b=v_sgu_b,
                 w_branch=v_w_branch, w_gate=v_w_gate, b_gate=v_b_gate, w_out=v_w_out, final_g=v_final_g)
    core = lax.axis_index("c").astype(jnp.int32).reshape(1)
    me = 4 * lax.axis_index("x") + 2 * lax.axis_index("y") + lax.axis_index("c")

    xs = x[0]
    saved, gathered, smalls = [], [], []
    for l in range(L):
        w_in_f, w_gate_f, w_br_f, w_out_f, conv_w_f = _all_gather(
            "gather_weights",
            [w_in[l].astype(BF16), w_gate[l].astype(BF16), w_branch[l].astype(BF16), w_out[l].astype(BF16),
             jnp.pad(conv_w[l], ((0, CONV_HALO - taps), (0, 0)))],
            [1, 1, 2, 0, 1])
        wl = dict(w_in=w_in_f, w_gate=w_gate_f, w_br=w_br_f, w_out=w_out_f, conv_w=conv_w_f, taps=taps)
        sp = dict(norm_g=norm_g[l][None], conv_b=conv_b[l][None], conv_ln_g=conv_ln_g[l][None],
                  conv_ln_b=conv_ln_b[l][None], sgu_ln_g=sgu_ln_g[l][None], sgu_ln_b=sgu_ln_b[l][None],
                  sgu_w=sgu_w[l], sgu_bt=sgu_b[l].T, b_gate=b_gate[l][None])
        xs, sv = _layer_fwd(xs, wl, sp)
        saved.append(sv)
        gathered.append(wl)
        smalls.append(sp)

    dx, d_final, loss_part = _loss_head(xs, loss_target[0], final_g[None])
    loss = lax.psum(loss_part[0, 0], ("x", "y", "c"))

    big_names = ("w_in", "w_gate", "w_branch", "w_out")
    big_axes = (1, 1, 1, 0)
    outs = {}
    small_grads = []
    for l in reversed(range(L)):
        dx, big, small = _layer_bwd(dx, saved[l], gathered[l], smalls[l])
        small_grads.append(small)
        g2d = [big["w_in"], big["w_gate"], big["w_br"].reshape(-1, D), big["w_out"]]
        recv = _rs_pair("reduce_pair", g2d, big_axes)
        parts = [_pair_sum("pair_sum_" + nm, g, r, ax, core) for nm, g, r, ax in zip(big_names, g2d, recv, big_axes)]
        total = _rs_chips("reduce_chips", parts)
        for nm, tot in zip(big_names, total):
            shard2d = tot.shape[1:]
            res = _adamw_sum("adamw_" + nm, tot, weights[nm][l].reshape(shard2d), mom_m[nm][l].reshape(shard2d),
                             mom_v[nm][l].reshape(shard2d))
            outs.setdefault(nm, []).append(res)
    small_grads.reverse()
    big_out = {nm: [jnp.stack([outs[nm][L - 1 - l][q] for l in range(L)]).reshape(weights[nm].shape) for q in range(4)]
               for nm in big_names}

    sg = {nm: jnp.stack([small_grads[l][nm] for l in range(L)]) for nm in SMALL[:-1]}
    sg["final_g"] = d_final[0]
    conv_w_full = jnp.stack([small_grads[l]["conv_w"] for l in range(L)])
    segs = [_rows128(sg[nm]) for nm in SMALL] + [_rows128(conv_w_full)]
    offs = [0]
    for s in segs:
        offs.append(offs[-1] + s.shape[0])
    pack = jnp.concatenate(segs, axis=0)
    (allp,) = _all_gather("gather_small_grads", [pack[None]], [0])

    def packed(src):
        return jnp.concatenate([_rows128(src[nm]) for nm in SMALL] + [jnp.zeros_like(segs[-1])], axis=0)

    s_g, s_d, s_m, s_v = _adamw_sum("adamw_small", allp, packed(weights), packed(mom_m), packed(mom_v))

    def unpack(buf, i, like):
        n = like.size // LANES
        return buf[offs[i]:offs[i] + n].reshape(like.shape)

    small_out = {nm: [unpack(b, i, weights[nm]) for b in (s_g, s_d, s_m, s_v)] for i, nm in enumerate(SMALL)}
    Wc = conv_w.shape[2]
    cw_sum = lax.dynamic_slice_in_dim(unpack(s_g, len(SMALL), conv_w_full), me * Wc, Wc, axis=2)
    cshape = (L * conv_w.shape[1], Wc)
    conv_out = [o.reshape(conv_w.shape) for o in _adamw_sum(
        "adamw_conv_w", cw_sum.reshape((1,) + cshape), conv_w.reshape(cshape), m_conv_w.reshape(cshape),
        v_conv_w.reshape(cshape))]

    order = ["norm_g", "w_in", "conv_w", "conv_b", "conv_ln_g", "conv_ln_b", "sgu_ln_g", "sgu_ln_b", "sgu_w", "sgu_b",
             "w_branch", "w_gate", "b_gate", "w_out", "final_g"]
    table = dict(small_out)
    table.update(big_out)
    table["conv_w"] = conv_out
    result = [loss, dx[None]]
    for q in range(4):
        result += [table[nm][q] for nm in order]
    return tuple(result)
```

```python
import functools

import jax
import jax.numpy as jnp
from jax import lax
from jax.experimental import pallas as pl
from jax.experimental.pallas import tpu as pltpu

F32 = jnp.float32
BF16 = jnp.bfloat16
LANES = 128
SUBLANES = 8
CONV_HALO = 32
SB_KEY_BLOCK = 512
DIL_STEP_ROWS = 2048
DIL_UNROLL = 4
NORM_EPS = 1e-6
NEG = -1e30
N_DEV = 8
N_CHIP = 4
DIL_PATTERNS = ((128, 1), (512, 4), (2048, 16))

ADAM_LR = 0.001
ADAM_B1 = 0.9
ADAM_B2 = 0.999
ADAM_EPS = 1e-08
ADAM_WD = 0.01
ADAM_STEP = 10

MESH = pl.DeviceIdType.MESH
NN = (((1,), (0,)), ((), ()))
NT = (((1,), (1,)), ((), ()))
TN = (((0,), (0,)), ((), ()))
VMEM_LIMIT = 52 << 20


def _sds(shape, dtype):
    return jax.ShapeDtypeStruct(tuple(shape), dtype)


def _cp(*sem):
    return pltpu.CompilerParams(dimension_semantics=tuple(sem), vmem_limit_bytes=VMEM_LIMIT)


def _pick(n, target, quantum=LANES):
    if n <= target:
        return n
    t = (target // quantum) * quantum
    while t >= quantum:
        if n % t == 0:
            return t
        t -= quantum
    return n


def _sigmoid(x):
    return 1.0 / (1.0 + jnp.exp(-x))


def _silu(x):
    return x * _sigmoid(x)


def _dsilu(x):
    s = _sigmoid(x)
    return s * (1.0 + x * (1.0 - s))


_GELU_K = 0.7978845608028654
_GELU_A = 0.044715


def _gelu(x):
    return 0.5 * x * (1.0 + jnp.tanh(_GELU_K * (x + _GELU_A * x * x * x)))


def _dgelu(x):
    t = jnp.tanh(_GELU_K * (x + _GELU_A * x * x * x))
    return 0.5 * (1.0 + t) + 0.5 * x * (1.0 - t * t) * _GELU_K * (1.0 + 3.0 * _GELU_A * x * x)


def _ln_stats(v):
    mu = jnp.mean(v, axis=-1, keepdims=True)
    d = v - mu
    var = jnp.mean(d * d, axis=-1, keepdims=True)
    r = lax.rsqrt(var + NORM_EPS)
    return d * r, r


def _ln_bwd(dxh, xh, r):
    return r * (dxh - jnp.mean(dxh, axis=-1, keepdims=True) - xh * jnp.mean(dxh * xh, axis=-1, keepdims=True))


def _acc_rows(ref, row, val, first):
    @pl.when(first)
    def _():
        ref[...] = jnp.zeros_like(ref)
    ref[row:row + 1, :] += val


def _mm(name, a, b, *, grid, kaxis, dims, a_spec, b_spec, acc_shape, out_shape, out_specs,
        epilogue, extras=(), extra_specs=(), sem):
    nk = grid[kaxis]
    ne = len(extras)

    def body(*refs):
        a_ref, b_ref = refs[0], refs[1]
        ex = refs[2:2 + ne]
        outs = refs[2 + ne:-1]
        acc = refs[-1]
        k = pl.program_id(kaxis)

        @pl.when(k == 0)
        def _():
            acc[...] = jnp.zeros_like(acc)

        acc[...] += lax.dot_general(a_ref[...].astype(BF16), b_ref[...].astype(BF16), dims,
                                    preferred_element_type=F32)

        @pl.when(k == nk - 1)
        def _():
            epilogue(acc[...], ex, outs)

    return pl.pallas_call(
        body, name=name, grid=grid, in_specs=[a_spec, b_spec, *extra_specs], out_specs=out_specs,
        out_shape=out_shape, scratch_shapes=[pltpu.VMEM(acc_shape, F32)], compiler_params=_cp(*sem),
    )(a, b, *extras)


def _store(dtype):
    def ep(acc, ex, outs):
        outs[0][...] = acc.astype(dtype)
    return ep


def _mm_nn(name, a, b, out_dtype, epilogue=None, extras=(), extra_specs=(), tm=1024, tn=1024):
    M, K = a.shape
    N = b.shape[1]
    tm, tn = _pick(M, tm, SUBLANES), _pick(N, tn)
    return _mm(name, a, b, grid=(M // tm, N // tn, 1), kaxis=2, dims=NN,
               a_spec=pl.BlockSpec((tm, K), lambda i, j, k: (i, 0)),
               b_spec=pl.BlockSpec((K, tn), lambda i, j, k: (0, j)),
               acc_shape=(tm, tn), out_shape=_sds((M, N), out_dtype),
               out_specs=pl.BlockSpec((tm, tn), lambda i, j, k: (i, j)),
               epilogue=epilogue or _store(out_dtype), extras=extras, extra_specs=extra_specs,
               sem=("parallel", "parallel", "arbitrary"))


def _mm_nt(name, a, b, out_dtype, addend=None, tm=1024, tn=1024, tk=1024):
    M, K = a.shape
    N = b.shape[0]
    tm, tn, tk = _pick(M, tm, SUBLANES), _pick(N, tn), _pick(K, tk)
    extras, extra_specs = (), ()
    if addend is not None:
        extras = (addend,)
        extra_specs = (pl.BlockSpec((tm, tn), lambda i, j, k: (i, j)),)

    def ep(acc, ex, outs):
        if ex:
            acc = acc + ex[0][...]
        outs[0][...] = acc.astype(out_dtype)

    return _mm(name, a, b, grid=(M // tm, N // tn, K // tk), kaxis=2, dims=NT,
               a_spec=pl.BlockSpec((tm, tk), lambda i, j, k: (i, k)),
               b_spec=pl.BlockSpec((tn, tk), lambda i, j, k: (j, k)),
               acc_shape=(tm, tn), out_shape=_sds((M, N), out_dtype),
               out_specs=pl.BlockSpec((tm, tn), lambda i, j, k: (i, j)),
               epilogue=ep, extras=extras, extra_specs=extra_specs,
               sem=("parallel", "parallel", "arbitrary"))


def _mm_tn(name, a, b, tm=1024, tn=1024, tk=1024, out_dtype=BF16):
    K, M = a.shape
    N = b.shape[1]
    tm, tn, tk = _pick(M, tm), _pick(N, tn), _pick(K, tk, SUBLANES)
    return _mm(name, a, b, grid=(M // tm, N // tn, K // tk), kaxis=2, dims=TN,
               a_spec=pl.BlockSpec((tk, tm), lambda i, j, k: (k, i)),
               b_spec=pl.BlockSpec((tk, tn), lambda i, j, k: (k, j)),
               acc_shape=(tm, tn), out_shape=_sds((M, N), out_dtype),
               out_specs=pl.BlockSpec((tm, tn), lambda i, j, k: (i, j)),
               epilogue=_store(out_dtype), sem=("parallel", "parallel", "arbitrary"))


def _rmsnorm_fwd(x, g_row):
    T, D = x.shape
    tm = _pick(T, 512, SUBLANES)

    def body(x_ref, g_ref, h_ref):
        xv = x_ref[...]
        r = lax.rsqrt(jnp.mean(xv * xv, axis=-1, keepdims=True) + NORM_EPS)
        h_ref[...] = (xv * r * g_ref[...]).astype(BF16)

    row = pl.BlockSpec((tm, D), lambda i: (i, 0))
    return pl.pallas_call(body, name="rmsnorm_fwd", grid=(T // tm,),
                          in_specs=[row, pl.BlockSpec((1, D), lambda i: (0, 0))], out_specs=row,
                          out_shape=_sds((T, D), BF16), compiler_params=_cp("parallel"))(x, g_row)


def _rmsnorm_bwd(x, dh, dout, g_row):
    T, D = x.shape
    tm = _pick(T, 256, SUBLANES)

    def body(x_ref, dh_ref, do_ref, g_ref, dx_ref, dg_ref):
        xv = x_ref[...]
        r = lax.rsqrt(jnp.mean(xv * xv, axis=-1, keepdims=True) + NORM_EPS)
        xh = xv * r
        dhv = dh_ref[...]
        dxh = dhv * g_ref[...]
        dx_ref[...] = do_ref[...] + r * (dxh - xh * jnp.mean(dxh * xh, axis=-1, keepdims=True))
        _acc_rows(dg_ref, 0, jnp.sum(dhv * xh, axis=0, keepdims=True), pl.program_id(0) == 0)

    row = pl.BlockSpec((tm, D), lambda i: (i, 0))
    return pl.pallas_call(body, name="rmsnorm_bwd", grid=(T // tm,),
                          in_specs=[row, row, row, pl.BlockSpec((1, D), lambda i: (0, 0))],
                          out_specs=[row, pl.BlockSpec((SUBLANES, D), lambda i: (0, 0))],
                          out_shape=[_sds((T, D), F32), _sds((SUBLANES, D), F32)],
                          compiler_params=_cp("arbitrary"))(x, dh, dout, g_row)


def _loss_head(x, target, g_row):
    T, D = x.shape
    tm = _pick(T, 256, SUBLANES)

    def body(x_ref, t_ref, g_ref, dx_ref, dg_ref, loss_ref):
        first = pl.program_id(0) == 0
        xv = x_ref[...]
        g = g_ref[...]
        r = lax.rsqrt(jnp.mean(xv * xv, axis=-1, keepdims=True) + NORM_EPS)
        xh = xv * r
        err = xh * g - t_ref[...]
        part = 0.5 * jnp.sum(jnp.mean(err * err, axis=-1, keepdims=True), axis=0, keepdims=True)

        @pl.when(first)
        def _():
            loss_ref[...] = jnp.zeros_like(loss_ref)

        loss_ref[...] += jnp.broadcast_to(part, loss_ref.shape)
        dy = err / D
        _acc_rows(dg_ref, 0, jnp.sum(dy * xh, axis=0, keepdims=True), first)
        dxh = dy * g
        dx_ref[...] = r * (dxh - xh * jnp.mean(dxh * xh, axis=-1, keepdims=True))

    row = pl.BlockSpec((tm, D), lambda i: (i, 0))
    return pl.pallas_call(body, name="loss_head", grid=(T // tm,),
                          in_specs=[row, row, pl.BlockSpec((1, D), lambda i: (0, 0))],
                          out_specs=[row, pl.BlockSpec((SUBLANES, D), lambda i: (0, 0)),
                                     pl.BlockSpec((SUBLANES, LANES), lambda i: (0, 0))],
                          out_shape=[_sds((T, D), F32), _sds((SUBLANES, D), F32), _sds((SUBLANES, LANES), F32)],
                          compiler_params=_cp("arbitrary"))(x, target, g_row)


def _tri(cmp):
    r = lax.broadcasted_iota(jnp.int32, (LANES, LANES), 0)
    c = lax.broadcasted_iota(jnp.int32, (LANES, LANES), 1)
    return cmp(r, c).astype(BF16)


def _scan_mm(x, tri):
    hi = x.astype(BF16)
    lo = (x - hi.astype(F32)).astype(BF16)
    return (jnp.dot(hi, tri, preferred_element_type=F32) + jnp.dot(lo, tri, preferred_element_type=F32))


def _sb_scores(qs, kb, causal):
    z = lax.dot_general(qs, kb, NT, preferred_element_type=F32)
    lb = jnp.minimum(z, 0.0) - jnp.log(1.0 + jnp.exp(-jnp.abs(z)))
    l1 = lb - z
    return lb, (l1 if causal is None else jnp.where(causal, l1, 0.0))


def _masked(causal, x):
    return x if causal is None else jnp.where(causal, x, 0.0)


def _lanes(x, s):
    return x[:, s * LANES:(s + 1) * LANES]


def _sb_fwd(proj, W):
    T = proj.shape[0]
    H = W // LANES
    assert T // LANES <= LANES
    tq = _pick(T, 256, LANES)
    kblk = _pick(T, SB_KEY_BLOCK, LANES)
    assert kblk % tq == 0
    nsub = kblk // LANES
    scale = LANES ** -0.5

    def body(q_ref, k_ref, v_ref, g_ref, o_ref, y_ref, c_ref, run):
        i = pl.program_id(1)
        qs = (q_ref[...] * scale).astype(BF16)
        row = i * tq + lax.broadcasted_iota(jnp.int32, (tq, kblk), 0)
        key = lax.broadcasted_iota(jnp.int32, (tq, kblk), 1)
        col = lax.broadcasted_iota(jnp.int32, (tq, LANES), 1)
        tri = _tri(lambda r, c: r > c)
        nkb = ((i + 1) * tq + kblk - 1) // kblk
        o_ref[...] = jnp.zeros_like(o_ref)
        c_ref[...] = jnp.zeros_like(c_ref)
        run[...] = jnp.zeros_like(run)

        def make_step(diagonal):
            def step(jj, carry):
                j = nkb - 1 - jj
                off = pl.multiple_of(j * kblk, kblk)
                kb = k_ref[pl.ds(off, kblk), :].astype(BF16)
                vb = v_ref[pl.ds(off, kblk), :].astype(BF16)
                causal = (key + off < row) if diagonal else None
                lb, l1m = _sb_scores(qs, kb, causal)
                c_after = run[...]
                cs = c_ref[...]
                after = [None] * nsub
                for s in reversed(range(nsub)):
                    part = _lanes(l1m, s)
                    after[s] = c_after + _scan_mm(part, tri)
                    cs = jnp.where(col == j * nsub + s, c_after, cs)
                    c_after = c_after + jnp.sum(part, axis=1, keepdims=True)
                run[...] = c_after
                c_ref[...] = cs
                w = _masked(causal, jnp.exp(lb + jnp.concatenate(after, axis=1)))
                o_ref[...] += jnp.dot(w.astype(BF16), vb, preferred_element_type=F32)
                return carry
            return step

        make_step(True)(0, 0)
        lax.fori_loop(1, nkb, make_step(False), 0)
        y_ref[...] = (o_ref[...] * _silu(g_ref[...])).astype(BF16)

    qspec = lambda c0: pl.BlockSpec((tq, LANES), lambda h, i: (i, c0 + h))
    kvspec = lambda c0: pl.BlockSpec((T, LANES), lambda h, i: (0, c0 + h))
    out = pl.BlockSpec((tq, LANES), lambda h, i: (i, h))
    return pl.pallas_call(body, name="sb_fwd", grid=(H, T // tq),
                          in_specs=[qspec(0), kvspec(H), kvspec(2 * H), qspec(3 * H)], out_specs=[out, out, out],
                          out_shape=[_sds((T, W), F32), _sds((T, W), BF16), _sds((T, W), F32)],
                          scratch_shapes=[pltpu.VMEM((tq, LANES), F32)],
                          compiler_params=_cp("parallel", "arbitrary"))(proj, proj, proj, proj)


def _sb_bwd(proj, o, carries, dy, W):
    T = proj.shape[0]
    H = W // LANES
    tq = _pick(T, 256, LANES)
    kblk = _pick(T, SB_KEY_BLOCK, LANES)
    assert kblk % tq == 0
    nsub = kblk // LANES
    scale = LANES ** -0.5

    def body(q_ref, k_ref, v_ref, g_ref, o_ref, c_ref, dy_ref, dq_ref, dk_ref, dv_ref, dg_ref, run):
        i = pl.program_id(1)

        @pl.when(i == 0)
        def _():
            dk_ref[...] = jnp.zeros_like(dk_ref)
            dv_ref[...] = jnp.zeros_like(dv_ref)

        gv = g_ref[...]
        dyv = dy_ref[...]
        dg_ref[...] = (dyv * o_ref[...] * _dsilu(gv)).astype(BF16)
        dob = (dyv * _silu(gv)).astype(BF16)
        qs = (q_ref[...] * scale).astype(BF16)
        row = i * tq + lax.broadcasted_iota(jnp.int32, (tq, kblk), 0)
        key = lax.broadcasted_iota(jnp.int32, (tq, kblk), 1)
        col = lax.broadcasted_iota(jnp.int32, (tq, LANES), 1)
        tri_after = _tri(lambda r, c: r > c)
        tri_before = _tri(lambda r, c: r < c)
        dq_ref[...] = jnp.zeros_like(dq_ref)
        run[...] = jnp.zeros_like(run)

        def make_step(diagonal):
            def step(j, carry):
                off = pl.multiple_of(j * kblk, kblk)
                kb = k_ref[pl.ds(off, kblk), :].astype(BF16)
                vb = v_ref[pl.ds(off, kblk), :].astype(BF16)
                causal = (key + off < row) if diagonal else None
                lb, l1m = _sb_scores(qs, kb, causal)
                cs = c_ref[...]
                after = [jnp.sum(jnp.where(col == j * nsub + s, cs, 0.0), axis=1, keepdims=True)
                         + _scan_mm(_lanes(l1m, s), tri_after) for s in range(nsub)]
                w = _masked(causal, jnp.exp(lb + jnp.concatenate(after, axis=1)))
                gw = w * lax.dot_general(dob, vb, NT, preferred_element_type=F32)
                gpre = run[...]
                before = [None] * nsub
                for s in range(nsub):
                    part = _lanes(gw, s)
                    before[s] = gpre + _scan_mm(part, tri_before)
                    gpre = gpre + jnp.sum(part, axis=1, keepdims=True)
                run[...] = gpre
                beta = jnp.exp(lb)
                dz = _masked(causal, gw * (1.0 - beta) - jnp.concatenate(before, axis=1) * beta).astype(BF16)
                dq_ref[...] += jnp.dot(dz, kb, preferred_element_type=F32)
                dk_ref[pl.ds(off, kblk), :] += lax.dot_general(dz, qs, TN, preferred_element_type=F32)
                dv_ref[pl.ds(off, kblk), :] += lax.dot_general(w.astype(BF16), dob, TN, preferred_element_type=F32)
                return carry
            return step

        last = ((i + 1) * tq + kblk - 1) // kblk - 1
        lax.fori_loop(0, last, make_step(False), 0)
        make_step(True)(last, 0)
        dq_ref[...] = dq_ref[...] * scale

    qspec = lambda c0: pl.BlockSpec((tq, LANES), lambda h, i: (i, c0 + h))
    kvspec = lambda c0: pl.BlockSpec((T, LANES), lambda h, i: (0, c0 + h))
    blk = pl.BlockSpec((tq, LANES), lambda h, i: (i, h))
    full = pl.BlockSpec((T, LANES), lambda h, i: (0, h))
    return pl.pallas_call(body, name="sb_bwd", grid=(H, T // tq),
                          in_specs=[qspec(0), kvspec(H), kvspec(2 * H), qspec(3 * H), blk, blk,
                                    pl.BlockSpec((None, tq, LANES), lambda h, i: (0, i, h))],
                          out_specs=[blk, full, full, blk],
                          out_shape=[_sds((T, W), F32), _sds((T, W), F32), _sds((T, W), F32), _sds((T, W), BF16)],
                          scratch_shapes=[pltpu.VMEM((tq, LANES), F32)],
                          compiler_params=_cp("parallel", "arbitrary"))(proj, proj, proj, proj, o, carries, dy)


def _conv_specs(T, W, tm, col_a):
    per = tm // CONV_HALO
    cur = lambda c: pl.BlockSpec((tm, W), lambda i: (i, c))
    prev = lambda c: pl.BlockSpec((CONV_HALO, W), lambda i: (jnp.maximum(i * per - 1, 0), c))
    return cur, prev


def _conv_fwd(proj, conv_w, K, conv_b, ln_g, ln_b, W):
    T = proj.shape[0]
    tm = _pick(T, 256, CONV_HALO)
    lead = CONV_HALO - (K - 1)
    cur, prev = _conv_specs(T, W, tm, 4)

    def body(a_ref, b_ref, ah_ref, bh_ref, g_ref, w_ref, cb_ref, lg_ref, lb_ref, c_ref, y_ref, glu):
        i = pl.program_id(0)
        glu[0:CONV_HALO, :] = jnp.where(i > 0, ah_ref[...] * _sigmoid(bh_ref[...]), 0.0)
        glu[CONV_HALO:, :] = a_ref[...] * _sigmoid(b_ref[...])
        c = jnp.broadcast_to(cb_ref[...], (tm, W))
        for k in range(K):
            c = c + w_ref[k:k + 1, :] * glu[lead + k:lead + k + tm, :]
        c_ref[...] = c
        xh, _ = _ln_stats(c)
        y_ref[...] = (_silu(xh * lg_ref[...] + lb_ref[...]) * _silu(g_ref[...])).astype(BF16)

    vec = pl.BlockSpec((1, W), lambda i: (0, 0))
    row = pl.BlockSpec((tm, W), lambda i: (i, 0))
    return pl.pallas_call(body, name="conv_fwd", grid=(T // tm,),
                          in_specs=[cur(4), cur(5), prev(4), prev(5), cur(6),
                                    pl.BlockSpec((CONV_HALO, W), lambda i: (0, 0)), vec, vec, vec],
                          out_specs=[row, row], out_shape=[_sds((T, W), F32), _sds((T, W), BF16)],
                          scratch_shapes=[pltpu.VMEM((tm + CONV_HALO, W), F32)],
                          compiler_params=_cp("parallel"))(proj, proj, proj, proj, proj, conv_w, conv_b, ln_g, ln_b)


def _conv_bwd_ln(proj, c, dy, ln_g, ln_b, W):
    T = proj.shape[0]
    tm = _pick(T, 256, SUBLANES)

    def body(g_ref, c_ref, dy_ref, lg_ref, lb_ref, dc_ref, dg_ref, st_ref):
        first = pl.program_id(0) == 0
        gv = g_ref[...]
        dyv = dy_ref[...]
        xh, r = _ln_stats(c_ref[...])
        lg = lg_ref[...]
        ln = xh * lg + lb_ref[...]
        dg_ref[...] = (dyv * _silu(ln) * _dsilu(gv)).astype(BF16)
        dln = dyv * _silu(gv) * _dsilu(ln)
        dc = _ln_bwd(dln * lg, xh, r)
        dc_ref[...] = dc
        _acc_rows(st_ref, 0, jnp.sum(dln * xh, axis=0, keepdims=True), first)
        st_ref[1:2, :] += jnp.sum(dln, axis=0, keepdims=True)
        st_ref[2:3, :] += jnp.sum(dc, axis=0, keepdims=True)

    row = pl.BlockSpec((tm, W), lambda i: (i, 0))
    vec = pl.BlockSpec((1, W), lambda i: (0, 0))
    return pl.pallas_call(body, name="conv_bwd_ln", grid=(T // tm,),
                          in_specs=[pl.BlockSpec((tm, W), lambda i: (i, 6)), row,
                                    pl.BlockSpec((None, tm, W), lambda i: (1, i, 0)), vec, vec],
                          out_specs=[row, row, pl.BlockSpec((SUBLANES, W), lambda i: (0, 0))],
                          out_shape=[_sds((T, W), F32), _sds((T, W), BF16), _sds((SUBLANES, W), F32)],
                          compiler_params=_cp("arbitrary"))(proj, c, dy, ln_g, ln_b)


def _conv_bwd_taps(proj, dc, conv_w, K, W):
    T = proj.shape[0]
    tm = _pick(T, 256, CONV_HALO)
    lead = CONV_HALO - (K - 1)
    per = tm // CONV_HALO
    nblk = T // tm
    cur, prev = _conv_specs(T, W, tm, 4)

    def body(a_ref, b_ref, ah_ref, bh_ref, dc_ref, dcn_ref, w_ref, da_ref, db_ref, dw_ref, glu, dcs):
        i = pl.program_id(0)
        av = a_ref[...]
        sb = _sigmoid(b_ref[...])
        glu[0:CONV_HALO, :] = jnp.where(i > 0, ah_ref[...] * _sigmoid(bh_ref[...]), 0.0)
        glu[CONV_HALO:, :] = av * sb
        dcv = dc_ref[...]
        dcs[0:tm, :] = dcv
        dcs[tm:, :] = jnp.where(i < nblk - 1, dcn_ref[...], 0.0)

        @pl.when(i == 0)
        def _():
            dw_ref[...] = jnp.zeros_like(dw_ref)

        dglu = jnp.zeros((tm, W), F32)
        for k in range(K):
            dglu = dglu + w_ref[k:k + 1, :] * dcs[K - 1 - k:K - 1 - k + tm, :]
            dw_ref[k:k + 1, :] += jnp.sum(dcv * glu[lead + k:lead + k + tm, :], axis=0, keepdims=True)
        da_ref[...] = (dglu * sb).astype(BF16)
        db_ref[...] = (dglu * av * sb * (1.0 - sb)).astype(BF16)

    row = pl.BlockSpec((tm, W), lambda i: (i, 0))
    nxt = pl.BlockSpec((CONV_HALO, W), lambda i: (jnp.minimum((i + 1) * per, T // CONV_HALO - 1), 0))
    return pl.pallas_call(body, name="conv_bwd_taps", grid=(nblk,),
                          in_specs=[cur(4), cur(5), prev(4), prev(5), row, nxt,
                                    pl.BlockSpec((CONV_HALO, W), lambda i: (0, 0))],
                          out_specs=[row, row, pl.BlockSpec((CONV_HALO, W), lambda i: (0, 0))],
                          out_shape=[_sds((T, W), BF16), _sds((T, W), BF16), _sds((CONV_HALO, W), F32)],
                          scratch_shapes=[pltpu.VMEM((tm + CONV_HALO, W), F32), pltpu.VMEM((tm + CONV_HALO, W), F32)],
                          compiler_params=_cp("arbitrary"))(proj, proj, proj, proj, dc, dc, conv_w)


def _sgu_common(cu, cv, lg, lb, w_ref, bt_ref, z_scr, G, nch):
    u = _gelu(cu)
    xh, r = _ln_stats(_gelu(cv))
    vn = (xh * lg + lb).astype(BF16)
    rr = lax.broadcasted_iota(jnp.int32, (LANES, LANES), 0)
    cc = lax.broadcasted_iota(jnp.int32, (LANES, LANES), 1)
    tril = rr >= cc
    wts = [jnp.where(tril, w_ref[g], 0.0).astype(BF16) for g in range(G)]
    for ch in range(nch):
        rs = slice(ch * LANES, (ch + 1) * LANES)
        for g in range(G):
            cs = slice(g * LANES, (g + 1) * LANES)
            z_scr[rs, cs] = jnp.dot(wts[g], vn[rs, cs], preferred_element_type=F32) + bt_ref[:, g:g + 1]
    return u, xh, r, vn, wts, tril


def _sgu_fwd(proj, sgu_w, sgu_bt, ln_g, ln_b, W):
    T = proj.shape[0]
    G = W // LANES
    tm = _pick(T, 256, LANES)
    nch = tm // LANES

    def body(u_ref, v_ref, g_ref, w_ref, bt_ref, lg_ref, lb_ref, y_ref, z_scr):
        u, *_ = _sgu_common(u_ref[...], v_ref[...], lg_ref[...], lb_ref[...], w_ref, bt_ref, z_scr, G, nch)
        y_ref[...] = (u * z_scr[...] * _silu(g_ref[...])).astype(BF16)

    cur = lambda c: pl.BlockSpec((tm, W), lambda i: (i, c))
    vec = pl.BlockSpec((1, W), lambda i: (0, 0))
    return pl.pallas_call(body, name="sgu_fwd", grid=(T // tm,),
                          in_specs=[cur(7), cur(8), cur(9), pl.BlockSpec((G, LANES, LANES), lambda i: (0, 0, 0)),
                                    pl.BlockSpec((LANES, G), lambda i: (0, 0)), vec, vec],
                          out_specs=pl.BlockSpec((tm, W), lambda i: (i, 0)), out_shape=_sds((T, W), BF16),
                          scratch_shapes=[pltpu.VMEM((tm, W), F32)],
                          compiler_params=_cp("parallel"))(proj, proj, proj, sgu_w, sgu_bt, ln_g, ln_b)


def _sgu_bwd(proj, dy, sgu_w, sgu_bt, ln_g, ln_b, W):
    T = proj.shape[0]
    G = W // LANES
    tm = _pick(T, 256, LANES)
    nch = tm // LANES

    def body(u_ref, v_ref, g_ref, dy_ref, w_ref, bt_ref, lg_ref, lb_ref,
             du_ref, dv_ref, dg_ref, dw_ref, dbt_ref, st_ref, z_scr, dvn_scr):
        first = pl.program_id(0) == 0
        cu, cv, gv, dyv = u_ref[...], v_ref[...], g_ref[...], dy_ref[...]
        lg = lg_ref[...]
        u, xh, r, vn, wts, tril = _sgu_common(cu, cv, lg, lb_ref[...], w_ref, bt_ref, z_scr, G, nch)
        z = z_scr[...]
        sg = _silu(gv)
        dg_ref[...] = (dyv * u * z * _dsilu(gv)).astype(BF16)
        du_ref[...] = (dyv * z * sg * _dgelu(cu)).astype(BF16)
        dz = dyv * u * sg
        dzb = dz.astype(BF16)

        @pl.when(first)
        def _():
            dw_ref[...] = jnp.zeros_like(dw_ref)
            dbt_ref[...] = jnp.zeros_like(dbt_ref)

        for g in range(G):
            cs = slice(g * LANES, (g + 1) * LANES)
            dwg = jnp.zeros((LANES, LANES), F32)
            dbg = jnp.zeros((LANES, 1), F32)
            for ch in range(nch):
                rs = slice(ch * LANES, (ch + 1) * LANES)
                dwg = dwg + lax.dot_general(dzb[rs, cs], vn[rs, cs], NT, preferred_element_type=F32)
                dbg = dbg + jnp.sum(dz[rs, cs], axis=1, keepdims=True)
                dvn_scr[rs, cs] = lax.dot_general(wts[g], dzb[rs, cs], TN, preferred_element_type=F32)
            dw_ref[g] += jnp.where(tril, dwg, 0.0)
            dbt_ref[:, g:g + 1] += dbg
        dvn = dvn_scr[...]
        _acc_rows(st_ref, 0, jnp.sum(dvn * xh, axis=0, keepdims=True), first)
        st_ref[1:2, :] += jnp.sum(dvn, axis=0, keepdims=True)
        dv_ref[...] = (_ln_bwd(dvn * lg, xh, r) * _dgelu(cv)).astype(BF16)

    cur = lambda c: pl.BlockSpec((tm, W), lambda i: (i, c))
    row = pl.BlockSpec((tm, W), lambda i: (i, 0))
    vec = pl.BlockSpec((1, W), lambda i: (0, 0))
    wspec = pl.BlockSpec((G, LANES, LANES), lambda i: (0, 0, 0))
    bspec = pl.BlockSpec((LANES, G), lambda i: (0, 0))
    return pl.pallas_call(body, name="sgu_bwd", grid=(T // tm,),
                          in_specs=[cur(7), cur(8), cur(9), pl.BlockSpec((None, tm, W), lambda i: (2, i, 0)),
                                    wspec, bspec, vec, vec],
                          out_specs=[row, row, row, wspec, bspec, pl.BlockSpec((SUBLANES, W), lambda i: (0, 0))],
                          out_shape=[_sds((T, W), BF16)] * 3 + [_sds((G, LANES, LANES), F32), _sds((LANES, G), F32),
                                                                 _sds((SUBLANES, W), F32)],
                          scratch_shapes=[pltpu.VMEM((tm, W), F32), pltpu.VMEM((tm, W), F32)],
                          compiler_params=_cp("arbitrary"))(proj, proj, proj, dy, sgu_w, sgu_bt, ln_g, ln_b)


def _rows(start, dil):
    return pl.ds(start, LANES, stride=dil) if dil > 1 else pl.ds(start, LANES)


def _dil_masks():
    a = lax.broadcasted_iota(jnp.int32, (LANES, LANES), 0)
    c = lax.broadcasted_iota(jnp.int32, (LANES, LANES), 1)
    return c <= a, c >= a


def _dil_geometry(T, dil):
    sbr = LANES * dil
    nb = max(1, min(DIL_STEP_ROWS, T) // sbr)
    return sbr, nb, T // (sbr * nb)


def _for_units(nb, dil, unit):
    for blk in range(nb):
        if dil <= DIL_UNROLL:
            for r in range(dil):
                unit(blk, r)
        else:
            def chunk(it, carry):
                for u in range(DIL_UNROLL):
                    unit(blk, it * DIL_UNROLL + u)
                return carry
            lax.fori_loop(0, dil // DIL_UNROLL, chunk, 0)


def _dil_fwd_group(proj, W, gi, dil):
    T = proj.shape[0]
    H = W // LANES
    sbr, nb, nsteps = _dil_geometry(T, dil)
    scale = LANES ** -0.5
    cq, ck, cv = (10 + gi) * H, (13 + gi) * H, 16 * H

    def body(q_ref, kc_ref, kp_ref, vc_ref, vp_ref, o_ref, l_ref):
        b = pl.program_id(1)
        m_cur, m_prev = _dil_masks()
        m_first = m_prev & (b > 0)

        def unit(blk, r):
            sl = _rows(blk * sbr + r, dil)
            if blk == 0:
                kp, vp, mp = kp_ref[_rows(r, dil), :], vp_ref[_rows(r, dil), :], m_first
            else:
                sp_ = _rows((blk - 1) * sbr + r, dil)
                kp, vp, mp = kc_ref[sp_, :], vc_ref[sp_, :], m_prev
            q = (q_ref[sl, :] * scale).astype(BF16)
            sc = lax.dot_general(q, kc_ref[sl, :].astype(BF16), NT, preferred_element_type=F32)
            sp = lax.dot_general(q, kp.astype(BF16), NT, preferred_element_type=F32)
            sc = jnp.where(m_cur, sc, NEG)
            sp = jnp.where(mp, sp, NEG)
            m = jnp.maximum(jnp.max(sc, axis=1, keepdims=True), jnp.max(sp, axis=1, keepdims=True))
            pc = jnp.exp(sc - m)
            pp = jnp.exp(sp - m)
            den = jnp.sum(pc, axis=1, keepdims=True) + jnp.sum(pp, axis=1, keepdims=True)
            pv = (jnp.dot(pc.astype(BF16), vc_ref[sl, :].astype(BF16), preferred_element_type=F32)
                  + jnp.dot(pp.astype(BF16), vp.astype(BF16), preferred_element_type=F32))
            o_ref[sl, :] = pv / den
            l_ref[sl, :] = jnp.broadcast_to(m + jnp.log(den), (LANES, LANES))

        _for_units(nb, dil, unit)

    cur = lambda c0: pl.BlockSpec((sbr * nb, LANES), lambda h, b: (b, c0 + h))
    prv = lambda c0: pl.BlockSpec((sbr, LANES), lambda h, b: (jnp.maximum(b * nb - 1, 0), c0 + h))
    out = pl.BlockSpec((sbr * nb, LANES), lambda h, b: (b, h))
    return pl.pallas_call(body, name=f"dil_fwd_g{gi}", grid=(H, nsteps),
                          in_specs=[cur(cq), cur(ck), prv(ck), cur(cv), prv(cv)], out_specs=[out, out],
                          out_shape=[_sds((T, W), F32), _sds((T, W), F32)],
                          compiler_params=_cp("parallel", "parallel"))(proj, proj, proj, proj, proj)


def _dil_combine(proj, os_, ls_, W):
    T = proj.shape[0]
    tm = _pick(T, 256, SUBLANES)

    def body(g_ref, o0, o1, o2, l0, l1, l2, od_ref, lse_ref, y_ref):
        a0, a1, a2 = l0[...], l1[...], l2[...]
        m = jnp.maximum(jnp.maximum(a0, a1), a2)
        e0, e1, e2 = jnp.exp(a0 - m), jnp.exp(a1 - m), jnp.exp(a2 - m)
        s = e0 + e1 + e2
        od = (e0 / s) * o0[...] + (e1 / s) * o1[...] + (e2 / s) * o2[...]
        od_ref[...] = od
        lse_ref[...] = m + jnp.log(s)
        y_ref[...] = (od * _silu(g_ref[...])).astype(BF16)

    row = pl.BlockSpec((tm, W), lambda i: (i, 0))
    return pl.pallas_call(body, name="dil_combine", grid=(T // tm,),
                          in_specs=[pl.BlockSpec((tm, W), lambda i: (i, 17))] + [row] * 6, out_specs=[row, row, row],
                          out_shape=[_sds((T, W), F32), _sds((T, W), F32), _sds((T, W), BF16)],
                          compiler_params=_cp("parallel"))(proj, *os_, *ls_)


def _dil_bwd_pre(proj, od, dy, W):
    T = proj.shape[0]
    H = W // LANES
    tm = _pick(T, 512, SUBLANES)

    def body(g_ref, od_ref, dy_ref, do_ref, dl_ref, dg_ref):
        gv, odv, dyv = g_ref[...], od_ref[...], dy_ref[...]
        do = dyv * _silu(gv)
        do_ref[...] = do
        dl_ref[...] = jnp.broadcast_to(jnp.sum(do * odv, axis=1, keepdims=True), (tm, LANES))
        dg_ref[...] = (dyv * odv * _dsilu(gv)).astype(BF16)

    blk = pl.BlockSpec((tm, LANES), lambda i, h: (i, h))
    return pl.pallas_call(body, name="dil_bwd_pre", grid=(T // tm, H),
                          in_specs=[pl.BlockSpec((tm, LANES), lambda i, h: (i, 17 * H + h)), blk,
                                    pl.BlockSpec((None, tm, LANES), lambda i, h: (3, i, h))],
                          out_specs=[blk, blk, blk],
                          out_shape=[_sds((T, W), F32), _sds((T, W), F32), _sds((T, W), BF16)],
                          compiler_params=_cp("parallel", "parallel"))(proj, od, dy)


def _dil_bwd_group(proj, do, lse, delta, W, gi, dil):
    T = proj.shape[0]
    H = W // LANES
    sbr, nb, nsteps = _dil_geometry(T, dil)
    scale = LANES ** -0.5
    cq, ck, cv = (10 + gi) * H, (13 + gi) * H, 16 * H

    def body(qc_ref, qn_ref, kc_ref, kp_ref, vc_ref, vp_ref, doc_ref, don_ref, lc_ref, ln_ref, dc_ref, dn_ref,
             dq_ref, dk_ref, dv_ref):
        b = pl.program_id(1)
        m_cur, m_prev = _dil_masks()
        m_first = m_prev & (b > 0)
        m_last = m_prev & (b < nsteps - 1)

        def probs(q, k, mask, l):
            s = lax.dot_general(q, k, NT, preferred_element_type=F32)
            return jnp.exp(jnp.where(mask, s - l, NEG))

        def unit(blk, r):
            sl = _rows(blk * sbr + r, dil)
            if blk == 0:
                edge = _rows(r, dil)
                k_p, v_p, m_cp = kp_ref[edge, :], vp_ref[edge, :], m_first
            else:
                sp_ = _rows((blk - 1) * sbr + r, dil)
                k_p, v_p, m_cp = kc_ref[sp_, :], vc_ref[sp_, :], m_prev
            if blk == nb - 1:
                edge = _rows(r, dil)
                q_n, do_n, l_n, d_n, m_nc = qn_ref[edge, :], don_ref[edge, :], ln_ref[edge, :], dn_ref[edge, :], m_last
            else:
                sn_ = _rows((blk + 1) * sbr + r, dil)
                q_n, do_n, l_n, d_n, m_nc = qc_ref[sn_, :], doc_ref[sn_, :], lc_ref[sn_, :], dc_ref[sn_, :], m_prev
            q_c = (qc_ref[sl, :] * scale).astype(BF16)
            q_n = (q_n * scale).astype(BF16)
            k_c, k_p = kc_ref[sl, :].astype(BF16), k_p.astype(BF16)
            v_c, v_p = vc_ref[sl, :].astype(BF16), v_p.astype(BF16)
            do_c, do_n = doc_ref[sl, :].astype(BF16), do_n.astype(BF16)
            l_c, d_c = lc_ref[sl, :], dc_ref[sl, :]
            p_cc = probs(q_c, k_c, m_cur, l_c)
            p_cp = probs(q_c, k_p, m_cp, l_c)
            p_nc = probs(q_n, k_c, m_nc, l_n)
            ds_cc = (p_cc * (lax.dot_general(do_c, v_c, NT, preferred_element_type=F32) - d_c)).astype(BF16)
            ds_cp = (p_cp * (lax.dot_general(do_c, v_p, NT, preferred_element_type=F32) - d_c)).astype(BF16)
            ds_nc = (p_nc * (lax.dot_general(do_n, v_c, NT, preferred_element_type=F32) - d_n)).astype(BF16)
            dq = (jnp.dot(ds_cc, k_c, preferred_element_type=F32) + jnp.dot(ds_cp, k_p, preferred_element_type=F32))
            dk = (lax.dot_general(ds_cc, q_c, TN, preferred_element_type=F32)
                  + lax.dot_general(ds_nc, q_n, TN, preferred_element_type=F32))
            dv = (lax.dot_general(p_cc.astype(BF16), do_c, TN, preferred_element_type=F32)
                  + lax.dot_general(p_nc.astype(BF16), do_n, TN, preferred_element_type=F32))
            dq_ref[sl, :] = dq * scale
            dk_ref[sl, :] = dk
            dv_ref[sl, :] = dv

        _for_units(nb, dil, unit)

    cur = lambda c0: pl.BlockSpec((sbr * nb, LANES), lambda h, b: (b, c0 + h))
    prv = lambda c0: pl.BlockSpec((sbr, LANES), lambda h, b: (jnp.maximum(b * nb - 1, 0), c0 + h))
    nxt = lambda c0: pl.BlockSpec((sbr, LANES), lambda h, b: (jnp.minimum((b + 1) * nb, T // sbr - 1), c0 + h))
    out = pl.BlockSpec((sbr * nb, LANES), lambda h, b: (b, h))
    return pl.pallas_call(body, name=f"dil_bwd_g{gi}", grid=(H, nsteps),
                          in_specs=[cur(cq), nxt(cq), cur(ck), prv(ck), cur(cv), prv(cv),
                                    cur(0), nxt(0), cur(0), nxt(0), cur(0), nxt(0)],
                          out_specs=[out, out, out], out_shape=[_sds((T, W), F32)] * 3,
                          compiler_params=_cp("parallel", "parallel"))(
        proj, proj, proj, proj, proj, proj, do, do, lse, lse, delta, delta)


def _sum3_bf16(a, b, c):
    T, W = a.shape
    tm = _pick(T, 512, SUBLANES)

    def body(a_ref, b_ref, c_ref, o_ref):
        o_ref[...] = (a_ref[...] + b_ref[...] + c_ref[...]).astype(BF16)

    row = pl.BlockSpec((tm, W), lambda i: (i, 0))
    return pl.pallas_call(body, name="dil_dv_sum", grid=(T // tm,), in_specs=[row, row, row], out_specs=row,
                          out_shape=_sds((T, W), BF16), compiler_params=_cp("parallel"))(a, b, c)


def _to_bf16(name, parts):
    T, W = parts[0].shape
    n = len(parts)
    tm = _pick(T, 512, SUBLANES)

    def body(*refs):
        o_ref = refs[n]
        for p in range(n):
            o_ref[:, p * W:(p + 1) * W] = refs[p][...].astype(BF16)

    row = pl.BlockSpec((tm, W), lambda i: (i, 0))
    return pl.pallas_call(body, name=name, grid=(T // tm,), in_specs=[row] * n,
                          out_specs=pl.BlockSpec((tm, n * W), lambda i: (i, 0)),
                          out_shape=_sds((T, n * W), BF16), compiler_params=_cp("parallel"))(*parts)


def _branch_merge(ys, w_br, gates):
    NB, T, W = ys.shape
    D = w_br.shape[2]
    tm, tn = _pick(T, 1024, SUBLANES), _pick(D, 1024)
    nj = D // tn

    def body(y_ref, w_ref, g_ref, yp_ref, m_ref, acc):
        n = pl.program_id(2)
        yp = jnp.dot(y_ref[...], w_ref[...], preferred_element_type=F32)
        yp_ref[...] = yp

        @pl.when(n == 0)
        def _():
            acc[...] = jnp.zeros_like(acc)

        acc[...] += g_ref[...] * yp

        @pl.when(n == NB - 1)
        def _():
            m_ref[...] = acc[...].astype(BF16)

    return pl.pallas_call(
        body, name="branch_merge", grid=(T // tm, nj, NB),
        in_specs=[pl.BlockSpec((None, tm, W), lambda i, j, n: (n, i, 0)),
                  pl.BlockSpec((None, W, tn), lambda i, j, n: (n, 0, j)),
                  pl.BlockSpec((tm, tn), lambda i, j, n: (i, n * nj + j))],
        out_specs=[pl.BlockSpec((None, tm, tn), lambda i, j, n: (n, i, j)),
                   pl.BlockSpec((tm, tn), lambda i, j, n: (i, j))],
        out_shape=[_sds((NB, T, D), F32), _sds((T, D), BF16)],
        scratch_shapes=[pltpu.VMEM((tm, tn), F32)],
        compiler_params=_cp("parallel", "parallel", "arbitrary"))(ys, w_br, gates)


def _merge_bwd(dmerged, gates, yproj):
    NB, T, D = yproj.shape
    tm = _pick(T, 256, SUBLANES)

    def body(dm_ref, g_ref, yp_ref, dyp_ref, dz_ref, db_ref):
        dm, g = dm_ref[...], g_ref[...]
        dyp_ref[...] = (dm * g).astype(BF16)
        dz = dm * yp_ref[...] * g * (1.0 - g)
        dz_ref[...] = dz.astype(BF16)
        _acc_rows(db_ref, 0, jnp.sum(dz, axis=0, keepdims=True), pl.program_id(1) == 0)

    return pl.pallas_call(
        body, name="merge_bwd", grid=(NB, T // tm),
        in_specs=[pl.BlockSpec((tm, D), lambda n, i: (i, 0)), pl.BlockSpec((tm, D), lambda n, i: (i, n)),
                  pl.BlockSpec((None, tm, D), lambda n, i: (n, i, 0))],
        out_specs=[pl.BlockSpec((None, tm, D), lambda n, i: (n, i, 0)), pl.BlockSpec((tm, D), lambda n, i: (i, n)),
                   pl.BlockSpec((SUBLANES, D), lambda n, i: (0, n))],
        out_shape=[_sds((NB, T, D), BF16), _sds((T, NB * D), BF16), _sds((SUBLANES, NB * D), F32)],
        compiler_params=_cp("parallel", "arbitrary"))(dmerged, gates, yproj)


def _branch_bwd_dy(dyp, w_br):
    NB, T, D = dyp.shape
    W = w_br.shape[1]
    tm = _pick(T, 1024, SUBLANES)

    def body(a_ref, w_ref, o_ref):
        o_ref[...] = lax.dot_general(a_ref[...], w_ref[...], NT, preferred_element_type=F32)

    return pl.pallas_call(body, name="branch_bwd_dy", grid=(NB, T // tm),
                          in_specs=[pl.BlockSpec((None, tm, D), lambda n, i: (n, i, 0)),
                                    pl.BlockSpec((None, W, D), lambda n, i: (n, 0, 0))],
                          out_specs=pl.BlockSpec((None, tm, W), lambda n, i: (n, i, 0)),
                          out_shape=_sds((NB, T, W), F32), compiler_params=_cp("parallel", "parallel"))(dyp, w_br)


def _branch_bwd_dw(ys, dyp):
    NB, T, W = ys.shape
    D = dyp.shape[2]
    tm, tn, tk = _pick(W, 1024), _pick(D, 1024), _pick(T, 1024, SUBLANES)
    return _mm("branch_bwd_dw", ys, dyp, grid=(NB, W // tm, D // tn, T // tk), kaxis=3, dims=TN,
               a_spec=pl.BlockSpec((None, tk, tm), lambda n, i, j, k: (n, k, i)),
               b_spec=pl.BlockSpec((None, tk, tn), lambda n, i, j, k: (n, k, j)),
               acc_shape=(tm, tn), out_shape=_sds((NB, W, D), BF16),
               out_specs=pl.BlockSpec((None, tm, tn), lambda n, i, j, k: (n, i, j)),
               epilogue=_store(BF16), sem=("parallel", "parallel", "parallel", "arbitrary"))


def _layer_fwd(x, wl, sp):
    W = sp["conv_b"].shape[-1]
    D = x.shape[1]
    h = _rmsnorm_fwd(x, sp["norm_g"])
    proj = _mm_nn("proj", h, wl["w_in"], F32, tn=768)

    def gate_ep(acc, ex, outs):
        outs[0][...] = _sigmoid(acc + ex[0][...])

    tn_g = _pick(4 * D, 1024)
    gates = _mm_nn("gates", h, wl["w_gate"], F32, epilogue=gate_ep, extras=(sp["b_gate"],),
                   extra_specs=(pl.BlockSpec((1, tn_g), lambda i, j, k: (0, j)),), tn=tn_g)
    oa, ya, sb_carries = _sb_fwd(proj, W)
    cpre, yb = _conv_fwd(proj, wl["conv_w"], wl["taps"], sp["conv_b"], sp["conv_ln_g"], sp["conv_ln_b"], W)
    yc = _sgu_fwd(proj, sp["sgu_w"], sp["sgu_bt"], sp["sgu_ln_g"], sp["sgu_ln_b"], W)
    os_, ls_ = zip(*[_dil_fwd_group(proj, W, gi, dil) for gi, (_, dil) in enumerate(DIL_PATTERNS)])
    od, lse, yd = _dil_combine(proj, os_, ls_, W)
    ys = jnp.stack([ya, yb, yc, yd])
    yproj, merged = _branch_merge(ys, wl["w_br"], gates)

    def res_ep(acc, ex, outs):
        outs[0][...] = ex[0][...] + acc

    tm_o, tn_o = _pick(x.shape[0], 1024, SUBLANES), _pick(D, 1024)
    xn = _mm_nn("out_proj", merged, wl["w_out"], F32, epilogue=res_ep, extras=(x,),
                extra_specs=(pl.BlockSpec((tm_o, tn_o), lambda i, j, k: (i, j)),), tm=tm_o, tn=tn_o)
    saved = dict(x=x, h=h, proj=proj, gates=gates, oa=oa, sb_carries=sb_carries, cpre=cpre, od=od, lse=lse, ys=ys, yproj=yproj, merged=merged)
    return xn, saved


def _layer_bwd(dout, sv, wl, sp):
    W = sp["conv_b"].shape[-1]
    proj = sv["proj"]
    dmerged = _mm_nt("out_proj_bwd_dx", dout, wl["w_out"], F32, tm=512)
    g_w_out = _mm_tn("out_proj_bwd_dw", sv["merged"], dout)
    dyp, dzg, db_gate = _merge_bwd(dmerged, sv["gates"], sv["yproj"])
    dy = _branch_bwd_dy(dyp, wl["w_br"])
    g_w_br = _branch_bwd_dw(sv["ys"], dyp)
    g_w_gate = _mm_tn("gate_bwd_dw", sv["h"], dzg)

    a_dq, a_dk, a_dv, a_dg = _sb_bwd(proj, sv["oa"], sv["sb_carries"], dy, W)
    a_qkv = _to_bf16("sb_bwd_cast", [a_dq, a_dk, a_dv])

    dc, b_dg, conv_stats = _conv_bwd_ln(proj, sv["cpre"], dy, sp["conv_ln_g"], sp["conv_ln_b"], W)
    b_da, b_db, g_conv_w = _conv_bwd_taps(proj, dc, wl["conv_w"], wl["taps"], W)

    c_du, c_dv, c_dg, g_sgu_w, g_sgu_bt, sgu_stats = _sgu_bwd(proj, dy, sp["sgu_w"], sp["sgu_bt"],
                                                              sp["sgu_ln_g"], sp["sgu_ln_b"], W)

    do, delta, d_dg = _dil_bwd_pre(proj, sv["od"], dy, W)
    dqs, dks, dvs = zip(*[_dil_bwd_group(proj, do, sv["lse"], delta, W, gi, dil)
                          for gi, (_, dil) in enumerate(DIL_PATTERNS)])
    d_qk = _to_bf16("dil_bwd_cast", [*dqs, *dks])
    d_dv = _sum3_bf16(*dvs)

    dproj = jnp.concatenate([a_qkv, a_dg, b_da, b_db, b_dg, c_du, c_dv, c_dg, d_qk, d_dv, d_dg], axis=1)
    g_w_in = _mm_tn("proj_bwd_dw", sv["h"], dproj, tn=768)
    dh = _mm_nt("gate_bwd_dh", dzg, wl["w_gate"], F32)
    dh = _mm_nt("proj_bwd_dh", dproj, wl["w_in"], F32, addend=dh)
    dx, dnorm = _rmsnorm_bwd(sv["x"], dh, dout, sp["norm_g"])

    K = wl["taps"]
    small = dict(norm_g=dnorm[0], conv_w=g_conv_w[:K], conv_b=conv_stats[2], conv_ln_g=conv_stats[0],
                 conv_ln_b=conv_stats[1], sgu_ln_g=sgu_stats[0], sgu_ln_b=sgu_stats[1], sgu_w=g_sgu_w,
                 sgu_b=g_sgu_bt.T, b_gate=db_gate[0])
    big = dict(w_in=g_w_in, w_gate=g_w_gate, w_br=g_w_br, w_out=g_w_out)
    return dx, big, small


HBM = pl.BlockSpec(memory_space=pl.ANY)


def _mesh_pos():
    return lax.axis_index("x"), lax.axis_index("y"), lax.axis_index("c")


def _other_chips(x, y):
    return [(1 - x, y), (x, 1 - y), (1 - x, 1 - y)]


def _shard_of(ref, axis, size, index):
    idx = [slice(None)] * len(ref.shape)
    idx[axis] = pl.ds(index * size, size)
    return ref.at[tuple(idx)]


def _all_gather(name, shards, axes):
    n = len(shards)
    out_shape = []
    for s, ax in zip(shards, axes):
        shp = list(s.shape)
        shp[ax] *= N_DEV
        out_shape.append(_sds(shp, s.dtype))

    def body(*refs):
        ins, outs = refs[:n], refs[n:2 * n]
        send, recv, lsem = refs[2 * n:]
        x, y, c = _mesh_pos()
        me, sib = (x, y, c), (x, y, 1 - c)
        chips = _other_chips(x, y)
        dev = lambda px, py, pc: 4 * px + 2 * py + pc

        def blk(a, d):
            return _shard_of(outs[a], axes[a], ins[a].shape[axes[a]], d)

        def cp(a, k, d, to, src=None):
            return pltpu.make_async_remote_copy(src_ref=blk(a, d) if src is None else src, dst_ref=blk(a, d),
                                                send_sem=send.at[a * 7 + k], recv_sem=recv.at[a * 7 + k],
                                                device_id=to, device_id_type=MESH)

        own = [pltpu.make_async_copy(ins[a], blk(a, dev(*me)), lsem.at[a]) for a in range(n)]
        for o in own:
            o.start()
        first = []
        for a in range(n):
            first.append(cp(a, 0, dev(*me), sib, src=ins[a]))
            first += [cp(a, 1 + j, dev(*me), (*ch, c), src=ins[a]) for j, ch in enumerate(chips)]
        for f in first:
            f.start()
        passed = []
        for j, ch in enumerate(chips):
            for a in range(n):
                cp(a, 1 + j, dev(*ch, c), me).wait_recv()
                p = cp(a, 4 + j, dev(*ch, c), sib)
                p.start()
                passed.append(p)
        for a in range(n):
            cp(a, 0, dev(*sib), me).wait_recv()
            for j, ch in enumerate(chips):
                cp(a, 4 + j, dev(*ch, 1 - c), me).wait_recv()
        for f in first + passed:
            f.wait_send()
        for o in own:
            o.wait()

    return pl.pallas_call(body, name=name, in_specs=[HBM] * n, out_specs=[HBM] * n, out_shape=out_shape,
                          scratch_shapes=[pltpu.SemaphoreType.DMA((7 * n,)), pltpu.SemaphoreType.DMA((7 * n,)),
                                          pltpu.SemaphoreType.DMA((n,))])(*shards)


def _rs_pair(name, grads, axes):
    n = len(grads)
    sizes = [g.shape[ax] // N_DEV for g, ax in zip(grads, axes)]
    out_shape = []
    for g, ax, sz in zip(grads, axes, sizes):
        shp = list(g.shape)
        shp[ax] = sz
        out_shape.append(_sds([N_CHIP] + shp, g.dtype))

    def body(*refs):
        ins, outs = refs[:n], refs[n:2 * n]
        send, recv = refs[2 * n:]
        x, y, c = _mesh_pos()
        cps = []
        for a in range(n):
            for k in range(N_CHIP):
                cps.append(pltpu.make_async_remote_copy(
                    src_ref=_shard_of(ins[a], axes[a], sizes[a], 2 * k + (1 - c)), dst_ref=outs[a].at[k],
                    send_sem=send.at[a * N_CHIP + k], recv_sem=recv.at[a * N_CHIP + k],
                    device_id=(x, y, 1 - c), device_id_type=MESH))
        for cp in cps:
            cp.start()
        for cp in cps:
            cp.wait()

    return pl.pallas_call(body, name=name, in_specs=[HBM] * n, out_specs=[HBM] * n, out_shape=out_shape,
                          scratch_shapes=[pltpu.SemaphoreType.DMA((N_CHIP * n,)),
                                          pltpu.SemaphoreType.DMA((N_CHIP * n,))])(*grads)


def _pair_sum(name, grad, recv, axis, core):
    R, C = grad.shape
    _, r, cw = recv.shape
    tr = _pick(r, 512, SUBLANES)
    nr = r // tr

    def body(c_ref, g_ref, r_ref, o_ref):
        o_ref[...] = (g_ref[...] + r_ref[...]).astype(BF16)

    if axis == 1:
        gspec = pl.BlockSpec((tr, cw), lambda k, i, c_ref: (i, 2 * k + c_ref[0]))
    else:
        gspec = pl.BlockSpec((tr, cw), lambda k, i, c_ref: ((2 * k + c_ref[0]) * nr + i, 0))
    part = pl.BlockSpec((None, tr, cw), lambda k, i, c_ref: (k, i, 0))
    return pl.pallas_call(
        body, name=name, out_shape=_sds((N_CHIP, r, cw), BF16),
        grid_spec=pltpu.PrefetchScalarGridSpec(num_scalar_prefetch=1, grid=(N_CHIP, nr), in_specs=[gspec, part],
                                               out_specs=part),
        compiler_params=_cp("parallel", "parallel"))(core, grad, recv)


def _rs_chips(name, parts):
    n = len(parts)

    def body(*refs):
        ins, outs = refs[:n], refs[n:2 * n]
        send, recv, lsem = refs[2 * n:]
        x, y, c = _mesh_pos()
        mine = 2 * x + y
        own = [pltpu.make_async_copy(ins[a].at[mine], outs[a].at[mine], lsem.at[a]) for a in range(n)]
        for o in own:
            o.start()
        cps = []
        for a in range(n):
            for j, (px, py) in enumerate(_other_chips(x, y)):
                cps.append(pltpu.make_async_remote_copy(
                    src_ref=ins[a].at[2 * px + py], dst_ref=outs[a].at[mine],
                    send_sem=send.at[a * 3 + j], recv_sem=recv.at[a * 3 + j],
                    device_id=(px, py, c), device_id_type=MESH))
        for cp in cps:
            cp.start()
        for a in range(n):
            for j, (px, py) in enumerate(_other_chips(x, y)):
                pltpu.make_async_remote_copy(
                    src_ref=ins[a].at[mine], dst_ref=outs[a].at[2 * px + py],
                    send_sem=send.at[a * 3 + j], recv_sem=recv.at[a * 3 + j],
                    device_id=(x, y, c), device_id_type=MESH).wait_recv()
        for cp in cps:
            cp.wait_send()
        for o in own:
            o.wait()

    return pl.pallas_call(body, name=name, in_specs=[HBM] * n, out_specs=[HBM] * n,
                          out_shape=[_sds(p.shape, p.dtype) for p in parts],
                          scratch_shapes=[pltpu.SemaphoreType.DMA((3 * n,)), pltpu.SemaphoreType.DMA((3 * n,)),
                                          pltpu.SemaphoreType.DMA((n,))])(*parts)


HBM_ONLY = pl.BlockSpec(memory_space=pltpu.HBM)
SEM_SPEC = pl.BlockSpec(memory_space=pltpu.SEMAPHORE)
N_PEER = N_DEV - 1


def _peer(x, y, c, m):
    flip = lambda v, bit: 1 - v if bit else v
    return flip(x, (m >> 2) & 1), flip(y, (m >> 1) & 1), flip(c, m & 1)


def _exchange_copies(kind, srcs, lands, send, recv, axes):
    x, y, c = _mesh_pos()
    me = 4 * x + 2 * y + c
    cps = []
    for a in range(len(srcs)):
        for m in range(1, N_DEV):
            px, py, pc = _peer(x, y, c, m)
            if kind == "gather":
                src = srcs[a]
                dst = _shard_of(lands[a], axes[a], srcs[a].shape[axes[a]], me)
            else:
                src = _shard_of(srcs[a], axes[a], srcs[a].shape[axes[a]] // N_DEV, 4 * px + 2 * py + pc)
                dst = lands[a].at[me]
            cps.append(pltpu.make_async_remote_copy(
                src_ref=src, dst_ref=dst, send_sem=send.at[a * N_PEER + m - 1], recv_sem=recv.at[a * N_PEER + m - 1],
                device_id=(px, py, pc), device_id_type=MESH))
    return cps


def _exchange_start(name, kind, srcs, lands, axes, carry):
    n = len(srcs)
    hbm = lambda a: pltpu.with_memory_space_constraint(a, pltpu.HBM)

    def body(*refs):
        send, recv = refs[2 * n + 1], refs[2 * n + 2]
        for cp in _exchange_copies(kind, refs[:n], refs[n:2 * n], send, recv, axes):
            cp.start()

    thru = [*srcs, *lands, carry]
    res = pl.pallas_call(
        body, name=name,
        out_shape=(pltpu.SemaphoreType.DMA((n * N_PEER,)), pltpu.SemaphoreType.DMA((n * N_PEER,)),
                   *[pltpu.HBM(a.shape, a.dtype) for a in thru]),
        in_specs=[HBM_ONLY] * len(thru), out_specs=(SEM_SPEC, SEM_SPEC, *[HBM_ONLY] * len(thru)),
        input_output_aliases={i: 2 + i for i in range(len(thru))},
        compiler_params=pltpu.CompilerParams(has_side_effects=pltpu.SideEffectType.DATAFLOW_SIDE_EFFECTING),
    )(*[hbm(a) for a in thru])
    return res[0], res[1], list(res[2:2 + n]), list(res[2 + n:2 + 2 * n]), res[2 + 2 * n]


def _exchange_wait(name, kind, send, recv, srcs, lands, axes, after):
    n = len(srcs)

    def body(*refs):
        send_ref, recv_ref = refs[2 * n], refs[2 * n + 1]
        for cp in _exchange_copies(kind, refs[:n], refs[n:2 * n], send_ref, recv_ref, axes):
            cp.wait_send()
            cp.wait_recv()

    thru = [*srcs, *lands]
    res = pl.pallas_call(
        body, name=name, out_shape=tuple(pltpu.HBM(a.shape, a.dtype) for a in thru),
        in_specs=[*[HBM_ONLY] * len(thru), SEM_SPEC, SEM_SPEC, HBM], out_specs=tuple([HBM_ONLY] * len(thru)),
        input_output_aliases={i: i for i in range(len(thru))},
        compiler_params=pltpu.CompilerParams(has_side_effects=pltpu.SideEffectType.DATAFLOW_SIDE_EFFECTING),
    )(*thru, send, recv, after)
    return list(res[n:])


def _adamw_scatter(name, land, grad, axis, me, w, m, v):
    P, R, C = land.shape
    tr = _pick(R, max(SUBLANES, (1 << 18) // C // 16 * 16), 16)
    nr = R // tr

    def body(me_ref, l_ref, o_ref, w_ref, m_ref, v_ref, g_ref, d_ref, mo_ref, vo_ref):
        own = o_ref[...].astype(F32)
        g = jnp.where(me_ref[0] == 0, own, l_ref[0].astype(F32))
        for k in range(1, P):
            g = g + jnp.where(me_ref[0] == k, own, l_ref[k].astype(F32))
        mn = ADAM_B1 * m_ref[...] + (1.0 - ADAM_B1) * g
        vn = ADAM_B2 * v_ref[...] + (1.0 - ADAM_B2) * (g * g)
        m_hat = mn / (1.0 - ADAM_B1 ** ADAM_STEP)
        v_hat = vn / (1.0 - ADAM_B2 ** ADAM_STEP)
        g_ref[...] = g
        d_ref[...] = -ADAM_LR * (m_hat / (jnp.sqrt(v_hat) + ADAM_EPS) + ADAM_WD * w_ref[...])
        mo_ref[...] = mn
        vo_ref[...] = vn

    if axis == 1:
        own_spec = pl.BlockSpec((tr, C), lambda i, me_ref: (i, me_ref[0]))
    else:
        own_spec = pl.BlockSpec((tr, C), lambda i, me_ref: (me_ref[0] * nr + i, 0))
    row = pl.BlockSpec((tr, C), lambda i, me_ref: (i, 0))
    return pl.pallas_call(
        body, name=name, out_shape=[_sds((R, C), F32)] * 4,
        grid_spec=pltpu.PrefetchScalarGridSpec(
            num_scalar_prefetch=1, grid=(nr,),
            in_specs=[pl.BlockSpec((P, tr, C), lambda i, me_ref: (0, i, 0)), own_spec, row, row, row],
            out_specs=[row] * 4),
        compiler_params=_cp("parallel"))(me, land, grad, w, m, v)


def _adamw_sum(name, parts, w, m, v):
    P, R, C = parts.shape
    tr = _pick(R, max(SUBLANES, (1 << 19) // C // SUBLANES * SUBLANES), SUBLANES)

    def body(p_ref, w_ref, m_ref, v_ref, g_ref, d_ref, mo_ref, vo_ref):
        g = p_ref[0].astype(F32)
        for k in range(1, P):
            g = g + p_ref[k].astype(F32)
        mn = ADAM_B1 * m_ref[...] + (1.0 - ADAM_B1) * g
        vn = ADAM_B2 * v_ref[...] + (1.0 - ADAM_B2) * (g * g)
        m_hat = mn / (1.0 - ADAM_B1 ** ADAM_STEP)
        v_hat = vn / (1.0 - ADAM_B2 ** ADAM_STEP)
        g_ref[...] = g
        d_ref[...] = -ADAM_LR * (m_hat / (jnp.sqrt(v_hat) + ADAM_EPS) + ADAM_WD * w_ref[...])
        mo_ref[...] = mn
        vo_ref[...] = vn

    row = pl.BlockSpec((tr, C), lambda i: (i, 0))
    return pl.pallas_call(body, name=name, grid=(R // tr,),
                          in_specs=[pl.BlockSpec((P, tr, C), lambda i: (0, i, 0)), row, row, row],
                          out_specs=[row] * 4, out_shape=[_sds((R, C), F32)] * 4,
                          compiler_params=_cp("parallel"))(parts, w, m, v)


def _rows128(a, pad_rows=SUBLANES):
    flat = a.reshape(-1, LANES)
    pad = (-flat.shape[0]) % pad_rows
    return jnp.pad(flat, ((0, pad), (0, 0))) if pad else flat


SMALL = ("norm_g", "conv_b", "conv_ln_g", "conv_ln_b", "sgu_ln_g", "sgu_ln_b", "sgu_w", "sgu_b", "b_gate", "final_g")


def kernel(x, norm_g, w_in, conv_w, conv_b, conv_ln_g, conv_ln_b, sgu_ln_g, sgu_ln_b, sgu_w, sgu_b, w_branch, w_gate, b_gate, w_out, final_g, loss_target, m_norm_g, m_w_in, m_conv_w, m_conv_b, m_conv_ln_g, m_conv_ln_b, m_sgu_ln_g, m_sgu_ln_b, m_sgu_w, m_sgu_b, m_w_branch, m_w_gate, m_b_gate, m_w_out, m_final_g, v_norm_g, v_w_in, v_conv_w, v_conv_b, v_conv_ln_g, v_conv_ln_b, v_sgu_ln_g, v_sgu_ln_b, v_sgu_w, v_sgu_b, v_w_branch, v_w_gate, v_b_gate, v_w_out, v_final_g):
    L, D = norm_g.shape
    W = conv_b.shape[1]
    taps = conv_w.shape[1]
    weights = dict(norm_g=norm_g, w_in=w_in, conv_w=conv_w, conv_b=conv_b, conv_ln_g=conv_ln_g, conv_ln_b=conv_ln_b,
                   sgu_ln_g=sgu_ln_g, sgu_ln_b=sgu_ln_b, sgu_w=sgu_w, sgu_b=sgu_b, w_branch=w_branch, w_gate=w_gate,
                   b_gate=b_gate, w_out=w_out, final_g=final_g)
    mom_m = dict(norm_g=m_norm_g, w_in=m_w_in, conv_w=m_conv_w, conv_b=m_conv_b, conv_ln_g=m_conv_ln_g,
                 conv_ln_b=m_conv_ln_b, sgu_ln_g=m_sgu_ln_g, sgu_ln_b=m_sgu_ln_b, sgu_w=m_sgu_w, sgu_b=m_sgu_b,
                 w_branch=m_w_branch, w_gate=m_w_gate, b_gate=m_b_gate, w_out=m_w_out, final_g=m_final_g)
    mom_v = dict(norm_g=v_norm_g, w_in=v_w_in, conv_w=v_conv_w, conv_b=v_conv_b, conv_ln_g=v_conv_ln_g,
                 conv_ln_b=v_conv_ln_b, sgu_ln_g=v_sgu_ln_g, sgu_ln_b=v_sgu_ln_b, sgu_w=v_sgu_w, sgu_b=v_sgu_b,
                 w_branch=v_w_branch, w_gate=v_w_gate, b_gate=v_b_gate, w_out=v_w_out, final_g=v_final_g)
    core = lax.axis_index("c").astype(jnp.int32).reshape(1)
    me = 4 * lax.axis_index("x") + 2 * lax.axis_index("y") + lax.axis_index("c")

    gather_axes = [1, 1, 2, 0, 1]

    def gather_start(l, carry):
        shards = [w_in[l].astype(BF16), w_gate[l].astype(BF16), w_branch[l].astype(BF16), w_out[l].astype(BF16),
                  jnp.pad(conv_w[l], ((0, CONV_HALO - taps), (0, 0)))]
        lands = []
        for s, ax in zip(shards, gather_axes):
            full = list(s.shape)
            full[ax] *= N_DEV
            lands.append(lax.dynamic_update_slice_in_dim(jnp.zeros(full, s.dtype), s, me * s.shape[ax], ax))
        return _exchange_start(f"gather_start_{l}", "gather", shards, lands, gather_axes, carry)

    def gather_wait(l, started, after):
        send, recv, shards, lands, _ = started
        full = _exchange_wait(f"gather_wait_{l}", "gather", send, recv, shards, lands, gather_axes, after)
        return dict(w_in=full[0], w_gate=full[1], w_br=full[2], w_out=full[3], conv_w=full[4], taps=taps)

    xs = x[0]
    saved, gathered, smalls = [], [], []
    started = gather_start(0, xs)
    xs = started[4]
    wl = gather_wait(0, started, xs)
    for l in range(L):
        if l + 1 < L:
            started = gather_start(l + 1, xs)
            xs = started[4]
        sp = dict(norm_g=norm_g[l][None], conv_b=conv_b[l][None], conv_ln_g=conv_ln_g[l][None],
                  conv_ln_b=conv_ln_b[l][None], sgu_ln_g=sgu_ln_g[l][None], sgu_ln_b=sgu_ln_b[l][None],
                  sgu_w=sgu_w[l], sgu_bt=sgu_b[l].T, b_gate=b_gate[l][None])
        xs, sv = _layer_fwd(xs, wl, sp)
        saved.append(sv)
        gathered.append(wl)
        smalls.append(sp)
        if l + 1 < L:
            wl = gather_wait(l + 1, started, xs)

    dx, d_final, loss_part = _loss_head(xs, loss_target[0], final_g[None])
    loss = lax.psum(loss_part[0, 0], ("x", "y", "c"))

    big_names = ("w_in", "w_gate", "w_branch", "w_out")
    big_axes = (1, 1, 1, 0)
    me1 = me.astype(jnp.int32).reshape(1)
    outs = {}
    small_grads = []

    def scatter_start(l, g2d, carry):
        lands = []
        for g, ax in zip(g2d, big_axes):
            blk = list(g.shape)
            blk[ax] //= N_DEV
            lands.append(jnp.zeros([N_DEV] + blk, g.dtype))
        return _exchange_start(f"scatter_start_{l}", "scatter", g2d, lands, big_axes, carry)

    def scatter_finish(l, started, after):
        send, recv, g2d, lands, _ = started
        lands = _exchange_wait(f"scatter_wait_{l}", "scatter", send, recv, g2d, lands, big_axes, after)
        for nm, land, g, ax in zip(big_names, lands, g2d, big_axes):
            shard2d = land.shape[1:]
            outs.setdefault(nm, {})[l] = _adamw_scatter(
                "adamw_" + nm, land, g, ax, me1, weights[nm][l].reshape(shard2d), mom_m[nm][l].reshape(shard2d),
                mom_v[nm][l].reshape(shard2d))

    pending = None
    for l in reversed(range(L)):
        dx, big, small = _layer_bwd(dx, saved[l], gathered[l], smalls[l])
        small_grads.append(small)
        if pending is not None:
            scatter_finish(l + 1, pending, dx)
        g2d = [big["w_in"], big["w_gate"], big["w_br"].reshape(-1, D), big["w_out"]]
        if l > 0:
            pending = scatter_start(l, g2d, dx)
            dx = pending[4]
        else:
            pending = scatter_start(l, g2d, d_final)
            d_final = pending[4]
    small_grads.reverse()

    sg = {nm: jnp.stack([small_grads[l][nm] for l in range(L)]) for nm in SMALL[:-1]}
    sg["final_g"] = d_final[0]
    conv_w_full = jnp.stack([small_grads[l]["conv_w"] for l in range(L)])
    segs = [_rows128(sg[nm]) for nm in SMALL] + [_rows128(conv_w_full)]
    offs = [0]
    for s in segs:
        offs.append(offs[-1] + s.shape[0])
    pack = jnp.concatenate(segs, axis=0)
    (allp,) = _all_gather("gather_small_grads", [pack[None]], [0])

    def packed(src):
        return jnp.concatenate([_rows128(src[nm]) for nm in SMALL] + [jnp.zeros_like(segs[-1])], axis=0)

    s_g, s_d, s_m, s_v = _adamw_sum("adamw_small", allp, packed(weights), packed(mom_m), packed(mom_v))
    scatter_finish(0, pending, s_g)
    big_out = {nm: [jnp.stack([outs[nm][l][q] for l in range(L)]).reshape(weights[nm].shape) for q in range(4)]
               for nm in big_names}

    def unpack(buf, i, like):
        n = like.size // LANES
        return buf[offs[i]:offs[i] + n].reshape(like.shape)

    small_out = {nm: [unpack(b, i, weights[nm]) for b in (s_g, s_d, s_m, s_v)] for i, nm in enumerate(SMALL)}
    Wc = conv_w.shape[2]
    cw_sum = lax.dynamic_slice_in_dim(unpack(s_g, len(SMALL), conv_w_full), me * Wc, Wc, axis=2)
    cshape = (L * conv_w.shape[1], Wc)
    conv_out = [o.reshape(conv_w.shape) for o in _adamw_sum(
        "adamw_conv_w", cw_sum.reshape((1,) + cshape), conv_w.reshape(cshape), m_conv_w.reshape(cshape),
        v_conv_w.reshape(cshape))]

    order = ["norm_g", "w_in", "conv_w", "conv_b", "conv_ln_g", "conv_ln_b", "sgu_ln_g", "sgu_ln_b", "sgu_w", "sgu_b",
             "w_branch", "w_gate", "b_gate", "w_out", "final_g"]
    table = dict(small_out)
    table.update(big_out)
    table["conv_w"] = conv_out
    result = [loss, dx[None]]
    for q in range(4):
        result += [table[nm][q] for nm in order]
    return tuple(result)
```

```python
import functools

import jax
import jax.numpy as jnp
from jax import lax
from jax.experimental import pallas as pl
from jax.experimental.pallas import tpu as pltpu

F32 = jnp.float32
BF16 = jnp.bfloat16
LANES = 128
SUBLANES = 8
CONV_HALO = 32
SB_QUERY_BLOCK = 512
SB_QUERY_BLOCK_BWD = 256
SB_KEY_BLOCK = 1024
DIL_STEP_ROWS = 2048
DIL_UNROLL = 4
NORM_EPS = 1e-6
NEG = -1e30
N_DEV = 8
N_CHIP = 4
DIL_PATTERNS = ((128, 1), (512, 4), (2048, 16))

ADAM_LR = 0.001
ADAM_B1 = 0.9
ADAM_B2 = 0.999
ADAM_EPS = 1e-08
ADAM_WD = 0.01
ADAM_STEP = 10

MESH = pl.DeviceIdType.MESH
NN = (((1,), (0,)), ((), ()))
NT = (((1,), (1,)), ((), ()))
TN = (((0,), (0,)), ((), ()))
VMEM_LIMIT = 52 << 20


def _sds(shape, dtype):
    return jax.ShapeDtypeStruct(tuple(shape), dtype)


def _cp(*sem):
    return pltpu.CompilerParams(dimension_semantics=tuple(sem), vmem_limit_bytes=VMEM_LIMIT)


def _pick(n, target, quantum=LANES):
    if n <= target:
        return n
    t = (target // quantum) * quantum
    while t >= quantum:
        if n % t == 0:
            return t
        t -= quantum
    return n


def _sigmoid(x):
    return 1.0 / (1.0 + jnp.exp(-x))


def _silu(x):
    return x * _sigmoid(x)


def _dsilu(x):
    s = _sigmoid(x)
    return s * (1.0 + x * (1.0 - s))


_GELU_K = 0.7978845608028654
_GELU_A = 0.044715


def _gelu(x):
    return 0.5 * x * (1.0 + jnp.tanh(_GELU_K * (x + _GELU_A * x * x * x)))


def _dgelu(x):
    t = jnp.tanh(_GELU_K * (x + _GELU_A * x * x * x))
    return 0.5 * (1.0 + t) + 0.5 * x * (1.0 - t * t) * _GELU_K * (1.0 + 3.0 * _GELU_A * x * x)


def _ln_stats(v):
    mu = jnp.mean(v, axis=-1, keepdims=True)
    d = v - mu
    var = jnp.mean(d * d, axis=-1, keepdims=True)
    r = lax.rsqrt(var + NORM_EPS)
    return d * r, r


def _ln_bwd(dxh, xh, r):
    return r * (dxh - jnp.mean(dxh, axis=-1, keepdims=True) - xh * jnp.mean(dxh * xh, axis=-1, keepdims=True))


def _acc_rows(ref, row, val, first):
    @pl.when(first)
    def _():
        ref[...] = jnp.zeros_like(ref)
    ref[row:row + 1, :] += val


def _mm(name, a, b, *, grid, kaxis, dims, a_spec, b_spec, acc_shape, out_shape, out_specs,
        epilogue, extras=(), extra_specs=(), sem):
    nk = grid[kaxis]
    ne = len(extras)

    def body(*refs):
        a_ref, b_ref = refs[0], refs[1]
        ex = refs[2:2 + ne]
        outs = refs[2 + ne:-1]
        acc = refs[-1]
        k = pl.program_id(kaxis)

        @pl.when(k == 0)
        def _():
            acc[...] = jnp.zeros_like(acc)

        acc[...] += lax.dot_general(a_ref[...].astype(BF16), b_ref[...].astype(BF16), dims,
                                    preferred_element_type=F32)

        @pl.when(k == nk - 1)
        def _():
            epilogue(acc[...], ex, outs)

    return pl.pallas_call(
        body, name=name, grid=grid, in_specs=[a_spec, b_spec, *extra_specs], out_specs=out_specs,
        out_shape=out_shape, scratch_shapes=[pltpu.VMEM(acc_shape, F32)], compiler_params=_cp(*sem),
    )(a, b, *extras)


def _store(dtype):
    def ep(acc, ex, outs):
        outs[0][...] = acc.astype(dtype)
    return ep


def _mm_nn(name, a, b, out_dtype, epilogue=None, extras=(), extra_specs=(), tm=1024, tn=1024):
    M, K = a.shape
    N = b.shape[1]
    tm, tn = _pick(M, tm, SUBLANES), _pick(N, tn)
    return _mm(name, a, b, grid=(M // tm, N // tn, 1), kaxis=2, dims=NN,
               a_spec=pl.BlockSpec((tm, K), lambda i, j, k: (i, 0)),
               b_spec=pl.BlockSpec((K, tn), lambda i, j, k: (0, j)),
               acc_shape=(tm, tn), out_shape=_sds((M, N), out_dtype),
               out_specs=pl.BlockSpec((tm, tn), lambda i, j, k: (i, j)),
               epilogue=epilogue or _store(out_dtype), extras=extras, extra_specs=extra_specs,
               sem=("parallel", "parallel", "arbitrary"))


def _mm_nt(name, a, b, out_dtype, addend=None, tm=1024, tn=1024, tk=1024):
    M, K = a.shape
    N = b.shape[0]
    tm, tn, tk = _pick(M, tm, SUBLANES), _pick(N, tn), _pick(K, tk)
    extras, extra_specs = (), ()
    if addend is not None:
        extras = (addend,)
        extra_specs = (pl.BlockSpec((tm, tn), lambda i, j, k: (i, j)),)

    def ep(acc, ex, outs):
        if ex:
            acc = acc + ex[0][...]
        outs[0][...] = acc.astype(out_dtype)

    return _mm(name, a, b, grid=(M // tm, N // tn, K // tk), kaxis=2, dims=NT,
               a_spec=pl.BlockSpec((tm, tk), lambda i, j, k: (i, k)),
               b_spec=pl.BlockSpec((tn, tk), lambda i, j, k: (j, k)),
               acc_shape=(tm, tn), out_shape=_sds((M, N), out_dtype),
               out_specs=pl.BlockSpec((tm, tn), lambda i, j, k: (i, j)),
               epilogue=ep, extras=extras, extra_specs=extra_specs,
               sem=("parallel", "parallel", "arbitrary"))


def _mm_tn(name, a, b, tm=1024, tn=1024, tk=1024, out_dtype=BF16):
    K, M = a.shape
    N = b.shape[1]
    tm, tn, tk = _pick(M, tm), _pick(N, tn), _pick(K, tk, SUBLANES)
    return _mm(name, a, b, grid=(M // tm, N // tn, K // tk), kaxis=2, dims=TN,
               a_spec=pl.BlockSpec((tk, tm), lambda i, j, k: (k, i)),
               b_spec=pl.BlockSpec((tk, tn), lambda i, j, k: (k, j)),
               acc_shape=(tm, tn), out_shape=_sds((M, N), out_dtype),
               out_specs=pl.BlockSpec((tm, tn), lambda i, j, k: (i, j)),
               epilogue=_store(out_dtype), sem=("parallel", "parallel", "arbitrary"))


def _rmsnorm_fwd(x, g_row):
    T, D = x.shape
    tm = _pick(T, 512, SUBLANES)

    def body(x_ref, g_ref, h_ref):
        xv = x_ref[...]
        r = lax.rsqrt(jnp.mean(xv * xv, axis=-1, keepdims=True) + NORM_EPS)
        h_ref[...] = (xv * r * g_ref[...]).astype(BF16)

    row = pl.BlockSpec((tm, D), lambda i: (i, 0))
    return pl.pallas_call(body, name="rmsnorm_fwd", grid=(T // tm,),
                          in_specs=[row, pl.BlockSpec((1, D), lambda i: (0, 0))], out_specs=row,
                          out_shape=_sds((T, D), BF16), compiler_params=_cp("parallel"))(x, g_row)


def _rmsnorm_bwd(x, dh, dout, g_row):
    T, D = x.shape
    tm = _pick(T, 256, SUBLANES)

    def body(x_ref, dh_ref, do_ref, g_ref, dx_ref, dg_ref):
        xv = x_ref[...]
        r = lax.rsqrt(jnp.mean(xv * xv, axis=-1, keepdims=True) + NORM_EPS)
        xh = xv * r
        dhv = dh_ref[...]
        dxh = dhv * g_ref[...]
        dx_ref[...] = do_ref[...] + r * (dxh - xh * jnp.mean(dxh * xh, axis=-1, keepdims=True))
        _acc_rows(dg_ref, 0, jnp.sum(dhv * xh, axis=0, keepdims=True), pl.program_id(0) == 0)

    row = pl.BlockSpec((tm, D), lambda i: (i, 0))
    return pl.pallas_call(body, name="rmsnorm_bwd", grid=(T // tm,),
                          in_specs=[row, row, row, pl.BlockSpec((1, D), lambda i: (0, 0))],
                          out_specs=[row, pl.BlockSpec((SUBLANES, D), lambda i: (0, 0))],
                          out_shape=[_sds((T, D), F32), _sds((SUBLANES, D), F32)],
                          compiler_params=_cp("arbitrary"))(x, dh, dout, g_row)


def _loss_head(x, target, g_row):
    T, D = x.shape
    tm = _pick(T, 256, SUBLANES)

    def body(x_ref, t_ref, g_ref, dx_ref, dg_ref, loss_ref):
        first = pl.program_id(0) == 0
        xv = x_ref[...]
        g = g_ref[...]
        r = lax.rsqrt(jnp.mean(xv * xv, axis=-1, keepdims=True) + NORM_EPS)
        xh = xv * r
        err = xh * g - t_ref[...]
        part = 0.5 * jnp.sum(jnp.mean(err * err, axis=-1, keepdims=True), axis=0, keepdims=True)

        @pl.when(first)
        def _():
            loss_ref[...] = jnp.zeros_like(loss_ref)

        loss_ref[...] += jnp.broadcast_to(part, loss_ref.shape)
        dy = err / D
        _acc_rows(dg_ref, 0, jnp.sum(dy * xh, axis=0, keepdims=True), first)
        dxh = dy * g
        dx_ref[...] = r * (dxh - xh * jnp.mean(dxh * xh, axis=-1, keepdims=True))

    row = pl.BlockSpec((tm, D), lambda i: (i, 0))
    return pl.pallas_call(body, name="loss_head", grid=(T // tm,),
                          in_specs=[row, row, pl.BlockSpec((1, D), lambda i: (0, 0))],
                          out_specs=[row, pl.BlockSpec((SUBLANES, D), lambda i: (0, 0)),
                                     pl.BlockSpec((SUBLANES, LANES), lambda i: (0, 0))],
                          out_shape=[_sds((T, D), F32), _sds((SUBLANES, D), F32), _sds((SUBLANES, LANES), F32)],
                          compiler_params=_cp("arbitrary"))(x, target, g_row)


def _tri(cmp):
    r = lax.broadcasted_iota(jnp.int32, (LANES, LANES), 0)
    c = lax.broadcasted_iota(jnp.int32, (LANES, LANES), 1)
    return cmp(r, c).astype(BF16)


def _scan_mm(x, tri):
    hi = x.astype(BF16)
    lo = (x - hi.astype(F32)).astype(BF16)
    return (jnp.dot(hi, tri, preferred_element_type=F32) + jnp.dot(lo, tri, preferred_element_type=F32))


def _sb_scores(qs, kb, causal):
    z = lax.dot_general(qs, kb, NT, preferred_element_type=F32)
    lb = jnp.minimum(z, 0.0) - jnp.log(1.0 + jnp.exp(-jnp.abs(z)))
    l1 = lb - z
    return lb, (l1 if causal is None else jnp.where(causal, l1, 0.0))


def _masked(causal, x):
    return x if causal is None else jnp.where(causal, x, 0.0)


def _lanes(x, s):
    return x[:, s * LANES:(s + 1) * LANES]


def _sb_fwd(proj, W):
    T = proj.shape[0]
    H = W // LANES
    assert T // LANES <= LANES
    tq = _pick(T, SB_QUERY_BLOCK, LANES)
    kblk = _pick(T, SB_KEY_BLOCK, LANES)
    assert kblk % tq == 0
    nsub = kblk // LANES
    scale = LANES ** -0.5

    def body(q_ref, k_ref, v_ref, g_ref, o_ref, y_ref, c_ref, run):
        i = pl.program_id(1)
        qs = (q_ref[...] * scale).astype(BF16)
        row = i * tq + lax.broadcasted_iota(jnp.int32, (tq, kblk), 0)
        key = lax.broadcasted_iota(jnp.int32, (tq, kblk), 1)
        col = lax.broadcasted_iota(jnp.int32, (tq, LANES), 1)
        tri = _tri(lambda r, c: r > c)
        nkb = ((i + 1) * tq + kblk - 1) // kblk
        o_ref[...] = jnp.zeros_like(o_ref)
        c_ref[...] = jnp.zeros_like(c_ref)
        run[...] = jnp.zeros_like(run)

        def make_step(diagonal):
            def step(jj, carry):
                j = nkb - 1 - jj
                off = pl.multiple_of(j * kblk, kblk)
                kb = k_ref[pl.ds(off, kblk), :].astype(BF16)
                vb = v_ref[pl.ds(off, kblk), :].astype(BF16)
                causal = (key + off < row) if diagonal else None
                lb, l1m = _sb_scores(qs, kb, causal)
                c_after = run[...]
                cs = c_ref[...]
                after = [None] * nsub
                for s in reversed(range(nsub)):
                    part = _lanes(l1m, s)
                    after[s] = c_after + _scan_mm(part, tri)
                    cs = jnp.where(col == j * nsub + s, c_after, cs)
                    c_after = c_after + jnp.sum(part, axis=1, keepdims=True)
                run[...] = c_after
                c_ref[...] = cs
                w = _masked(causal, jnp.exp(lb + jnp.concatenate(after, axis=1)))
                o_ref[...] += jnp.dot(w.astype(BF16), vb, preferred_element_type=F32)
                return carry
            return step

        make_step(True)(0, 0)
        lax.fori_loop(1, nkb, make_step(False), 0)
        y_ref[...] = (o_ref[...] * _silu(g_ref[...])).astype(BF16)

    qspec = lambda c0: pl.BlockSpec((tq, LANES), lambda h, i: (i, c0 + h))
    kvspec = lambda c0: pl.BlockSpec((T, LANES), lambda h, i: (0, c0 + h))
    out = pl.BlockSpec((tq, LANES), lambda h, i: (i, h))
    return pl.pallas_call(body, name="sb_fwd", grid=(H, T // tq),
                          in_specs=[qspec(0), kvspec(H), kvspec(2 * H), qspec(3 * H)], out_specs=[out, out, out],
                          out_shape=[_sds((T, W), F32), _sds((T, W), BF16), _sds((T, W), F32)],
                          scratch_shapes=[pltpu.VMEM((tq, LANES), F32)],
                          compiler_params=_cp("parallel", "arbitrary"))(proj, proj, proj, proj)


def _sb_bwd(proj, o, carries, dy, W):
    T = proj.shape[0]
    H = W // LANES
    tq = _pick(T, SB_QUERY_BLOCK_BWD, LANES)
    kblk = _pick(T, SB_KEY_BLOCK, LANES)
    assert kblk % tq == 0
    nsub = kblk // LANES
    scale = LANES ** -0.5

    def body(q_ref, k_ref, v_ref, g_ref, o_ref, c_ref, dy_ref, dq_ref, dk_ref, dv_ref, dg_ref, run):
        i = pl.program_id(1)

        @pl.when(i == 0)
        def _():
            dk_ref[...] = jnp.zeros_like(dk_ref)
            dv_ref[...] = jnp.zeros_like(dv_ref)

        gv = g_ref[...]
        dyv = dy_ref[...]
        dg_ref[...] = (dyv * o_ref[...] * _dsilu(gv)).astype(BF16)
        dob = (dyv * _silu(gv)).astype(BF16)
        qs = (q_ref[...] * scale).astype(BF16)
        row = i * tq + lax.broadcasted_iota(jnp.int32, (tq, kblk), 0)
        key = lax.broadcasted_iota(jnp.int32, (tq, kblk), 1)
        col = lax.broadcasted_iota(jnp.int32, (tq, LANES), 1)
        tri_after = _tri(lambda r, c: r > c)
        tri_before = _tri(lambda r, c: r < c)
        dq_ref[...] = jnp.zeros_like(dq_ref)
        run[...] = jnp.zeros_like(run)

        def make_step(diagonal):
            def step(j, carry):
                off = pl.multiple_of(j * kblk, kblk)
                kb = k_ref[pl.ds(off, kblk), :].astype(BF16)
                vb = v_ref[pl.ds(off, kblk), :].astype(BF16)
                causal = (key + off < row) if diagonal else None
                lb, l1m = _sb_scores(qs, kb, causal)
                cs = c_ref[...]
                after = [jnp.sum(jnp.where(col == j * nsub + s, cs, 0.0), axis=1, keepdims=True)
                         + _scan_mm(_lanes(l1m, s), tri_after) for s in range(nsub)]
                w = _masked(causal, jnp.exp(lb + jnp.concatenate(after, axis=1)))
                gw = w * lax.dot_general(dob, vb, NT, preferred_element_type=F32)
                gpre = run[...]
                before = [None] * nsub
                for s in range(nsub):
                    part = _lanes(gw, s)
                    before[s] = gpre + _scan_mm(part, tri_before)
                    gpre = gpre + jnp.sum(part, axis=1, keepdims=True)
                run[...] = gpre
                beta = jnp.exp(lb)
                dz = _masked(causal, gw * (1.0 - beta) - jnp.concatenate(before, axis=1) * beta).astype(BF16)
                dq_ref[...] += jnp.dot(dz, kb, preferred_element_type=F32)
                dk_ref[pl.ds(off, kblk), :] += lax.dot_general(dz, qs, TN, preferred_element_type=F32)
                dv_ref[pl.ds(off, kblk), :] += lax.dot_general(w.astype(BF16), dob, TN, preferred_element_type=F32)
                return carry
            return step

        last = ((i + 1) * tq + kblk - 1) // kblk - 1
        lax.fori_loop(0, last, make_step(False), 0)
        make_step(True)(last, 0)
        dq_ref[...] = dq_ref[...] * scale

    qspec = lambda c0: pl.BlockSpec((tq, LANES), lambda h, i: (i, c0 + h))
    kvspec = lambda c0: pl.BlockSpec((T, LANES), lambda h, i: (0, c0 + h))
    blk = pl.BlockSpec((tq, LANES), lambda h, i: (i, h))
    full = pl.BlockSpec((T, LANES), lambda h, i: (0, h))
    return pl.pallas_call(body, name="sb_bwd", grid=(H, T // tq),
                          in_specs=[qspec(0), kvspec(H), kvspec(2 * H), qspec(3 * H), blk, blk,
                                    pl.BlockSpec((None, tq, LANES), lambda h, i: (0, i, h))],
                          out_specs=[blk, full, full, blk],
                          out_shape=[_sds((T, W), F32), _sds((T, W), F32), _sds((T, W), F32), _sds((T, W), BF16)],
                          scratch_shapes=[pltpu.VMEM((tq, LANES), F32)],
                          compiler_params=_cp("parallel", "arbitrary"))(proj, proj, proj, proj, o, carries, dy)


def _conv_specs(T, W, tm, col_a):
    per = tm // CONV_HALO
    cur = lambda c: pl.BlockSpec((tm, W), lambda i: (i, c))
    prev = lambda c: pl.BlockSpec((CONV_HALO, W), lambda i: (jnp.maximum(i * per - 1, 0), c))
    return cur, prev


def _conv_fwd(proj, conv_w, K, conv_b, ln_g, ln_b, W):
    T = proj.shape[0]
    tm = _pick(T, 256, CONV_HALO)
    lead = CONV_HALO - (K - 1)
    cur, prev = _conv_specs(T, W, tm, 4)

    def body(a_ref, b_ref, ah_ref, bh_ref, g_ref, w_ref, cb_ref, lg_ref, lb_ref, c_ref, y_ref, glu):
        i = pl.program_id(0)
        glu[0:CONV_HALO, :] = jnp.where(i > 0, ah_ref[...] * _sigmoid(bh_ref[...]), 0.0)
        glu[CONV_HALO:, :] = a_ref[...] * _sigmoid(b_ref[...])
        c = jnp.broadcast_to(cb_ref[...], (tm, W))
        for k in range(K):
            c = c + w_ref[k:k + 1, :] * glu[lead + k:lead + k + tm, :]
        c_ref[...] = c
        xh, _ = _ln_stats(c)
        y_ref[...] = (_silu(xh * lg_ref[...] + lb_ref[...]) * _silu(g_ref[...])).astype(BF16)

    vec = pl.BlockSpec((1, W), lambda i: (0, 0))
    row = pl.BlockSpec((tm, W), lambda i: (i, 0))
    return pl.pallas_call(body, name="conv_fwd", grid=(T // tm,),
                          in_specs=[cur(4), cur(5), prev(4), prev(5), cur(6),
                                    pl.BlockSpec((CONV_HALO, W), lambda i: (0, 0)), vec, vec, vec],
                          out_specs=[row, row], out_shape=[_sds((T, W), F32), _sds((T, W), BF16)],
                          scratch_shapes=[pltpu.VMEM((tm + CONV_HALO, W), F32)],
                          compiler_params=_cp("parallel"))(proj, proj, proj, proj, proj, conv_w, conv_b, ln_g, ln_b)


def _conv_bwd_ln(proj, c, dy, ln_g, ln_b, W):
    T = proj.shape[0]
    tm = _pick(T, 256, SUBLANES)

    def body(g_ref, c_ref, dy_ref, lg_ref, lb_ref, dc_ref, dg_ref, st_ref):
        first = pl.program_id(0) == 0
        gv = g_ref[...]
        dyv = dy_ref[...]
        xh, r = _ln_stats(c_ref[...])
        lg = lg_ref[...]
        ln = xh * lg + lb_ref[...]
        dg_ref[...] = (dyv * _silu(ln) * _dsilu(gv)).astype(BF16)
        dln = dyv * _silu(gv) * _dsilu(ln)
        dc = _ln_bwd(dln * lg, xh, r)
        dc_ref[...] = dc
        _acc_rows(st_ref, 0, jnp.sum(dln * xh, axis=0, keepdims=True), first)
        st_ref[1:2, :] += jnp.sum(dln, axis=0, keepdims=True)
        st_ref[2:3, :] += jnp.sum(dc, axis=0, keepdims=True)

    row = pl.BlockSpec((tm, W), lambda i: (i, 0))
    vec = pl.BlockSpec((1, W), lambda i: (0, 0))
    return pl.pallas_call(body, name="conv_bwd_ln", grid=(T // tm,),
                          in_specs=[pl.BlockSpec((tm, W), lambda i: (i, 6)), row,
                                    pl.BlockSpec((None, tm, W), lambda i: (1, i, 0)), vec, vec],
                          out_specs=[row, row, pl.BlockSpec((SUBLANES, W), lambda i: (0, 0))],
                          out_shape=[_sds((T, W), F32), _sds((T, W), BF16), _sds((SUBLANES, W), F32)],
                          compiler_params=_cp("arbitrary"))(proj, c, dy, ln_g, ln_b)


def _conv_bwd_taps(proj, dc, conv_w, K, W):
    T = proj.shape[0]
    tm = _pick(T, 256, CONV_HALO)
    lead = CONV_HALO - (K - 1)
    per = tm // CONV_HALO
    nblk = T // tm
    cur, prev = _conv_specs(T, W, tm, 4)

    def body(a_ref, b_ref, ah_ref, bh_ref, dc_ref, dcn_ref, w_ref, da_ref, db_ref, dw_ref, glu, dcs):
        i = pl.program_id(0)
        av = a_ref[...]
        sb = _sigmoid(b_ref[...])
        glu[0:CONV_HALO, :] = jnp.where(i > 0, ah_ref[...] * _sigmoid(bh_ref[...]), 0.0)
        glu[CONV_HALO:, :] = av * sb
        dcv = dc_ref[...]
        dcs[0:tm, :] = dcv
        dcs[tm:, :] = jnp.where(i < nblk - 1, dcn_ref[...], 0.0)

        @pl.when(i == 0)
        def _():
            dw_ref[...] = jnp.zeros_like(dw_ref)

        dglu = jnp.zeros((tm, W), F32)
        for k in range(K):
            dglu = dglu + w_ref[k:k + 1, :] * dcs[K - 1 - k:K - 1 - k + tm, :]
            dw_ref[k:k + 1, :] += jnp.sum(dcv * glu[lead + k:lead + k + tm, :], axis=0, keepdims=True)
        da_ref[...] = (dglu * sb).astype(BF16)
        db_ref[...] = (dglu * av * sb * (1.0 - sb)).astype(BF16)

    row = pl.BlockSpec((tm, W), lambda i: (i, 0))
    nxt = pl.BlockSpec((CONV_HALO, W), lambda i: (jnp.minimum((i + 1) * per, T // CONV_HALO - 1), 0))
    return pl.pallas_call(body, name="conv_bwd_taps", grid=(nblk,),
                          in_specs=[cur(4), cur(5), prev(4), prev(5), row, nxt,
                                    pl.BlockSpec((CONV_HALO, W), lambda i: (0, 0))],
                          out_specs=[row, row, pl.BlockSpec((CONV_HALO, W), lambda i: (0, 0))],
                          out_shape=[_sds((T, W), BF16), _sds((T, W), BF16), _sds((CONV_HALO, W), F32)],
                          scratch_shapes=[pltpu.VMEM((tm + CONV_HALO, W), F32), pltpu.VMEM((tm + CONV_HALO, W), F32)],
                          compiler_params=_cp("arbitrary"))(proj, proj, proj, proj, dc, dc, conv_w)


def _sgu_common(cu, cv, lg, lb, w_ref, bt_ref, z_scr, G, nch):
    u = _gelu(cu)
    xh, r = _ln_stats(_gelu(cv))
    vn = (xh * lg + lb).astype(BF16)
    rr = lax.broadcasted_iota(jnp.int32, (LANES, LANES), 0)
    cc = lax.broadcasted_iota(jnp.int32, (LANES, LANES), 1)
    tril = rr >= cc
    wts = [jnp.where(tril, w_ref[g], 0.0).astype(BF16) for g in range(G)]
    for ch in range(nch):
        rs = slice(ch * LANES, (ch + 1) * LANES)
        for g in range(G):
            cs = slice(g * LANES, (g + 1) * LANES)
            z_scr[rs, cs] = jnp.dot(wts[g], vn[rs, cs], preferred_element_type=F32) + bt_ref[:, g:g + 1]
    return u, xh, r, vn, wts, tril


def _sgu_fwd(proj, sgu_w, sgu_bt, ln_g, ln_b, W):
    T = proj.shape[0]
    G = W // LANES
    tm = _pick(T, 256, LANES)
    nch = tm // LANES

    def body(u_ref, v_ref, g_ref, w_ref, bt_ref, lg_ref, lb_ref, y_ref, z_scr):
        u, *_ = _sgu_common(u_ref[...], v_ref[...], lg_ref[...], lb_ref[...], w_ref, bt_ref, z_scr, G, nch)
        y_ref[...] = (u * z_scr[...] * _silu(g_ref[...])).astype(BF16)

    cur = lambda c: pl.BlockSpec((tm, W), lambda i: (i, c))
    vec = pl.BlockSpec((1, W), lambda i: (0, 0))
    return pl.pallas_call(body, name="sgu_fwd", grid=(T // tm,),
                          in_specs=[cur(7), cur(8), cur(9), pl.BlockSpec((G, LANES, LANES), lambda i: (0, 0, 0)),
                                    pl.BlockSpec((LANES, G), lambda i: (0, 0)), vec, vec],
                          out_specs=pl.BlockSpec((tm, W), lambda i: (i, 0)), out_shape=_sds((T, W), BF16),
                          scratch_shapes=[pltpu.VMEM((tm, W), F32)],
                          compiler_params=_cp("parallel"))(proj, proj, proj, sgu_w, sgu_bt, ln_g, ln_b)


def _sgu_bwd(proj, dy, sgu_w, sgu_bt, ln_g, ln_b, W):
    T = proj.shape[0]
    G = W // LANES
    tm = _pick(T, 256, LANES)
    nch = tm // LANES

    def body(u_ref, v_ref, g_ref, dy_ref, w_ref, bt_ref, lg_ref, lb_ref,
             du_ref, dv_ref, dg_ref, dw_ref, dbt_ref, st_ref, z_scr, dvn_scr):
        first = pl.program_id(0) == 0
        cu, cv, gv, dyv = u_ref[...], v_ref[...], g_ref[...], dy_ref[...]
        lg = lg_ref[...]
        u, xh, r, vn, wts, tril = _sgu_common(cu, cv, lg, lb_ref[...], w_ref, bt_ref, z_scr, G, nch)
        z = z_scr[...]
        sg = _silu(gv)
        dg_ref[...] = (dyv * u * z * _dsilu(gv)).astype(BF16)
        du_ref[...] = (dyv * z * sg * _dgelu(cu)).astype(BF16)
        dz = dyv * u * sg
        dzb = dz.astype(BF16)

        @pl.when(first)
        def _():
            dw_ref[...] = jnp.zeros_like(dw_ref)
            dbt_ref[...] = jnp.zeros_like(dbt_ref)

        for g in range(G):
            cs = slice(g * LANES, (g + 1) * LANES)
            dwg = jnp.zeros((LANES, LANES), F32)
            dbg = jnp.zeros((LANES, 1), F32)
            for ch in range(nch):
                rs = slice(ch * LANES, (ch + 1) * LANES)
                dwg = dwg + lax.dot_general(dzb[rs, cs], vn[rs, cs], NT, preferred_element_type=F32)
                dbg = dbg + jnp.sum(dz[rs, cs], axis=1, keepdims=True)
                dvn_scr[rs, cs] = lax.dot_general(wts[g], dzb[rs, cs], TN, preferred_element_type=F32)
            dw_ref[g] += jnp.where(tril, dwg, 0.0)
            dbt_ref[:, g:g + 1] += dbg
        dvn = dvn_scr[...]
        _acc_rows(st_ref, 0, jnp.sum(dvn * xh, axis=0, keepdims=True), first)
        st_ref[1:2, :] += jnp.sum(dvn, axis=0, keepdims=True)
        dv_ref[...] = (_ln_bwd(dvn * lg, xh, r) * _dgelu(cv)).astype(BF16)

    cur = lambda c: pl.BlockSpec((tm, W), lambda i: (i, c))
    row = pl.BlockSpec((tm, W), lambda i: (i, 0))
    vec = pl.BlockSpec((1, W), lambda i: (0, 0))
    wspec = pl.BlockSpec((G, LANES, LANES), lambda i: (0, 0, 0))
    bspec = pl.BlockSpec((LANES, G), lambda i: (0, 0))
    return pl.pallas_call(body, name="sgu_bwd", grid=(T // tm,),
                          in_specs=[cur(7), cur(8), cur(9), pl.BlockSpec((None, tm, W), lambda i: (2, i, 0)),
                                    wspec, bspec, vec, vec],
                          out_specs=[row, row, row, wspec, bspec, pl.BlockSpec((SUBLANES, W), lambda i: (0, 0))],
                          out_shape=[_sds((T, W), BF16)] * 3 + [_sds((G, LANES, LANES), F32), _sds((LANES, G), F32),
                                                                 _sds((SUBLANES, W), F32)],
                          scratch_shapes=[pltpu.VMEM((tm, W), F32), pltpu.VMEM((tm, W), F32)],
                          compiler_params=_cp("arbitrary"))(proj, proj, proj, dy, sgu_w, sgu_bt, ln_g, ln_b)


def _rows(start, dil):
    return pl.ds(start, LANES, stride=dil) if dil > 1 else pl.ds(start, LANES)


def _dil_masks():
    a = lax.broadcasted_iota(jnp.int32, (LANES, LANES), 0)
    c = lax.broadcasted_iota(jnp.int32, (LANES, LANES), 1)
    return c <= a, c >= a


def _dil_geometry(T, dil):
    sbr = LANES * dil
    nb = max(1, min(DIL_STEP_ROWS, T) // sbr)
    return sbr, nb, T // (sbr * nb)


def _for_units(nb, dil, unit):
    for blk in range(nb):
        if dil <= DIL_UNROLL:
            for r in range(dil):
                unit(blk, r)
        else:
            def chunk(it, carry):
                for u in range(DIL_UNROLL):
                    unit(blk, it * DIL_UNROLL + u)
                return carry
            lax.fori_loop(0, dil // DIL_UNROLL, chunk, 0)


def _dil_fwd_group(proj, W, gi, dil):
    T = proj.shape[0]
    H = W // LANES
    sbr, nb, nsteps = _dil_geometry(T, dil)
    scale = LANES ** -0.5
    cq, ck, cv = (10 + gi) * H, (13 + gi) * H, 16 * H

    def body(q_ref, kc_ref, kp_ref, vc_ref, vp_ref, o_ref, l_ref):
        b = pl.program_id(1)
        m_cur, m_prev = _dil_masks()
        m_first = m_prev & (b > 0)

        def unit(blk, r):
            sl = _rows(blk * sbr + r, dil)
            if blk == 0:
                kp, vp, mp = kp_ref[_rows(r, dil), :], vp_ref[_rows(r, dil), :], m_first
            else:
                sp_ = _rows((blk - 1) * sbr + r, dil)
                kp, vp, mp = kc_ref[sp_, :], vc_ref[sp_, :], m_prev
            q = (q_ref[sl, :] * scale).astype(BF16)
            sc = lax.dot_general(q, kc_ref[sl, :].astype(BF16), NT, preferred_element_type=F32)
            sp = lax.dot_general(q, kp.astype(BF16), NT, preferred_element_type=F32)
            sc = jnp.where(m_cur, sc, NEG)
            sp = jnp.where(mp, sp, NEG)
            m = jnp.maximum(jnp.max(sc, axis=1, keepdims=True), jnp.max(sp, axis=1, keepdims=True))
            pc = jnp.exp(sc - m)
            pp = jnp.exp(sp - m)
            den = jnp.sum(pc, axis=1, keepdims=True) + jnp.sum(pp, axis=1, keepdims=True)
            pv = (jnp.dot(pc.astype(BF16), vc_ref[sl, :].astype(BF16), preferred_element_type=F32)
                  + jnp.dot(pp.astype(BF16), vp.astype(BF16), preferred_element_type=F32))
            o_ref[sl, :] = pv / den
            l_ref[sl, :] = jnp.broadcast_to(m + jnp.log(den), (LANES, LANES))

        _for_units(nb, dil, unit)

    cur = lambda c0: pl.BlockSpec((sbr * nb, LANES), lambda h, b: (b, c0 + h))
    prv = lambda c0: pl.BlockSpec((sbr, LANES), lambda h, b: (jnp.maximum(b * nb - 1, 0), c0 + h))
    out = pl.BlockSpec((sbr * nb, LANES), lambda h, b: (b, h))
    return pl.pallas_call(body, name=f"dil_fwd_g{gi}", grid=(H, nsteps),
                          in_specs=[cur(cq), cur(ck), prv(ck), cur(cv), prv(cv)], out_specs=[out, out],
                          out_shape=[_sds((T, W), F32), _sds((T, W), F32)],
                          compiler_params=_cp("parallel", "parallel"))(proj, proj, proj, proj, proj)


def _dil_combine(proj, os_, ls_, W):
    T = proj.shape[0]
    tm = _pick(T, 256, SUBLANES)

    def body(g_ref, o0, o1, o2, l0, l1, l2, od_ref, lse_ref, y_ref):
        a0, a1, a2 = l0[...], l1[...], l2[...]
        m = jnp.maximum(jnp.maximum(a0, a1), a2)
        e0, e1, e2 = jnp.exp(a0 - m), jnp.exp(a1 - m), jnp.exp(a2 - m)
        s = e0 + e1 + e2
        od = (e0 / s) * o0[...] + (e1 / s) * o1[...] + (e2 / s) * o2[...]
        od_ref[...] = od
        lse_ref[...] = m + jnp.log(s)
        y_ref[...] = (od * _silu(g_ref[...])).astype(BF16)

    row = pl.BlockSpec((tm, W), lambda i: (i, 0))
    return pl.pallas_call(body, name="dil_combine", grid=(T // tm,),
                          in_specs=[pl.BlockSpec((tm, W), lambda i: (i, 17))] + [row] * 6, out_specs=[row, row, row],
                          out_shape=[_sds((T, W), F32), _sds((T, W), F32), _sds((T, W), BF16)],
                          compiler_params=_cp("parallel"))(proj, *os_, *ls_)


def _dil_bwd_pre(proj, od, dy, W):
    T = proj.shape[0]
    H = W // LANES
    tm = _pick(T, 512, SUBLANES)

    def body(g_ref, od_ref, dy_ref, do_ref, dl_ref, dg_ref):
        gv, odv, dyv = g_ref[...], od_ref[...], dy_ref[...]
        do = dyv * _silu(gv)
        do_ref[...] = do
        dl_ref[...] = jnp.broadcast_to(jnp.sum(do * odv, axis=1, keepdims=True), (tm, LANES))
        dg_ref[...] = (dyv * odv * _dsilu(gv)).astype(BF16)

    blk = pl.BlockSpec((tm, LANES), lambda i, h: (i, h))
    return pl.pallas_call(body, name="dil_bwd_pre", grid=(T // tm, H),
                          in_specs=[pl.BlockSpec((tm, LANES), lambda i, h: (i, 17 * H + h)), blk,
                                    pl.BlockSpec((None, tm, LANES), lambda i, h: (3, i, h))],
                          out_specs=[blk, blk, blk],
                          out_shape=[_sds((T, W), F32), _sds((T, W), F32), _sds((T, W), BF16)],
                          compiler_params=_cp("parallel", "parallel"))(proj, od, dy)


def _dil_bwd_group(proj, do, lse, delta, W, gi, dil):
    T = proj.shape[0]
    H = W // LANES
    sbr, nb, nsteps = _dil_geometry(T, dil)
    scale = LANES ** -0.5
    cq, ck, cv = (10 + gi) * H, (13 + gi) * H, 16 * H

    def body(qc_ref, qn_ref, kc_ref, kp_ref, vc_ref, vp_ref, doc_ref, don_ref, lc_ref, ln_ref, dc_ref, dn_ref,
             dq_ref, dk_ref, dv_ref):
        b = pl.program_id(1)
        m_cur, m_prev = _dil_masks()
        m_first = m_prev & (b > 0)
        m_last = m_prev & (b < nsteps - 1)

        def probs(q, k, mask, l):
            s = lax.dot_general(q, k, NT, preferred_element_type=F32)
            return jnp.exp(jnp.where(mask, s - l, NEG))

        def unit(blk, r):
            sl = _rows(blk * sbr + r, dil)
            if blk == 0:
                edge = _rows(r, dil)
                k_p, v_p, m_cp = kp_ref[edge, :], vp_ref[edge, :], m_first
            else:
                sp_ = _rows((blk - 1) * sbr + r, dil)
                k_p, v_p, m_cp = kc_ref[sp_, :], vc_ref[sp_, :], m_prev
            if blk == nb - 1:
                edge = _rows(r, dil)
                q_n, do_n, l_n, d_n, m_nc = qn_ref[edge, :], don_ref[edge, :], ln_ref[edge, :], dn_ref[edge, :], m_last
            else:
                sn_ = _rows((blk + 1) * sbr + r, dil)
                q_n, do_n, l_n, d_n, m_nc = qc_ref[sn_, :], doc_ref[sn_, :], lc_ref[sn_, :], dc_ref[sn_, :], m_prev
            q_c = (qc_ref[sl, :] * scale).astype(BF16)
            q_n = (q_n * scale).astype(BF16)
            k_c, k_p = kc_ref[sl, :].astype(BF16), k_p.astype(BF16)
            v_c, v_p = vc_ref[sl, :].astype(BF16), v_p.astype(BF16)
            do_c, do_n = doc_ref[sl, :].astype(BF16), do_n.astype(BF16)
            l_c, d_c = lc_ref[sl, :], dc_ref[sl, :]
            p_cc = probs(q_c, k_c, m_cur, l_c)
            p_cp = probs(q_c, k_p, m_cp, l_c)
            p_nc = probs(q_n, k_c, m_nc, l_n)
            ds_cc = (p_cc * (lax.dot_general(do_c, v_c, NT, preferred_element_type=F32) - d_c)).astype(BF16)
            ds_cp = (p_cp * (lax.dot_general(do_c, v_p, NT, preferred_element_type=F32) - d_c)).astype(BF16)
            ds_nc = (p_nc * (lax.dot_general(do_n, v_c, NT, preferred_element_type=F32) - d_n)).astype(BF16)
            dq = (jnp.dot(ds_cc, k_c, preferred_element_type=F32) + jnp.dot(ds_cp, k_p, preferred_element_type=F32))
            dk = (lax.dot_general(ds_cc, q_c, TN, preferred_element_type=F32)
                  + lax.dot_general(ds_nc, q_n, TN, preferred_element_type=F32))
            dv = (lax.dot_general(p_cc.astype(BF16), do_c, TN, preferred_element_type=F32)
                  + lax.dot_general(p_nc.astype(BF16), do_n, TN, preferred_element_type=F32))
            dq_ref[sl, :] = dq * scale
            dk_ref[sl, :] = dk
            dv_ref[sl, :] = dv

        _for_units(nb, dil, unit)

    cur = lambda c0: pl.BlockSpec((sbr * nb, LANES), lambda h, b: (b, c0 + h))
    prv = lambda c0: pl.BlockSpec((sbr, LANES), lambda h, b: (jnp.maximum(b * nb - 1, 0), c0 + h))
    nxt = lambda c0: pl.BlockSpec((sbr, LANES), lambda h, b: (jnp.minimum((b + 1) * nb, T // sbr - 1), c0 + h))
    out = pl.BlockSpec((sbr * nb, LANES), lambda h, b: (b, h))
    return pl.pallas_call(body, name=f"dil_bwd_g{gi}", grid=(H, nsteps),
                          in_specs=[cur(cq), nxt(cq), cur(ck), prv(ck), cur(cv), prv(cv),
                                    cur(0), nxt(0), cur(0), nxt(0), cur(0), nxt(0)],
                          out_specs=[out, out, out], out_shape=[_sds((T, W), F32)] * 3,
                          compiler_params=_cp("parallel", "parallel"))(
        proj, proj, proj, proj, proj, proj, do, do, lse, lse, delta, delta)


def _sum3_bf16(a, b, c):
    T, W = a.shape
    tm = _pick(T, 512, SUBLANES)

    def body(a_ref, b_ref, c_ref, o_ref):
        o_ref[...] = (a_ref[...] + b_ref[...] + c_ref[...]).astype(BF16)

    row = pl.BlockSpec((tm, W), lambda i: (i, 0))
    return pl.pallas_call(body, name="dil_dv_sum", grid=(T // tm,), in_specs=[row, row, row], out_specs=row,
                          out_shape=_sds((T, W), BF16), compiler_params=_cp("parallel"))(a, b, c)


def _to_bf16(name, parts):
    T, W = parts[0].shape
    n = len(parts)
    tm = _pick(T, 512, SUBLANES)

    def body(*refs):
        o_ref = refs[n]
        for p in range(n):
            o_ref[:, p * W:(p + 1) * W] = refs[p][...].astype(BF16)

    row = pl.BlockSpec((tm, W), lambda i: (i, 0))
    return pl.pallas_call(body, name=name, grid=(T // tm,), in_specs=[row] * n,
                          out_specs=pl.BlockSpec((tm, n * W), lambda i: (i, 0)),
                          out_shape=_sds((T, n * W), BF16), compiler_params=_cp("parallel"))(*parts)


def _branch_merge(ys, w_br, gates):
    NB, T, W = ys.shape
    D = w_br.shape[2]
    tm, tn = _pick(T, 1024, SUBLANES), _pick(D, 1024)
    nj = D // tn

    def body(y_ref, w_ref, g_ref, yp_ref, m_ref, acc):
        n = pl.program_id(2)
        yp = jnp.dot(y_ref[...], w_ref[...], preferred_element_type=F32)
        yp_ref[...] = yp

        @pl.when(n == 0)
        def _():
            acc[...] = jnp.zeros_like(acc)

        acc[...] += g_ref[...] * yp

        @pl.when(n == NB - 1)
        def _():
            m_ref[...] = acc[...].astype(BF16)

    return pl.pallas_call(
        body, name="branch_merge", grid=(T // tm, nj, NB),
        in_specs=[pl.BlockSpec((None, tm, W), lambda i, j, n: (n, i, 0)),
                  pl.BlockSpec((None, W, tn), lambda i, j, n: (n, 0, j)),
                  pl.BlockSpec((tm, tn), lambda i, j, n: (i, n * nj + j))],
        out_specs=[pl.BlockSpec((None, tm, tn), lambda i, j, n: (n, i, j)),
                   pl.BlockSpec((tm, tn), lambda i, j, n: (i, j))],
        out_shape=[_sds((NB, T, D), F32), _sds((T, D), BF16)],
        scratch_shapes=[pltpu.VMEM((tm, tn), F32)],
        compiler_params=_cp("parallel", "parallel", "arbitrary"))(ys, w_br, gates)


def _merge_bwd(dmerged, gates, yproj):
    NB, T, D = yproj.shape
    tm = _pick(T, 256, SUBLANES)

    def body(dm_ref, g_ref, yp_ref, dyp_ref, dz_ref, db_ref):
        dm, g = dm_ref[...], g_ref[...]
        dyp_ref[...] = (dm * g).astype(BF16)
        dz = dm * yp_ref[...] * g * (1.0 - g)
        dz_ref[...] = dz.astype(BF16)
        _acc_rows(db_ref, 0, jnp.sum(dz, axis=0, keepdims=True), pl.program_id(1) == 0)

    return pl.pallas_call(
        body, name="merge_bwd", grid=(NB, T // tm),
        in_specs=[pl.BlockSpec((tm, D), lambda n, i: (i, 0)), pl.BlockSpec((tm, D), lambda n, i: (i, n)),
                  pl.BlockSpec((None, tm, D), lambda n, i: (n, i, 0))],
        out_specs=[pl.BlockSpec((None, tm, D), lambda n, i: (n, i, 0)), pl.BlockSpec((tm, D), lambda n, i: (i, n)),
                   pl.BlockSpec((SUBLANES, D), lambda n, i: (0, n))],
        out_shape=[_sds((NB, T, D), BF16), _sds((T, NB * D), BF16), _sds((SUBLANES, NB * D), F32)],
        compiler_params=_cp("parallel", "arbitrary"))(dmerged, gates, yproj)


def _branch_bwd_dy(dyp, w_br):
    NB, T, D = dyp.shape
    W = w_br.shape[1]
    tm = _pick(T, 1024, SUBLANES)

    def body(a_ref, w_ref, o_ref):
        o_ref[...] = lax.dot_general(a_ref[...], w_ref[...], NT, preferred_element_type=F32)

    return pl.pallas_call(body, name="branch_bwd_dy", grid=(NB, T // tm),
                          in_specs=[pl.BlockSpec((None, tm, D), lambda n, i: (n, i, 0)),
                                    pl.BlockSpec((None, W, D), lambda n, i: (n, 0, 0))],
                          out_specs=pl.BlockSpec((None, tm, W), lambda n, i: (n, i, 0)),
                          out_shape=_sds((NB, T, W), F32), compiler_params=_cp("parallel", "parallel"))(dyp, w_br)


def _branch_bwd_dw(ys, dyp):
    NB, T, W = ys.shape
    D = dyp.shape[2]
    tm, tn, tk = _pick(W, 1024), _pick(D, 1024), _pick(T, 1024, SUBLANES)
    return _mm("branch_bwd_dw", ys, dyp, grid=(NB, W // tm, D // tn, T // tk), kaxis=3, dims=TN,
               a_spec=pl.BlockSpec((None, tk, tm), lambda n, i, j, k: (n, k, i)),
               b_spec=pl.BlockSpec((None, tk, tn), lambda n, i, j, k: (n, k, j)),
               acc_shape=(tm, tn), out_shape=_sds((NB, W, D), BF16),
               out_specs=pl.BlockSpec((None, tm, tn), lambda n, i, j, k: (n, i, j)),
               epilogue=_store(BF16), sem=("parallel", "parallel", "parallel", "arbitrary"))


def _layer_fwd(x, wl, sp):
    W = sp["conv_b"].shape[-1]
    D = x.shape[1]
    h = _rmsnorm_fwd(x, sp["norm_g"])
    proj = _mm_nn("proj", h, wl["w_in"], F32, tn=768)
    if "late" in wl:
        wl.update(wl.pop("late")(proj))

    def gate_ep(acc, ex, outs):
        outs[0][...] = _sigmoid(acc + ex[0][...])

    tn_g = _pick(4 * D, 1024)
    gates = _mm_nn("gates", h, wl["w_gate"], F32, epilogue=gate_ep, extras=(sp["b_gate"],),
                   extra_specs=(pl.BlockSpec((1, tn_g), lambda i, j, k: (0, j)),), tn=tn_g)
    oa, ya, sb_carries = _sb_fwd(proj, W)
    cpre, yb = _conv_fwd(proj, wl["conv_w"], wl["taps"], sp["conv_b"], sp["conv_ln_g"], sp["conv_ln_b"], W)
    yc = _sgu_fwd(proj, sp["sgu_w"], sp["sgu_bt"], sp["sgu_ln_g"], sp["sgu_ln_b"], W)
    os_, ls_ = zip(*[_dil_fwd_group(proj, W, gi, dil) for gi, (_, dil) in enumerate(DIL_PATTERNS)])
    od, lse, yd = _dil_combine(proj, os_, ls_, W)
    ys = jnp.stack([ya, yb, yc, yd])
    yproj, merged = _branch_merge(ys, wl["w_br"], gates)

    def res_ep(acc, ex, outs):
        outs[0][...] = ex[0][...] + acc

    tm_o, tn_o = _pick(x.shape[0], 1024, SUBLANES), _pick(D, 1024)
    xn = _mm_nn("out_proj", merged, wl["w_out"], F32, epilogue=res_ep, extras=(x,),
                extra_specs=(pl.BlockSpec((tm_o, tn_o), lambda i, j, k: (i, j)),), tm=tm_o, tn=tn_o)
    saved = dict(x=x, h=h, proj=proj, gates=gates, oa=oa, sb_carries=sb_carries, cpre=cpre, od=od, lse=lse, ys=ys, yproj=yproj, merged=merged)
    return xn, saved


def _layer_bwd(dout, sv, wl, sp):
    W = sp["conv_b"].shape[-1]
    proj = sv["proj"]
    dmerged = _mm_nt("out_proj_bwd_dx", dout, wl["w_out"], F32, tm=512)
    g_w_out = _mm_tn("out_proj_bwd_dw", sv["merged"], dout)
    dyp, dzg, db_gate = _merge_bwd(dmerged, sv["gates"], sv["yproj"])
    dy = _branch_bwd_dy(dyp, wl["w_br"])
    g_w_br = _branch_bwd_dw(sv["ys"], dyp)
    g_w_gate = _mm_tn("gate_bwd_dw", sv["h"], dzg)

    a_dq, a_dk, a_dv, a_dg = _sb_bwd(proj, sv["oa"], sv["sb_carries"], dy, W)
    a_qkv = _to_bf16("sb_bwd_cast", [a_dq, a_dk, a_dv])

    dc, b_dg, conv_stats = _conv_bwd_ln(proj, sv["cpre"], dy, sp["conv_ln_g"], sp["conv_ln_b"], W)
    b_da, b_db, g_conv_w = _conv_bwd_taps(proj, dc, wl["conv_w"], wl["taps"], W)

    c_du, c_dv, c_dg, g_sgu_w, g_sgu_bt, sgu_stats = _sgu_bwd(proj, dy, sp["sgu_w"], sp["sgu_bt"],
                                                              sp["sgu_ln_g"], sp["sgu_ln_b"], W)

    do, delta, d_dg = _dil_bwd_pre(proj, sv["od"], dy, W)
    dqs, dks, dvs = zip(*[_dil_bwd_group(proj, do, sv["lse"], delta, W, gi, dil)
                          for gi, (_, dil) in enumerate(DIL_PATTERNS)])
    d_qk = _to_bf16("dil_bwd_cast", [*dqs, *dks])
    d_dv = _sum3_bf16(*dvs)

    dproj = jnp.concatenate([a_qkv, a_dg, b_da, b_db, b_dg, c_du, c_dv, c_dg, d_qk, d_dv, d_dg], axis=1)
    g_w_in = _mm_tn("proj_bwd_dw", sv["h"], dproj, tn=768)
    dh = _mm_nt("gate_bwd_dh", dzg, wl["w_gate"], F32)
    dh = _mm_nt("proj_bwd_dh", dproj, wl["w_in"], F32, addend=dh)
    dx, dnorm = _rmsnorm_bwd(sv["x"], dh, dout, sp["norm_g"])

    K = wl["taps"]
    small = dict(norm_g=dnorm[0], conv_w=g_conv_w[:K], conv_b=conv_stats[2], conv_ln_g=conv_stats[0],
                 conv_ln_b=conv_stats[1], sgu_ln_g=sgu_stats[0], sgu_ln_b=sgu_stats[1], sgu_w=g_sgu_w,
                 sgu_b=g_sgu_bt.T, b_gate=db_gate[0])
    big = dict(w_in=g_w_in, w_gate=g_w_gate, w_br=g_w_br, w_out=g_w_out)
    return dx, big, small


HBM = pl.BlockSpec(memory_space=pl.ANY)


def _mesh_pos():
    return lax.axis_index("x"), lax.axis_index("y"), lax.axis_index("c")


def _other_chips(x, y):
    return [(1 - x, y), (x, 1 - y), (1 - x, 1 - y)]


def _shard_of(ref, axis, size, index):
    idx = [slice(None)] * len(ref.shape)
    idx[axis] = pl.ds(index * size, size)
    return ref.at[tuple(idx)]


def _all_gather(name, shards, axes):
    n = len(shards)
    out_shape = []
    for s, ax in zip(shards, axes):
        shp = list(s.shape)
        shp[ax] *= N_DEV
        out_shape.append(_sds(shp, s.dtype))

    def body(*refs):
        ins, outs = refs[:n], refs[n:2 * n]
        send, recv, lsem = refs[2 * n:]
        x, y, c = _mesh_pos()
        me, sib = (x, y, c), (x, y, 1 - c)
        chips = _other_chips(x, y)
        dev = lambda px, py, pc: 4 * px + 2 * py + pc

        def blk(a, d):
            return _shard_of(outs[a], axes[a], ins[a].shape[axes[a]], d)

        def cp(a, k, d, to, src=None):
            return pltpu.make_async_remote_copy(src_ref=blk(a, d) if src is None else src, dst_ref=blk(a, d),
                                                send_sem=send.at[a * 7 + k], recv_sem=recv.at[a * 7 + k],
                                                device_id=to, device_id_type=MESH)

        own = [pltpu.make_async_copy(ins[a], blk(a, dev(*me)), lsem.at[a]) for a in range(n)]
        for o in own:
            o.start()
        first = []
        for a in range(n):
            first.append(cp(a, 0, dev(*me), sib, src=ins[a]))
            first += [cp(a, 1 + j, dev(*me), (*ch, c), src=ins[a]) for j, ch in enumerate(chips)]
        for f in first:
            f.start()
        passed = []
        for j, ch in enumerate(chips):
            for a in range(n):
                cp(a, 1 + j, dev(*ch, c), me).wait_recv()
                p = cp(a, 4 + j, dev(*ch, c), sib)
                p.start()
                passed.append(p)
        for a in range(n):
            cp(a, 0, dev(*sib), me).wait_recv()
            for j, ch in enumerate(chips):
                cp(a, 4 + j, dev(*ch, 1 - c), me).wait_recv()
        for f in first + passed:
            f.wait_send()
        for o in own:
            o.wait()

    return pl.pallas_call(body, name=name, in_specs=[HBM] * n, out_specs=[HBM] * n, out_shape=out_shape,
                          scratch_shapes=[pltpu.SemaphoreType.DMA((7 * n,)), pltpu.SemaphoreType.DMA((7 * n,)),
                                          pltpu.SemaphoreType.DMA((n,))])(*shards)


def _rs_pair(name, grads, axes):
    n = len(grads)
    sizes = [g.shape[ax] // N_DEV for g, ax in zip(grads, axes)]
    out_shape = []
    for g, ax, sz in zip(grads, axes, sizes):
        shp = list(g.shape)
        shp[ax] = sz
        out_shape.append(_sds([N_CHIP] + shp, g.dtype))

    def body(*refs):
        ins, outs = refs[:n], refs[n:2 * n]
        send, recv = refs[2 * n:]
        x, y, c = _mesh_pos()
        cps = []
        for a in range(n):
            for k in range(N_CHIP):
                cps.append(pltpu.make_async_remote_copy(
                    src_ref=_shard_of(ins[a], axes[a], sizes[a], 2 * k + (1 - c)), dst_ref=outs[a].at[k],
                    send_sem=send.at[a * N_CHIP + k], recv_sem=recv.at[a * N_CHIP + k],
                    device_id=(x, y, 1 - c), device_id_type=MESH))
        for cp in cps:
            cp.start()
        for cp in cps:
            cp.wait()

    return pl.pallas_call(body, name=name, in_specs=[HBM] * n, out_specs=[HBM] * n, out_shape=out_shape,
                          scratch_shapes=[pltpu.SemaphoreType.DMA((N_CHIP * n,)),
                                          pltpu.SemaphoreType.DMA((N_CHIP * n,))])(*grads)


def _pair_sum(name, grad, recv, axis, core):
    R, C = grad.shape
    _, r, cw = recv.shape
    tr = _pick(r, 512, SUBLANES)
    nr = r // tr

    def body(c_ref, g_ref, r_ref, o_ref):
        o_ref[...] = (g_ref[...] + r_ref[...]).astype(BF16)

    if axis == 1:
        gspec = pl.BlockSpec((tr, cw), lambda k, i, c_ref: (i, 2 * k + c_ref[0]))
    else:
        gspec = pl.BlockSpec((tr, cw), lambda k, i, c_ref: ((2 * k + c_ref[0]) * nr + i, 0))
    part = pl.BlockSpec((None, tr, cw), lambda k, i, c_ref: (k, i, 0))
    return pl.pallas_call(
        body, name=name, out_shape=_sds((N_CHIP, r, cw), BF16),
        grid_spec=pltpu.PrefetchScalarGridSpec(num_scalar_prefetch=1, grid=(N_CHIP, nr), in_specs=[gspec, part],
                                               out_specs=part),
        compiler_params=_cp("parallel", "parallel"))(core, grad, recv)


def _rs_chips(name, parts):
    n = len(parts)

    def body(*refs):
        ins, outs = refs[:n], refs[n:2 * n]
        send, recv, lsem = refs[2 * n:]
        x, y, c = _mesh_pos()
        mine = 2 * x + y
        own = [pltpu.make_async_copy(ins[a].at[mine], outs[a].at[mine], lsem.at[a]) for a in range(n)]
        for o in own:
            o.start()
        cps = []
        for a in range(n):
            for j, (px, py) in enumerate(_other_chips(x, y)):
                cps.append(pltpu.make_async_remote_copy(
                    src_ref=ins[a].at[2 * px + py], dst_ref=outs[a].at[mine],
                    send_sem=send.at[a * 3 + j], recv_sem=recv.at[a * 3 + j],
                    device_id=(px, py, c), device_id_type=MESH))
        for cp in cps:
            cp.start()
        for a in range(n):
            for j, (px, py) in enumerate(_other_chips(x, y)):
                pltpu.make_async_remote_copy(
                    src_ref=ins[a].at[mine], dst_ref=outs[a].at[2 * px + py],
                    send_sem=send.at[a * 3 + j], recv_sem=recv.at[a * 3 + j],
                    device_id=(x, y, c), device_id_type=MESH).wait_recv()
        for cp in cps:
            cp.wait_send()
        for o in own:
            o.wait()

    return pl.pallas_call(body, name=name, in_specs=[HBM] * n, out_specs=[HBM] * n,
                          out_shape=[_sds(p.shape, p.dtype) for p in parts],
                          scratch_shapes=[pltpu.SemaphoreType.DMA((3 * n,)), pltpu.SemaphoreType.DMA((3 * n,)),
                                          pltpu.SemaphoreType.DMA((n,))])(*parts)


HBM_ONLY = pl.BlockSpec(memory_space=pltpu.HBM)
SEM_SPEC = pl.BlockSpec(memory_space=pltpu.SEMAPHORE)
N_PEER = N_DEV - 1


def _peer(x, y, c, m):
    flip = lambda v, bit: 1 - v if bit else v
    return flip(x, (m >> 2) & 1), flip(y, (m >> 1) & 1), flip(c, m & 1)


def _exchange_copies(kind, srcs, lands, send, recv, axes):
    x, y, c = _mesh_pos()
    me = 4 * x + 2 * y + c
    n = len(srcs)
    remote, local = [], []
    for a in range(n):
        for m in range(1, N_DEV):
            px, py, pc = _peer(x, y, c, m)
            if kind == "gather":
                src = srcs[a]
                dst = _shard_of(lands[a], axes[a], srcs[a].shape[axes[a]], me)
            else:
                src = _shard_of(srcs[a], axes[a], srcs[a].shape[axes[a]] // N_DEV, 4 * px + 2 * py + pc)
                dst = lands[a].at[me]
            remote.append(pltpu.make_async_remote_copy(
                src_ref=src, dst_ref=dst, send_sem=send.at[a * N_PEER + m - 1], recv_sem=recv.at[a * N_PEER + m - 1],
                device_id=(px, py, pc), device_id_type=MESH))
        if kind == "gather":
            local.append(pltpu.make_async_copy(srcs[a], _shard_of(lands[a], axes[a], srcs[a].shape[axes[a]], me),
                                               send.at[n * N_PEER + a]))
    return remote, local


def _exchange_start(name, kind, srcs, lands, axes, carry):
    n = len(srcs)
    hbm = lambda a: pltpu.with_memory_space_constraint(a, pltpu.HBM)

    def body(*refs):
        send, recv = refs[2 * n + 1], refs[2 * n + 2]
        remote, local = _exchange_copies(kind, refs[:n], refs[n:2 * n], send, recv, axes)
        for cp in remote + local:
            cp.start()

    thru = [*srcs, *lands, carry]
    res = pl.pallas_call(
        body, name=name,
        out_shape=(pltpu.SemaphoreType.DMA((n * N_DEV,)), pltpu.SemaphoreType.DMA((n * N_PEER,)),
                   *[pltpu.HBM(a.shape, a.dtype) for a in thru]),
        in_specs=[HBM_ONLY] * len(thru), out_specs=(SEM_SPEC, SEM_SPEC, *[HBM_ONLY] * len(thru)),
        input_output_aliases={i: 2 + i for i in range(len(thru))},
        compiler_params=pltpu.CompilerParams(has_side_effects=pltpu.SideEffectType.DATAFLOW_SIDE_EFFECTING),
    )(*[hbm(a) for a in thru])
    return res[0], res[1], list(res[2:2 + n]), list(res[2 + n:2 + 2 * n]), res[2 + 2 * n]


def _exchange_wait(name, kind, send, recv, srcs, lands, axes, after):
    n = len(srcs)

    def body(*refs):
        send_ref, recv_ref = refs[2 * n], refs[2 * n + 1]
        remote, local = _exchange_copies(kind, refs[:n], refs[n:2 * n], send_ref, recv_ref, axes)
        for cp in remote:
            cp.wait_send()
            cp.wait_recv()
        for cp in local:
            cp.wait()

    thru = [*srcs, *lands]
    res = pl.pallas_call(
        body, name=name, out_shape=tuple(pltpu.HBM(a.shape, a.dtype) for a in thru),
        in_specs=[*[HBM_ONLY] * len(thru), SEM_SPEC, SEM_SPEC, HBM], out_specs=tuple([HBM_ONLY] * len(thru)),
        input_output_aliases={i: i for i in range(len(thru))},
        compiler_params=pltpu.CompilerParams(has_side_effects=pltpu.SideEffectType.DATAFLOW_SIDE_EFFECTING),
    )(*thru, send, recv, after)
    return list(res[n:])


def _adamw_scatter(name, land, grad, axis, me, w, m, v):
    P, R, C = land.shape
    tr = _pick(R, max(SUBLANES, (1 << 18) // C // 16 * 16), 16)
    nr = R // tr

    def body(me_ref, l_ref, o_ref, w_ref, m_ref, v_ref, g_ref, d_ref, mo_ref, vo_ref):
        own = o_ref[...].astype(F32)
        g = jnp.where(me_ref[0] == 0, own, l_ref[0].astype(F32))
        for k in range(1, P):
            g = g + jnp.where(me_ref[0] == k, own, l_ref[k].astype(F32))
        mn = ADAM_B1 * m_ref[...] + (1.0 - ADAM_B1) * g
        vn = ADAM_B2 * v_ref[...] + (1.0 - ADAM_B2) * (g * g)
        m_hat = mn / (1.0 - ADAM_B1 ** ADAM_STEP)
        v_hat = vn / (1.0 - ADAM_B2 ** ADAM_STEP)
        g_ref[...] = g
        d_ref[...] = -ADAM_LR * (m_hat / (jnp.sqrt(v_hat) + ADAM_EPS) + ADAM_WD * w_ref[...])
        mo_ref[...] = mn
        vo_ref[...] = vn

    if axis == 1:
        own_spec = pl.BlockSpec((tr, C), lambda i, me_ref: (i, me_ref[0]))
    else:
        own_spec = pl.BlockSpec((tr, C), lambda i, me_ref: (me_ref[0] * nr + i, 0))
    row = pl.BlockSpec((tr, C), lambda i, me_ref: (i, 0))
    return pl.pallas_call(
        body, name=name, out_shape=[_sds((R, C), F32)] * 4,
        grid_spec=pltpu.PrefetchScalarGridSpec(
            num_scalar_prefetch=1, grid=(nr,),
            in_specs=[pl.BlockSpec((P, tr, C), lambda i, me_ref: (0, i, 0)), own_spec, row, row, row],
            out_specs=[row] * 4),
        compiler_params=_cp("parallel"))(me, land, grad, w, m, v)


def _adamw_sum(name, parts, w, m, v):
    P, R, C = parts.shape
    tr = _pick(R, max(SUBLANES, (1 << 19) // C // SUBLANES * SUBLANES), SUBLANES)

    def body(p_ref, w_ref, m_ref, v_ref, g_ref, d_ref, mo_ref, vo_ref):
        g = p_ref[0].astype(F32)
        for k in range(1, P):
            g = g + p_ref[k].astype(F32)
        mn = ADAM_B1 * m_ref[...] + (1.0 - ADAM_B1) * g
        vn = ADAM_B2 * v_ref[...] + (1.0 - ADAM_B2) * (g * g)
        m_hat = mn / (1.0 - ADAM_B1 ** ADAM_STEP)
        v_hat = vn / (1.0 - ADAM_B2 ** ADAM_STEP)
        g_ref[...] = g
        d_ref[...] = -ADAM_LR * (m_hat / (jnp.sqrt(v_hat) + ADAM_EPS) + ADAM_WD * w_ref[...])
        mo_ref[...] = mn
        vo_ref[...] = vn

    row = pl.BlockSpec((tr, C), lambda i: (i, 0))
    return pl.pallas_call(body, name=name, grid=(R // tr,),
                          in_specs=[pl.BlockSpec((P, tr, C), lambda i: (0, i, 0)), row, row, row],
                          out_specs=[row] * 4, out_shape=[_sds((R, C), F32)] * 4,
                          compiler_params=_cp("parallel"))(parts, w, m, v)


def _rows128(a, pad_rows=SUBLANES):
    flat = a.reshape(-1, LANES)
    pad = (-flat.shape[0]) % pad_rows
    return jnp.pad(flat, ((0, pad), (0, 0))) if pad else flat


SMALL = ("norm_g", "conv_b", "conv_ln_g", "conv_ln_b", "sgu_ln_g", "sgu_ln_b", "sgu_w", "sgu_b", "b_gate", "final_g")


def kernel(x, norm_g, w_in, conv_w, conv_b, conv_ln_g, conv_ln_b, sgu_ln_g, sgu_ln_b, sgu_w, sgu_b, w_branch, w_gate, b_gate, w_out, final_g, loss_target, m_norm_g, m_w_in, m_conv_w, m_conv_b, m_conv_ln_g, m_conv_ln_b, m_sgu_ln_g, m_sgu_ln_b, m_sgu_w, m_sgu_b, m_w_branch, m_w_gate, m_b_gate, m_w_out, m_final_g, v_norm_g, v_w_in, v_conv_w, v_conv_b, v_conv_ln_g, v_conv_ln_b, v_sgu_ln_g, v_sgu_ln_b, v_sgu_w, v_sgu_b, v_w_branch, v_w_gate, v_b_gate, v_w_out, v_final_g):
    L, D = norm_g.shape
    W = conv_b.shape[1]
    taps = conv_w.shape[1]
    weights = dict(norm_g=norm_g, w_in=w_in, conv_w=conv_w, conv_b=conv_b, conv_ln_g=conv_ln_g, conv_ln_b=conv_ln_b,
                   sgu_ln_g=sgu_ln_g, sgu_ln_b=sgu_ln_b, sgu_w=sgu_w, sgu_b=sgu_b, w_branch=w_branch, w_gate=w_gate,
                   b_gate=b_gate, w_out=w_out, final_g=final_g)
    mom_m = dict(norm_g=m_norm_g, w_in=m_w_in, conv_w=m_conv_w, conv_b=m_conv_b, conv_ln_g=m_conv_ln_g,
                 conv_ln_b=m_conv_ln_b, sgu_ln_g=m_sgu_ln_g, sgu_ln_b=m_sgu_ln_b, sgu_w=m_sgu_w, sgu_b=m_sgu_b,
                 w_branch=m_w_branch, w_gate=m_w_gate, b_gate=m_b_gate, w_out=m_w_out, final_g=m_final_g)
    mom_v = dict(norm_g=v_norm_g, w_in=v_w_in, conv_w=v_conv_w, conv_b=v_conv_b, conv_ln_g=v_conv_ln_g,
                 conv_ln_b=v_conv_ln_b, sgu_ln_g=v_sgu_ln_g, sgu_ln_b=v_sgu_ln_b, sgu_w=v_sgu_w, sgu_b=v_sgu_b,
                 w_branch=v_w_branch, w_gate=v_w_gate, b_gate=v_b_gate, w_out=v_w_out, final_g=v_final_g)
    core = lax.axis_index("c").astype(jnp.int32).reshape(1)
    me = 4 * lax.axis_index("x") + 2 * lax.axis_index("y") + lax.axis_index("c")

    gather_names = ("w_in", "w_gate", "w_br", "w_out", "conv_w")
    gather_axes = (1, 1, 2, 0, 1)

    def gather_start(tag, l, which, carry):
        shards = [w_in[l].astype(BF16), w_gate[l].astype(BF16), w_branch[l].astype(BF16), w_out[l].astype(BF16),
                  jnp.pad(conv_w[l], ((0, CONV_HALO - taps), (0, 0)))]
        shards = [shards[i] for i in which]
        axes = [gather_axes[i] for i in which]
        lands = []
        for s, ax in zip(shards, axes):
            full = list(s.shape)
            full[ax] *= N_DEV
            lands.append(lax.empty(tuple(full), s.dtype))
        return _exchange_start(f"gather_start_{tag}", "gather", shards, lands, axes, carry) + (which,)

    def gather_wait(tag, started, after):
        send, recv, shards, lands, _, which = started
        full = _exchange_wait(f"gather_wait_{tag}", "gather", send, recv, shards, lands,
                              [gather_axes[i] for i in which], after)
        return {gather_names[i]: f for i, f in zip(which, full)}

    xs = x[0]
    saved, gathered, smalls = [], [], []
    first = gather_start("0a", 0, (0,), xs)
    rest = gather_start("0b", 0, (1, 2, 3, 4), first[4])
    xs = rest[4]
    wl = gather_wait("0a", first, xs)
    wl["late"] = functools.partial(gather_wait, "0b", rest)
    for l in range(L):
        wl["taps"] = taps
        if l + 1 < L:
            started = gather_start(l + 1, l + 1, (0, 1, 2, 3, 4), xs)
            xs = started[4]
        sp = dict(norm_g=norm_g[l][None], conv_b=conv_b[l][None], conv_ln_g=conv_ln_g[l][None],
                  conv_ln_b=conv_ln_b[l][None], sgu_ln_g=sgu_ln_g[l][None], sgu_ln_b=sgu_ln_b[l][None],
                  sgu_w=sgu_w[l], sgu_bt=sgu_b[l].T, b_gate=b_gate[l][None])
        xs, sv = _layer_fwd(xs, wl, sp)
        saved.append(sv)
        gathered.append(wl)
        smalls.append(sp)
        if l + 1 < L:
            wl = gather_wait(l + 1, started, xs)

    dx, d_final, loss_part = _loss_head(xs, loss_target[0], final_g[None])
    loss = lax.psum(loss_part[0, 0], ("x", "y", "c"))

    big_names = ("w_in", "w_gate", "w_branch", "w_out")
    big_axes = (1, 1, 1, 0)
    me1 = me.astype(jnp.int32).reshape(1)
    outs = {}
    small_grads = []

    def scatter_start(l, g2d, carry):
        lands = []
        for g, ax in zip(g2d, big_axes):
            blk = list(g.shape)
            blk[ax] //= N_DEV
            lands.append(lax.empty((N_DEV, *blk), g.dtype))
        return _exchange_start(f"scatter_start_{l}", "scatter", g2d, lands, big_axes, carry)

    def scatter_finish(l, started, after):
        send, recv, g2d, lands, _ = started
        lands = _exchange_wait(f"scatter_wait_{l}", "scatter", send, recv, g2d, lands, big_axes, after)
        for nm, land, g, ax in zip(big_names, lands, g2d, big_axes):
            shard2d = land.shape[1:]
            outs.setdefault(nm, {})[l] = _adamw_scatter(
                "adamw_" + nm, land, g, ax, me1, weights[nm][l].reshape(shard2d), mom_m[nm][l].reshape(shard2d),
                mom_v[nm][l].reshape(shard2d))

    pending = None
    for l in reversed(range(L)):
        dx, big, small = _layer_bwd(dx, saved[l], gathered[l], smalls[l])
        small_grads.append(small)
        if pending is not None:
            scatter_finish(l + 1, pending, dx)
        g2d = [big["w_in"], big["w_gate"], big["w_br"].reshape(-1, D), big["w_out"]]
        if l > 0:
            pending = scatter_start(l, g2d, dx)
            dx = pending[4]
        else:
            pending = scatter_start(l, g2d, d_final)
            d_final = pending[4]
    small_grads.reverse()

    sg = {nm: jnp.stack([small_grads[l][nm] for l in range(L)]) for nm in SMALL[:-1]}
    sg["final_g"] = d_final[0]
    conv_w_full = jnp.stack([small_grads[l]["conv_w"] for l in range(L)])
    segs = [_rows128(sg[nm]) for nm in SMALL] + [_rows128(conv_w_full)]
    offs = [0]
    for s in segs:
        offs.append(offs[-1] + s.shape[0])
    pack = jnp.concatenate(segs, axis=0)
    (allp,) = _all_gather("gather_small_grads", [pack[None]], [0])

    def packed(src):
        return jnp.concatenate([_rows128(src[nm]) for nm in SMALL] + [jnp.zeros_like(segs[-1])], axis=0)

    s_g, s_d, s_m, s_v = _adamw_sum("adamw_small", allp, packed(weights), packed(mom_m), packed(mom_v))
    scatter_finish(0, pending, s_g)
    big_out = {nm: [jnp.stack([outs[nm][l][q] for l in range(L)]).reshape(weights[nm].shape) for q in range(4)]
               for nm in big_names}

    def unpack(buf, i, like):
        n = like.size // LANES
        return buf[offs[i]:offs[i] + n].reshape(like.shape)

    small_out = {nm: [unpack(b, i, weights[nm]) for b in (s_g, s_d, s_m, s_v)] for i, nm in enumerate(SMALL)}
    Wc = conv_w.shape[2]
    cw_sum = lax.dynamic_slice_in_dim(unpack(s_g, len(SMALL), conv_w_full), me * Wc, Wc, axis=2)
    cshape = (L * conv_w.shape[1], Wc)
    conv_out = [o.reshape(conv_w.shape) for o in _adamw_sum(
        "adamw_conv_w", cw_sum.reshape((1,) + cshape), conv_w.reshape(cshape), m_conv_w.reshape(cshape),
        v_conv_w.reshape(cshape))]

    order = ["norm_g", "w_in", "conv_w", "conv_b", "conv_ln_g", "conv_ln_b", "sgu_ln_g", "sgu_ln_b", "sgu_w", "sgu_b",
             "w_branch", "w_gate", "b_gate", "w_out", "final_g"]
    table = dict(small_out)
    table.update(big_out)
    table["conv_w"] = conv_out
    result = [loss, dx[None]]
    for q in range(4):
        result += [table[nm][q] for nm in order]
    return tuple(result)
```

```python
import functools

import jax
import jax.numpy as jnp
from jax import lax
from jax.experimental import pallas as pl
from jax.experimental.pallas import tpu as pltpu

F32 = jnp.float32
BF16 = jnp.bfloat16
LANES = 128
SUBLANES = 8
CONV_HALO = 32
SB_QUERY_BLOCK = 512
SB_QUERY_BLOCK_BWD = 512
SB_KEY_BLOCK = 1024
DIL_STEP_ROWS = 2048
DIL_UNROLL = 4
NORM_EPS = 1e-6
NEG = -1e30
N_DEV = 8
N_CHIP = 4
DIL_PATTERNS = ((128, 1), (512, 4), (2048, 16))

ADAM_LR = 0.001
ADAM_B1 = 0.9
ADAM_B2 = 0.999
ADAM_EPS = 1e-08
ADAM_WD = 0.01
ADAM_STEP = 10

MESH = pl.DeviceIdType.MESH
NN = (((1,), (0,)), ((), ()))
NT = (((1,), (1,)), ((), ()))
TN = (((0,), (0,)), ((), ()))
VMEM_LIMIT = 52 << 20


def _sds(shape, dtype):
    return jax.ShapeDtypeStruct(tuple(shape), dtype)


def _cp(*sem):
    return pltpu.CompilerParams(dimension_semantics=tuple(sem), vmem_limit_bytes=VMEM_LIMIT)


def _pick(n, target, quantum=LANES):
    if n <= target:
        return n
    t = (target // quantum) * quantum
    while t >= quantum:
        if n % t == 0:
            return t
        t -= quantum
    return n


def _sigmoid(x):
    return 1.0 / (1.0 + jnp.exp(-x))


def _silu(x):
    return x * _sigmoid(x)


def _dsilu(x):
    s = _sigmoid(x)
    return s * (1.0 + x * (1.0 - s))


_GELU_K = 0.7978845608028654
_GELU_A = 0.044715


def _gelu(x):
    return 0.5 * x * (1.0 + jnp.tanh(_GELU_K * (x + _GELU_A * x * x * x)))


def _dgelu(x):
    t = jnp.tanh(_GELU_K * (x + _GELU_A * x * x * x))
    return 0.5 * (1.0 + t) + 0.5 * x * (1.0 - t * t) * _GELU_K * (1.0 + 3.0 * _GELU_A * x * x)


def _ln_stats(v):
    mu = jnp.mean(v, axis=-1, keepdims=True)
    d = v - mu
    var = jnp.mean(d * d, axis=-1, keepdims=True)
    r = lax.rsqrt(var + NORM_EPS)
    return d * r, r


def _ln_bwd(dxh, xh, r):
    return r * (dxh - jnp.mean(dxh, axis=-1, keepdims=True) - xh * jnp.mean(dxh * xh, axis=-1, keepdims=True))


def _acc_rows(ref, row, val, first):
    @pl.when(first)
    def _():
        ref[...] = jnp.zeros_like(ref)
    ref[row:row + 1, :] += val


def _mm(name, a, b, *, grid, kaxis, dims, a_spec, b_spec, acc_shape, out_shape, out_specs,
        epilogue, extras=(), extra_specs=(), sem):
    nk = grid[kaxis]
    ne = len(extras)

    def body(*refs):
        a_ref, b_ref = refs[0], refs[1]
        ex = refs[2:2 + ne]
        outs = refs[2 + ne:-1]
        acc = refs[-1]
        k = pl.program_id(kaxis)

        @pl.when(k == 0)
        def _():
            acc[...] = jnp.zeros_like(acc)

        acc[...] += lax.dot_general(a_ref[...].astype(BF16), b_ref[...].astype(BF16), dims,
                                    preferred_element_type=F32)

        @pl.when(k == nk - 1)
        def _():
            epilogue(acc[...], ex, outs)

    return pl.pallas_call(
        body, name=name, grid=grid, in_specs=[a_spec, b_spec, *extra_specs], out_specs=out_specs,
        out_shape=out_shape, scratch_shapes=[pltpu.VMEM(acc_shape, F32)], compiler_params=_cp(*sem),
    )(a, b, *extras)


def _store(dtype):
    def ep(acc, ex, outs):
        outs[0][...] = acc.astype(dtype)
    return ep


def _mm_nn(name, a, b, out_dtype, epilogue=None, extras=(), extra_specs=(), tm=1024, tn=1024):
    M, K = a.shape
    N = b.shape[1]
    tm, tn = _pick(M, tm, SUBLANES), _pick(N, tn)
    return _mm(name, a, b, grid=(M // tm, N // tn, 1), kaxis=2, dims=NN,
               a_spec=pl.BlockSpec((tm, K), lambda i, j, k: (i, 0)),
               b_spec=pl.BlockSpec((K, tn), lambda i, j, k: (0, j)),
               acc_shape=(tm, tn), out_shape=_sds((M, N), out_dtype),
               out_specs=pl.BlockSpec((tm, tn), lambda i, j, k: (i, j)),
               epilogue=epilogue or _store(out_dtype), extras=extras, extra_specs=extra_specs,
               sem=("parallel", "parallel", "arbitrary"))


def _mm_nt(name, a, b, out_dtype, addend=None, tm=1024, tn=1024, tk=1024):
    M, K = a.shape
    N = b.shape[0]
    tm, tn, tk = _pick(M, tm, SUBLANES), _pick(N, tn), _pick(K, tk)
    extras, extra_specs = (), ()
    if addend is not None:
        extras = (addend,)
        extra_specs = (pl.BlockSpec((tm, tn), lambda i, j, k: (i, j)),)

    def ep(acc, ex, outs):
        if ex:
            acc = acc + ex[0][...]
        outs[0][...] = acc.astype(out_dtype)

    return _mm(name, a, b, grid=(M // tm, N // tn, K // tk), kaxis=2, dims=NT,
               a_spec=pl.BlockSpec((tm, tk), lambda i, j, k: (i, k)),
               b_spec=pl.BlockSpec((tn, tk), lambda i, j, k: (j, k)),
               acc_shape=(tm, tn), out_shape=_sds((M, N), out_dtype),
               out_specs=pl.BlockSpec((tm, tn), lambda i, j, k: (i, j)),
               epilogue=ep, extras=extras, extra_specs=extra_specs,
               sem=("parallel", "parallel", "arbitrary"))


def _mm_tn(name, a, b, tm=1024, tn=1024, tk=1024, out_dtype=BF16):
    K, M = a.shape
    N = b.shape[1]
    tm, tn, tk = _pick(M, tm), _pick(N, tn), _pick(K, tk, SUBLANES)
    return _mm(name, a, b, grid=(M // tm, N // tn, K // tk), kaxis=2, dims=TN,
               a_spec=pl.BlockSpec((tk, tm), lambda i, j, k: (k, i)),
               b_spec=pl.BlockSpec((tk, tn), lambda i, j, k: (k, j)),
               acc_shape=(tm, tn), out_shape=_sds((M, N), out_dtype),
               out_specs=pl.BlockSpec((tm, tn), lambda i, j, k: (i, j)),
               epilogue=_store(out_dtype), sem=("parallel", "parallel", "arbitrary"))


def _rmsnorm_fwd(x, g_row):
    T, D = x.shape
    tm = _pick(T, 512, SUBLANES)

    def body(x_ref, g_ref, h_ref):
        xv = x_ref[...]
        r = lax.rsqrt(jnp.mean(xv * xv, axis=-1, keepdims=True) + NORM_EPS)
        h_ref[...] = (xv * r * g_ref[...]).astype(BF16)

    row = pl.BlockSpec((tm, D), lambda i: (i, 0))
    return pl.pallas_call(body, name="rmsnorm_fwd", grid=(T // tm,),
                          in_specs=[row, pl.BlockSpec((1, D), lambda i: (0, 0))], out_specs=row,
                          out_shape=_sds((T, D), BF16), compiler_params=_cp("parallel"))(x, g_row)


def _rmsnorm_bwd(x, dh, dout, g_row):
    T, D = x.shape
    tm = _pick(T, 256, SUBLANES)

    def body(x_ref, dh_ref, do_ref, g_ref, dx_ref, dg_ref):
        xv = x_ref[...]
        r = lax.rsqrt(jnp.mean(xv * xv, axis=-1, keepdims=True) + NORM_EPS)
        xh = xv * r
        dhv = dh_ref[...]
        dxh = dhv * g_ref[...]
        dx_ref[...] = do_ref[...] + r * (dxh - xh * jnp.mean(dxh * xh, axis=-1, keepdims=True))
        _acc_rows(dg_ref, 0, jnp.sum(dhv * xh, axis=0, keepdims=True), pl.program_id(0) == 0)

    row = pl.BlockSpec((tm, D), lambda i: (i, 0))
    return pl.pallas_call(body, name="rmsnorm_bwd", grid=(T // tm,),
                          in_specs=[row, row, row, pl.BlockSpec((1, D), lambda i: (0, 0))],
                          out_specs=[row, pl.BlockSpec((SUBLANES, D), lambda i: (0, 0))],
                          out_shape=[_sds((T, D), F32), _sds((SUBLANES, D), F32)],
                          compiler_params=_cp("arbitrary"))(x, dh, dout, g_row)


def _loss_head(x, target, g_row):
    T, D = x.shape
    tm = _pick(T, 256, SUBLANES)

    def body(x_ref, t_ref, g_ref, dx_ref, dg_ref, loss_ref):
        first = pl.program_id(0) == 0
        xv = x_ref[...]
        g = g_ref[...]
        r = lax.rsqrt(jnp.mean(xv * xv, axis=-1, keepdims=True) + NORM_EPS)
        xh = xv * r
        err = xh * g - t_ref[...]
        part = 0.5 * jnp.sum(jnp.mean(err * err, axis=-1, keepdims=True), axis=0, keepdims=True)

        @pl.when(first)
        def _():
            loss_ref[...] = jnp.zeros_like(loss_ref)

        loss_ref[...] += jnp.broadcast_to(part, loss_ref.shape)
        dy = err / D
        _acc_rows(dg_ref, 0, jnp.sum(dy * xh, axis=0, keepdims=True), first)
        dxh = dy * g
        dx_ref[...] = r * (dxh - xh * jnp.mean(dxh * xh, axis=-1, keepdims=True))

    row = pl.BlockSpec((tm, D), lambda i: (i, 0))
    return pl.pallas_call(body, name="loss_head", grid=(T // tm,),
                          in_specs=[row, row, pl.BlockSpec((1, D), lambda i: (0, 0))],
                          out_specs=[row, pl.BlockSpec((SUBLANES, D), lambda i: (0, 0)),
                                     pl.BlockSpec((SUBLANES, LANES), lambda i: (0, 0))],
                          out_shape=[_sds((T, D), F32), _sds((SUBLANES, D), F32), _sds((SUBLANES, LANES), F32)],
                          compiler_params=_cp("arbitrary"))(x, target, g_row)


def _tri(cmp):
    r = lax.broadcasted_iota(jnp.int32, (LANES, LANES), 0)
    c = lax.broadcasted_iota(jnp.int32, (LANES, LANES), 1)
    return cmp(r, c).astype(BF16)


def _scan_mm(x, tri):
    return jnp.dot(x.astype(BF16), tri, preferred_element_type=F32)


def _sb_scores(qs, kb, causal):
    z = lax.dot_general(qs, kb, NT, preferred_element_type=F32)
    lb = jnp.minimum(z, 0.0) - jnp.log(1.0 + jnp.exp(-jnp.abs(z)))
    l1 = lb - z
    return lb, (l1 if causal is None else jnp.where(causal, l1, 0.0))


def _masked(causal, x):
    return x if causal is None else jnp.where(causal, x, 0.0)


def _lanes(x, s):
    return x[:, s * LANES:(s + 1) * LANES]


def _sb_fwd(proj, W, ys):
    T = proj.shape[0]
    H = W // LANES
    assert T // LANES <= LANES
    tq = _pick(T, SB_QUERY_BLOCK, LANES)
    kblk = _pick(T, SB_KEY_BLOCK, LANES)
    assert kblk % tq == 0
    nsub = kblk // LANES
    scale = LANES ** -0.5

    def body(q_ref, k_ref, v_ref, g_ref, ys_in, o_ref, y_ref, c_ref, run):
        i = pl.program_id(1)
        qs = (q_ref[...] * scale).astype(BF16)
        row = i * tq + lax.broadcasted_iota(jnp.int32, (tq, kblk), 0)
        key = lax.broadcasted_iota(jnp.int32, (tq, kblk), 1)
        col = lax.broadcasted_iota(jnp.int32, (tq, LANES), 1)
        tri = _tri(lambda r, c: r > c)
        nkb = ((i + 1) * tq + kblk - 1) // kblk
        o_ref[...] = jnp.zeros_like(o_ref)
        c_ref[...] = jnp.zeros_like(c_ref)
        run[...] = jnp.zeros_like(run)

        def make_step(diagonal):
            def step(jj, carry):
                j = nkb - 1 - jj
                off = pl.multiple_of(j * kblk, kblk)
                kb = k_ref[pl.ds(off, kblk), :].astype(BF16)
                vb = v_ref[pl.ds(off, kblk), :].astype(BF16)
                causal = (key + off < row) if diagonal else None
                lb, l1m = _sb_scores(qs, kb, causal)
                c_after = run[...]
                cs = c_ref[...]
                after = [None] * nsub
                for s in reversed(range(nsub)):
                    part = _lanes(l1m, s)
                    after[s] = c_after + _scan_mm(part, tri)
                    cs = jnp.where(col == j * nsub + s, c_after, cs)
                    c_after = c_after + jnp.sum(part, axis=1, keepdims=True)
                run[...] = c_after
                c_ref[...] = cs
                w = _masked(causal, jnp.exp(lb + jnp.concatenate(after, axis=1)))
                o_ref[...] += jnp.dot(w.astype(BF16), vb, preferred_element_type=F32)
                return carry
            return step

        make_step(True)(0, 0)
        lax.fori_loop(1, nkb, make_step(False), 0)
        y_ref[...] = (o_ref[...] * _silu(g_ref[...])).astype(BF16)

    qspec = lambda c0: pl.BlockSpec((tq, LANES), lambda h, i: (i, c0 + h))
    kvspec = lambda c0: pl.BlockSpec((T, LANES), lambda h, i: (0, c0 + h))
    out = pl.BlockSpec((tq, LANES), lambda h, i: (i, h))
    return pl.pallas_call(body, name="sb_fwd", grid=(H, T // tq),
                          in_specs=[qspec(0), kvspec(H), kvspec(2 * H), qspec(3 * H), HBM],
                          out_specs=[out, pl.BlockSpec((None, tq, LANES), lambda h, i: (0, i, h)), out],
                          out_shape=[_sds((T, W), F32), _sds(ys.shape, BF16), _sds((T, W), F32)],
                          input_output_aliases={4: 1},
                          scratch_shapes=[pltpu.VMEM((tq, LANES), F32)],
                          compiler_params=_cp("parallel", "arbitrary"))(proj, proj, proj, proj, ys)


def _sb_bwd(proj, o, carries, dy, W):
    T = proj.shape[0]
    H = W // LANES
    tq = _pick(T, SB_QUERY_BLOCK_BWD, LANES)
    kblk = _pick(T, SB_KEY_BLOCK, LANES)
    assert kblk % tq == 0
    nsub = kblk // LANES
    scale = LANES ** -0.5

    def body(q_ref, k_ref, v_ref, g_ref, o_ref, c_ref, dy_ref, dq_ref, dk_ref, dv_ref, dg_ref, run):
        i = pl.program_id(1)

        @pl.when(i == 0)
        def _():
            dk_ref[...] = jnp.zeros_like(dk_ref)
            dv_ref[...] = jnp.zeros_like(dv_ref)

        gv = g_ref[...]
        dyv = dy_ref[...]
        dg_ref[...] = (dyv * o_ref[...] * _dsilu(gv)).astype(BF16)
        dob = (dyv * _silu(gv)).astype(BF16)
        qs = (q_ref[...] * scale).astype(BF16)
        row = i * tq + lax.broadcasted_iota(jnp.int32, (tq, kblk), 0)
        key = lax.broadcasted_iota(jnp.int32, (tq, kblk), 1)
        col = lax.broadcasted_iota(jnp.int32, (tq, LANES), 1)
        tri_after = _tri(lambda r, c: r > c)
        tri_before = _tri(lambda r, c: r < c)
        dq_ref[...] = jnp.zeros_like(dq_ref)
        run[...] = jnp.zeros_like(run)

        def make_step(diagonal):
            def step(j, carry):
                off = pl.multiple_of(j * kblk, kblk)
                kb = k_ref[pl.ds(off, kblk), :].astype(BF16)
                vb = v_ref[pl.ds(off, kblk), :].astype(BF16)
                causal = (key + off < row) if diagonal else None
                lb, l1m = _sb_scores(qs, kb, causal)
                cs = c_ref[...]
                after = [jnp.sum(jnp.where(col == j * nsub + s, cs, 0.0), axis=1, keepdims=True)
                         + _scan_mm(_lanes(l1m, s), tri_after) for s in range(nsub)]
                w = _masked(causal, jnp.exp(lb + jnp.concatenate(after, axis=1)))
                gw = w * lax.dot_general(dob, vb, NT, preferred_element_type=F32)
                gpre = run[...]
                before = [None] * nsub
                for s in range(nsub):
                    part = _lanes(gw, s)
                    before[s] = gpre + _scan_mm(part, tri_before)
                    gpre = gpre + jnp.sum(part, axis=1, keepdims=True)
                run[...] = gpre
                beta = jnp.exp(lb)
                dz = _masked(causal, gw * (1.0 - beta) - jnp.concatenate(before, axis=1) * beta).astype(BF16)
                dq_ref[...] += jnp.dot(dz, kb, preferred_element_type=F32)
                dk_ref[pl.ds(off, kblk), :] += lax.dot_general(dz, qs, TN, preferred_element_type=F32)
                dv_ref[pl.ds(off, kblk), :] += lax.dot_general(w.astype(BF16), dob, TN, preferred_element_type=F32)
                return carry
            return step

        last = ((i + 1) * tq + kblk - 1) // kblk - 1
        lax.fori_loop(0, last, make_step(False), 0)
        make_step(True)(last, 0)
        dq_ref[...] = dq_ref[...] * scale

    qspec = lambda c0: pl.BlockSpec((tq, LANES), lambda h, i: (i, c0 + h))
    kvspec = lambda c0: pl.BlockSpec((T, LANES), lambda h, i: (0, c0 + h), pipeline_mode=pl.Buffered(1))
    blk = pl.BlockSpec((tq, LANES), lambda h, i: (i, h))
    full = pl.BlockSpec((T, LANES), lambda h, i: (0, h))
    return pl.pallas_call(body, name="sb_bwd", grid=(H, T // tq),
                          in_specs=[qspec(0), kvspec(H), kvspec(2 * H), qspec(3 * H), blk, blk,
                                    pl.BlockSpec((None, tq, LANES), lambda h, i: (0, i, h))],
                          out_specs=[blk, full, full, blk],
                          out_shape=[_sds((T, W), F32), _sds((T, W), F32), _sds((T, W), F32), _sds((T, W), BF16)],
                          scratch_shapes=[pltpu.VMEM((tq, LANES), F32)],
                          compiler_params=_cp("parallel", "arbitrary"))(proj, proj, proj, proj, o, carries, dy)


def _conv_specs(T, W, tm, col_a):
    per = tm // CONV_HALO
    cur = lambda c: pl.BlockSpec((tm, W), lambda i: (i, c))
    prev = lambda c: pl.BlockSpec((CONV_HALO, W), lambda i: (jnp.maximum(i * per - 1, 0), c))
    return cur, prev


def _conv_fwd(proj, conv_w, K, conv_b, ln_g, ln_b, W, ys):
    T = proj.shape[0]
    tm = _pick(T, 256, CONV_HALO)
    lead = CONV_HALO - (K - 1)
    cur, prev = _conv_specs(T, W, tm, 4)

    def body(a_ref, b_ref, ah_ref, bh_ref, g_ref, w_ref, cb_ref, lg_ref, lb_ref, ys_in, c_ref, y_ref, glu):
        i = pl.program_id(0)
        glu[0:CONV_HALO, :] = jnp.where(i > 0, ah_ref[...] * _sigmoid(bh_ref[...]), 0.0)
        glu[CONV_HALO:, :] = a_ref[...] * _sigmoid(b_ref[...])
        c = jnp.broadcast_to(cb_ref[...], (tm, W))
        for k in range(K):
            c = c + w_ref[k:k + 1, :] * glu[lead + k:lead + k + tm, :]
        c_ref[...] = c
        xh, _ = _ln_stats(c)
        y_ref[...] = (_silu(xh * lg_ref[...] + lb_ref[...]) * _silu(g_ref[...])).astype(BF16)

    vec = pl.BlockSpec((1, W), lambda i: (0, 0))
    row = pl.BlockSpec((tm, W), lambda i: (i, 0))
    return pl.pallas_call(body, name="conv_fwd", grid=(T // tm,),
                          in_specs=[cur(4), cur(5), prev(4), prev(5), cur(6),
                                    pl.BlockSpec((CONV_HALO, W), lambda i: (0, 0)), vec, vec, vec, HBM],
                          out_specs=[row, pl.BlockSpec((None, tm, W), lambda i: (1, i, 0))],
                          out_shape=[_sds((T, W), F32), _sds(ys.shape, BF16)], input_output_aliases={9: 1},
                          scratch_shapes=[pltpu.VMEM((tm + CONV_HALO, W), F32)],
                          compiler_params=_cp("parallel"))(proj, proj, proj, proj, proj, conv_w, conv_b, ln_g, ln_b, ys)


def _conv_bwd_ln(proj, c, dy, ln_g, ln_b, W):
    T = proj.shape[0]
    tm = _pick(T, 256, SUBLANES)

    def body(g_ref, c_ref, dy_ref, lg_ref, lb_ref, dc_ref, dg_ref, st_ref):
        first = pl.program_id(0) == 0
        gv = g_ref[...]
        dyv = dy_ref[...]
        xh, r = _ln_stats(c_ref[...])
        lg = lg_ref[...]
        ln = xh * lg + lb_ref[...]
        dg_ref[...] = (dyv * _silu(ln) * _dsilu(gv)).astype(BF16)
        dln = dyv * _silu(gv) * _dsilu(ln)
        dc = _ln_bwd(dln * lg, xh, r)
        dc_ref[...] = dc
        _acc_rows(st_ref, 0, jnp.sum(dln * xh, axis=0, keepdims=True), first)
        st_ref[1:2, :] += jnp.sum(dln, axis=0, keepdims=True)
        st_ref[2:3, :] += jnp.sum(dc, axis=0, keepdims=True)

    row = pl.BlockSpec((tm, W), lambda i: (i, 0))
    vec = pl.BlockSpec((1, W), lambda i: (0, 0))
    return pl.pallas_call(body, name="conv_bwd_ln", grid=(T // tm,),
                          in_specs=[pl.BlockSpec((tm, W), lambda i: (i, 6)), row,
                                    pl.BlockSpec((None, tm, W), lambda i: (1, i, 0)), vec, vec],
                          out_specs=[row, row, pl.BlockSpec((SUBLANES, W), lambda i: (0, 0))],
                          out_shape=[_sds((T, W), F32), _sds((T, W), BF16), _sds((SUBLANES, W), F32)],
                          compiler_params=_cp("arbitrary"))(proj, c, dy, ln_g, ln_b)


def _conv_bwd_taps(proj, dc, conv_w, K, W):
    T = proj.shape[0]
    tm = _pick(T, 256, CONV_HALO)
    lead = CONV_HALO - (K - 1)
    per = tm // CONV_HALO
    nblk = T // tm
    cur, prev = _conv_specs(T, W, tm, 4)

    def body(a_ref, b_ref, ah_ref, bh_ref, dc_ref, dcn_ref, w_ref, da_ref, db_ref, dw_ref, glu, dcs):
        i = pl.program_id(0)
        av = a_ref[...]
        sb = _sigmoid(b_ref[...])
        glu[0:CONV_HALO, :] = jnp.where(i > 0, ah_ref[...] * _sigmoid(bh_ref[...]), 0.0)
        glu[CONV_HALO:, :] = av * sb
        dcv = dc_ref[...]
        dcs[0:tm, :] = dcv
        dcs[tm:, :] = jnp.where(i < nblk - 1, dcn_ref[...], 0.0)

        @pl.when(i == 0)
        def _():
            dw_ref[...] = jnp.zeros_like(dw_ref)

        dglu = jnp.zeros((tm, W), F32)
        for k in range(K):
            dglu = dglu + w_ref[k:k + 1, :] * dcs[K - 1 - k:K - 1 - k + tm, :]
            dw_ref[k:k + 1, :] += jnp.sum(dcv * glu[lead + k:lead + k + tm, :], axis=0, keepdims=True)
        da_ref[...] = (dglu * sb).astype(BF16)
        db_ref[...] = (dglu * av * sb * (1.0 - sb)).astype(BF16)

    row = pl.BlockSpec((tm, W), lambda i: (i, 0))
    nxt = pl.BlockSpec((CONV_HALO, W), lambda i: (jnp.minimum((i + 1) * per, T // CONV_HALO - 1), 0))
    return pl.pallas_call(body, name="conv_bwd_taps", grid=(nblk,),
                          in_specs=[cur(4), cur(5), prev(4), prev(5), row, nxt,
                                    pl.BlockSpec((CONV_HALO, W), lambda i: (0, 0))],
                          out_specs=[row, row, pl.BlockSpec((CONV_HALO, W), lambda i: (0, 0))],
                          out_shape=[_sds((T, W), BF16), _sds((T, W), BF16), _sds((CONV_HALO, W), F32)],
                          scratch_shapes=[pltpu.VMEM((tm + CONV_HALO, W), F32), pltpu.VMEM((tm + CONV_HALO, W), F32)],
                          compiler_params=_cp("arbitrary"))(proj, proj, proj, proj, dc, dc, conv_w)


def _sgu_common(cu, cv, lg, lb, w_ref, bt_ref, z_scr, G, nch):
    u = _gelu(cu)
    xh, r = _ln_stats(_gelu(cv))
    vn = (xh * lg + lb).astype(BF16)
    rr = lax.broadcasted_iota(jnp.int32, (LANES, LANES), 0)
    cc = lax.broadcasted_iota(jnp.int32, (LANES, LANES), 1)
    tril = rr >= cc
    wts = [jnp.where(tril, w_ref[g], 0.0).astype(BF16) for g in range(G)]
    for ch in range(nch):
        rs = slice(ch * LANES, (ch + 1) * LANES)
        for g in range(G):
            cs = slice(g * LANES, (g + 1) * LANES)
            z_scr[rs, cs] = jnp.dot(wts[g], vn[rs, cs], preferred_element_type=F32) + bt_ref[:, g:g + 1]
    return u, xh, r, vn, wts, tril


def _sgu_fwd(proj, sgu_w, sgu_bt, ln_g, ln_b, W, ys):
    T = proj.shape[0]
    G = W // LANES
    tm = _pick(T, 256, LANES)
    nch = tm // LANES

    def body(u_ref, v_ref, g_ref, w_ref, bt_ref, lg_ref, lb_ref, ys_in, y_ref, z_scr):
        u, *_ = _sgu_common(u_ref[...], v_ref[...], lg_ref[...], lb_ref[...], w_ref, bt_ref, z_scr, G, nch)
        y_ref[...] = (u * z_scr[...] * _silu(g_ref[...])).astype(BF16)

    cur = lambda c: pl.BlockSpec((tm, W), lambda i: (i, c))
    vec = pl.BlockSpec((1, W), lambda i: (0, 0))
    return pl.pallas_call(body, name="sgu_fwd", grid=(T // tm,),
                          in_specs=[cur(7), cur(8), cur(9), pl.BlockSpec((G, LANES, LANES), lambda i: (0, 0, 0)),
                                    pl.BlockSpec((LANES, G), lambda i: (0, 0)), vec, vec, HBM],
                          out_specs=pl.BlockSpec((None, tm, W), lambda i: (2, i, 0)), out_shape=_sds(ys.shape, BF16),
                          input_output_aliases={7: 0}, scratch_shapes=[pltpu.VMEM((tm, W), F32)],
                          compiler_params=_cp("parallel"))(proj, proj, proj, sgu_w, sgu_bt, ln_g, ln_b, ys)


def _sgu_bwd(proj, dy, sgu_w, sgu_bt, ln_g, ln_b, W):
    T = proj.shape[0]
    G = W // LANES
    tm = _pick(T, 256, LANES)
    nch = tm // LANES

    def body(u_ref, v_ref, g_ref, dy_ref, w_ref, bt_ref, lg_ref, lb_ref,
             du_ref, dv_ref, dg_ref, dw_ref, dbt_ref, st_ref, z_scr, dvn_scr):
        first = pl.program_id(0) == 0
        cu, cv, gv, dyv = u_ref[...], v_ref[...], g_ref[...], dy_ref[...]
        lg = lg_ref[...]
        u, xh, r, vn, wts, tril = _sgu_common(cu, cv, lg, lb_ref[...], w_ref, bt_ref, z_scr, G, nch)
        z = z_scr[...]
        sg = _silu(gv)
        dg_ref[...] = (dyv * u * z * _dsilu(gv)).astype(BF16)
        du_ref[...] = (dyv * z * sg * _dgelu(cu)).astype(BF16)
        dz = dyv * u * sg
        dzb = dz.astype(BF16)

        @pl.when(first)
        def _():
            dw_ref[...] = jnp.zeros_like(dw_ref)
            dbt_ref[...] = jnp.zeros_like(dbt_ref)

        for g in range(G):
            cs = slice(g * LANES, (g + 1) * LANES)
            dwg = jnp.zeros((LANES, LANES), F32)
            dbg = jnp.zeros((LANES, 1), F32)
            for ch in range(nch):
                rs = slice(ch * LANES, (ch + 1) * LANES)
                dwg = dwg + lax.dot_general(dzb[rs, cs], vn[rs, cs], NT, preferred_element_type=F32)
                dbg = dbg + jnp.sum(dz[rs, cs], axis=1, keepdims=True)
                dvn_scr[rs, cs] = lax.dot_general(wts[g], dzb[rs, cs], TN, preferred_element_type=F32)
            dw_ref[g] += jnp.where(tril, dwg, 0.0)
            dbt_ref[:, g:g + 1] += dbg
        dvn = dvn_scr[...]
        _acc_rows(st_ref, 0, jnp.sum(dvn * xh, axis=0, keepdims=True), first)
        st_ref[1:2, :] += jnp.sum(dvn, axis=0, keepdims=True)
        dv_ref[...] = (_ln_bwd(dvn * lg, xh, r) * _dgelu(cv)).astype(BF16)

    cur = lambda c: pl.BlockSpec((tm, W), lambda i: (i, c))
    row = pl.BlockSpec((tm, W), lambda i: (i, 0))
    vec = pl.BlockSpec((1, W), lambda i: (0, 0))
    wspec = pl.BlockSpec((G, LANES, LANES), lambda i: (0, 0, 0))
    bspec = pl.BlockSpec((LANES, G), lambda i: (0, 0))
    return pl.pallas_call(body, name="sgu_bwd", grid=(T // tm,),
                          in_specs=[cur(7), cur(8), cur(9), pl.BlockSpec((None, tm, W), lambda i: (2, i, 0)),
                                    wspec, bspec, vec, vec],
                          out_specs=[row, row, row, wspec, bspec, pl.BlockSpec((SUBLANES, W), lambda i: (0, 0))],
                          out_shape=[_sds((T, W), BF16)] * 3 + [_sds((G, LANES, LANES), F32), _sds((LANES, G), F32),
                                                                 _sds((SUBLANES, W), F32)],
                          scratch_shapes=[pltpu.VMEM((tm, W), F32), pltpu.VMEM((tm, W), F32)],
                          compiler_params=_cp("arbitrary"))(proj, proj, proj, dy, sgu_w, sgu_bt, ln_g, ln_b)


def _rows(start, dil):
    return pl.ds(start, LANES, stride=dil) if dil > 1 else pl.ds(start, LANES)


def _dil_masks():
    a = lax.broadcasted_iota(jnp.int32, (LANES, LANES), 0)
    c = lax.broadcasted_iota(jnp.int32, (LANES, LANES), 1)
    return c <= a, c >= a


def _dil_geometry(T, dil):
    sbr = LANES * dil
    nb = max(1, min(DIL_STEP_ROWS, T) // sbr)
    return sbr, nb, T // (sbr * nb)


def _for_units(nb, dil, unit):
    for blk in range(nb):
        if dil <= DIL_UNROLL:
            for r in range(dil):
                unit(blk, r)
        else:
            def chunk(it, carry):
                for u in range(DIL_UNROLL):
                    unit(blk, it * DIL_UNROLL + u)
                return carry
            lax.fori_loop(0, dil // DIL_UNROLL, chunk, 0)


def _dil_fwd_group(proj, W, gi, dil):
    T = proj.shape[0]
    H = W // LANES
    sbr, nb, nsteps = _dil_geometry(T, dil)
    scale = LANES ** -0.5
    cq, ck, cv = (10 + gi) * H, (13 + gi) * H, 16 * H

    def body(q_ref, kc_ref, kp_ref, vc_ref, vp_ref, o_ref, l_ref):
        b = pl.program_id(1)
        m_cur, m_prev = _dil_masks()
        m_first = m_prev & (b > 0)

        def unit(blk, r):
            sl = _rows(blk * sbr + r, dil)
            if blk == 0:
                kp, vp, mp = kp_ref[_rows(r, dil), :], vp_ref[_rows(r, dil), :], m_first
            else:
                sp_ = _rows((blk - 1) * sbr + r, dil)
                kp, vp, mp = kc_ref[sp_, :], vc_ref[sp_, :], m_prev
            q = (q_ref[sl, :] * scale).astype(BF16)
            sc = lax.dot_general(q, kc_ref[sl, :].astype(BF16), NT, preferred_element_type=F32)
            sp = lax.dot_general(q, kp.astype(BF16), NT, preferred_element_type=F32)
            sc = jnp.where(m_cur, sc, NEG)
            sp = jnp.where(mp, sp, NEG)
            m = jnp.maximum(jnp.max(sc, axis=1, keepdims=True), jnp.max(sp, axis=1, keepdims=True))
            pc = jnp.exp(sc - m)
            pp = jnp.exp(sp - m)
            den = jnp.sum(pc, axis=1, keepdims=True) + jnp.sum(pp, axis=1, keepdims=True)
            pv = (jnp.dot(pc.astype(BF16), vc_ref[sl, :].astype(BF16), preferred_element_type=F32)
                  + jnp.dot(pp.astype(BF16), vp.astype(BF16), preferred_element_type=F32))
            o_ref[sl, :] = pv / den
            l_ref[sl, :] = jnp.broadcast_to(m + jnp.log(den), (LANES, LANES))

        _for_units(nb, dil, unit)

    cur = lambda c0: pl.BlockSpec((sbr * nb, LANES), lambda h, b: (b, c0 + h))
    prv = lambda c0: pl.BlockSpec((sbr, LANES), lambda h, b: (jnp.maximum(b * nb - 1, 0), c0 + h))
    out = pl.BlockSpec((sbr * nb, LANES), lambda h, b: (b, h))
    return pl.pallas_call(body, name=f"dil_fwd_g{gi}", grid=(H, nsteps),
                          in_specs=[cur(cq), cur(ck), prv(ck), cur(cv), prv(cv)], out_specs=[out, out],
                          out_shape=[_sds((T, W), F32), _sds((T, W), F32)],
                          compiler_params=_cp("parallel", "parallel"))(proj, proj, proj, proj, proj)


def _dil_combine(proj, os_, ls_, W, ys):
    T = proj.shape[0]
    tm = _pick(T, 256, SUBLANES)

    def body(g_ref, o0, o1, o2, l0, l1, l2, ys_in, od_ref, lse_ref, y_ref):
        a0, a1, a2 = l0[...], l1[...], l2[...]
        m = jnp.maximum(jnp.maximum(a0, a1), a2)
        e0, e1, e2 = jnp.exp(a0 - m), jnp.exp(a1 - m), jnp.exp(a2 - m)
        s = e0 + e1 + e2
        od = (e0 / s) * o0[...] + (e1 / s) * o1[...] + (e2 / s) * o2[...]
        od_ref[...] = od
        lse_ref[...] = m + jnp.log(s)
        y_ref[...] = (od * _silu(g_ref[...])).astype(BF16)

    row = pl.BlockSpec((tm, W), lambda i: (i, 0))
    return pl.pallas_call(body, name="dil_combine", grid=(T // tm,),
                          in_specs=[pl.BlockSpec((tm, W), lambda i: (i, 17))] + [row] * 6 + [HBM],
                          out_specs=[row, row, pl.BlockSpec((None, tm, W), lambda i: (3, i, 0))],
                          out_shape=[_sds((T, W), F32), _sds((T, W), F32), _sds(ys.shape, BF16)],
                          input_output_aliases={7: 2},
                          compiler_params=_cp("parallel"))(proj, *os_, *ls_, ys)


def _dil_bwd_pre(proj, od, dy, W):
    T = proj.shape[0]
    H = W // LANES
    tm = _pick(T, 512, SUBLANES)

    def body(g_ref, od_ref, dy_ref, do_ref, dl_ref, dg_ref):
        gv, odv, dyv = g_ref[...], od_ref[...], dy_ref[...]
        do = dyv * _silu(gv)
        do_ref[...] = do
        dl_ref[...] = jnp.broadcast_to(jnp.sum(do * odv, axis=1, keepdims=True), (tm, LANES))
        dg_ref[...] = (dyv * odv * _dsilu(gv)).astype(BF16)

    blk = pl.BlockSpec((tm, LANES), lambda i, h: (i, h))
    return pl.pallas_call(body, name="dil_bwd_pre", grid=(T // tm, H),
                          in_specs=[pl.BlockSpec((tm, LANES), lambda i, h: (i, 17 * H + h)), blk,
                                    pl.BlockSpec((None, tm, LANES), lambda i, h: (3, i, h))],
                          out_specs=[blk, blk, blk],
                          out_shape=[_sds((T, W), F32), _sds((T, W), F32), _sds((T, W), BF16)],
                          compiler_params=_cp("parallel", "parallel"))(proj, od, dy)


def _dil_bwd_group(proj, do, lse, delta, W, gi, dil):
    T = proj.shape[0]
    H = W // LANES
    sbr, nb, nsteps = _dil_geometry(T, dil)
    scale = LANES ** -0.5
    cq, ck, cv = (10 + gi) * H, (13 + gi) * H, 16 * H

    def body(qc_ref, qn_ref, kc_ref, kp_ref, vc_ref, vp_ref, doc_ref, don_ref, lc_ref, ln_ref, dc_ref, dn_ref,
             dq_ref, dk_ref, dv_ref):
        b = pl.program_id(1)
        m_cur, m_prev = _dil_masks()
        m_first = m_prev & (b > 0)
        m_last = m_prev & (b < nsteps - 1)

        def probs(q, k, mask, l):
            s = lax.dot_general(q, k, NT, preferred_element_type=F32)
            return jnp.exp(jnp.where(mask, s - l, NEG))

        def unit(blk, r):
            sl = _rows(blk * sbr + r, dil)
            if blk == 0:
                edge = _rows(r, dil)
                k_p, v_p, m_cp = kp_ref[edge, :], vp_ref[edge, :], m_first
            else:
                sp_ = _rows((blk - 1) * sbr + r, dil)
                k_p, v_p, m_cp = kc_ref[sp_, :], vc_ref[sp_, :], m_prev
            if blk == nb - 1:
                edge = _rows(r, dil)
                q_n, do_n, l_n, d_n, m_nc = qn_ref[edge, :], don_ref[edge, :], ln_ref[edge, :], dn_ref[edge, :], m_last
            else:
                sn_ = _rows((blk + 1) * sbr + r, dil)
                q_n, do_n, l_n, d_n, m_nc = qc_ref[sn_, :], doc_ref[sn_, :], lc_ref[sn_, :], dc_ref[sn_, :], m_prev
            q_c = (qc_ref[sl, :] * scale).astype(BF16)
            q_n = (q_n * scale).astype(BF16)
            k_c, k_p = kc_ref[sl, :].astype(BF16), k_p.astype(BF16)
            v_c, v_p = vc_ref[sl, :].astype(BF16), v_p.astype(BF16)
            do_c, do_n = doc_ref[sl, :].astype(BF16), do_n.astype(BF16)
            l_c, d_c = lc_ref[sl, :], dc_ref[sl, :]
            p_cc = probs(q_c, k_c, m_cur, l_c)
            p_cp = probs(q_c, k_p, m_cp, l_c)
            p_nc = probs(q_n, k_c, m_nc, l_n)
            ds_cc = (p_cc * (lax.dot_general(do_c, v_c, NT, preferred_element_type=F32) - d_c)).astype(BF16)
            ds_cp = (p_cp * (lax.dot_general(do_c, v_p, NT, preferred_element_type=F32) - d_c)).astype(BF16)
            ds_nc = (p_nc * (lax.dot_general(do_n, v_c, NT, preferred_element_type=F32) - d_n)).astype(BF16)
            dq = (jnp.dot(ds_cc, k_c, preferred_element_type=F32) + jnp.dot(ds_cp, k_p, preferred_element_type=F32))
            dk = (lax.dot_general(ds_cc, q_c, TN, preferred_element_type=F32)
                  + lax.dot_general(ds_nc, q_n, TN, preferred_element_type=F32))
            dv = (lax.dot_general(p_cc.astype(BF16), do_c, TN, preferred_element_type=F32)
                  + lax.dot_general(p_nc.astype(BF16), do_n, TN, preferred_element_type=F32))
            dq_ref[sl, :] = dq * scale
            dk_ref[sl, :] = dk
            dv_ref[sl, :] = dv

        _for_units(nb, dil, unit)

    cur = lambda c0: pl.BlockSpec((sbr * nb, LANES), lambda h, b: (b, c0 + h))
    prv = lambda c0: pl.BlockSpec((sbr, LANES), lambda h, b: (jnp.maximum(b * nb - 1, 0), c0 + h))
    nxt = lambda c0: pl.BlockSpec((sbr, LANES), lambda h, b: (jnp.minimum((b + 1) * nb, T // sbr - 1), c0 + h))
    out = pl.BlockSpec((sbr * nb, LANES), lambda h, b: (b, h))
    return pl.pallas_call(body, name=f"dil_bwd_g{gi}", grid=(H, nsteps),
                          in_specs=[cur(cq), nxt(cq), cur(ck), prv(ck), cur(cv), prv(cv),
                                    cur(0), nxt(0), cur(0), nxt(0), cur(0), nxt(0)],
                          out_specs=[out, out, out], out_shape=[_sds((T, W), F32)] * 3,
                          compiler_params=_cp("parallel", "parallel"))(
        proj, proj, proj, proj, proj, proj, do, do, lse, lse, delta, delta)


def _sum3_bf16(a, b, c):
    T, W = a.shape
    tm = _pick(T, 512, SUBLANES)

    def body(a_ref, b_ref, c_ref, o_ref):
        o_ref[...] = (a_ref[...] + b_ref[...] + c_ref[...]).astype(BF16)

    row = pl.BlockSpec((tm, W), lambda i: (i, 0))
    return pl.pallas_call(body, name="dil_dv_sum", grid=(T // tm,), in_specs=[row, row, row], out_specs=row,
                          out_shape=_sds((T, W), BF16), compiler_params=_cp("parallel"))(a, b, c)


def _to_bf16(name, parts):
    T, W = parts[0].shape
    n = len(parts)
    tm = _pick(T, 512, SUBLANES)

    def body(*refs):
        o_ref = refs[n]
        for p in range(n):
            o_ref[:, p * W:(p + 1) * W] = refs[p][...].astype(BF16)

    row = pl.BlockSpec((tm, W), lambda i: (i, 0))
    return pl.pallas_call(body, name=name, grid=(T // tm,), in_specs=[row] * n,
                          out_specs=pl.BlockSpec((tm, n * W), lambda i: (i, 0)),
                          out_shape=_sds((T, n * W), BF16), compiler_params=_cp("parallel"))(*parts)


def _branch_merge(ys, w_br, gates):
    NB, T, W = ys.shape
    D = w_br.shape[2]
    tm, tn = _pick(T, 1024, SUBLANES), _pick(D, 1024)
    nj = D // tn

    def body(y_ref, w_ref, g_ref, yp_ref, m_ref, acc):
        n = pl.program_id(2)
        yp = jnp.dot(y_ref[...], w_ref[...], preferred_element_type=F32)
        yp_ref[...] = yp.astype(BF16)

        @pl.when(n == 0)
        def _():
            acc[...] = jnp.zeros_like(acc)

        acc[...] += g_ref[...].astype(F32) * yp

        @pl.when(n == NB - 1)
        def _():
            m_ref[...] = acc[...].astype(BF16)

    return pl.pallas_call(
        body, name="branch_merge", grid=(T // tm, nj, NB),
        in_specs=[pl.BlockSpec((None, tm, W), lambda i, j, n: (n, i, 0)),
                  pl.BlockSpec((None, W, tn), lambda i, j, n: (n, 0, j)),
                  pl.BlockSpec((tm, tn), lambda i, j, n: (i, n * nj + j))],
        out_specs=[pl.BlockSpec((None, tm, tn), lambda i, j, n: (n, i, j)),
                   pl.BlockSpec((tm, tn), lambda i, j, n: (i, j))],
        out_shape=[_sds((NB, T, D), BF16), _sds((T, D), BF16)],
        scratch_shapes=[pltpu.VMEM((tm, tn), F32)],
        compiler_params=_cp("parallel", "parallel", "arbitrary"))(ys, w_br, gates)


def _merge_bwd(dmerged, gates, yproj):
    NB, T, D = yproj.shape
    tm = _pick(T, 256, SUBLANES)

    def body(dm_ref, g_ref, yp_ref, dyp_ref, dz_ref, db_ref):
        dm, g = dm_ref[...], g_ref[...].astype(F32)
        dyp_ref[...] = (dm * g).astype(BF16)
        dz = dm * yp_ref[...].astype(F32) * g * (1.0 - g)
        dz_ref[...] = dz.astype(BF16)
        _acc_rows(db_ref, 0, jnp.sum(dz, axis=0, keepdims=True), pl.program_id(1) == 0)

    return pl.pallas_call(
        body, name="merge_bwd", grid=(NB, T // tm),
        in_specs=[pl.BlockSpec((tm, D), lambda n, i: (i, 0)), pl.BlockSpec((tm, D), lambda n, i: (i, n)),
                  pl.BlockSpec((None, tm, D), lambda n, i: (n, i, 0))],
        out_specs=[pl.BlockSpec((None, tm, D), lambda n, i: (n, i, 0)), pl.BlockSpec((tm, D), lambda n, i: (i, n)),
                   pl.BlockSpec((SUBLANES, D), lambda n, i: (0, n))],
        out_shape=[_sds((NB, T, D), BF16), _sds((T, NB * D), BF16), _sds((SUBLANES, NB * D), F32)],
        compiler_params=_cp("parallel", "arbitrary"))(dmerged, gates, yproj)


def _branch_bwd_dy(dyp, w_br):
    NB, T, D = dyp.shape
    W = w_br.shape[1]
    tm = _pick(T, 1024, SUBLANES)

    def body(a_ref, w_ref, o_ref):
        o_ref[...] = lax.dot_general(a_ref[...], w_ref[...], NT, preferred_element_type=F32)

    return pl.pallas_call(body, name="branch_bwd_dy", grid=(NB, T // tm),
                          in_specs=[pl.BlockSpec((None, tm, D), lambda n, i: (n, i, 0)),
                                    pl.BlockSpec((None, W, D), lambda n, i: (n, 0, 0))],
                          out_specs=pl.BlockSpec((None, tm, W), lambda n, i: (n, i, 0)),
                          out_shape=_sds((NB, T, W), F32), compiler_params=_cp("parallel", "parallel"))(dyp, w_br)


def _branch_bwd_dw(ys, dyp):
    NB, T, W = ys.shape
    D = dyp.shape[2]
    tm, tn, tk = _pick(W, 1024), _pick(D, 1024), _pick(T, 1024, SUBLANES)
    return _mm("branch_bwd_dw", ys, dyp, grid=(NB, W // tm, D // tn, T // tk), kaxis=3, dims=TN,
               a_spec=pl.BlockSpec((None, tk, tm), lambda n, i, j, k: (n, k, i)),
               b_spec=pl.BlockSpec((None, tk, tn), lambda n, i, j, k: (n, k, j)),
               acc_shape=(tm, tn), out_shape=_sds((NB, W, D), BF16),
               out_specs=pl.BlockSpec((None, tm, tn), lambda n, i, j, k: (n, i, j)),
               epilogue=_store(BF16), sem=("parallel", "parallel", "parallel", "arbitrary"))


def _layer_fwd(x, wl, sp):
    W = sp["conv_b"].shape[-1]
    D = x.shape[1]
    h = _rmsnorm_fwd(x, sp["norm_g"])
    proj = _mm_nn("proj", h, wl["w_in"], F32, tn=768)
    if "late" in wl:
        wl.update(wl.pop("late")(proj))

    def gate_ep(acc, ex, outs):
        outs[0][...] = _sigmoid(acc + ex[0][...]).astype(BF16)

    tn_g = _pick(4 * D, 1024)
    gates = _mm_nn("gates", h, wl["w_gate"], BF16, epilogue=gate_ep, extras=(sp["b_gate"],),
                   extra_specs=(pl.BlockSpec((1, tn_g), lambda i, j, k: (0, j)),), tn=tn_g)
    ys = lax.empty((4, x.shape[0], W), BF16)
    oa, ys, sb_carries = _sb_fwd(proj, W, ys)
    cpre, ys = _conv_fwd(proj, wl["conv_w"], wl["taps"], sp["conv_b"], sp["conv_ln_g"], sp["conv_ln_b"], W, ys)
    ys = _sgu_fwd(proj, sp["sgu_w"], sp["sgu_bt"], sp["sgu_ln_g"], sp["sgu_ln_b"], W, ys)
    os_, ls_ = zip(*[_dil_fwd_group(proj, W, gi, dil) for gi, (_, dil) in enumerate(DIL_PATTERNS)])
    od, lse, ys = _dil_combine(proj, os_, ls_, W, ys)
    yproj, merged = _branch_merge(ys, wl["w_br"], gates)

    def res_ep(acc, ex, outs):
        outs[0][...] = ex[0][...] + acc

    tm_o, tn_o = _pick(x.shape[0], 1024, SUBLANES), _pick(D, 1024)
    xn = _mm_nn("out_proj", merged, wl["w_out"], F32, epilogue=res_ep, extras=(x,),
                extra_specs=(pl.BlockSpec((tm_o, tn_o), lambda i, j, k: (i, j)),), tm=tm_o, tn=tn_o)
    saved = dict(x=x, h=h, proj=proj, gates=gates, oa=oa, sb_carries=sb_carries, cpre=cpre, od=od, lse=lse, ys=ys, yproj=yproj, merged=merged)
    return xn, saved


def _layer_bwd(dout, sv, wl, sp):
    W = sp["conv_b"].shape[-1]
    proj = sv["proj"]
    dmerged = _mm_nt("out_proj_bwd_dx", dout, wl["w_out"], F32)
    g_w_out = _mm_tn("out_proj_bwd_dw", sv["merged"], dout)
    dyp, dzg, db_gate = _merge_bwd(dmerged, sv["gates"], sv["yproj"])
    dy = _branch_bwd_dy(dyp, wl["w_br"])
    g_w_br = _branch_bwd_dw(sv["ys"], dyp)
    g_w_gate = _mm_tn("gate_bwd_dw", sv["h"], dzg)

    a_dq, a_dk, a_dv, a_dg = _sb_bwd(proj, sv["oa"], sv["sb_carries"], dy, W)
    a_qkv = _to_bf16("sb_bwd_cast", [a_dq, a_dk, a_dv])

    dc, b_dg, conv_stats = _conv_bwd_ln(proj, sv["cpre"], dy, sp["conv_ln_g"], sp["conv_ln_b"], W)
    b_da, b_db, g_conv_w = _conv_bwd_taps(proj, dc, wl["conv_w"], wl["taps"], W)

    c_du, c_dv, c_dg, g_sgu_w, g_sgu_bt, sgu_stats = _sgu_bwd(proj, dy, sp["sgu_w"], sp["sgu_bt"],
                                                              sp["sgu_ln_g"], sp["sgu_ln_b"], W)

    do, delta, d_dg = _dil_bwd_pre(proj, sv["od"], dy, W)
    dqs, dks, dvs = zip(*[_dil_bwd_group(proj, do, sv["lse"], delta, W, gi, dil)
                          for gi, (_, dil) in enumerate(DIL_PATTERNS)])
    d_qk = _to_bf16("dil_bwd_cast", [*dqs, *dks])
    d_dv = _sum3_bf16(*dvs)

    dproj = jnp.concatenate([a_qkv, a_dg, b_da, b_db, b_dg, c_du, c_dv, c_dg, d_qk, d_dv, d_dg], axis=1)
    g_w_in = _mm_tn("proj_bwd_dw", sv["h"], dproj, tn=768)
    dh = _mm_nt("gate_bwd_dh", dzg, wl["w_gate"], F32)
    dh = _mm_nt("proj_bwd_dh", dproj, wl["w_in"], F32, addend=dh)
    dx, dnorm = _rmsnorm_bwd(sv["x"], dh, dout, sp["norm_g"])

    K = wl["taps"]
    small = dict(norm_g=dnorm[0], conv_w=g_conv_w[:K], conv_b=conv_stats[2], conv_ln_g=conv_stats[0],
                 conv_ln_b=conv_stats[1], sgu_ln_g=sgu_stats[0], sgu_ln_b=sgu_stats[1], sgu_w=g_sgu_w,
                 sgu_b=g_sgu_bt.T, b_gate=db_gate[0])
    big = dict(w_in=g_w_in, w_gate=g_w_gate, w_br=g_w_br, w_out=g_w_out)
    return dx, big, small


HBM = pl.BlockSpec(memory_space=pl.ANY)


def _mesh_pos():
    return lax.axis_index("x"), lax.axis_index("y"), lax.axis_index("c")


def _other_chips(x, y):
    return [(1 - x, y), (x, 1 - y), (1 - x, 1 - y)]


def _shard_of(ref, axis, size, index):
    idx = [slice(None)] * len(ref.shape)
    idx[axis] = pl.ds(index * size, size)
    return ref.at[tuple(idx)]


def _all_gather(name, shards, axes):
    n = len(shards)
    out_shape = []
    for s, ax in zip(shards, axes):
        shp = list(s.shape)
        shp[ax] *= N_DEV
        out_shape.append(_sds(shp, s.dtype))

    def body(*refs):
        ins, outs = refs[:n], refs[n:2 * n]
        send, recv, lsem = refs[2 * n:]
        x, y, c = _mesh_pos()
        me, sib = (x, y, c), (x, y, 1 - c)
        chips = _other_chips(x, y)
        dev = lambda px, py, pc: 4 * px + 2 * py + pc

        def blk(a, d):
            return _shard_of(outs[a], axes[a], ins[a].shape[axes[a]], d)

        def cp(a, k, d, to, src=None):
            return pltpu.make_async_remote_copy(src_ref=blk(a, d) if src is None else src, dst_ref=blk(a, d),
                                                send_sem=send.at[a * 7 + k], recv_sem=recv.at[a * 7 + k],
                                                device_id=to, device_id_type=MESH)

        own = [pltpu.make_async_copy(ins[a], blk(a, dev(*me)), lsem.at[a]) for a in range(n)]
        for o in own:
            o.start()
        first = []
        for a in range(n):
            first.append(cp(a, 0, dev(*me), sib, src=ins[a]))
            first += [cp(a, 1 + j, dev(*me), (*ch, c), src=ins[a]) for j, ch in enumerate(chips)]
        for f in first:
            f.start()
        passed = []
        for j, ch in enumerate(chips):
            for a in range(n):
                cp(a, 1 + j, dev(*ch, c), me).wait_recv()
                p = cp(a, 4 + j, dev(*ch, c), sib)
                p.start()
                passed.append(p)
        for a in range(n):
            cp(a, 0, dev(*sib), me).wait_recv()
            for j, ch in enumerate(chips):
                cp(a, 4 + j, dev(*ch, 1 - c), me).wait_recv()
        for f in first + passed:
            f.wait_send()
        for o in own:
            o.wait()

    return pl.pallas_call(body, name=name, in_specs=[HBM] * n, out_specs=[HBM] * n, out_shape=out_shape,
                          scratch_shapes=[pltpu.SemaphoreType.DMA((7 * n,)), pltpu.SemaphoreType.DMA((7 * n,)),
                                          pltpu.SemaphoreType.DMA((n,))])(*shards)


def _rs_pair(name, grads, axes):
    n = len(grads)
    sizes = [g.shape[ax] // N_DEV for g, ax in zip(grads, axes)]
    out_shape = []
    for g, ax, sz in zip(grads, axes, sizes):
        shp = list(g.shape)
        shp[ax] = sz
        out_shape.append(_sds([N_CHIP] + shp, g.dtype))

    def body(*refs):
        ins, outs = refs[:n], refs[n:2 * n]
        send, recv = refs[2 * n:]
        x, y, c = _mesh_pos()
        cps = []
        for a in range(n):
            for k in range(N_CHIP):
                cps.append(pltpu.make_async_remote_copy(
                    src_ref=_shard_of(ins[a], axes[a], sizes[a], 2 * k + (1 - c)), dst_ref=outs[a].at[k],
                    send_sem=send.at[a * N_CHIP + k], recv_sem=recv.at[a * N_CHIP + k],
                    device_id=(x, y, 1 - c), device_id_type=MESH))
        for cp in cps:
            cp.start()
        for cp in cps:
            cp.wait()

    return pl.pallas_call(body, name=name, in_specs=[HBM] * n, out_specs=[HBM] * n, out_shape=out_shape,
                          scratch_shapes=[pltpu.SemaphoreType.DMA((N_CHIP * n,)),
                                          pltpu.SemaphoreType.DMA((N_CHIP * n,))])(*grads)


def _pair_sum(name, grad, recv, axis, core):
    R, C = grad.shape
    _, r, cw = recv.shape
    tr = _pick(r, 512, SUBLANES)
    nr = r // tr

    def body(c_ref, g_ref, r_ref, o_ref):
        o_ref[...] = (g_ref[...] + r_ref[...]).astype(BF16)

    if axis == 1:
        gspec = pl.BlockSpec((tr, cw), lambda k, i, c_ref: (i, 2 * k + c_ref[0]))
    else:
        gspec = pl.BlockSpec((tr, cw), lambda k, i, c_ref: ((2 * k + c_ref[0]) * nr + i, 0))
    part = pl.BlockSpec((None, tr, cw), lambda k, i, c_ref: (k, i, 0))
    return pl.pallas_call(
        body, name=name, out_shape=_sds((N_CHIP, r, cw), BF16),
        grid_spec=pltpu.PrefetchScalarGridSpec(num_scalar_prefetch=1, grid=(N_CHIP, nr), in_specs=[gspec, part],
                                               out_specs=part),
        compiler_params=_cp("parallel", "parallel"))(core, grad, recv)


def _rs_chips(name, parts):
    n = len(parts)

    def body(*refs):
        ins, outs = refs[:n], refs[n:2 * n]
        send, recv, lsem = refs[2 * n:]
        x, y, c = _mesh_pos()
        mine = 2 * x + y
        own = [pltpu.make_async_copy(ins[a].at[mine], outs[a].at[mine], lsem.at[a]) for a in range(n)]
        for o in own:
            o.start()
        cps = []
        for a in range(n):
            for j, (px, py) in enumerate(_other_chips(x, y)):
                cps.append(pltpu.make_async_remote_copy(
                    src_ref=ins[a].at[2 * px + py], dst_ref=outs[a].at[mine],
                    send_sem=send.at[a * 3 + j], recv_sem=recv.at[a * 3 + j],
                    device_id=(px, py, c), device_id_type=MESH))
        for cp in cps:
            cp.start()
        for a in range(n):
            for j, (px, py) in enumerate(_other_chips(x, y)):
                pltpu.make_async_remote_copy(
                    src_ref=ins[a].at[mine], dst_ref=outs[a].at[2 * px + py],
                    send_sem=send.at[a * 3 + j], recv_sem=recv.at[a * 3 + j],
                    device_id=(x, y, c), device_id_type=MESH).wait_recv()
        for cp in cps:
            cp.wait_send()
        for o in own:
            o.wait()

    return pl.pallas_call(body, name=name, in_specs=[HBM] * n, out_specs=[HBM] * n,
                          out_shape=[_sds(p.shape, p.dtype) for p in parts],
                          scratch_shapes=[pltpu.SemaphoreType.DMA((3 * n,)), pltpu.SemaphoreType.DMA((3 * n,)),
                                          pltpu.SemaphoreType.DMA((n,))])(*parts)


HBM_ONLY = pl.BlockSpec(memory_space=pltpu.HBM)
SEM_SPEC = pl.BlockSpec(memory_space=pltpu.SEMAPHORE)
N_PEER = N_DEV - 1


def _peer(x, y, c, m):
    flip = lambda v, bit: 1 - v if bit else v
    return flip(x, (m >> 2) & 1), flip(y, (m >> 1) & 1), flip(c, m & 1)


def _exchange_copies(kind, srcs, lands, send, recv, axes):
    x, y, c = _mesh_pos()
    me = 4 * x + 2 * y + c
    n = len(srcs)
    remote, local = [], []
    for a in range(n):
        for m in range(1, N_DEV):
            px, py, pc = _peer(x, y, c, m)
            if kind == "gather":
                src = srcs[a]
                dst = _shard_of(lands[a], axes[a], srcs[a].shape[axes[a]], me)
            else:
                src = _shard_of(srcs[a], axes[a], srcs[a].shape[axes[a]] // N_DEV, 4 * px + 2 * py + pc)
                dst = lands[a].at[me]
            remote.append(pltpu.make_async_remote_copy(
                src_ref=src, dst_ref=dst, send_sem=send.at[a * N_PEER + m - 1], recv_sem=recv.at[a * N_PEER + m - 1],
                device_id=(px, py, pc), device_id_type=MESH))
        if kind == "gather":
            local.append(pltpu.make_async_copy(srcs[a], _shard_of(lands[a], axes[a], srcs[a].shape[axes[a]], me),
                                               send.at[n * N_PEER + a]))
    return remote, local


def _exchange_start(name, kind, srcs, lands, axes, carry):
    n = len(srcs)
    hbm = lambda a: pltpu.with_memory_space_constraint(a, pltpu.HBM)

    def body(*refs):
        send, recv = refs[2 * n + 1], refs[2 * n + 2]
        remote, local = _exchange_copies(kind, refs[:n], refs[n:2 * n], send, recv, axes)
        for cp in remote + local:
            cp.start()

    thru = [*srcs, *lands, carry]
    res = pl.pallas_call(
        body, name=name,
        out_shape=(pltpu.SemaphoreType.DMA((n * N_DEV,)), pltpu.SemaphoreType.DMA((n * N_PEER,)),
                   *[pltpu.HBM(a.shape, a.dtype) for a in thru]),
        in_specs=[HBM_ONLY] * len(thru), out_specs=(SEM_SPEC, SEM_SPEC, *[HBM_ONLY] * len(thru)),
        input_output_aliases={i: 2 + i for i in range(len(thru))},
        compiler_params=pltpu.CompilerParams(has_side_effects=pltpu.SideEffectType.DATAFLOW_SIDE_EFFECTING),
    )(*[hbm(a) for a in thru])
    return res[0], res[1], list(res[2:2 + n]), list(res[2 + n:2 + 2 * n]), res[2 + 2 * n]


def _exchange_wait(name, kind, send, recv, srcs, lands, axes, after):
    n = len(srcs)

    def body(*refs):
        send_ref, recv_ref = refs[2 * n], refs[2 * n + 1]
        remote, local = _exchange_copies(kind, refs[:n], refs[n:2 * n], send_ref, recv_ref, axes)
        for cp in remote:
            cp.wait_send()
            cp.wait_recv()
        for cp in local:
            cp.wait()

    thru = [*srcs, *lands]
    res = pl.pallas_call(
        body, name=name, out_shape=tuple(pltpu.HBM(a.shape, a.dtype) for a in thru),
        in_specs=[*[HBM_ONLY] * len(thru), SEM_SPEC, SEM_SPEC, HBM], out_specs=tuple([HBM_ONLY] * len(thru)),
        input_output_aliases={i: i for i in range(len(thru))},
        compiler_params=pltpu.CompilerParams(has_side_effects=pltpu.SideEffectType.DATAFLOW_SIDE_EFFECTING),
    )(*thru, send, recv, after)
    return list(res[n:])


def _adamw_scatter(name, land, grad, axis, me, w, m, v):
    P, R, C = land.shape
    tr = _pick(R, max(SUBLANES, (1 << 18) // C // 16 * 16), 16)
    nr = R // tr

    def body(me_ref, l_ref, o_ref, w_ref, m_ref, v_ref, g_ref, d_ref, mo_ref, vo_ref):
        own = o_ref[...].astype(F32)
        g = jnp.where(me_ref[0] == 0, own, l_ref[0].astype(F32))
        for k in range(1, P):
            g = g + jnp.where(me_ref[0] == k, own, l_ref[k].astype(F32))
        mn = ADAM_B1 * m_ref[...] + (1.0 - ADAM_B1) * g
        vn = ADAM_B2 * v_ref[...] + (1.0 - ADAM_B2) * (g * g)
        m_hat = mn / (1.0 - ADAM_B1 ** ADAM_STEP)
        v_hat = vn / (1.0 - ADAM_B2 ** ADAM_STEP)
        g_ref[...] = g
        d_ref[...] = -ADAM_LR * (m_hat / (jnp.sqrt(v_hat) + ADAM_EPS) + ADAM_WD * w_ref[...])
        mo_ref[...] = mn
        vo_ref[...] = vn

    if axis == 1:
        own_spec = pl.BlockSpec((tr, C), lambda i, me_ref: (i, me_ref[0]))
    else:
        own_spec = pl.BlockSpec((tr, C), lambda i, me_ref: (me_ref[0] * nr + i, 0))
    row = pl.BlockSpec((tr, C), lambda i, me_ref: (i, 0))
    return pl.pallas_call(
        body, name=name, out_shape=[_sds((R, C), F32)] * 4,
        grid_spec=pltpu.PrefetchScalarGridSpec(
            num_scalar_prefetch=1, grid=(nr,),
            in_specs=[pl.BlockSpec((P, tr, C), lambda i, me_ref: (0, i, 0)), own_spec, row, row, row],
            out_specs=[row] * 4),
        compiler_params=_cp("parallel"))(me, land, grad, w, m, v)


def _adamw_sum(name, parts, w, m, v):
    P, R, C = parts.shape
    tr = _pick(R, max(SUBLANES, (1 << 19) // C // SUBLANES * SUBLANES), SUBLANES)

    def body(p_ref, w_ref, m_ref, v_ref, g_ref, d_ref, mo_ref, vo_ref):
        g = p_ref[0].astype(F32)
        for k in range(1, P):
            g = g + p_ref[k].astype(F32)
        mn = ADAM_B1 * m_ref[...] + (1.0 - ADAM_B1) * g
        vn = ADAM_B2 * v_ref[...] + (1.0 - ADAM_B2) * (g * g)
        m_hat = mn / (1.0 - ADAM_B1 ** ADAM_STEP)
        v_hat = vn / (1.0 - ADAM_B2 ** ADAM_STEP)
        g_ref[...] = g
        d_ref[...] = -ADAM_LR * (m_hat / (jnp.sqrt(v_hat) + ADAM_EPS) + ADAM_WD * w_ref[...])
        mo_ref[...] = mn
        vo_ref[...] = vn

    row = pl.BlockSpec((tr, C), lambda i: (i, 0))
    return pl.pallas_call(body, name=name, grid=(R // tr,),
                          in_specs=[pl.BlockSpec((P, tr, C), lambda i: (0, i, 0)), row, row, row],
                          out_specs=[row] * 4, out_shape=[_sds((R, C), F32)] * 4,
                          compiler_params=_cp("parallel"))(parts, w, m, v)


def _rows128(a, pad_rows=SUBLANES):
    flat = a.reshape(-1, LANES)
    pad = (-flat.shape[0]) % pad_rows
    return jnp.pad(flat, ((0, pad), (0, 0))) if pad else flat


SMALL = ("norm_g", "conv_b", "conv_ln_g", "conv_ln_b", "sgu_ln_g", "sgu_ln_b", "sgu_w", "sgu_b", "b_gate", "final_g")


def kernel(x, norm_g, w_in, conv_w, conv_b, conv_ln_g, conv_ln_b, sgu_ln_g, sgu_ln_b, sgu_w, sgu_b, w_branch, w_gate, b_gate, w_out, final_g, loss_target, m_norm_g, m_w_in, m_conv_w, m_conv_b, m_conv_ln_g, m_conv_ln_b, m_sgu_ln_g, m_sgu_ln_b, m_sgu_w, m_sgu_b, m_w_branch, m_w_gate, m_b_gate, m_w_out, m_final_g, v_norm_g, v_w_in, v_conv_w, v_conv_b, v_conv_ln_g, v_conv_ln_b, v_sgu_ln_g, v_sgu_ln_b, v_sgu_w, v_sgu_b, v_w_branch, v_w_gate, v_b_gate, v_w_out, v_final_g):
    L, D = norm_g.shape
    W = conv_b.shape[1]
    taps = conv_w.shape[1]
    weights = dict(norm_g=norm_g, w_in=w_in, conv_w=conv_w, conv_b=conv_b, conv_ln_g=conv_ln_g, conv_ln_b=conv_ln_b,
                   sgu_ln_g=sgu_ln_g, sgu_ln_b=sgu_ln_b, sgu_w=sgu_w, sgu_b=sgu_b, w_branch=w_branch, w_gate=w_gate,
                   b_gate=b_gate, w_out=w_out, final_g=final_g)
    mom_m = dict(norm_g=m_norm_g, w_in=m_w_in, conv_w=m_conv_w, conv_b=m_conv_b, conv_ln_g=m_conv_ln_g,
                 conv_ln_b=m_conv_ln_b, sgu_ln_g=m_sgu_ln_g, sgu_ln_b=m_sgu_ln_b, sgu_w=m_sgu_w, sgu_b=m_sgu_b,
                 w_branch=m_w_branch, w_gate=m_w_gate, b_gate=m_b_gate, w_out=m_w_out, final_g=m_final_g)
    mom_v = dict(norm_g=v_norm_g, w_in=v_w_in, conv_w=v_conv_w, conv_b=v_conv_b, conv_ln_g=v_conv_ln_g,
                 conv_ln_b=v_conv_ln_b, sgu_ln_g=v_sgu_ln_g, sgu_ln_b=v_sgu_ln_b, sgu_w=v_sgu_w, sgu_b=v_sgu_b,
                 w_branch=v_w_branch, w_gate=v_w_gate, b_gate=v_b_gate, w_out=v_w_out, final_g=v_final_g)
    core = lax.axis_index("c").astype(jnp.int32).reshape(1)
    me = 4 * lax.axis_index("x") + 2 * lax.axis_index("y") + lax.axis_index("c")

    gather_names = ("w_in", "w_gate", "w_br", "w_out", "conv_w")
    gather_axes = (1, 1, 2, 0, 1)

    def gather_start(tag, l, which, carry):
        shards = [w_in[l].astype(BF16), w_gate[l].astype(BF16), w_branch[l].astype(BF16), w_out[l].astype(BF16),
                  jnp.pad(conv_w[l], ((0, CONV_HALO - taps), (0, 0)))]
        shards = [shards[i] for i in which]
        axes = [gather_axes[i] for i in which]
        lands = []
        for s, ax in zip(shards, axes):
            full = list(s.shape)
            full[ax] *= N_DEV
            lands.append(lax.empty(tuple(full), s.dtype))
        return _exchange_start(f"gather_start_{tag}", "gather", shards, lands, axes, carry) + (which,)

    def gather_wait(tag, started, after):
        send, recv, shards, lands, _, which = started
        full = _exchange_wait(f"gather_wait_{tag}", "gather", send, recv, shards, lands,
                              [gather_axes[i] for i in which], after)
        return {gather_names[i]: f for i, f in zip(which, full)}

    xs = x[0]
    saved, gathered, smalls = [], [], []
    first = gather_start("0a", 0, (0,), xs)
    rest = gather_start("0b", 0, (1, 2, 3, 4), first[4])
    xs = rest[4]
    wl = gather_wait("0a", first, xs)
    wl["late"] = functools.partial(gather_wait, "0b", rest)
    for l in range(L):
        wl["taps"] = taps
        if l + 1 < L:
            started = gather_start(l + 1, l + 1, (0, 1, 2, 3, 4), xs)
            xs = started[4]
        sp = dict(norm_g=norm_g[l][None], conv_b=conv_b[l][None], conv_ln_g=conv_ln_g[l][None],
                  conv_ln_b=conv_ln_b[l][None], sgu_ln_g=sgu_ln_g[l][None], sgu_ln_b=sgu_ln_b[l][None],
                  sgu_w=sgu_w[l], sgu_bt=sgu_b[l].T, b_gate=b_gate[l][None])
        xs, sv = _layer_fwd(xs, wl, sp)
        saved.append(sv)
        gathered.append(wl)
        smalls.append(sp)
        if l + 1 < L:
            wl = gather_wait(l + 1, started, xs)

    dx, d_final, loss_part = _loss_head(xs, loss_target[0], final_g[None])
    loss = lax.psum(loss_part[0, 0], ("x", "y", "c"))

    big_names = ("w_in", "w_gate", "w_branch", "w_out")
    big_axes = (1, 1, 1, 0)
    me1 = me.astype(jnp.int32).reshape(1)
    outs = {}
    small_grads = []

    def scatter_start(l, g2d, carry):
        lands = []
        for g, ax in zip(g2d, big_axes):
            blk = list(g.shape)
            blk[ax] //= N_DEV
            lands.append(lax.empty((N_DEV, *blk), g.dtype))
        return _exchange_start(f"scatter_start_{l}", "scatter", g2d, lands, big_axes, carry)

    def scatter_finish(l, started, after):
        send, recv, g2d, lands, _ = started
        lands = _exchange_wait(f"scatter_wait_{l}", "scatter", send, recv, g2d, lands, big_axes, after)
        for nm, land, g, ax in zip(big_names, lands, g2d, big_axes):
            shard2d = land.shape[1:]
            outs.setdefault(nm, {})[l] = _adamw_scatter(
                "adamw_" + nm, land, g, ax, me1, weights[nm][l].reshape(shard2d), mom_m[nm][l].reshape(shard2d),
                mom_v[nm][l].reshape(shard2d))

    pending = None
    for l in reversed(range(L)):
        dx, big, small = _layer_bwd(dx, saved[l], gathered[l], smalls[l])
        small_grads.append(small)
        if pending is not None:
            scatter_finish(l + 1, pending, dx)
        g2d = [big["w_in"], big["w_gate"], big["w_br"].reshape(-1, D), big["w_out"]]
        if l > 0:
            pending = scatter_start(l, g2d, dx)
            dx = pending[4]
        else:
            pending = scatter_start(l, g2d, d_final)
            d_final = pending[4]
    small_grads.reverse()

    sg = {nm: jnp.stack([small_grads[l][nm] for l in range(L)]) for nm in SMALL[:-1]}
    sg["final_g"] = d_final[0]
    conv_w_full = jnp.stack([small_grads[l]["conv_w"] for l in range(L)])
    segs = [_rows128(sg[nm]) for nm in SMALL] + [_rows128(conv_w_full)]
    offs = [0]
    for s in segs:
        offs.append(offs[-1] + s.shape[0])
    pack = jnp.concatenate(segs, axis=0)
    (allp,) = _all_gather("gather_small_grads", [pack[None]], [0])

    def packed(src):
        return jnp.concatenate([_rows128(src[nm]) for nm in SMALL] + [jnp.zeros_like(segs[-1])], axis=0)

    s_g, s_d, s_m, s_v = _adamw_sum("adamw_small", allp, packed(weights), packed(mom_m), packed(mom_v))
    scatter_finish(0, pending, s_g)
    big_out = {nm: [jnp.stack([outs[nm][l][q] for l in range(L)]).reshape(weights[nm].shape) for q in range(4)]
               for nm in big_names}

    def unpack(buf, i, like):
        n = like.size // LANES
        return buf[offs[i]:offs[i] + n].reshape(like.shape)

    small_out = {nm: [unpack(b, i, weights[nm]) for b in (s_g, s_d, s_m, s_v)] for i, nm in enumerate(SMALL)}
    Wc = conv_w.shape[2]
    cw_sum = lax.dynamic_slice_in_dim(unpack(s_g, len(SMALL), conv_w_full), me * Wc, Wc, axis=2)
    cshape = (L * conv_w.shape[1], Wc)
    conv_out = [o.reshape(conv_w.shape) for o in _adamw_sum(
        "adamw_conv_w", cw_sum.reshape((1,) + cshape), conv_w.reshape(cshape), m_conv_w.reshape(cshape),
        v_conv_w.reshape(cshape))]

    order = ["norm_g", "w_in", "conv_w", "conv_b", "conv_ln_g", "conv_ln_b", "sgu_ln_g", "sgu_ln_b", "sgu_w", "sgu_b",
             "w_branch", "w_gate", "b_gate", "w_out", "final_g"]
    table = dict(small_out)
    table.update(big_out)
    table["conv_w"] = conv_out
    result = [loss, dx[None]]
    for q in range(4):
        result += [table[nm][q] for nm in order]
    return tuple(result)
```

```python
import functools

import jax
import jax.numpy as jnp
from jax import lax
from jax.experimental import pallas as pl
from jax.experimental.pallas import tpu as pltpu

F32 = jnp.float32
BF16 = jnp.bfloat16
LANES = 128
SUBLANES = 8
CONV_HALO = 32
SB_QUERY_BLOCK = 512
SB_QUERY_BLOCK_BWD = 512
SB_KEY_BLOCK = 1024
DIL_STEP_ROWS = 2048
DIL_UNROLL = 4
NORM_EPS = 1e-6
NEG = -1e30
N_DEV = 8
N_CHIP = 4
DIL_PATTERNS = ((128, 1), (512, 4), (2048, 16))

ADAM_LR = 0.001
ADAM_B1 = 0.9
ADAM_B2 = 0.999
ADAM_EPS = 1e-08
ADAM_WD = 0.01
ADAM_STEP = 10

MESH = pl.DeviceIdType.MESH
NN = (((1,), (0,)), ((), ()))
NT = (((1,), (1,)), ((), ()))
TN = (((0,), (0,)), ((), ()))
VMEM_LIMIT = 52 << 20


def _sds(shape, dtype):
    return jax.ShapeDtypeStruct(tuple(shape), dtype)


def _cp(*sem):
    return pltpu.CompilerParams(dimension_semantics=tuple(sem), vmem_limit_bytes=VMEM_LIMIT)


def _pick(n, target, quantum=LANES):
    if n <= target:
        return n
    t = (target // quantum) * quantum
    while t >= quantum:
        if n % t == 0:
            return t
        t -= quantum
    return n


def _sigmoid(x):
    return 1.0 / (1.0 + jnp.exp(-x))


def _silu(x):
    return x * _sigmoid(x)


def _dsilu(x):
    s = _sigmoid(x)
    return s * (1.0 + x * (1.0 - s))


_GELU_K = 0.7978845608028654
_GELU_A = 0.044715


def _gelu(x):
    return 0.5 * x * (1.0 + jnp.tanh(_GELU_K * (x + _GELU_A * x * x * x)))


def _dgelu(x):
    t = jnp.tanh(_GELU_K * (x + _GELU_A * x * x * x))
    return 0.5 * (1.0 + t) + 0.5 * x * (1.0 - t * t) * _GELU_K * (1.0 + 3.0 * _GELU_A * x * x)


def _ln_stats(v):
    mu = jnp.mean(v, axis=-1, keepdims=True)
    d = v - mu
    var = jnp.mean(d * d, axis=-1, keepdims=True)
    r = lax.rsqrt(var + NORM_EPS)
    return d * r, r


def _ln_bwd(dxh, xh, r):
    return r * (dxh - jnp.mean(dxh, axis=-1, keepdims=True) - xh * jnp.mean(dxh * xh, axis=-1, keepdims=True))


def _acc_rows(ref, row, val, first):
    @pl.when(first)
    def _():
        ref[...] = jnp.zeros_like(ref)
    ref[row:row + 1, :] += val


def _mm(name, a, b, *, grid, kaxis, dims, a_spec, b_spec, acc_shape, out_shape, out_specs,
        epilogue, extras=(), extra_specs=(), sem):
    nk = grid[kaxis]
    ne = len(extras)

    def body(*refs):
        a_ref, b_ref = refs[0], refs[1]
        ex = refs[2:2 + ne]
        outs = refs[2 + ne:-1]
        acc = refs[-1]
        k = pl.program_id(kaxis)

        @pl.when(k == 0)
        def _():
            acc[...] = jnp.zeros_like(acc)

        acc[...] += lax.dot_general(a_ref[...].astype(BF16), b_ref[...].astype(BF16), dims,
                                    preferred_element_type=F32)

        @pl.when(k == nk - 1)
        def _():
            epilogue(acc[...], ex, outs)

    return pl.pallas_call(
        body, name=name, grid=grid, in_specs=[a_spec, b_spec, *extra_specs], out_specs=out_specs,
        out_shape=out_shape, scratch_shapes=[pltpu.VMEM(acc_shape, F32)], compiler_params=_cp(*sem),
    )(a, b, *extras)


def _store(dtype):
    def ep(acc, ex, outs):
        outs[0][...] = acc.astype(dtype)
    return ep


def _mm_nn(name, a, b, out_dtype, epilogue=None, extras=(), extra_specs=(), tm=1024, tn=1024):
    M, K = a.shape
    N = b.shape[1]
    tm, tn = _pick(M, tm, SUBLANES), _pick(N, tn)
    return _mm(name, a, b, grid=(M // tm, N // tn, 1), kaxis=2, dims=NN,
               a_spec=pl.BlockSpec((tm, K), lambda i, j, k: (i, 0)),
               b_spec=pl.BlockSpec((K, tn), lambda i, j, k: (0, j)),
               acc_shape=(tm, tn), out_shape=_sds((M, N), out_dtype),
               out_specs=pl.BlockSpec((tm, tn), lambda i, j, k: (i, j)),
               epilogue=epilogue or _store(out_dtype), extras=extras, extra_specs=extra_specs,
               sem=("parallel", "parallel", "arbitrary"))


def _mm_nt(name, a, b, out_dtype, addend=None, tm=1024, tn=1024, tk=2048):
    M, K = a.shape
    N = b.shape[0]
    tm, tn, tk = _pick(M, tm, SUBLANES), _pick(N, tn), _pick(K, tk)
    extras, extra_specs = (), ()
    if addend is not None:
        extras = (addend,)
        extra_specs = (pl.BlockSpec((tm, tn), lambda i, j, k: (i, j)),)

    def ep(acc, ex, outs):
        if ex:
            acc = acc + ex[0][...]
        outs[0][...] = acc.astype(out_dtype)

    return _mm(name, a, b, grid=(M // tm, N // tn, K // tk), kaxis=2, dims=NT,
               a_spec=pl.BlockSpec((tm, tk), lambda i, j, k: (i, k)),
               b_spec=pl.BlockSpec((tn, tk), lambda i, j, k: (j, k)),
               acc_shape=(tm, tn), out_shape=_sds((M, N), out_dtype),
               out_specs=pl.BlockSpec((tm, tn), lambda i, j, k: (i, j)),
               epilogue=ep, extras=extras, extra_specs=extra_specs,
               sem=("parallel", "parallel", "arbitrary"))


def _mm_tn(name, a, b, tm=1024, tn=1024, tk=2048, out_dtype=BF16):
    K, M = a.shape
    N = b.shape[1]
    tm, tn, tk = _pick(M, tm), _pick(N, tn), _pick(K, tk, SUBLANES)
    return _mm(name, a, b, grid=(M // tm, N // tn, K // tk), kaxis=2, dims=TN,
               a_spec=pl.BlockSpec((tk, tm), lambda i, j, k: (k, i)),
               b_spec=pl.BlockSpec((tk, tn), lambda i, j, k: (k, j)),
               acc_shape=(tm, tn), out_shape=_sds((M, N), out_dtype),
               out_specs=pl.BlockSpec((tm, tn), lambda i, j, k: (i, j)),
               epilogue=_store(out_dtype), sem=("parallel", "parallel", "arbitrary"))


def _rmsnorm_fwd(x, g_row):
    T, D = x.shape
    tm = _pick(T, 512, SUBLANES)

    def body(x_ref, g_ref, h_ref):
        xv = x_ref[...]
        r = lax.rsqrt(jnp.mean(xv * xv, axis=-1, keepdims=True) + NORM_EPS)
        h_ref[...] = (xv * r * g_ref[...]).astype(BF16)

    row = pl.BlockSpec((tm, D), lambda i: (i, 0))
    return pl.pallas_call(body, name="rmsnorm_fwd", grid=(T // tm,),
                          in_specs=[row, pl.BlockSpec((1, D), lambda i: (0, 0))], out_specs=row,
                          out_shape=_sds((T, D), BF16), compiler_params=_cp("parallel"))(x, g_row)


def _rmsnorm_bwd(x, dh, dout, g_row):
    T, D = x.shape
    tm = _pick(T, 256, SUBLANES)

    def body(x_ref, dh_ref, do_ref, g_ref, dx_ref, dg_ref):
        xv = x_ref[...]
        r = lax.rsqrt(jnp.mean(xv * xv, axis=-1, keepdims=True) + NORM_EPS)
        xh = xv * r
        dhv = dh_ref[...]
        dxh = dhv * g_ref[...]
        dx_ref[...] = do_ref[...] + r * (dxh - xh * jnp.mean(dxh * xh, axis=-1, keepdims=True))
        _acc_rows(dg_ref, 0, jnp.sum(dhv * xh, axis=0, keepdims=True), pl.program_id(0) == 0)

    row = pl.BlockSpec((tm, D), lambda i: (i, 0))
    return pl.pallas_call(body, name="rmsnorm_bwd", grid=(T // tm,),
                          in_specs=[row, row, row, pl.BlockSpec((1, D), lambda i: (0, 0))],
                          out_specs=[row, pl.BlockSpec((SUBLANES, D), lambda i: (0, 0))],
                          out_shape=[_sds((T, D), F32), _sds((SUBLANES, D), F32)],
                          compiler_params=_cp("arbitrary"))(x, dh, dout, g_row)


def _loss_head(x, target, g_row):
    T, D = x.shape
    tm = _pick(T, 256, SUBLANES)

    def body(x_ref, t_ref, g_ref, dx_ref, dg_ref, loss_ref):
        first = pl.program_id(0) == 0
        xv = x_ref[...]
        g = g_ref[...]
        r = lax.rsqrt(jnp.mean(xv * xv, axis=-1, keepdims=True) + NORM_EPS)
        xh = xv * r
        err = xh * g - t_ref[...]
        part = 0.5 * jnp.sum(jnp.mean(err * err, axis=-1, keepdims=True), axis=0, keepdims=True)

        @pl.when(first)
        def _():
            loss_ref[...] = jnp.zeros_like(loss_ref)

        loss_ref[...] += jnp.broadcast_to(part, loss_ref.shape)
        dy = err / D
        _acc_rows(dg_ref, 0, jnp.sum(dy * xh, axis=0, keepdims=True), first)
        dxh = dy * g
        dx_ref[...] = r * (dxh - xh * jnp.mean(dxh * xh, axis=-1, keepdims=True))

    row = pl.BlockSpec((tm, D), lambda i: (i, 0))
    return pl.pallas_call(body, name="loss_head", grid=(T // tm,),
                          in_specs=[row, row, pl.BlockSpec((1, D), lambda i: (0, 0))],
                          out_specs=[row, pl.BlockSpec((SUBLANES, D), lambda i: (0, 0)),
                                     pl.BlockSpec((SUBLANES, LANES), lambda i: (0, 0))],
                          out_shape=[_sds((T, D), F32), _sds((SUBLANES, D), F32), _sds((SUBLANES, LANES), F32)],
                          compiler_params=_cp("arbitrary"))(x, target, g_row)


MXU_DIM = 256


def _tri(cmp):
    r = lax.broadcasted_iota(jnp.int32, (MXU_DIM, MXU_DIM), 0)
    c = lax.broadcasted_iota(jnp.int32, (MXU_DIM, MXU_DIM), 1)
    same_tile = (r // LANES) == (c // LANES)
    return (same_tile & cmp(r % LANES, c % LANES)).astype(BF16)


def _tile_scans(x, tri):
    out = []
    for p in range(x.shape[1] // MXU_DIM):
        sc = jnp.dot(x[:, p * MXU_DIM:(p + 1) * MXU_DIM].astype(BF16), tri, preferred_element_type=F32)
        out += [sc[:, :LANES], sc[:, LANES:]]
    return out


def _sb_scores(qs, kb, causal):
    z = lax.dot_general(qs, kb, NT, preferred_element_type=F32)
    lb = jnp.minimum(z, 0.0) - jnp.log(1.0 + jnp.exp(-jnp.abs(z)))
    l1 = lb - z
    return lb, (l1 if causal is None else jnp.where(causal, l1, 0.0))


def _masked(causal, x):
    return x if causal is None else jnp.where(causal, x, 0.0)


def _lanes(x, s):
    return x[:, s * LANES:(s + 1) * LANES]


def _sb_fwd(proj, W, ys):
    T = proj.shape[0]
    H = W // LANES
    assert T // LANES <= LANES
    tq = _pick(T, SB_QUERY_BLOCK, LANES)
    kblk = _pick(T, SB_KEY_BLOCK, LANES)
    assert kblk % tq == 0
    nsub = kblk // LANES
    scale = LANES ** -0.5

    def body(q_ref, k_ref, v_ref, g_ref, ys_in, o_ref, y_ref, c_ref, run):
        i = pl.program_id(1)
        qs = (q_ref[...] * scale).astype(BF16)
        row = i * tq + lax.broadcasted_iota(jnp.int32, (tq, kblk), 0)
        key = lax.broadcasted_iota(jnp.int32, (tq, kblk), 1)
        col = lax.broadcasted_iota(jnp.int32, (tq, LANES), 1)
        tri = _tri(lambda r, c: r > c)
        nkb = ((i + 1) * tq + kblk - 1) // kblk
        o_ref[...] = jnp.zeros_like(o_ref)
        c_ref[...] = jnp.zeros_like(c_ref)
        run[...] = jnp.zeros_like(run)

        def make_step(diagonal):
            def step(jj, carry):
                j = nkb - 1 - jj
                off = pl.multiple_of(j * kblk, kblk)
                kb = k_ref[pl.ds(off, kblk), :].astype(BF16)
                vb = v_ref[pl.ds(off, kblk), :].astype(BF16)
                causal = (key + off < row) if diagonal else None
                lb, l1m = _sb_scores(qs, kb, causal)
                c_after = run[...]
                cs = c_ref[...]
                after = _tile_scans(l1m, tri)
                for s in reversed(range(nsub)):
                    after[s] = c_after + after[s]
                    cs = jnp.where(col == j * nsub + s, c_after, cs)
                    c_after = c_after + jnp.sum(_lanes(l1m, s), axis=1, keepdims=True)
                run[...] = c_after
                c_ref[...] = cs
                w = _masked(causal, jnp.exp(lb + jnp.concatenate(after, axis=1)))
                o_ref[...] += jnp.dot(w.astype(BF16), vb, preferred_element_type=F32)
                return carry
            return step

        make_step(True)(0, 0)
        lax.fori_loop(1, nkb, make_step(False), 0)
        y_ref[...] = (o_ref[...] * _silu(g_ref[...])).astype(BF16)

    qspec = lambda c0: pl.BlockSpec((tq, LANES), lambda h, i: (i, c0 + h))
    kvspec = lambda c0: pl.BlockSpec((T, LANES), lambda h, i: (0, c0 + h))
    out = pl.BlockSpec((tq, LANES), lambda h, i: (i, h))
    return pl.pallas_call(body, name="sb_fwd", grid=(H, T // tq),
                          in_specs=[qspec(0), kvspec(H), kvspec(2 * H), qspec(3 * H), HBM],
                          out_specs=[out, pl.BlockSpec((None, tq, LANES), lambda h, i: (0, i, h)), out],
                          out_shape=[_sds((T, W), F32), _sds(ys.shape, BF16), _sds((T, W), F32)],
                          input_output_aliases={4: 1},
                          scratch_shapes=[pltpu.VMEM((tq, LANES), F32)],
                          compiler_params=_cp("parallel", "arbitrary"))(proj, proj, proj, proj, ys)


def _sb_bwd(proj, o, carries, dy, W):
    T = proj.shape[0]
    H = W // LANES
    tq = _pick(T, SB_QUERY_BLOCK_BWD, LANES)
    kblk = _pick(T, SB_KEY_BLOCK, LANES)
    assert kblk % tq == 0
    nsub = kblk // LANES
    scale = LANES ** -0.5

    def body(q_ref, k_ref, v_ref, g_ref, o_ref, c_ref, dy_ref, dq_ref, dk_ref, dv_ref, dg_ref, run):
        i = pl.program_id(1)

        @pl.when(i == 0)
        def _():
            dk_ref[...] = jnp.zeros_like(dk_ref)
            dv_ref[...] = jnp.zeros_like(dv_ref)

        gv = g_ref[...]
        dyv = dy_ref[...]
        dg_ref[...] = (dyv * o_ref[...] * _dsilu(gv)).astype(BF16)
        dob = (dyv * _silu(gv)).astype(BF16)
        qs = (q_ref[...] * scale).astype(BF16)
        row = i * tq + lax.broadcasted_iota(jnp.int32, (tq, kblk), 0)
        key = lax.broadcasted_iota(jnp.int32, (tq, kblk), 1)
        col = lax.broadcasted_iota(jnp.int32, (tq, LANES), 1)
        tri_after = _tri(lambda r, c: r > c)
        tri_before = _tri(lambda r, c: r < c)
        dq_ref[...] = jnp.zeros_like(dq_ref)
        run[...] = jnp.zeros_like(run)

        def make_step(diagonal):
            def step(j, carry):
                off = pl.multiple_of(j * kblk, kblk)
                kb = k_ref[pl.ds(off, kblk), :].astype(BF16)
                vb = v_ref[pl.ds(off, kblk), :].astype(BF16)
                causal = (key + off < row) if diagonal else None
                lb, l1m = _sb_scores(qs, kb, causal)
                cs = c_ref[...]
                after = [jnp.sum(jnp.where(col == j * nsub + s, cs, 0.0), axis=1, keepdims=True) + sc
                         for s, sc in enumerate(_tile_scans(l1m, tri_after))]
                w = _masked(causal, jnp.exp(lb + jnp.concatenate(after, axis=1)))
                gw = w * lax.dot_general(dob, vb, NT, preferred_element_type=F32)
                gpre = run[...]
                before = _tile_scans(gw, tri_before)
                for s in range(nsub):
                    before[s] = gpre + before[s]
                    gpre = gpre + jnp.sum(_lanes(gw, s), axis=1, keepdims=True)
                run[...] = gpre
                beta = jnp.exp(lb)
                dz = _masked(causal, gw * (1.0 - beta) - jnp.concatenate(before, axis=1) * beta).astype(BF16)
                dq_ref[...] += jnp.dot(dz, kb, preferred_element_type=F32)
                dk_ref[pl.ds(off, kblk), :] += lax.dot_general(dz, qs, TN, preferred_element_type=F32)
                dv_ref[pl.ds(off, kblk), :] += lax.dot_general(w.astype(BF16), dob, TN, preferred_element_type=F32)
                return carry
            return step

        last = ((i + 1) * tq + kblk - 1) // kblk - 1
        lax.fori_loop(0, last, make_step(False), 0)
        make_step(True)(last, 0)
        dq_ref[...] = dq_ref[...] * scale

    qspec = lambda c0: pl.BlockSpec((tq, LANES), lambda h, i: (i, c0 + h))
    kvspec = lambda c0: pl.BlockSpec((T, LANES), lambda h, i: (0, c0 + h), pipeline_mode=pl.Buffered(1))
    blk = pl.BlockSpec((tq, LANES), lambda h, i: (i, h))
    full = pl.BlockSpec((T, LANES), lambda h, i: (0, h))
    return pl.pallas_call(body, name="sb_bwd", grid=(H, T // tq),
                          in_specs=[qspec(0), kvspec(H), kvspec(2 * H), qspec(3 * H), blk, blk,
                                    pl.BlockSpec((None, tq, LANES), lambda h, i: (0, i, h))],
                          out_specs=[blk, full, full, blk],
                          out_shape=[_sds((T, W), F32), _sds((T, W), F32), _sds((T, W), F32), _sds((T, W), BF16)],
                          scratch_shapes=[pltpu.VMEM((tq, LANES), F32)],
                          compiler_params=_cp("parallel", "arbitrary"))(proj, proj, proj, proj, o, carries, dy)


def _conv_specs(T, W, tm, col_a):
    per = tm // CONV_HALO
    cur = lambda c: pl.BlockSpec((tm, W), lambda i: (i, c))
    prev = lambda c: pl.BlockSpec((CONV_HALO, W), lambda i: (jnp.maximum(i * per - 1, 0), c))
    return cur, prev


def _conv_fwd(proj, conv_w, K, conv_b, ln_g, ln_b, W, ys):
    T = proj.shape[0]
    tm = _pick(T, 256, CONV_HALO)
    lead = CONV_HALO - (K - 1)
    cur, prev = _conv_specs(T, W, tm, 4)

    def body(a_ref, b_ref, ah_ref, bh_ref, g_ref, w_ref, cb_ref, lg_ref, lb_ref, ys_in, c_ref, y_ref, glu):
        i = pl.program_id(0)
        glu[0:CONV_HALO, :] = jnp.where(i > 0, ah_ref[...] * _sigmoid(bh_ref[...]), 0.0)
        glu[CONV_HALO:, :] = a_ref[...] * _sigmoid(b_ref[...])
        c = jnp.broadcast_to(cb_ref[...], (tm, W))
        for k in range(K):
            c = c + w_ref[k:k + 1, :] * glu[lead + k:lead + k + tm, :]
        c_ref[...] = c
        xh, _ = _ln_stats(c)
        y_ref[...] = (_silu(xh * lg_ref[...] + lb_ref[...]) * _silu(g_ref[...])).astype(BF16)

    vec = pl.BlockSpec((1, W), lambda i: (0, 0))
    row = pl.BlockSpec((tm, W), lambda i: (i, 0))
    return pl.pallas_call(body, name="conv_fwd", grid=(T // tm,),
                          in_specs=[cur(4), cur(5), prev(4), prev(5), cur(6),
                                    pl.BlockSpec((CONV_HALO, W), lambda i: (0, 0)), vec, vec, vec, HBM],
                          out_specs=[row, pl.BlockSpec((None, tm, W), lambda i: (1, i, 0))],
                          out_shape=[_sds((T, W), F32), _sds(ys.shape, BF16)], input_output_aliases={9: 1},
                          scratch_shapes=[pltpu.VMEM((tm + CONV_HALO, W), F32)],
                          compiler_params=_cp("parallel"))(proj, proj, proj, proj, proj, conv_w, conv_b, ln_g, ln_b, ys)


def _conv_bwd_ln(proj, c, dy, ln_g, ln_b, W):
    T = proj.shape[0]
    tm = _pick(T, 256, SUBLANES)

    def body(g_ref, c_ref, dy_ref, lg_ref, lb_ref, dc_ref, dg_ref, st_ref):
        first = pl.program_id(0) == 0
        gv = g_ref[...]
        dyv = dy_ref[...]
        xh, r = _ln_stats(c_ref[...])
        lg = lg_ref[...]
        ln = xh * lg + lb_ref[...]
        dg_ref[...] = (dyv * _silu(ln) * _dsilu(gv)).astype(BF16)
        dln = dyv * _silu(gv) * _dsilu(ln)
        dc = _ln_bwd(dln * lg, xh, r)
        dc_ref[...] = dc
        _acc_rows(st_ref, 0, jnp.sum(dln * xh, axis=0, keepdims=True), first)
        st_ref[1:2, :] += jnp.sum(dln, axis=0, keepdims=True)
        st_ref[2:3, :] += jnp.sum(dc, axis=0, keepdims=True)

    row = pl.BlockSpec((tm, W), lambda i: (i, 0))
    vec = pl.BlockSpec((1, W), lambda i: (0, 0))
    return pl.pallas_call(body, name="conv_bwd_ln", grid=(T // tm,),
                          in_specs=[pl.BlockSpec((tm, W), lambda i: (i, 6)), row,
                                    pl.BlockSpec((None, tm, W), lambda i: (1, i, 0)), vec, vec],
                          out_specs=[row, row, pl.BlockSpec((SUBLANES, W), lambda i: (0, 0))],
                          out_shape=[_sds((T, W), F32), _sds((T, W), BF16), _sds((SUBLANES, W), F32)],
                          compiler_params=_cp("arbitrary"))(proj, c, dy, ln_g, ln_b)


def _conv_bwd_taps(proj, dc, conv_w, K, W):
    T = proj.shape[0]
    tm = _pick(T, 256, CONV_HALO)
    lead = CONV_HALO - (K - 1)
    per = tm // CONV_HALO
    nblk = T // tm
    cur, prev = _conv_specs(T, W, tm, 4)

    def body(a_ref, b_ref, ah_ref, bh_ref, dc_ref, dcn_ref, w_ref, da_ref, db_ref, dw_ref, glu, dcs):
        i = pl.program_id(0)
        av = a_ref[...]
        sb = _sigmoid(b_ref[...])
        glu[0:CONV_HALO, :] = jnp.where(i > 0, ah_ref[...] * _sigmoid(bh_ref[...]), 0.0)
        glu[CONV_HALO:, :] = av * sb
        dcv = dc_ref[...]
        dcs[0:tm, :] = dcv
        dcs[tm:, :] = jnp.where(i < nblk - 1, dcn_ref[...], 0.0)

        @pl.when(i == 0)
        def _():
            dw_ref[...] = jnp.zeros_like(dw_ref)

        dglu = jnp.zeros((tm, W), F32)
        for k in range(K):
            dglu = dglu + w_ref[k:k + 1, :] * dcs[K - 1 - k:K - 1 - k + tm, :]
            dw_ref[k:k + 1, :] += jnp.sum(dcv * glu[lead + k:lead + k + tm, :], axis=0, keepdims=True)
        da_ref[...] = (dglu * sb).astype(BF16)
        db_ref[...] = (dglu * av * sb * (1.0 - sb)).astype(BF16)

    row = pl.BlockSpec((tm, W), lambda i: (i, 0))
    nxt = pl.BlockSpec((CONV_HALO, W), lambda i: (jnp.minimum((i + 1) * per, T // CONV_HALO - 1), 0))
    return pl.pallas_call(body, name="conv_bwd_taps", grid=(nblk,),
                          in_specs=[cur(4), cur(5), prev(4), prev(5), row, nxt,
                                    pl.BlockSpec((CONV_HALO, W), lambda i: (0, 0))],
                          out_specs=[row, row, pl.BlockSpec((CONV_HALO, W), lambda i: (0, 0))],
                          out_shape=[_sds((T, W), BF16), _sds((T, W), BF16), _sds((CONV_HALO, W), F32)],
                          scratch_shapes=[pltpu.VMEM((tm + CONV_HALO, W), F32), pltpu.VMEM((tm + CONV_HALO, W), F32)],
                          compiler_params=_cp("arbitrary"))(proj, proj, proj, proj, dc, dc, conv_w)


def _sgu_common(cu, cv, lg, lb, w_ref, bt_ref, z_scr, G, nch):
    u = _gelu(cu)
    xh, r = _ln_stats(_gelu(cv))
    vn = (xh * lg + lb).astype(BF16)
    rr = lax.broadcasted_iota(jnp.int32, (LANES, LANES), 0)
    cc = lax.broadcasted_iota(jnp.int32, (LANES, LANES), 1)
    tril = rr >= cc
    wts = [jnp.where(tril, w_ref[g], 0.0).astype(BF16) for g in range(G)]
    for ch in range(nch):
        rs = slice(ch * LANES, (ch + 1) * LANES)
        for g in range(G):
            cs = slice(g * LANES, (g + 1) * LANES)
            z_scr[rs, cs] = jnp.dot(wts[g], vn[rs, cs], preferred_element_type=F32) + bt_ref[:, g:g + 1]
    return u, xh, r, vn, wts, tril


def _sgu_fwd(proj, sgu_w, sgu_bt, ln_g, ln_b, W, ys):
    T = proj.shape[0]
    G = W // LANES
    tm = _pick(T, 256, LANES)
    nch = tm // LANES

    def body(u_ref, v_ref, g_ref, w_ref, bt_ref, lg_ref, lb_ref, ys_in, y_ref, z_scr):
        u, *_ = _sgu_common(u_ref[...], v_ref[...], lg_ref[...], lb_ref[...], w_ref, bt_ref, z_scr, G, nch)
        y_ref[...] = (u * z_scr[...] * _silu(g_ref[...])).astype(BF16)

    cur = lambda c: pl.BlockSpec((tm, W), lambda i: (i, c))
    vec = pl.BlockSpec((1, W), lambda i: (0, 0))
    return pl.pallas_call(body, name="sgu_fwd", grid=(T // tm,),
                          in_specs=[cur(7), cur(8), cur(9), pl.BlockSpec((G, LANES, LANES), lambda i: (0, 0, 0)),
                                    pl.BlockSpec((LANES, G), lambda i: (0, 0)), vec, vec, HBM],
                          out_specs=pl.BlockSpec((None, tm, W), lambda i: (2, i, 0)), out_shape=_sds(ys.shape, BF16),
                          input_output_aliases={7: 0}, scratch_shapes=[pltpu.VMEM((tm, W), F32)],
                          compiler_params=_cp("parallel"))(proj, proj, proj, sgu_w, sgu_bt, ln_g, ln_b, ys)


def _sgu_bwd(proj, dy, sgu_w, sgu_bt, ln_g, ln_b, W):
    T = proj.shape[0]
    G = W // LANES
    tm = _pick(T, 256, LANES)
    nch = tm // LANES

    def body(u_ref, v_ref, g_ref, dy_ref, w_ref, bt_ref, lg_ref, lb_ref,
             du_ref, dv_ref, dg_ref, dw_ref, dbt_ref, st_ref, z_scr, dvn_scr):
        first = pl.program_id(0) == 0
        cu, cv, gv, dyv = u_ref[...], v_ref[...], g_ref[...], dy_ref[...]
        lg = lg_ref[...]
        u, xh, r, vn, wts, tril = _sgu_common(cu, cv, lg, lb_ref[...], w_ref, bt_ref, z_scr, G, nch)
        z = z_scr[...]
        sg = _silu(gv)
        dg_ref[...] = (dyv * u * z * _dsilu(gv)).astype(BF16)
        du_ref[...] = (dyv * z * sg * _dgelu(cu)).astype(BF16)
        dz = dyv * u * sg
        dzb = dz.astype(BF16)

        @pl.when(first)
        def _():
            dw_ref[...] = jnp.zeros_like(dw_ref)
            dbt_ref[...] = jnp.zeros_like(dbt_ref)

        for g in range(G):
            cs = slice(g * LANES, (g + 1) * LANES)
            dwg = jnp.zeros((LANES, LANES), F32)
            dbg = jnp.zeros((LANES, 1), F32)
            for ch in range(nch):
                rs = slice(ch * LANES, (ch + 1) * LANES)
                dwg = dwg + lax.dot_general(dzb[rs, cs], vn[rs, cs], NT, preferred_element_type=F32)
                dbg = dbg + jnp.sum(dz[rs, cs], axis=1, keepdims=True)
                dvn_scr[rs, cs] = lax.dot_general(wts[g], dzb[rs, cs], TN, preferred_element_type=F32)
            dw_ref[g] += jnp.where(tril, dwg, 0.0)
            dbt_ref[:, g:g + 1] += dbg
        dvn = dvn_scr[...]
        _acc_rows(st_ref, 0, jnp.sum(dvn * xh, axis=0, keepdims=True), first)
        st_ref[1:2, :] += jnp.sum(dvn, axis=0, keepdims=True)
        dv_ref[...] = (_ln_bwd(dvn * lg, xh, r) * _dgelu(cv)).astype(BF16)

    cur = lambda c: pl.BlockSpec((tm, W), lambda i: (i, c))
    row = pl.BlockSpec((tm, W), lambda i: (i, 0))
    vec = pl.BlockSpec((1, W), lambda i: (0, 0))
    wspec = pl.BlockSpec((G, LANES, LANES), lambda i: (0, 0, 0))
    bspec = pl.BlockSpec((LANES, G), lambda i: (0, 0))
    return pl.pallas_call(body, name="sgu_bwd", grid=(T // tm,),
                          in_specs=[cur(7), cur(8), cur(9), pl.BlockSpec((None, tm, W), lambda i: (2, i, 0)),
                                    wspec, bspec, vec, vec],
                          out_specs=[row, row, row, wspec, bspec, pl.BlockSpec((SUBLANES, W), lambda i: (0, 0))],
                          out_shape=[_sds((T, W), BF16)] * 3 + [_sds((G, LANES, LANES), F32), _sds((LANES, G), F32),
                                                                 _sds((SUBLANES, W), F32)],
                          scratch_shapes=[pltpu.VMEM((tm, W), F32), pltpu.VMEM((tm, W), F32)],
                          compiler_params=_cp("arbitrary"))(proj, proj, proj, dy, sgu_w, sgu_bt, ln_g, ln_b)


def _rows(start, dil):
    return pl.ds(start, LANES, stride=dil) if dil > 1 else pl.ds(start, LANES)


def _dil_masks():
    a = lax.broadcasted_iota(jnp.int32, (LANES, LANES), 0)
    c = lax.broadcasted_iota(jnp.int32, (LANES, LANES), 1)
    return c <= a, c >= a


def _dil_geometry(T, dil):
    sbr = LANES * dil
    nb = max(1, min(DIL_STEP_ROWS, T) // sbr)
    return sbr, nb, T // (sbr * nb)


def _for_units(nb, dil, unit):
    for blk in range(nb):
        if dil <= DIL_UNROLL:
            for r in range(dil):
                unit(blk, r)
        else:
            def chunk(it, carry):
                for u in range(DIL_UNROLL):
                    unit(blk, it * DIL_UNROLL + u)
                return carry
            lax.fori_loop(0, dil // DIL_UNROLL, chunk, 0)


def _dil_fwd_group(proj, W, gi, dil):
    T = proj.shape[0]
    H = W // LANES
    sbr, nb, nsteps = _dil_geometry(T, dil)
    scale = LANES ** -0.5
    cq, ck, cv = (10 + gi) * H, (13 + gi) * H, 16 * H

    def body(q_ref, kc_ref, kp_ref, vc_ref, vp_ref, o_ref, l_ref):
        b = pl.program_id(1)
        m_cur, m_prev = _dil_masks()
        m_first = m_prev & (b > 0)

        def unit(blk, r):
            sl = _rows(blk * sbr + r, dil)
            if blk == 0:
                kp, vp, mp = kp_ref[_rows(r, dil), :], vp_ref[_rows(r, dil), :], m_first
            else:
                sp_ = _rows((blk - 1) * sbr + r, dil)
                kp, vp, mp = kc_ref[sp_, :], vc_ref[sp_, :], m_prev
            q = (q_ref[sl, :] * scale).astype(BF16)
            sc = lax.dot_general(q, kc_ref[sl, :].astype(BF16), NT, preferred_element_type=F32)
            sp = lax.dot_general(q, kp.astype(BF16), NT, preferred_element_type=F32)
            sc = jnp.where(m_cur, sc, NEG)
            sp = jnp.where(mp, sp, NEG)
            m = jnp.maximum(jnp.max(sc, axis=1, keepdims=True), jnp.max(sp, axis=1, keepdims=True))
            pc = jnp.exp(sc - m)
            pp = jnp.exp(sp - m)
            den = jnp.sum(pc, axis=1, keepdims=True) + jnp.sum(pp, axis=1, keepdims=True)
            pv = (jnp.dot(pc.astype(BF16), vc_ref[sl, :].astype(BF16), preferred_element_type=F32)
                  + jnp.dot(pp.astype(BF16), vp.astype(BF16), preferred_element_type=F32))
            o_ref[sl, :] = pv / den
            l_ref[sl, :] = jnp.broadcast_to(m + jnp.log(den), (LANES, LANES))

        _for_units(nb, dil, unit)

    cur = lambda c0: pl.BlockSpec((sbr * nb, LANES), lambda h, b: (b, c0 + h))
    prv = lambda c0: pl.BlockSpec((sbr, LANES), lambda h, b: (jnp.maximum(b * nb - 1, 0), c0 + h))
    out = pl.BlockSpec((sbr * nb, LANES), lambda h, b: (b, h))
    return pl.pallas_call(body, name=f"dil_fwd_g{gi}", grid=(H, nsteps),
                          in_specs=[cur(cq), cur(ck), prv(ck), cur(cv), prv(cv)], out_specs=[out, out],
                          out_shape=[_sds((T, W), F32), _sds((T, W), F32)],
                          compiler_params=_cp("parallel", "parallel"))(proj, proj, proj, proj, proj)


def _dil_combine(proj, os_, ls_, W, ys):
    T = proj.shape[0]
    tm = _pick(T, 256, SUBLANES)

    def body(g_ref, o0, o1, o2, l0, l1, l2, ys_in, od_ref, lse_ref, y_ref):
        a0, a1, a2 = l0[...], l1[...], l2[...]
        m = jnp.maximum(jnp.maximum(a0, a1), a2)
        e0, e1, e2 = jnp.exp(a0 - m), jnp.exp(a1 - m), jnp.exp(a2 - m)
        s = e0 + e1 + e2
        od = (e0 / s) * o0[...] + (e1 / s) * o1[...] + (e2 / s) * o2[...]
        od_ref[...] = od
        lse_ref[...] = m + jnp.log(s)
        y_ref[...] = (od * _silu(g_ref[...])).astype(BF16)

    row = pl.BlockSpec((tm, W), lambda i: (i, 0))
    return pl.pallas_call(body, name="dil_combine", grid=(T // tm,),
                          in_specs=[pl.BlockSpec((tm, W), lambda i: (i, 17))] + [row] * 6 + [HBM],
                          out_specs=[row, row, pl.BlockSpec((None, tm, W), lambda i: (3, i, 0))],
                          out_shape=[_sds((T, W), F32), _sds((T, W), F32), _sds(ys.shape, BF16)],
                          input_output_aliases={7: 2},
                          compiler_params=_cp("parallel"))(proj, *os_, *ls_, ys)


def _dil_bwd_pre(proj, od, dy, W):
    T = proj.shape[0]
    H = W // LANES
    tm = _pick(T, 512, SUBLANES)

    def body(g_ref, od_ref, dy_ref, do_ref, dl_ref, dg_ref):
        gv, odv, dyv = g_ref[...], od_ref[...], dy_ref[...]
        do = dyv * _silu(gv)
        do_ref[...] = do
        dl_ref[...] = jnp.broadcast_to(jnp.sum(do * odv, axis=1, keepdims=True), (tm, LANES))
        dg_ref[...] = (dyv * odv * _dsilu(gv)).astype(BF16)

    blk = pl.BlockSpec((tm, LANES), lambda i, h: (i, h))
    return pl.pallas_call(body, name="dil_bwd_pre", grid=(T // tm, H),
                          in_specs=[pl.BlockSpec((tm, LANES), lambda i, h: (i, 17 * H + h)), blk,
                                    pl.BlockSpec((None, tm, LANES), lambda i, h: (3, i, h))],
                          out_specs=[blk, blk, blk],
                          out_shape=[_sds((T, W), F32), _sds((T, W), F32), _sds((T, W), BF16)],
                          compiler_params=_cp("parallel", "parallel"))(proj, od, dy)


def _dil_bwd_group(proj, do, lse, delta, W, gi, dil):
    T = proj.shape[0]
    H = W // LANES
    sbr, nb, nsteps = _dil_geometry(T, dil)
    scale = LANES ** -0.5
    cq, ck, cv = (10 + gi) * H, (13 + gi) * H, 16 * H

    def body(qc_ref, qn_ref, kc_ref, kp_ref, vc_ref, vp_ref, doc_ref, don_ref, lc_ref, ln_ref, dc_ref, dn_ref,
             dq_ref, dk_ref, dv_ref):
        b = pl.program_id(1)
        m_cur, m_prev = _dil_masks()
        m_first = m_prev & (b > 0)
        m_last = m_prev & (b < nsteps - 1)

        def probs(q, k, mask, l):
            s = lax.dot_general(q, k, NT, preferred_element_type=F32)
            return jnp.exp(jnp.where(mask, s - l, NEG))

        def unit(blk, r):
            sl = _rows(blk * sbr + r, dil)
            if blk == 0:
                edge = _rows(r, dil)
                k_p, v_p, m_cp = kp_ref[edge, :], vp_ref[edge, :], m_first
            else:
                sp_ = _rows((blk - 1) * sbr + r, dil)
                k_p, v_p, m_cp = kc_ref[sp_, :], vc_ref[sp_, :], m_prev
            if blk == nb - 1:
                edge = _rows(r, dil)
                q_n, do_n, l_n, d_n, m_nc = qn_ref[edge, :], don_ref[edge, :], ln_ref[edge, :], dn_ref[edge, :], m_last
            else:
                sn_ = _rows((blk + 1) * sbr + r, dil)
                q_n, do_n, l_n, d_n, m_nc = qc_ref[sn_, :], doc_ref[sn_, :], lc_ref[sn_, :], dc_ref[sn_, :], m_prev
            q_c = (qc_ref[sl, :] * scale).astype(BF16)
            q_n = (q_n * scale).astype(BF16)
            k_c, k_p = kc_ref[sl, :].astype(BF16), k_p.astype(BF16)
            v_c, v_p = vc_ref[sl, :].astype(BF16), v_p.astype(BF16)
            do_c, do_n = doc_ref[sl, :].astype(BF16), do_n.astype(BF16)
            l_c, d_c = lc_ref[sl, :], dc_ref[sl, :]
            p_cc = probs(q_c, k_c, m_cur, l_c)
            p_cp = probs(q_c, k_p, m_cp, l_c)
            p_nc = probs(q_n, k_c, m_nc, l_n)
            ds_cc = (p_cc * (lax.dot_general(do_c, v_c, NT, preferred_element_type=F32) - d_c)).astype(BF16)
            ds_cp = (p_cp * (lax.dot_general(do_c, v_p, NT, preferred_element_type=F32) - d_c)).astype(BF16)
            ds_nc = (p_nc * (lax.dot_general(do_n, v_c, NT, preferred_element_type=F32) - d_n)).astype(BF16)
            dq = (jnp.dot(ds_cc, k_c, preferred_element_type=F32) + jnp.dot(ds_cp, k_p, preferred_element_type=F32))
            dk = (lax.dot_general(ds_cc, q_c, TN, preferred_element_type=F32)
                  + lax.dot_general(ds_nc, q_n, TN, preferred_element_type=F32))
            dv = (lax.dot_general(p_cc.astype(BF16), do_c, TN, preferred_element_type=F32)
                  + lax.dot_general(p_nc.astype(BF16), do_n, TN, preferred_element_type=F32))
            dq_ref[sl, :] = dq * scale
            dk_ref[sl, :] = dk
            dv_ref[sl, :] = dv

        _for_units(nb, dil, unit)

    cur = lambda c0: pl.BlockSpec((sbr * nb, LANES), lambda h, b: (b, c0 + h))
    prv = lambda c0: pl.BlockSpec((sbr, LANES), lambda h, b: (jnp.maximum(b * nb - 1, 0), c0 + h))
    nxt = lambda c0: pl.BlockSpec((sbr, LANES), lambda h, b: (jnp.minimum((b + 1) * nb, T // sbr - 1), c0 + h))
    out = pl.BlockSpec((sbr * nb, LANES), lambda h, b: (b, h))
    return pl.pallas_call(body, name=f"dil_bwd_g{gi}", grid=(H, nsteps),
                          in_specs=[cur(cq), nxt(cq), cur(ck), prv(ck), cur(cv), prv(cv),
                                    cur(0), nxt(0), cur(0), nxt(0), cur(0), nxt(0)],
                          out_specs=[out, out, out], out_shape=[_sds((T, W), F32)] * 3,
                          compiler_params=_cp("parallel", "parallel"))(
        proj, proj, proj, proj, proj, proj, do, do, lse, lse, delta, delta)


def _sum3_bf16(a, b, c):
    T, W = a.shape
    tm = _pick(T, 512, SUBLANES)

    def body(a_ref, b_ref, c_ref, o_ref):
        o_ref[...] = (a_ref[...] + b_ref[...] + c_ref[...]).astype(BF16)

    row = pl.BlockSpec((tm, W), lambda i: (i, 0))
    return pl.pallas_call(body, name="dil_dv_sum", grid=(T // tm,), in_specs=[row, row, row], out_specs=row,
                          out_shape=_sds((T, W), BF16), compiler_params=_cp("parallel"))(a, b, c)


def _to_bf16(name, parts):
    T, W = parts[0].shape
    n = len(parts)
    tm = _pick(T, 512, SUBLANES)

    def body(*refs):
        o_ref = refs[n]
        for p in range(n):
            o_ref[:, p * W:(p + 1) * W] = refs[p][...].astype(BF16)

    row = pl.BlockSpec((tm, W), lambda i: (i, 0))
    return pl.pallas_call(body, name=name, grid=(T // tm,), in_specs=[row] * n,
                          out_specs=pl.BlockSpec((tm, n * W), lambda i: (i, 0)),
                          out_shape=_sds((T, n * W), BF16), compiler_params=_cp("parallel"))(*parts)


def _branch_merge(ys, w_br, gates):
    NB, T, W = ys.shape
    D = w_br.shape[2]
    tm, tn = _pick(T, 1024, SUBLANES), _pick(D, 1024)
    nj = D // tn

    def body(y_ref, w_ref, g_ref, yp_ref, m_ref, acc):
        n = pl.program_id(2)
        yp = jnp.dot(y_ref[...], w_ref[...], preferred_element_type=F32)
        yp_ref[...] = yp.astype(BF16)

        @pl.when(n == 0)
        def _():
            acc[...] = jnp.zeros_like(acc)

        acc[...] += g_ref[...].astype(F32) * yp

        @pl.when(n == NB - 1)
        def _():
            m_ref[...] = acc[...].astype(BF16)

    return pl.pallas_call(
        body, name="branch_merge", grid=(T // tm, nj, NB),
        in_specs=[pl.BlockSpec((None, tm, W), lambda i, j, n: (n, i, 0)),
                  pl.BlockSpec((None, W, tn), lambda i, j, n: (n, 0, j)),
                  pl.BlockSpec((tm, tn), lambda i, j, n: (i, n * nj + j))],
        out_specs=[pl.BlockSpec((None, tm, tn), lambda i, j, n: (n, i, j)),
                   pl.BlockSpec((tm, tn), lambda i, j, n: (i, j))],
        out_shape=[_sds((NB, T, D), BF16), _sds((T, D), BF16)],
        scratch_shapes=[pltpu.VMEM((tm, tn), F32)],
        compiler_params=_cp("parallel", "parallel", "arbitrary"))(ys, w_br, gates)


def _merge_bwd(dmerged, gates, yproj):
    NB, T, D = yproj.shape
    tm = _pick(T, 256, SUBLANES)

    def body(dm_ref, g_ref, yp_ref, dyp_ref, dz_ref, db_ref):
        dm, g = dm_ref[...], g_ref[...].astype(F32)
        dyp_ref[...] = (dm * g).astype(BF16)
        dz = dm * yp_ref[...].astype(F32) * g * (1.0 - g)
        dz_ref[...] = dz.astype(BF16)
        _acc_rows(db_ref, 0, jnp.sum(dz, axis=0, keepdims=True), pl.program_id(1) == 0)

    return pl.pallas_call(
        body, name="merge_bwd", grid=(NB, T // tm),
        in_specs=[pl.BlockSpec((tm, D), lambda n, i: (i, 0)), pl.BlockSpec((tm, D), lambda n, i: (i, n)),
                  pl.BlockSpec((None, tm, D), lambda n, i: (n, i, 0))],
        out_specs=[pl.BlockSpec((None, tm, D), lambda n, i: (n, i, 0)), pl.BlockSpec((tm, D), lambda n, i: (i, n)),
                   pl.BlockSpec((SUBLANES, D), lambda n, i: (0, n))],
        out_shape=[_sds((NB, T, D), BF16), _sds((T, NB * D), BF16), _sds((SUBLANES, NB * D), F32)],
        compiler_params=_cp("parallel", "arbitrary"))(dmerged, gates, yproj)


def _branch_bwd_dy(dyp, w_br):
    NB, T, D = dyp.shape
    W = w_br.shape[1]
    tm = _pick(T, 1024, SUBLANES)

    def body(a_ref, w_ref, o_ref):
        o_ref[...] = lax.dot_general(a_ref[...], w_ref[...], NT, preferred_element_type=F32)

    return pl.pallas_call(body, name="branch_bwd_dy", grid=(NB, T // tm),
                          in_specs=[pl.BlockSpec((None, tm, D), lambda n, i: (n, i, 0)),
                                    pl.BlockSpec((None, W, D), lambda n, i: (n, 0, 0))],
                          out_specs=pl.BlockSpec((None, tm, W), lambda n, i: (n, i, 0)),
                          out_shape=_sds((NB, T, W), F32), compiler_params=_cp("parallel", "parallel"))(dyp, w_br)


def _branch_bwd_dw(ys, dyp):
    NB, T, W = ys.shape
    D = dyp.shape[2]
    tm, tn, tk = _pick(W, 1024), _pick(D, 1024), _pick(T, 2048, SUBLANES)
    return _mm("branch_bwd_dw", ys, dyp, grid=(NB, W // tm, D // tn, T // tk), kaxis=3, dims=TN,
               a_spec=pl.BlockSpec((None, tk, tm), lambda n, i, j, k: (n, k, i)),
               b_spec=pl.BlockSpec((None, tk, tn), lambda n, i, j, k: (n, k, j)),
               acc_shape=(tm, tn), out_shape=_sds((NB, W, D), BF16),
               out_specs=pl.BlockSpec((None, tm, tn), lambda n, i, j, k: (n, i, j)),
               epilogue=_store(BF16), sem=("parallel", "parallel", "parallel", "arbitrary"))


def _layer_fwd(x, wl, sp):
    W = sp["conv_b"].shape[-1]
    D = x.shape[1]
    h = _rmsnorm_fwd(x, sp["norm_g"])
    proj = _mm_nn("proj", h, wl["w_in"], F32, tn=768)
    if "late" in wl:
        wl.update(wl.pop("late")(proj))

    def gate_ep(acc, ex, outs):
        outs[0][...] = _sigmoid(acc + ex[0][...]).astype(BF16)

    tn_g = _pick(4 * D, 1024)
    gates = _mm_nn("gates", h, wl["w_gate"], BF16, epilogue=gate_ep, extras=(sp["b_gate"],),
                   extra_specs=(pl.BlockSpec((1, tn_g), lambda i, j, k: (0, j)),), tn=tn_g)
    ys = lax.empty((4, x.shape[0], W), BF16)
    oa, ys, sb_carries = _sb_fwd(proj, W, ys)
    cpre, ys = _conv_fwd(proj, wl["conv_w"], wl["taps"], sp["conv_b"], sp["conv_ln_g"], sp["conv_ln_b"], W, ys)
    ys = _sgu_fwd(proj, sp["sgu_w"], sp["sgu_bt"], sp["sgu_ln_g"], sp["sgu_ln_b"], W, ys)
    os_, ls_ = zip(*[_dil_fwd_group(proj, W, gi, dil) for gi, (_, dil) in enumerate(DIL_PATTERNS)])
    od, lse, ys = _dil_combine(proj, os_, ls_, W, ys)
    yproj, merged = _branch_merge(ys, wl["w_br"], gates)

    def res_ep(acc, ex, outs):
        outs[0][...] = ex[0][...] + acc

    tm_o, tn_o = _pick(x.shape[0], 1024, SUBLANES), _pick(D, 1024)
    xn = _mm_nn("out_proj", merged, wl["w_out"], F32, epilogue=res_ep, extras=(x,),
                extra_specs=(pl.BlockSpec((tm_o, tn_o), lambda i, j, k: (i, j)),), tm=tm_o, tn=tn_o)
    saved = dict(x=x, h=h, proj=proj, gates=gates, oa=oa, sb_carries=sb_carries, cpre=cpre, od=od, lse=lse, ys=ys, yproj=yproj, merged=merged)
    return xn, saved


def _layer_bwd(dout, sv, wl, sp):
    W = sp["conv_b"].shape[-1]
    proj = sv["proj"]
    dmerged = _mm_nt("out_proj_bwd_dx", dout, wl["w_out"], F32)
    g_w_out = _mm_tn("out_proj_bwd_dw", sv["merged"], dout)
    dyp, dzg, db_gate = _merge_bwd(dmerged, sv["gates"], sv["yproj"])
    dy = _branch_bwd_dy(dyp, wl["w_br"])
    g_w_br = _branch_bwd_dw(sv["ys"], dyp)
    g_w_gate = _mm_tn("gate_bwd_dw", sv["h"], dzg)

    a_dq, a_dk, a_dv, a_dg = _sb_bwd(proj, sv["oa"], sv["sb_carries"], dy, W)
    a_qkv = _to_bf16("sb_bwd_cast", [a_dq, a_dk, a_dv])

    dc, b_dg, conv_stats = _conv_bwd_ln(proj, sv["cpre"], dy, sp["conv_ln_g"], sp["conv_ln_b"], W)
    b_da, b_db, g_conv_w = _conv_bwd_taps(proj, dc, wl["conv_w"], wl["taps"], W)

    c_du, c_dv, c_dg, g_sgu_w, g_sgu_bt, sgu_stats = _sgu_bwd(proj, dy, sp["sgu_w"], sp["sgu_bt"],
                                                              sp["sgu_ln_g"], sp["sgu_ln_b"], W)

    do, delta, d_dg = _dil_bwd_pre(proj, sv["od"], dy, W)
    dqs, dks, dvs = zip(*[_dil_bwd_group(proj, do, sv["lse"], delta, W, gi, dil)
                          for gi, (_, dil) in enumerate(DIL_PATTERNS)])
    d_qk = _to_bf16("dil_bwd_cast", [*dqs, *dks])
    d_dv = _sum3_bf16(*dvs)

    dproj = jnp.concatenate([a_qkv, a_dg, b_da, b_db, b_dg, c_du, c_dv, c_dg, d_qk, d_dv, d_dg], axis=1)
    g_w_in = _mm_tn("proj_bwd_dw", sv["h"], dproj, tn=768)
    dh = _mm_nt("gate_bwd_dh", dzg, wl["w_gate"], F32)
    dh = _mm_nt("proj_bwd_dh", dproj, wl["w_in"], F32, addend=dh)
    dx, dnorm = _rmsnorm_bwd(sv["x"], dh, dout, sp["norm_g"])

    K = wl["taps"]
    small = dict(norm_g=dnorm[0], conv_w=g_conv_w[:K], conv_b=conv_stats[2], conv_ln_g=conv_stats[0],
                 conv_ln_b=conv_stats[1], sgu_ln_g=sgu_stats[0], sgu_ln_b=sgu_stats[1], sgu_w=g_sgu_w,
                 sgu_b=g_sgu_bt.T, b_gate=db_gate[0])
    big = dict(w_in=g_w_in, w_gate=g_w_gate, w_br=g_w_br, w_out=g_w_out)
    return dx, big, small


HBM = pl.BlockSpec(memory_space=pl.ANY)


def _mesh_pos():
    return lax.axis_index("x"), lax.axis_index("y"), lax.axis_index("c")


def _other_chips(x, y):
    return [(1 - x, y), (x, 1 - y), (1 - x, 1 - y)]


def _shard_of(ref, axis, size, index):
    idx = [slice(None)] * len(ref.shape)
    idx[axis] = pl.ds(index * size, size)
    return ref.at[tuple(idx)]


def _all_gather(name, shards, axes):
    n = len(shards)
    out_shape = []
    for s, ax in zip(shards, axes):
        shp = list(s.shape)
        shp[ax] *= N_DEV
        out_shape.append(_sds(shp, s.dtype))

    def body(*refs):
        ins, outs = refs[:n], refs[n:2 * n]
        send, recv, lsem = refs[2 * n:]
        x, y, c = _mesh_pos()
        me, sib = (x, y, c), (x, y, 1 - c)
        chips = _other_chips(x, y)
        dev = lambda px, py, pc: 4 * px + 2 * py + pc

        def blk(a, d):
            return _shard_of(outs[a], axes[a], ins[a].shape[axes[a]], d)

        def cp(a, k, d, to, src=None):
            return pltpu.make_async_remote_copy(src_ref=blk(a, d) if src is None else src, dst_ref=blk(a, d),
                                                send_sem=send.at[a * 7 + k], recv_sem=recv.at[a * 7 + k],
                                                device_id=to, device_id_type=MESH)

        own = [pltpu.make_async_copy(ins[a], blk(a, dev(*me)), lsem.at[a]) for a in range(n)]
        for o in own:
            o.start()
        first = []
        for a in range(n):
            first.append(cp(a, 0, dev(*me), sib, src=ins[a]))
            first += [cp(a, 1 + j, dev(*me), (*ch, c), src=ins[a]) for j, ch in enumerate(chips)]
        for f in first:
            f.start()
        passed = []
        for j, ch in enumerate(chips):
            for a in range(n):
                cp(a, 1 + j, dev(*ch, c), me).wait_recv()
                p = cp(a, 4 + j, dev(*ch, c), sib)
                p.start()
                passed.append(p)
        for a in range(n):
            cp(a, 0, dev(*sib), me).wait_recv()
            for j, ch in enumerate(chips):
                cp(a, 4 + j, dev(*ch, 1 - c), me).wait_recv()
        for f in first + passed:
            f.wait_send()
        for o in own:
            o.wait()

    return pl.pallas_call(body, name=name, in_specs=[HBM] * n, out_specs=[HBM] * n, out_shape=out_shape,
                          scratch_shapes=[pltpu.SemaphoreType.DMA((7 * n,)), pltpu.SemaphoreType.DMA((7 * n,)),
                                          pltpu.SemaphoreType.DMA((n,))])(*shards)


def _rs_pair(name, grads, axes):
    n = len(grads)
    sizes = [g.shape[ax] // N_DEV for g, ax in zip(grads, axes)]
    out_shape = []
    for g, ax, sz in zip(grads, axes, sizes):
        shp = list(g.shape)
        shp[ax] = sz
        out_shape.append(_sds([N_CHIP] + shp, g.dtype))

    def body(*refs):
        ins, outs = refs[:n], refs[n:2 * n]
        send, recv = refs[2 * n:]
        x, y, c = _mesh_pos()
        cps = []
        for a in range(n):
            for k in range(N_CHIP):
                cps.append(pltpu.make_async_remote_copy(
                    src_ref=_shard_of(ins[a], axes[a], sizes[a], 2 * k + (1 - c)), dst_ref=outs[a].at[k],
                    send_sem=send.at[a * N_CHIP + k], recv_sem=recv.at[a * N_CHIP + k],
                    device_id=(x, y, 1 - c), device_id_type=MESH))
        for cp in cps:
            cp.start()
        for cp in cps:
            cp.wait()

    return pl.pallas_call(body, name=name, in_specs=[HBM] * n, out_specs=[HBM] * n, out_shape=out_shape,
                          scratch_shapes=[pltpu.SemaphoreType.DMA((N_CHIP * n,)),
                                          pltpu.SemaphoreType.DMA((N_CHIP * n,))])(*grads)


def _pair_sum(name, grad, recv, axis, core):
    R, C = grad.shape
    _, r, cw = recv.shape
    tr = _pick(r, 512, SUBLANES)
    nr = r // tr

    def body(c_ref, g_ref, r_ref, o_ref):
        o_ref[...] = (g_ref[...] + r_ref[...]).astype(BF16)

    if axis == 1:
        gspec = pl.BlockSpec((tr, cw), lambda k, i, c_ref: (i, 2 * k + c_ref[0]))
    else:
        gspec = pl.BlockSpec((tr, cw), lambda k, i, c_ref: ((2 * k + c_ref[0]) * nr + i, 0))
    part = pl.BlockSpec((None, tr, cw), lambda k, i, c_ref: (k, i, 0))
    return pl.pallas_call(
        body, name=name, out_shape=_sds((N_CHIP, r, cw), BF16),
        grid_spec=pltpu.PrefetchScalarGridSpec(num_scalar_prefetch=1, grid=(N_CHIP, nr), in_specs=[gspec, part],
                                               out_specs=part),
        compiler_params=_cp("parallel", "parallel"))(core, grad, recv)


def _rs_chips(name, parts):
    n = len(parts)

    def body(*refs):
        ins, outs = refs[:n], refs[n:2 * n]
        send, recv, lsem = refs[2 * n:]
        x, y, c = _mesh_pos()
        mine = 2 * x + y
        own = [pltpu.make_async_copy(ins[a].at[mine], outs[a].at[mine], lsem.at[a]) for a in range(n)]
        for o in own:
            o.start()
        cps = []
        for a in range(n):
            for j, (px, py) in enumerate(_other_chips(x, y)):
                cps.append(pltpu.make_async_remote_copy(
                    src_ref=ins[a].at[2 * px + py], dst_ref=outs[a].at[mine],
                    send_sem=send.at[a * 3 + j], recv_sem=recv.at[a * 3 + j],
                    device_id=(px, py, c), device_id_type=MESH))
        for cp in cps:
            cp.start()
        for a in range(n):
            for j, (px, py) in enumerate(_other_chips(x, y)):
                pltpu.make_async_remote_copy(
                    src_ref=ins[a].at[mine], dst_ref=outs[a].at[2 * px + py],
                    send_sem=send.at[a * 3 + j], recv_sem=recv.at[a * 3 + j],
                    device_id=(x, y, c), device_id_type=MESH).wait_recv()
        for cp in cps:
            cp.wait_send()
        for o in own:
            o.wait()

    return pl.pallas_call(body, name=name, in_specs=[HBM] * n, out_specs=[HBM] * n,
                          out_shape=[_sds(p.shape, p.dtype) for p in parts],
                          scratch_shapes=[pltpu.SemaphoreType.DMA((3 * n,)), pltpu.SemaphoreType.DMA((3 * n,)),
                                          pltpu.SemaphoreType.DMA((n,))])(*parts)


HBM_ONLY = pl.BlockSpec(memory_space=pltpu.HBM)
SEM_SPEC = pl.BlockSpec(memory_space=pltpu.SEMAPHORE)
N_PEER = N_DEV - 1


def _peer(x, y, c, m):
    flip = lambda v, bit: 1 - v if bit else v
    return flip(x, (m >> 2) & 1), flip(y, (m >> 1) & 1), flip(c, m & 1)


def _exchange_copies(kind, srcs, lands, send, recv, axes):
    x, y, c = _mesh_pos()
    me = 4 * x + 2 * y + c
    n = len(srcs)
    remote, local = [], []
    for a in range(n):
        for m in range(1, N_DEV):
            px, py, pc = _peer(x, y, c, m)
            if kind == "gather":
                src = srcs[a]
                dst = _shard_of(lands[a], axes[a], srcs[a].shape[axes[a]], me)
            else:
                src = _shard_of(srcs[a], axes[a], srcs[a].shape[axes[a]] // N_DEV, 4 * px + 2 * py + pc)
                dst = lands[a].at[me]
            remote.append(pltpu.make_async_remote_copy(
                src_ref=src, dst_ref=dst, send_sem=send.at[a * N_PEER + m - 1], recv_sem=recv.at[a * N_PEER + m - 1],
                device_id=(px, py, pc), device_id_type=MESH))
        if kind == "gather":
            local.append(pltpu.make_async_copy(srcs[a], _shard_of(lands[a], axes[a], srcs[a].shape[axes[a]], me),
                                               send.at[n * N_PEER + a]))
    return remote, local


def _exchange_start(name, kind, srcs, lands, axes, carry):
    n = len(srcs)
    hbm = lambda a: pltpu.with_memory_space_constraint(a, pltpu.HBM)

    def body(*refs):
        send, recv = refs[2 * n + 1], refs[2 * n + 2]
        remote, local = _exchange_copies(kind, refs[:n], refs[n:2 * n], send, recv, axes)
        for cp in remote + local:
            cp.start()

    thru = [*srcs, *lands, carry]
    res = pl.pallas_call(
        body, name=name,
        out_shape=(pltpu.SemaphoreType.DMA((n * N_DEV,)), pltpu.SemaphoreType.DMA((n * N_PEER,)),
                   *[pltpu.HBM(a.shape, a.dtype) for a in thru]),
        in_specs=[HBM_ONLY] * len(thru), out_specs=(SEM_SPEC, SEM_SPEC, *[HBM_ONLY] * len(thru)),
        input_output_aliases={i: 2 + i for i in range(len(thru))},
        compiler_params=pltpu.CompilerParams(has_side_effects=pltpu.SideEffectType.DATAFLOW_SIDE_EFFECTING),
    )(*[hbm(a) for a in thru])
    return res[0], res[1], list(res[2:2 + n]), list(res[2 + n:2 + 2 * n]), res[2 + 2 * n]


def _exchange_wait(name, kind, send, recv, srcs, lands, axes, after):
    n = len(srcs)

    def body(*refs):
        send_ref, recv_ref = refs[2 * n], refs[2 * n + 1]
        remote, local = _exchange_copies(kind, refs[:n], refs[n:2 * n], send_ref, recv_ref, axes)
        for cp in remote:
            cp.wait_send()
            cp.wait_recv()
        for cp in local:
            cp.wait()

    thru = [*srcs, *lands]
    res = pl.pallas_call(
        body, name=name, out_shape=tuple(pltpu.HBM(a.shape, a.dtype) for a in thru),
        in_specs=[*[HBM_ONLY] * len(thru), SEM_SPEC, SEM_SPEC, HBM], out_specs=tuple([HBM_ONLY] * len(thru)),
        input_output_aliases={i: i for i in range(len(thru))},
        compiler_params=pltpu.CompilerParams(has_side_effects=pltpu.SideEffectType.DATAFLOW_SIDE_EFFECTING),
    )(*thru, send, recv, after)
    return list(res[:n]), list(res[n:])


def _adamw_scatter(name, land, grad, axis, me, w, m, v):
    P, R, C = land.shape
    tr = _pick(R, max(SUBLANES, (1 << 18) // C // 16 * 16), 16)
    nr = R // tr

    def body(me_ref, l_ref, o_ref, w_ref, m_ref, v_ref, g_ref, d_ref, mo_ref, vo_ref):
        own = o_ref[...].astype(F32)
        g = jnp.where(me_ref[0] == 0, own, l_ref[0].astype(F32))
        for k in range(1, P):
            g = g + jnp.where(me_ref[0] == k, own, l_ref[k].astype(F32))
        mn = ADAM_B1 * m_ref[...] + (1.0 - ADAM_B1) * g
        vn = ADAM_B2 * v_ref[...] + (1.0 - ADAM_B2) * (g * g)
        m_hat = mn / (1.0 - ADAM_B1 ** ADAM_STEP)
        v_hat = vn / (1.0 - ADAM_B2 ** ADAM_STEP)
        g_ref[...] = g
        d_ref[...] = -ADAM_LR * (m_hat / (jnp.sqrt(v_hat) + ADAM_EPS) + ADAM_WD * w_ref[...])
        mo_ref[...] = mn
        vo_ref[...] = vn

    if axis == 1:
        own_spec = pl.BlockSpec((tr, C), lambda i, me_ref: (i, me_ref[0]))
    else:
        own_spec = pl.BlockSpec((tr, C), lambda i, me_ref: (me_ref[0] * nr + i, 0))
    row = pl.BlockSpec((tr, C), lambda i, me_ref: (i, 0))
    return pl.pallas_call(
        body, name=name, out_shape=[_sds((R, C), F32)] * 4,
        grid_spec=pltpu.PrefetchScalarGridSpec(
            num_scalar_prefetch=1, grid=(nr,),
            in_specs=[pl.BlockSpec((P, tr, C), lambda i, me_ref: (0, i, 0)), own_spec, row, row, row],
            out_specs=[row] * 4),
        compiler_params=_cp("parallel"))(me, land, grad, w, m, v)


def _adamw_sum(name, parts, w, m, v):
    P, R, C = parts.shape
    tr = _pick(R, max(SUBLANES, (1 << 19) // C // SUBLANES * SUBLANES), SUBLANES)

    def body(p_ref, w_ref, m_ref, v_ref, g_ref, d_ref, mo_ref, vo_ref):
        g = p_ref[0].astype(F32)
        for k in range(1, P):
            g = g + p_ref[k].astype(F32)
        mn = ADAM_B1 * m_ref[...] + (1.0 - ADAM_B1) * g
        vn = ADAM_B2 * v_ref[...] + (1.0 - ADAM_B2) * (g * g)
        m_hat = mn / (1.0 - ADAM_B1 ** ADAM_STEP)
        v_hat = vn / (1.0 - ADAM_B2 ** ADAM_STEP)
        g_ref[...] = g
        d_ref[...] = -ADAM_LR * (m_hat / (jnp.sqrt(v_hat) + ADAM_EPS) + ADAM_WD * w_ref[...])
        mo_ref[...] = mn
        vo_ref[...] = vn

    row = pl.BlockSpec((tr, C), lambda i: (i, 0))
    return pl.pallas_call(body, name=name, grid=(R // tr,),
                          in_specs=[pl.BlockSpec((P, tr, C), lambda i: (0, i, 0)), row, row, row],
                          out_specs=[row] * 4, out_shape=[_sds((R, C), F32)] * 4,
                          compiler_params=_cp("parallel"))(parts, w, m, v)


def _rows128(a, pad_rows=SUBLANES):
    flat = a.reshape(-1, LANES)
    pad = (-flat.shape[0]) % pad_rows
    return jnp.pad(flat, ((0, pad), (0, 0))) if pad else flat


SMALL = ("norm_g", "conv_b", "conv_ln_g", "conv_ln_b", "sgu_ln_g", "sgu_ln_b", "sgu_w", "sgu_b", "b_gate", "final_g")


def kernel(x, norm_g, w_in, conv_w, conv_b, conv_ln_g, conv_ln_b, sgu_ln_g, sgu_ln_b, sgu_w, sgu_b, w_branch, w_gate, b_gate, w_out, final_g, loss_target, m_norm_g, m_w_in, m_conv_w, m_conv_b, m_conv_ln_g, m_conv_ln_b, m_sgu_ln_g, m_sgu_ln_b, m_sgu_w, m_sgu_b, m_w_branch, m_w_gate, m_b_gate, m_w_out, m_final_g, v_norm_g, v_w_in, v_conv_w, v_conv_b, v_conv_ln_g, v_conv_ln_b, v_sgu_ln_g, v_sgu_ln_b, v_sgu_w, v_sgu_b, v_w_branch, v_w_gate, v_b_gate, v_w_out, v_final_g):
    L, D = norm_g.shape
    W = conv_b.shape[1]
    taps = conv_w.shape[1]
    weights = dict(norm_g=norm_g, w_in=w_in, conv_w=conv_w, conv_b=conv_b, conv_ln_g=conv_ln_g, conv_ln_b=conv_ln_b,
                   sgu_ln_g=sgu_ln_g, sgu_ln_b=sgu_ln_b, sgu_w=sgu_w, sgu_b=sgu_b, w_branch=w_branch, w_gate=w_gate,
                   b_gate=b_gate, w_out=w_out, final_g=final_g)
    mom_m = dict(norm_g=m_norm_g, w_in=m_w_in, conv_w=m_conv_w, conv_b=m_conv_b, conv_ln_g=m_conv_ln_g,
                 conv_ln_b=m_conv_ln_b, sgu_ln_g=m_sgu_ln_g, sgu_ln_b=m_sgu_ln_b, sgu_w=m_sgu_w, sgu_b=m_sgu_b,
                 w_branch=m_w_branch, w_gate=m_w_gate, b_gate=m_b_gate, w_out=m_w_out, final_g=m_final_g)
    mom_v = dict(norm_g=v_norm_g, w_in=v_w_in, conv_w=v_conv_w, conv_b=v_conv_b, conv_ln_g=v_conv_ln_g,
                 conv_ln_b=v_conv_ln_b, sgu_ln_g=v_sgu_ln_g, sgu_ln_b=v_sgu_ln_b, sgu_w=v_sgu_w, sgu_b=v_sgu_b,
                 w_branch=v_w_branch, w_gate=v_w_gate, b_gate=v_b_gate, w_out=v_w_out, final_g=v_final_g)
    core = lax.axis_index("c").astype(jnp.int32).reshape(1)
    me = 4 * lax.axis_index("x") + 2 * lax.axis_index("y") + lax.axis_index("c")

    gather_names = ("w_in", "w_gate", "w_br", "w_out", "conv_w")
    gather_axes = (1, 1, 2, 0, 1)

    def gather_start(tag, l, which, carry):
        shards = [w_in[l].astype(BF16), w_gate[l].astype(BF16), w_branch[l].astype(BF16), w_out[l].astype(BF16),
                  jnp.pad(conv_w[l], ((0, CONV_HALO - taps), (0, 0)))]
        shards = [shards[i] for i in which]
        axes = [gather_axes[i] for i in which]
        lands = []
        for s, ax in zip(shards, axes):
            full = list(s.shape)
            full[ax] *= N_DEV
            lands.append(lax.empty(tuple(full), s.dtype))
        return _exchange_start(f"gather_start_{tag}", "gather", shards, lands, axes, carry) + (which,)

    def gather_wait(tag, started, after):
        send, recv, shards, lands, _, which = started
        _, full = _exchange_wait(f"gather_wait_{tag}", "gather", send, recv, shards, lands,
                                 [gather_axes[i] for i in which], after)
        return {gather_names[i]: f for i, f in zip(which, full)}

    xs = x[0]
    saved, gathered, smalls = [], [], []
    first = gather_start("0a", 0, (0,), xs)
    rest = gather_start("0b", 0, (1, 2, 3, 4), first[4])
    xs = rest[4]
    wl = gather_wait("0a", first, xs)
    wl["late"] = functools.partial(gather_wait, "0b", rest)
    for l in range(L):
        wl["taps"] = taps
        if l + 1 < L:
            started = gather_start(l + 1, l + 1, (0, 1, 2, 3, 4), xs)
            xs = started[4]
        sp = dict(norm_g=norm_g[l][None], conv_b=conv_b[l][None], conv_ln_g=conv_ln_g[l][None],
                  conv_ln_b=conv_ln_b[l][None], sgu_ln_g=sgu_ln_g[l][None], sgu_ln_b=sgu_ln_b[l][None],
                  sgu_w=sgu_w[l], sgu_bt=sgu_b[l].T, b_gate=b_gate[l][None])
        xs, sv = _layer_fwd(xs, wl, sp)
        saved.append(sv)
        gathered.append(wl)
        smalls.append(sp)
        if l + 1 < L:
            wl = gather_wait(l + 1, started, xs)

    dx, d_final, loss_part = _loss_head(xs, loss_target[0], final_g[None])
    loss = lax.psum(loss_part[0, 0], ("x", "y", "c"))

    big_names = ("w_in", "w_gate", "w_branch", "w_out")
    big_axes = (1, 1, 1, 0)
    me1 = me.astype(jnp.int32).reshape(1)
    outs = {}
    small_grads = []

    def scatter_start(l, g2d, carry):
        lands = []
        for g, ax in zip(g2d, big_axes):
            blk = list(g.shape)
            blk[ax] //= N_DEV
            lands.append(lax.empty((N_DEV, *blk), g.dtype))
        return _exchange_start(f"scatter_start_{l}", "scatter", g2d, lands, big_axes, carry)

    def scatter_finish(l, started, after):
        send, recv, g2d, lands, _ = started
        g2d, lands = _exchange_wait(f"scatter_wait_{l}", "scatter", send, recv, g2d, lands, big_axes, after)
        for nm, land, g, ax in zip(big_names, lands, g2d, big_axes):
            shard2d = land.shape[1:]
            outs.setdefault(nm, {})[l] = _adamw_scatter(
                "adamw_" + nm, land, g, ax, me1, weights[nm][l].reshape(shard2d), mom_m[nm][l].reshape(shard2d),
                mom_v[nm][l].reshape(shard2d))

    pending = None
    for l in reversed(range(L)):
        dx, big, small = _layer_bwd(dx, saved[l], gathered[l], smalls[l])
        small_grads.append(small)
        if pending is not None:
            scatter_finish(l + 1, pending, dx)
        g2d = [big["w_in"], big["w_gate"], big["w_br"].reshape(-1, D), big["w_out"]]
        if l > 0:
            pending = scatter_start(l, g2d, dx)
            dx = pending[4]
        else:
            pending = scatter_start(l, g2d, d_final)
            d_final = pending[4]
    small_grads.reverse()

    sg = {nm: jnp.stack([small_grads[l][nm] for l in range(L)]) for nm in SMALL[:-1]}
    sg["final_g"] = d_final[0]
    conv_w_full = jnp.stack([small_grads[l]["conv_w"] for l in range(L)])
    segs = [_rows128(sg[nm]) for nm in SMALL] + [_rows128(conv_w_full)]
    offs = [0]
    for s in segs:
        offs.append(offs[-1] + s.shape[0])
    pack = jnp.concatenate(segs, axis=0)
    (allp,) = _all_gather("gather_small_grads", [pack[None]], [0])

    def packed(src):
        return jnp.concatenate([_rows128(src[nm]) for nm in SMALL] + [jnp.zeros_like(segs[-1])], axis=0)

    s_g, s_d, s_m, s_v = _adamw_sum("adamw_small", allp, packed(weights), packed(mom_m), packed(mom_v))
    scatter_finish(0, pending, s_g)
    big_out = {nm: [jnp.stack([outs[nm][l][q] for l in range(L)]).reshape(weights[nm].shape) for q in range(4)]
               for nm in big_names}

    def unpack(buf, i, like):
        n = like.size // LANES
        return buf[offs[i]:offs[i] + n].reshape(like.shape)

    small_out = {nm: [unpack(b, i, weights[nm]) for b in (s_g, s_d, s_m, s_v)] for i, nm in enumerate(SMALL)}
    Wc = conv_w.shape[2]
    cw_sum = lax.dynamic_slice_in_dim(unpack(s_g, len(SMALL), conv_w_full), me * Wc, Wc, axis=2)
    cshape = (L * conv_w.shape[1], Wc)
    conv_out = [o.reshape(conv_w.shape) for o in _adamw_sum(
        "adamw_conv_w", cw_sum.reshape((1,) + cshape), conv_w.reshape(cshape), m_conv_w.reshape(cshape),
        v_conv_w.reshape(cshape))]

    order = ["norm_g", "w_in", "conv_w", "conv_b", "conv_ln_g", "conv_ln_b", "sgu_ln_g", "sgu_ln_b", "sgu_w", "sgu_b",
             "w_branch", "w_gate", "b_gate", "w_out", "final_g"]
    table = dict(small_out)
    table.update(big_out)
    table["conv_w"] = conv_out
    result = [loss, dx[None]]
    for q in range(4):
        result += [table[nm][q] for nm in order]
    return tuple(result)
```

```python
import functools

import jax
import jax.numpy as jnp
from jax import lax
from jax.experimental import pallas as pl
from jax.experimental.pallas import tpu as pltpu

F32 = jnp.float32
BF16 = jnp.bfloat16
LANES = 128
SUBLANES = 8
CONV_HALO = 32
SB_QUERY_BLOCK = 512
SB_QUERY_BLOCK_BWD = 512
SB_KEY_BLOCK = 1024
DIL_STEP_ROWS = 2048
DIL_UNROLL = 16
NORM_EPS = 1e-6
NEG = -1e30
N_DEV = 8
DIL_PATTERNS = ((128, 1), (512, 4), (2048, 16))

ADAM_LR = 0.001
ADAM_B1 = 0.9
ADAM_B2 = 0.999
ADAM_EPS = 1e-08
ADAM_WD = 0.01
ADAM_STEP = 10

MESH = pl.DeviceIdType.MESH
NN = (((1,), (0,)), ((), ()))
NT = (((1,), (1,)), ((), ()))
TN = (((0,), (0,)), ((), ()))
VMEM_LIMIT = 52 << 20


def _sds(shape, dtype):
    return jax.ShapeDtypeStruct(tuple(shape), dtype)


def _cp(*sem):
    return pltpu.CompilerParams(dimension_semantics=tuple(sem), vmem_limit_bytes=VMEM_LIMIT)


def _pick(n, target, quantum=LANES):
    if n <= target:
        return n
    t = (target // quantum) * quantum
    while t >= quantum:
        if n % t == 0:
            return t
        t -= quantum
    return n


def _sigmoid(x):
    return 1.0 / (1.0 + jnp.exp(-x))


def _silu(x):
    return x * _sigmoid(x)


def _dsilu(x):
    s = _sigmoid(x)
    return s * (1.0 + x * (1.0 - s))


_GELU_K = 0.7978845608028654
_GELU_A = 0.044715


def _gelu(x):
    return 0.5 * x * (1.0 + jnp.tanh(_GELU_K * (x + _GELU_A * x * x * x)))


def _dgelu(x):
    t = jnp.tanh(_GELU_K * (x + _GELU_A * x * x * x))
    return 0.5 * (1.0 + t) + 0.5 * x * (1.0 - t * t) * _GELU_K * (1.0 + 3.0 * _GELU_A * x * x)


def _ln_stats(v):
    mu = jnp.mean(v, axis=-1, keepdims=True)
    d = v - mu
    var = jnp.mean(d * d, axis=-1, keepdims=True)
    r = lax.rsqrt(var + NORM_EPS)
    return d * r, r


def _ln_bwd(dxh, xh, r):
    return r * (dxh - jnp.mean(dxh, axis=-1, keepdims=True) - xh * jnp.mean(dxh * xh, axis=-1, keepdims=True))


def _acc_rows(ref, row, val, first):
    @pl.when(first)
    def _():
        ref[...] = jnp.zeros_like(ref)
    ref[row:row + 1, :] += val


def _mm(name, a, b, *, grid, kaxis, dims, a_spec, b_spec, acc_shape, out_shape, out_specs,
        epilogue, extras=(), extra_specs=(), sem):
    nk = grid[kaxis]
    ne = len(extras)

    def body(*refs):
        a_ref, b_ref = refs[0], refs[1]
        ex = refs[2:2 + ne]
        outs = refs[2 + ne:-1]
        acc = refs[-1]
        k = pl.program_id(kaxis)

        @pl.when(k == 0)
        def _():
            acc[...] = jnp.zeros_like(acc)

        acc[...] += lax.dot_general(a_ref[...].astype(BF16), b_ref[...].astype(BF16), dims,
                                    preferred_element_type=F32)

        @pl.when(k == nk - 1)
        def _():
            epilogue(acc[...], ex, outs)

    return pl.pallas_call(
        body, name=name, grid=grid, in_specs=[a_spec, b_spec, *extra_specs], out_specs=out_specs,
        out_shape=out_shape, scratch_shapes=[pltpu.VMEM(acc_shape, F32)], compiler_params=_cp(*sem),
    )(a, b, *extras)


def _store(dtype):
    def ep(acc, ex, outs):
        outs[0][...] = acc.astype(dtype)
    return ep


def _mm_nn(name, a, b, out_dtype, epilogue=None, extras=(), extra_specs=(), tm=1024, tn=1024):
    M, K = a.shape
    N = b.shape[1]
    tm, tn = _pick(M, tm, SUBLANES), _pick(N, tn)
    return _mm(name, a, b, grid=(M // tm, N // tn, 1), kaxis=2, dims=NN,
               a_spec=pl.BlockSpec((tm, K), lambda i, j, k: (i, 0)),
               b_spec=pl.BlockSpec((K, tn), lambda i, j, k: (0, j)),
               acc_shape=(tm, tn), out_shape=_sds((M, N), out_dtype),
               out_specs=pl.BlockSpec((tm, tn), lambda i, j, k: (i, j)),
               epilogue=epilogue or _store(out_dtype), extras=extras, extra_specs=extra_specs,
               sem=("parallel", "parallel", "arbitrary"))


def _mm_nt(name, a, b, out_dtype, addend=None, tm=1024, tn=1024, tk=2048):
    M, K = a.shape
    N = b.shape[0]
    tm, tn, tk = _pick(M, tm, SUBLANES), _pick(N, tn), _pick(K, tk)
    extras, extra_specs = (), ()
    if addend is not None:
        extras = (addend,)
        extra_specs = (pl.BlockSpec((tm, tn), lambda i, j, k: (i, j)),)

    def ep(acc, ex, outs):
        if ex:
            acc = acc + ex[0][...]
        outs[0][...] = acc.astype(out_dtype)

    return _mm(name, a, b, grid=(M // tm, N // tn, K // tk), kaxis=2, dims=NT,
               a_spec=pl.BlockSpec((tm, tk), lambda i, j, k: (i, k)),
               b_spec=pl.BlockSpec((tn, tk), lambda i, j, k: (j, k)),
               acc_shape=(tm, tn), out_shape=_sds((M, N), out_dtype),
               out_specs=pl.BlockSpec((tm, tn), lambda i, j, k: (i, j)),
               epilogue=ep, extras=extras, extra_specs=extra_specs,
               sem=("parallel", "parallel", "arbitrary"))


def _mm_tn(name, a, b, tm=1024, tn=1024, tk=2048, out_dtype=BF16):
    K, M = a.shape
    N = b.shape[1]
    tm, tn, tk = _pick(M, tm), _pick(N, tn), _pick(K, tk, SUBLANES)
    return _mm(name, a, b, grid=(M // tm, N // tn, K // tk), kaxis=2, dims=TN,
               a_spec=pl.BlockSpec((tk, tm), lambda i, j, k: (k, i)),
               b_spec=pl.BlockSpec((tk, tn), lambda i, j, k: (k, j)),
               acc_shape=(tm, tn), out_shape=_sds((M, N), out_dtype),
               out_specs=pl.BlockSpec((tm, tn), lambda i, j, k: (i, j)),
               epilogue=_store(out_dtype), sem=("parallel", "parallel", "arbitrary"))


def _rmsnorm_fwd(x, g_row):
    T, D = x.shape
    tm = _pick(T, 512, SUBLANES)

    def body(x_ref, g_ref, h_ref):
        xv = x_ref[...]
        r = lax.rsqrt(jnp.mean(xv * xv, axis=-1, keepdims=True) + NORM_EPS)
        h_ref[...] = (xv * r * g_ref[...]).astype(BF16)

    row = pl.BlockSpec((tm, D), lambda i: (i, 0))
    return pl.pallas_call(body, name="rmsnorm_fwd", grid=(T // tm,),
                          in_specs=[row, pl.BlockSpec((1, D), lambda i: (0, 0))], out_specs=row,
                          out_shape=_sds((T, D), BF16), compiler_params=_cp("parallel"))(x, g_row)


def _rmsnorm_bwd(x, dh, dout, g_row):
    T, D = x.shape
    tm = _pick(T, 256, SUBLANES)

    def body(x_ref, dh_ref, do_ref, g_ref, dx_ref, dg_ref):
        xv = x_ref[...]
        r = lax.rsqrt(jnp.mean(xv * xv, axis=-1, keepdims=True) + NORM_EPS)
        xh = xv * r
        dhv = dh_ref[...]
        dxh = dhv * g_ref[...]
        dx_ref[...] = do_ref[...] + r * (dxh - xh * jnp.mean(dxh * xh, axis=-1, keepdims=True))
        _acc_rows(dg_ref, 0, jnp.sum(dhv * xh, axis=0, keepdims=True), pl.program_id(0) == 0)

    row = pl.BlockSpec((tm, D), lambda i: (i, 0))
    return pl.pallas_call(body, name="rmsnorm_bwd", grid=(T // tm,),
                          in_specs=[row, row, row, pl.BlockSpec((1, D), lambda i: (0, 0))],
                          out_specs=[row, pl.BlockSpec((SUBLANES, D), lambda i: (0, 0))],
                          out_shape=[_sds((T, D), F32), _sds((SUBLANES, D), F32)],
                          compiler_params=_cp("arbitrary"))(x, dh, dout, g_row)


def _loss_head(x, target, g_row):
    T, D = x.shape
    tm = _pick(T, 256, SUBLANES)

    def body(x_ref, t_ref, g_ref, dx_ref, dg_ref, loss_ref):
        first = pl.program_id(0) == 0
        xv = x_ref[...]
        g = g_ref[...]
        r = lax.rsqrt(jnp.mean(xv * xv, axis=-1, keepdims=True) + NORM_EPS)
        xh = xv * r
        err = xh * g - t_ref[...]
        part = 0.5 * jnp.sum(jnp.mean(err * err, axis=-1, keepdims=True), axis=0, keepdims=True)

        @pl.when(first)
        def _():
            loss_ref[...] = jnp.zeros_like(loss_ref)

        loss_ref[...] += jnp.broadcast_to(part, loss_ref.shape)
        dy = err / D
        _acc_rows(dg_ref, 0, jnp.sum(dy * xh, axis=0, keepdims=True), first)
        dxh = dy * g
        dx_ref[...] = r * (dxh - xh * jnp.mean(dxh * xh, axis=-1, keepdims=True))

    row = pl.BlockSpec((tm, D), lambda i: (i, 0))
    return pl.pallas_call(body, name="loss_head", grid=(T // tm,),
                          in_specs=[row, row, pl.BlockSpec((1, D), lambda i: (0, 0))],
                          out_specs=[row, pl.BlockSpec((SUBLANES, D), lambda i: (0, 0)),
                                     pl.BlockSpec((SUBLANES, LANES), lambda i: (0, 0))],
                          out_shape=[_sds((T, D), F32), _sds((SUBLANES, D), F32), _sds((SUBLANES, LANES), F32)],
                          compiler_params=_cp("arbitrary"))(x, target, g_row)


MXU_DIM = 256


def _tri(cmp):
    r = lax.broadcasted_iota(jnp.int32, (MXU_DIM, MXU_DIM), 0)
    c = lax.broadcasted_iota(jnp.int32, (MXU_DIM, MXU_DIM), 1)
    same_tile = (r // LANES) == (c // LANES)
    return (same_tile & cmp(r % LANES, c % LANES)).astype(BF16)


def _tile_scans(x, tri):
    out = []
    for p in range(x.shape[1] // MXU_DIM):
        sc = jnp.dot(x[:, p * MXU_DIM:(p + 1) * MXU_DIM].astype(BF16), tri, preferred_element_type=F32)
        out += [sc[:, :LANES], sc[:, LANES:]]
    return out


def _sb_scores(qs, kb, causal):
    z = lax.dot_general(qs, kb, NT, preferred_element_type=F32)
    lb = jnp.minimum(z, 0.0) - jnp.log(1.0 + jnp.exp(-jnp.abs(z)))
    l1 = lb - z
    return lb, (l1 if causal is None else jnp.where(causal, l1, 0.0))


def _masked(causal, x):
    return x if causal is None else jnp.where(causal, x, 0.0)


def _lanes(x, s):
    return x[:, s * LANES:(s + 1) * LANES]


def _sb_fwd(proj, W, ys):
    T = proj.shape[0]
    H = W // LANES
    assert T // LANES <= LANES
    tq = _pick(T, SB_QUERY_BLOCK, LANES)
    kblk = _pick(T, SB_KEY_BLOCK, LANES)
    assert kblk % tq == 0
    nsub = kblk // LANES
    scale = LANES ** -0.5

    def body(q_ref, k_ref, v_ref, g_ref, ys_in, o_ref, y_ref, c_ref, run):
        i = pl.program_id(1)
        qs = (q_ref[...] * scale).astype(BF16)
        row = i * tq + lax.broadcasted_iota(jnp.int32, (tq, kblk), 0)
        key = lax.broadcasted_iota(jnp.int32, (tq, kblk), 1)
        col = lax.broadcasted_iota(jnp.int32, (tq, LANES), 1)
        tri = _tri(lambda r, c: r > c)
        nkb = ((i + 1) * tq + kblk - 1) // kblk
        o_ref[...] = jnp.zeros_like(o_ref)
        c_ref[...] = jnp.zeros_like(c_ref)
        run[...] = jnp.zeros_like(run)

        def make_step(diagonal):
            def step(jj, carry):
                j = nkb - 1 - jj
                off = pl.multiple_of(j * kblk, kblk)
                kb = k_ref[pl.ds(off, kblk), :].astype(BF16)
                vb = v_ref[pl.ds(off, kblk), :].astype(BF16)
                causal = (key + off < row) if diagonal else None
                lb, l1m = _sb_scores(qs, kb, causal)
                c_after = run[...]
                cs = c_ref[...]
                after = _tile_scans(l1m, tri)
                for s in reversed(range(nsub)):
                    after[s] = c_after + after[s]
                    cs = jnp.where(col == j * nsub + s, c_after, cs)
                    c_after = c_after + jnp.sum(_lanes(l1m, s), axis=1, keepdims=True)
                run[...] = c_after
                c_ref[...] = cs
                w = _masked(causal, jnp.exp(lb + jnp.concatenate(after, axis=1)))
                o_ref[...] += jnp.dot(w.astype(BF16), vb, preferred_element_type=F32)
                return carry
            return step

        make_step(True)(0, 0)
        lax.fori_loop(1, nkb, make_step(False), 0)
        y_ref[...] = (o_ref[...] * _silu(g_ref[...])).astype(BF16)

    qspec = lambda c0: pl.BlockSpec((tq, LANES), lambda h, i: (i, c0 + h))
    kvspec = lambda c0: pl.BlockSpec((T, LANES), lambda h, i: (0, c0 + h))
    out = pl.BlockSpec((tq, LANES), lambda h, i: (i, h))
    return pl.pallas_call(body, name="sb_fwd", grid=(H, T // tq),
                          in_specs=[qspec(0), kvspec(H), kvspec(2 * H), qspec(3 * H), HBM],
                          out_specs=[out, pl.BlockSpec((None, tq, LANES), lambda h, i: (0, i, h)), out],
                          out_shape=[_sds((T, W), F32), _sds(ys.shape, BF16), _sds((T, W), F32)],
                          input_output_aliases={4: 1},
                          scratch_shapes=[pltpu.VMEM((tq, LANES), F32)],
                          compiler_params=_cp("parallel", "arbitrary"))(proj, proj, proj, proj, ys)


def _sb_bwd(proj, o, carries, dy, W):
    T = proj.shape[0]
    H = W // LANES
    tq = _pick(T, SB_QUERY_BLOCK_BWD, LANES)
    kblk = _pick(T, SB_KEY_BLOCK, LANES)
    assert kblk % tq == 0
    nsub = kblk // LANES
    scale = LANES ** -0.5

    def body(q_ref, k_ref, v_ref, g_ref, o_ref, c_ref, dy_ref, dq_ref, dk_ref, dv_ref, dg_ref, run):
        i = pl.program_id(1)

        @pl.when(i == 0)
        def _():
            dk_ref[...] = jnp.zeros_like(dk_ref)
            dv_ref[...] = jnp.zeros_like(dv_ref)

        gv = g_ref[...]
        dyv = dy_ref[...]
        dg_ref[...] = (dyv * o_ref[...] * _dsilu(gv)).astype(BF16)
        dob = (dyv * _silu(gv)).astype(BF16)
        qs = (q_ref[...] * scale).astype(BF16)
        row = i * tq + lax.broadcasted_iota(jnp.int32, (tq, kblk), 0)
        key = lax.broadcasted_iota(jnp.int32, (tq, kblk), 1)
        col = lax.broadcasted_iota(jnp.int32, (tq, LANES), 1)
        tri_after = _tri(lambda r, c: r > c)
        tri_before = _tri(lambda r, c: r < c)
        dq_ref[...] = jnp.zeros_like(dq_ref)
        run[...] = jnp.zeros_like(run)

        def make_step(diagonal):
            def step(j, carry):
                off = pl.multiple_of(j * kblk, kblk)
                kb = k_ref[pl.ds(off, kblk), :].astype(BF16)
                vb = v_ref[pl.ds(off, kblk), :].astype(BF16)
                causal = (key + off < row) if diagonal else None
                lb, l1m = _sb_scores(qs, kb, causal)
                cs = c_ref[...]
                after = [jnp.sum(jnp.where(col == j * nsub + s, cs, 0.0), axis=1, keepdims=True) + sc
                         for s, sc in enumerate(_tile_scans(l1m, tri_after))]
                w = _masked(causal, jnp.exp(lb + jnp.concatenate(after, axis=1)))
                gw = w * lax.dot_general(dob, vb, NT, preferred_element_type=F32)
                gpre = run[...]
                before = _tile_scans(gw, tri_before)
                for s in range(nsub):
                    before[s] = gpre + before[s]
                    gpre = gpre + jnp.sum(_lanes(gw, s), axis=1, keepdims=True)
                run[...] = gpre
                beta = jnp.exp(lb)
                dz = _masked(causal, gw * (1.0 - beta) - jnp.concatenate(before, axis=1) * beta).astype(BF16)
                dq_ref[...] += jnp.dot(dz, kb, preferred_element_type=F32)
                dk_ref[pl.ds(off, kblk), :] += lax.dot_general(dz, qs, TN, preferred_element_type=F32)
                dv_ref[pl.ds(off, kblk), :] += lax.dot_general(w.astype(BF16), dob, TN, preferred_element_type=F32)
                return carry
            return step

        last = ((i + 1) * tq + kblk - 1) // kblk - 1
        lax.fori_loop(0, last, make_step(False), 0)
        make_step(True)(last, 0)
        dq_ref[...] = dq_ref[...] * scale

    qspec = lambda c0: pl.BlockSpec((tq, LANES), lambda h, i: (i, c0 + h))
    kvspec = lambda c0: pl.BlockSpec((T, LANES), lambda h, i: (0, c0 + h), pipeline_mode=pl.Buffered(1))
    blk = pl.BlockSpec((tq, LANES), lambda h, i: (i, h))
    full = pl.BlockSpec((T, LANES), lambda h, i: (0, h))
    return pl.pallas_call(body, name="sb_bwd", grid=(H, T // tq),
                          in_specs=[qspec(0), kvspec(H), kvspec(2 * H), qspec(3 * H), blk, blk,
                                    pl.BlockSpec((None, tq, LANES), lambda h, i: (0, i, h))],
                          out_specs=[blk, full, full, blk],
                          out_shape=[_sds((T, W), F32), _sds((T, W), F32), _sds((T, W), F32), _sds((T, W), BF16)],
                          scratch_shapes=[pltpu.VMEM((tq, LANES), F32)],
                          compiler_params=_cp("parallel", "arbitrary"))(proj, proj, proj, proj, o, carries, dy)


def _conv_specs(W, tm):
    per = tm // CONV_HALO
    cur = lambda c: pl.BlockSpec((tm, W), lambda i: (i, c))
    prev = lambda c: pl.BlockSpec((CONV_HALO, W), lambda i: (jnp.maximum(i * per - 1, 0), c))
    return cur, prev


def _conv_fwd(proj, conv_w, K, conv_b, ln_g, ln_b, W, ys):
    T = proj.shape[0]
    tm = _pick(T, 256, CONV_HALO)
    lead = CONV_HALO - (K - 1)
    cur, prev = _conv_specs(W, tm)

    def body(a_ref, b_ref, ah_ref, bh_ref, g_ref, w_ref, cb_ref, lg_ref, lb_ref, ys_in, c_ref, y_ref, glu):
        i = pl.program_id(0)
        glu[0:CONV_HALO, :] = jnp.where(i > 0, ah_ref[...] * _sigmoid(bh_ref[...]), 0.0)
        glu[CONV_HALO:, :] = a_ref[...] * _sigmoid(b_ref[...])
        c = jnp.broadcast_to(cb_ref[...], (tm, W))
        for k in range(K):
            c = c + w_ref[k:k + 1, :] * glu[lead + k:lead + k + tm, :]
        c_ref[...] = c
        xh, _ = _ln_stats(c)
        y_ref[...] = (_silu(xh * lg_ref[...] + lb_ref[...]) * _silu(g_ref[...])).astype(BF16)

    vec = pl.BlockSpec((1, W), lambda i: (0, 0))
    row = pl.BlockSpec((tm, W), lambda i: (i, 0))
    return pl.pallas_call(body, name="conv_fwd", grid=(T // tm,),
                          in_specs=[cur(4), cur(5), prev(4), prev(5), cur(6),
                                    pl.BlockSpec((CONV_HALO, W), lambda i: (0, 0)), vec, vec, vec, HBM],
                          out_specs=[row, pl.BlockSpec((None, tm, W), lambda i: (1, i, 0))],
                          out_shape=[_sds((T, W), F32), _sds(ys.shape, BF16)], input_output_aliases={9: 1},
                          scratch_shapes=[pltpu.VMEM((tm + CONV_HALO, W), F32)],
                          compiler_params=_cp("parallel"))(proj, proj, proj, proj, proj, conv_w, conv_b, ln_g, ln_b, ys)


def _conv_bwd_ln(proj, c, dy, ln_g, ln_b, W):
    T = proj.shape[0]
    tm = _pick(T, 256, SUBLANES)

    def body(g_ref, c_ref, dy_ref, lg_ref, lb_ref, dc_ref, dg_ref, st_ref):
        first = pl.program_id(0) == 0
        gv = g_ref[...]
        dyv = dy_ref[...]
        xh, r = _ln_stats(c_ref[...])
        lg = lg_ref[...]
        ln = xh * lg + lb_ref[...]
        dg_ref[...] = (dyv * _silu(ln) * _dsilu(gv)).astype(BF16)
        dln = dyv * _silu(gv) * _dsilu(ln)
        dc = _ln_bwd(dln * lg, xh, r)
        dc_ref[...] = dc
        _acc_rows(st_ref, 0, jnp.sum(dln * xh, axis=0, keepdims=True), first)
        st_ref[1:2, :] += jnp.sum(dln, axis=0, keepdims=True)
        st_ref[2:3, :] += jnp.sum(dc, axis=0, keepdims=True)

    row = pl.BlockSpec((tm, W), lambda i: (i, 0))
    vec = pl.BlockSpec((1, W), lambda i: (0, 0))
    return pl.pallas_call(body, name="conv_bwd_ln", grid=(T // tm,),
                          in_specs=[pl.BlockSpec((tm, W), lambda i: (i, 6)), row,
                                    pl.BlockSpec((None, tm, W), lambda i: (1, i, 0)), vec, vec],
                          out_specs=[row, row, pl.BlockSpec((SUBLANES, W), lambda i: (0, 0))],
                          out_shape=[_sds((T, W), F32), _sds((T, W), BF16), _sds((SUBLANES, W), F32)],
                          compiler_params=_cp("arbitrary"))(proj, c, dy, ln_g, ln_b)


def _conv_bwd_taps(proj, dc, conv_w, K, W):
    T = proj.shape[0]
    tm = _pick(T, 256, CONV_HALO)
    lead = CONV_HALO - (K - 1)
    per = tm // CONV_HALO
    nblk = T // tm
    cur, prev = _conv_specs(W, tm)

    def body(a_ref, b_ref, ah_ref, bh_ref, dc_ref, dcn_ref, w_ref, da_ref, db_ref, dw_ref, glu, dcs):
        i = pl.program_id(0)
        av = a_ref[...]
        sb = _sigmoid(b_ref[...])
        glu[0:CONV_HALO, :] = jnp.where(i > 0, ah_ref[...] * _sigmoid(bh_ref[...]), 0.0)
        glu[CONV_HALO:, :] = av * sb
        dcv = dc_ref[...]
        dcs[0:tm, :] = dcv
        dcs[tm:, :] = jnp.where(i < nblk - 1, dcn_ref[...], 0.0)

        @pl.when(i == 0)
        def _():
            dw_ref[...] = jnp.zeros_like(dw_ref)

        dglu = jnp.zeros((tm, W), F32)
        for k in range(K):
            dglu = dglu + w_ref[k:k + 1, :] * dcs[K - 1 - k:K - 1 - k + tm, :]
            dw_ref[k:k + 1, :] += jnp.sum(dcv * glu[lead + k:lead + k + tm, :], axis=0, keepdims=True)
        da_ref[...] = (dglu * sb).astype(BF16)
        db_ref[...] = (dglu * av * sb * (1.0 - sb)).astype(BF16)

    row = pl.BlockSpec((tm, W), lambda i: (i, 0))
    nxt = pl.BlockSpec((CONV_HALO, W), lambda i: (jnp.minimum((i + 1) * per, T // CONV_HALO - 1), 0))
    return pl.pallas_call(body, name="conv_bwd_taps", grid=(nblk,),
                          in_specs=[cur(4), cur(5), prev(4), prev(5), row, nxt,
                                    pl.BlockSpec((CONV_HALO, W), lambda i: (0, 0))],
                          out_specs=[row, row, pl.BlockSpec((CONV_HALO, W), lambda i: (0, 0))],
                          out_shape=[_sds((T, W), BF16), _sds((T, W), BF16), _sds((CONV_HALO, W), F32)],
                          scratch_shapes=[pltpu.VMEM((tm + CONV_HALO, W), F32), pltpu.VMEM((tm + CONV_HALO, W), F32)],
                          compiler_params=_cp("arbitrary"))(proj, proj, proj, proj, dc, dc, conv_w)


def _sgu_common(cu, cv, lg, lb, w_ref, bt_ref, z_scr, G, nch):
    u = _gelu(cu)
    xh, r = _ln_stats(_gelu(cv))
    vn = (xh * lg + lb).astype(BF16)
    rr = lax.broadcasted_iota(jnp.int32, (LANES, LANES), 0)
    cc = lax.broadcasted_iota(jnp.int32, (LANES, LANES), 1)
    tril = rr >= cc
    wts = [jnp.where(tril, w_ref[g], 0.0).astype(BF16) for g in range(G)]
    for ch in range(nch):
        rs = slice(ch * LANES, (ch + 1) * LANES)
        for g in range(G):
            cs = slice(g * LANES, (g + 1) * LANES)
            z_scr[rs, cs] = jnp.dot(wts[g], vn[rs, cs], preferred_element_type=F32) + bt_ref[:, g:g + 1]
    return u, xh, r, vn, wts, tril


def _sgu_fwd(proj, sgu_w, sgu_bt, ln_g, ln_b, W, ys):
    T = proj.shape[0]
    G = W // LANES
    tm = _pick(T, 256, LANES)
    nch = tm // LANES

    def body(u_ref, v_ref, g_ref, w_ref, bt_ref, lg_ref, lb_ref, ys_in, y_ref, z_scr):
        u, *_ = _sgu_common(u_ref[...], v_ref[...], lg_ref[...], lb_ref[...], w_ref, bt_ref, z_scr, G, nch)
        y_ref[...] = (u * z_scr[...] * _silu(g_ref[...])).astype(BF16)

    cur = lambda c: pl.BlockSpec((tm, W), lambda i: (i, c))
    vec = pl.BlockSpec((1, W), lambda i: (0, 0))
    return pl.pallas_call(body, name="sgu_fwd", grid=(T // tm,),
                          in_specs=[cur(7), cur(8), cur(9), pl.BlockSpec((G, LANES, LANES), lambda i: (0, 0, 0)),
                                    pl.BlockSpec((LANES, G), lambda i: (0, 0)), vec, vec, HBM],
                          out_specs=pl.BlockSpec((None, tm, W), lambda i: (2, i, 0)), out_shape=_sds(ys.shape, BF16),
                          input_output_aliases={7: 0}, scratch_shapes=[pltpu.VMEM((tm, W), F32)],
                          compiler_params=_cp("parallel"))(proj, proj, proj, sgu_w, sgu_bt, ln_g, ln_b, ys)


def _sgu_bwd(proj, dy, sgu_w, sgu_bt, ln_g, ln_b, W):
    T = proj.shape[0]
    G = W // LANES
    tm = _pick(T, 256, LANES)
    nch = tm // LANES

    def body(u_ref, v_ref, g_ref, dy_ref, w_ref, bt_ref, lg_ref, lb_ref,
             du_ref, dv_ref, dg_ref, dw_ref, dbt_ref, st_ref, z_scr, dvn_scr):
        first = pl.program_id(0) == 0
        cu, cv, gv, dyv = u_ref[...], v_ref[...], g_ref[...], dy_ref[...]
        lg = lg_ref[...]
        u, xh, r, vn, wts, tril = _sgu_common(cu, cv, lg, lb_ref[...], w_ref, bt_ref, z_scr, G, nch)
        z = z_scr[...]
        sg = _silu(gv)
        dg_ref[...] = (dyv * u * z * _dsilu(gv)).astype(BF16)
        du_ref[...] = (dyv * z * sg * _dgelu(cu)).astype(BF16)
        dz = dyv * u * sg
        dzb = dz.astype(BF16)

        @pl.when(first)
        def _():
            dw_ref[...] = jnp.zeros_like(dw_ref)
            dbt_ref[...] = jnp.zeros_like(dbt_ref)

        for g in range(G):
            cs = slice(g * LANES, (g + 1) * LANES)
            dwg = jnp.zeros((LANES, LANES), F32)
            dbg = jnp.zeros((LANES, 1), F32)
            for ch in range(nch):
                rs = slice(ch * LANES, (ch + 1) * LANES)
                dwg = dwg + lax.dot_general(dzb[rs, cs], vn[rs, cs], NT, preferred_element_type=F32)
                dbg = dbg + jnp.sum(dz[rs, cs], axis=1, keepdims=True)
                dvn_scr[rs, cs] = lax.dot_general(wts[g], dzb[rs, cs], TN, preferred_element_type=F32)
            dw_ref[g] += jnp.where(tril, dwg, 0.0)
            dbt_ref[:, g:g + 1] += dbg
        dvn = dvn_scr[...]
        _acc_rows(st_ref, 0, jnp.sum(dvn * xh, axis=0, keepdims=True), first)
        st_ref[1:2, :] += jnp.sum(dvn, axis=0, keepdims=True)
        dv_ref[...] = (_ln_bwd(dvn * lg, xh, r) * _dgelu(cv)).astype(BF16)

    cur = lambda c: pl.BlockSpec((tm, W), lambda i: (i, c))
    row = pl.BlockSpec((tm, W), lambda i: (i, 0))
    vec = pl.BlockSpec((1, W), lambda i: (0, 0))
    wspec = pl.BlockSpec((G, LANES, LANES), lambda i: (0, 0, 0))
    bspec = pl.BlockSpec((LANES, G), lambda i: (0, 0))
    return pl.pallas_call(body, name="sgu_bwd", grid=(T // tm,),
                          in_specs=[cur(7), cur(8), cur(9), pl.BlockSpec((None, tm, W), lambda i: (2, i, 0)),
                                    wspec, bspec, vec, vec],
                          out_specs=[row, row, row, wspec, bspec, pl.BlockSpec((SUBLANES, W), lambda i: (0, 0))],
                          out_shape=[_sds((T, W), BF16)] * 3 + [_sds((G, LANES, LANES), F32), _sds((LANES, G), F32),
                                                                 _sds((SUBLANES, W), F32)],
                          scratch_shapes=[pltpu.VMEM((tm, W), F32), pltpu.VMEM((tm, W), F32)],
                          compiler_params=_cp("arbitrary"))(proj, proj, proj, dy, sgu_w, sgu_bt, ln_g, ln_b)


def _rows(start, dil):
    return pl.ds(start, LANES, stride=dil) if dil > 1 else pl.ds(start, LANES)


def _dil_masks():
    a = lax.broadcasted_iota(jnp.int32, (LANES, LANES), 0)
    c = lax.broadcasted_iota(jnp.int32, (LANES, LANES), 1)
    return c <= a, c >= a


def _dil_geometry(T, dil):
    sbr = LANES * dil
    nb = max(1, min(DIL_STEP_ROWS, T) // sbr)
    return sbr, nb, T // (sbr * nb)


def _for_units(nb, dil, unit):
    for blk in range(nb):
        if dil <= DIL_UNROLL:
            for r in range(dil):
                unit(blk, r)
        else:
            def chunk(it, carry):
                for u in range(DIL_UNROLL):
                    unit(blk, it * DIL_UNROLL + u)
                return carry
            lax.fori_loop(0, dil // DIL_UNROLL, chunk, 0)


def _dil_fwd_group(proj, W, gi, dil):
    T = proj.shape[0]
    H = W // LANES
    sbr, nb, nsteps = _dil_geometry(T, dil)
    scale = LANES ** -0.5
    cq, ck, cv = (10 + gi) * H, (13 + gi) * H, 16 * H

    def body(q_ref, kc_ref, kp_ref, vc_ref, vp_ref, o_ref, l_ref):
        b = pl.program_id(1)
        m_cur, m_prev = _dil_masks()
        m_first = m_prev & (b > 0)

        def unit(blk, r):
            sl = _rows(blk * sbr + r, dil)
            if blk == 0:
                kp, vp, mp = kp_ref[_rows(r, dil), :], vp_ref[_rows(r, dil), :], m_first
            else:
                sp_ = _rows((blk - 1) * sbr + r, dil)
                kp, vp, mp = kc_ref[sp_, :], vc_ref[sp_, :], m_prev
            q = (q_ref[sl, :] * scale).astype(BF16)
            sc = lax.dot_general(q, kc_ref[sl, :].astype(BF16), NT, preferred_element_type=F32)
            sp = lax.dot_general(q, kp.astype(BF16), NT, preferred_element_type=F32)
            sc = jnp.where(m_cur, sc, NEG)
            sp = jnp.where(mp, sp, NEG)
            m = jnp.maximum(jnp.max(sc, axis=1, keepdims=True), jnp.max(sp, axis=1, keepdims=True))
            pc = jnp.exp(sc - m)
            pp = jnp.exp(sp - m)
            den = jnp.sum(pc, axis=1, keepdims=True) + jnp.sum(pp, axis=1, keepdims=True)
            pv = (jnp.dot(pc.astype(BF16), vc_ref[sl, :].astype(BF16), preferred_element_type=F32)
                  + jnp.dot(pp.astype(BF16), vp.astype(BF16), preferred_element_type=F32))
            o_ref[sl, :] = pv / den
            l_ref[sl, :] = jnp.broadcast_to(m + jnp.log(den), (LANES, LANES))

        _for_units(nb, dil, unit)

    cur = lambda c0: pl.BlockSpec((sbr * nb, LANES), lambda h, b: (b, c0 + h))
    prv = lambda c0: pl.BlockSpec((sbr, LANES), lambda h, b: (jnp.maximum(b * nb - 1, 0), c0 + h))
    out = pl.BlockSpec((sbr * nb, LANES), lambda h, b: (b, h))
    return pl.pallas_call(body, name=f"dil_fwd_g{gi}", grid=(H, nsteps),
                          in_specs=[cur(cq), cur(ck), prv(ck), cur(cv), prv(cv)], out_specs=[out, out],
                          out_shape=[_sds((T, W), F32), _sds((T, W), F32)],
                          compiler_params=_cp("parallel", "parallel"))(proj, proj, proj, proj, proj)


def _dil_combine(proj, os_, ls_, W, ys):
    T = proj.shape[0]
    tm = _pick(T, 256, SUBLANES)

    def body(g_ref, o0, o1, o2, l0, l1, l2, ys_in, od_ref, lse_ref, y_ref):
        a0, a1, a2 = l0[...], l1[...], l2[...]
        m = jnp.maximum(jnp.maximum(a0, a1), a2)
        e0, e1, e2 = jnp.exp(a0 - m), jnp.exp(a1 - m), jnp.exp(a2 - m)
        s = e0 + e1 + e2
        od = (e0 / s) * o0[...] + (e1 / s) * o1[...] + (e2 / s) * o2[...]
        od_ref[...] = od
        lse_ref[...] = m + jnp.log(s)
        y_ref[...] = (od * _silu(g_ref[...])).astype(BF16)

    row = pl.BlockSpec((tm, W), lambda i: (i, 0))
    return pl.pallas_call(body, name="dil_combine", grid=(T // tm,),
                          in_specs=[pl.BlockSpec((tm, W), lambda i: (i, 17))] + [row] * 6 + [HBM],
                          out_specs=[row, row, pl.BlockSpec((None, tm, W), lambda i: (3, i, 0))],
                          out_shape=[_sds((T, W), F32), _sds((T, W), F32), _sds(ys.shape, BF16)],
                          input_output_aliases={7: 2},
                          compiler_params=_cp("parallel"))(proj, *os_, *ls_, ys)


def _dil_bwd_pre(proj, od, dy, W):
    T = proj.shape[0]
    H = W // LANES
    tm = _pick(T, 512, SUBLANES)

    def body(g_ref, od_ref, dy_ref, do_ref, dl_ref, dg_ref):
        gv, odv, dyv = g_ref[...], od_ref[...], dy_ref[...]
        do = dyv * _silu(gv)
        do_ref[...] = do
        dl_ref[...] = jnp.broadcast_to(jnp.sum(do * odv, axis=1, keepdims=True), (tm, LANES))
        dg_ref[...] = (dyv * odv * _dsilu(gv)).astype(BF16)

    blk = pl.BlockSpec((tm, LANES), lambda i, h: (i, h))
    return pl.pallas_call(body, name="dil_bwd_pre", grid=(T // tm, H),
                          in_specs=[pl.BlockSpec((tm, LANES), lambda i, h: (i, 17 * H + h)), blk,
                                    pl.BlockSpec((None, tm, LANES), lambda i, h: (3, i, h))],
                          out_specs=[blk, blk, blk],
                          out_shape=[_sds((T, W), F32), _sds((T, W), F32), _sds((T, W), BF16)],
                          compiler_params=_cp("parallel", "parallel"))(proj, od, dy)


def _dil_bwd_group(proj, do, lse, delta, W, gi, dil):
    T = proj.shape[0]
    H = W // LANES
    sbr, nb, nsteps = _dil_geometry(T, dil)
    scale = LANES ** -0.5
    cq, ck, cv = (10 + gi) * H, (13 + gi) * H, 16 * H

    def body(qc_ref, qn_ref, kc_ref, kp_ref, vc_ref, vp_ref, doc_ref, don_ref, lc_ref, ln_ref, dc_ref, dn_ref,
             dq_ref, dk_ref, dv_ref):
        b = pl.program_id(1)
        m_cur, m_prev = _dil_masks()
        m_first = m_prev & (b > 0)
        m_last = m_prev & (b < nsteps - 1)

        def probs(q, k, mask, l):
            s = lax.dot_general(q, k, NT, preferred_element_type=F32)
            return jnp.exp(jnp.where(mask, s - l, NEG))

        def unit(blk, r):
            sl = _rows(blk * sbr + r, dil)
            if blk == 0:
                edge = _rows(r, dil)
                k_p, v_p, m_cp = kp_ref[edge, :], vp_ref[edge, :], m_first
            else:
                sp_ = _rows((blk - 1) * sbr + r, dil)
                k_p, v_p, m_cp = kc_ref[sp_, :], vc_ref[sp_, :], m_prev
            if blk == nb - 1:
                edge = _rows(r, dil)
                q_n, do_n, l_n, d_n, m_nc = qn_ref[edge, :], don_ref[edge, :], ln_ref[edge, :], dn_ref[edge, :], m_last
            else:
                sn_ = _rows((blk + 1) * sbr + r, dil)
                q_n, do_n, l_n, d_n, m_nc = qc_ref[sn_, :], doc_ref[sn_, :], lc_ref[sn_, :], dc_ref[sn_, :], m_prev
            q_c = (qc_ref[sl, :] * scale).astype(BF16)
            q_n = (q_n * scale).astype(BF16)
            k_c, k_p = kc_ref[sl, :].astype(BF16), k_p.astype(BF16)
            v_c, v_p = vc_ref[sl, :].astype(BF16), v_p.astype(BF16)
            do_c, do_n = doc_ref[sl, :].astype(BF16), do_n.astype(BF16)
            l_c, d_c = lc_ref[sl, :], dc_ref[sl, :]
            p_cc = probs(q_c, k_c, m_cur, l_c)
            p_cp = probs(q_c, k_p, m_cp, l_c)
            p_nc = probs(q_n, k_c, m_nc, l_n)
            ds_cc = (p_cc * (lax.dot_general(do_c, v_c, NT, preferred_element_type=F32) - d_c)).astype(BF16)
            ds_cp = (p_cp * (lax.dot_general(do_c, v_p, NT, preferred_element_type=F32) - d_c)).astype(BF16)
            ds_nc = (p_nc * (lax.dot_general(do_n, v_c, NT, preferred_element_type=F32) - d_n)).astype(BF16)
            dq = (jnp.dot(ds_cc, k_c, preferred_element_type=F32) + jnp.dot(ds_cp, k_p, preferred_element_type=F32))
            dk = (lax.dot_general(ds_cc, q_c, TN, preferred_element_type=F32)
                  + lax.dot_general(ds_nc, q_n, TN, preferred_element_type=F32))
            dv = (lax.dot_general(p_cc.astype(BF16), do_c, TN, preferred_element_type=F32)
                  + lax.dot_general(p_nc.astype(BF16), do_n, TN, preferred_element_type=F32))
            dq_ref[sl, :] = dq * scale
            dk_ref[sl, :] = dk
            dv_ref[sl, :] = dv

        _for_units(nb, dil, unit)

    cur = lambda c0: pl.BlockSpec((sbr * nb, LANES), lambda h, b: (b, c0 + h))
    prv = lambda c0: pl.BlockSpec((sbr, LANES), lambda h, b: (jnp.maximum(b * nb - 1, 0), c0 + h))
    nxt = lambda c0: pl.BlockSpec((sbr, LANES), lambda h, b: (jnp.minimum((b + 1) * nb, T // sbr - 1), c0 + h))
    out = pl.BlockSpec((sbr * nb, LANES), lambda h, b: (b, h))
    return pl.pallas_call(body, name=f"dil_bwd_g{gi}", grid=(H, nsteps),
                          in_specs=[cur(cq), nxt(cq), cur(ck), prv(ck), cur(cv), prv(cv),
                                    cur(0), nxt(0), cur(0), nxt(0), cur(0), nxt(0)],
                          out_specs=[out, out, out], out_shape=[_sds((T, W), F32)] * 3,
                          compiler_params=_cp("parallel", "parallel"))(
        proj, proj, proj, proj, proj, proj, do, do, lse, lse, delta, delta)


def _sum3_bf16(a, b, c):
    T, W = a.shape
    tm = _pick(T, 512, SUBLANES)

    def body(a_ref, b_ref, c_ref, o_ref):
        o_ref[...] = (a_ref[...] + b_ref[...] + c_ref[...]).astype(BF16)

    row = pl.BlockSpec((tm, W), lambda i: (i, 0))
    return pl.pallas_call(body, name="dil_dv_sum", grid=(T // tm,), in_specs=[row, row, row], out_specs=row,
                          out_shape=_sds((T, W), BF16), compiler_params=_cp("parallel"))(a, b, c)


def _to_bf16(name, parts):
    T, W = parts[0].shape
    n = len(parts)
    tm = _pick(T, 512, SUBLANES)

    def body(*refs):
        o_ref = refs[n]
        for p in range(n):
            o_ref[:, p * W:(p + 1) * W] = refs[p][...].astype(BF16)

    row = pl.BlockSpec((tm, W), lambda i: (i, 0))
    return pl.pallas_call(body, name=name, grid=(T // tm,), in_specs=[row] * n,
                          out_specs=pl.BlockSpec((tm, n * W), lambda i: (i, 0)),
                          out_shape=_sds((T, n * W), BF16), compiler_params=_cp("parallel"))(*parts)


def _branch_merge(ys, w_br, gates):
    NB, T, W = ys.shape
    D = w_br.shape[2]
    tm, tn = _pick(T, 1024, SUBLANES), _pick(D, 1024)
    nj = D // tn

    def body(y_ref, w_ref, g_ref, yp_ref, m_ref, acc):
        n = pl.program_id(2)
        yp = jnp.dot(y_ref[...], w_ref[...], preferred_element_type=F32)
        yp_ref[...] = yp.astype(BF16)

        @pl.when(n == 0)
        def _():
            acc[...] = jnp.zeros_like(acc)

        acc[...] += g_ref[...].astype(F32) * yp

        @pl.when(n == NB - 1)
        def _():
            m_ref[...] = acc[...].astype(BF16)

    return pl.pallas_call(
        body, name="branch_merge", grid=(T // tm, nj, NB),
        in_specs=[pl.BlockSpec((None, tm, W), lambda i, j, n: (n, i, 0)),
                  pl.BlockSpec((None, W, tn), lambda i, j, n: (n, 0, j)),
                  pl.BlockSpec((tm, tn), lambda i, j, n: (i, n * nj + j))],
        out_specs=[pl.BlockSpec((None, tm, tn), lambda i, j, n: (n, i, j)),
                   pl.BlockSpec((tm, tn), lambda i, j, n: (i, j))],
        out_shape=[_sds((NB, T, D), BF16), _sds((T, D), BF16)],
        scratch_shapes=[pltpu.VMEM((tm, tn), F32)],
        compiler_params=_cp("parallel", "parallel", "arbitrary"))(ys, w_br, gates)


def _merge_bwd(dmerged, gates, yproj):
    NB, T, D = yproj.shape
    tm = _pick(T, 256, SUBLANES)

    def body(dm_ref, g_ref, yp_ref, dyp_ref, dz_ref, db_ref):
        dm, g = dm_ref[...], g_ref[...].astype(F32)
        dyp_ref[...] = (dm * g).astype(BF16)
        dz = dm * yp_ref[...].astype(F32) * g * (1.0 - g)
        dz_ref[...] = dz.astype(BF16)
        _acc_rows(db_ref, 0, jnp.sum(dz, axis=0, keepdims=True), pl.program_id(1) == 0)

    return pl.pallas_call(
        body, name="merge_bwd", grid=(NB, T // tm),
        in_specs=[pl.BlockSpec((tm, D), lambda n, i: (i, 0)), pl.BlockSpec((tm, D), lambda n, i: (i, n)),
                  pl.BlockSpec((None, tm, D), lambda n, i: (n, i, 0))],
        out_specs=[pl.BlockSpec((None, tm, D), lambda n, i: (n, i, 0)), pl.BlockSpec((tm, D), lambda n, i: (i, n)),
                   pl.BlockSpec((SUBLANES, D), lambda n, i: (0, n))],
        out_shape=[_sds((NB, T, D), BF16), _sds((T, NB * D), BF16), _sds((SUBLANES, NB * D), F32)],
        compiler_params=_cp("parallel", "arbitrary"))(dmerged, gates, yproj)


def _branch_bwd_dy(dyp, w_br):
    NB, T, D = dyp.shape
    W = w_br.shape[1]
    tm = _pick(T, 1024, SUBLANES)

    def body(a_ref, w_ref, o_ref):
        o_ref[...] = lax.dot_general(a_ref[...], w_ref[...], NT, preferred_element_type=F32)

    return pl.pallas_call(body, name="branch_bwd_dy", grid=(NB, T // tm),
                          in_specs=[pl.BlockSpec((None, tm, D), lambda n, i: (n, i, 0)),
                                    pl.BlockSpec((None, W, D), lambda n, i: (n, 0, 0))],
                          out_specs=pl.BlockSpec((None, tm, W), lambda n, i: (n, i, 0)),
                          out_shape=_sds((NB, T, W), F32), compiler_params=_cp("parallel", "parallel"))(dyp, w_br)


def _branch_bwd_dw(ys, dyp):
    NB, T, W = ys.shape
    D = dyp.shape[2]
    tm, tn, tk = _pick(W, 1024), _pick(D, 1024), _pick(T, 2048, SUBLANES)
    return _mm("branch_bwd_dw", ys, dyp, grid=(NB, W // tm, D // tn, T // tk), kaxis=3, dims=TN,
               a_spec=pl.BlockSpec((None, tk, tm), lambda n, i, j, k: (n, k, i)),
               b_spec=pl.BlockSpec((None, tk, tn), lambda n, i, j, k: (n, k, j)),
               acc_shape=(tm, tn), out_shape=_sds((NB, W, D), BF16),
               out_specs=pl.BlockSpec((None, tm, tn), lambda n, i, j, k: (n, i, j)),
               epilogue=_store(BF16), sem=("parallel", "parallel", "parallel", "arbitrary"))


def _layer_fwd(x, wl, sp):
    W = sp["conv_b"].shape[-1]
    D = x.shape[1]
    h = _rmsnorm_fwd(x, sp["norm_g"])
    proj = _mm_nn("proj", h, wl["w_in"], F32, tn=768)
    if "late" in wl:
        wl.update(wl.pop("late")(proj))

    def gate_ep(acc, ex, outs):
        outs[0][...] = _sigmoid(acc + ex[0][...]).astype(BF16)

    tn_g = _pick(4 * D, 1024)
    gates = _mm_nn("gates", h, wl["w_gate"], BF16, epilogue=gate_ep, extras=(sp["b_gate"],),
                   extra_specs=(pl.BlockSpec((1, tn_g), lambda i, j, k: (0, j)),), tn=tn_g)
    ys = lax.empty((4, x.shape[0], W), BF16)
    oa, ys, sb_carries = _sb_fwd(proj, W, ys)
    cpre, ys = _conv_fwd(proj, wl["conv_w"], wl["taps"], sp["conv_b"], sp["conv_ln_g"], sp["conv_ln_b"], W, ys)
    ys = _sgu_fwd(proj, sp["sgu_w"], sp["sgu_bt"], sp["sgu_ln_g"], sp["sgu_ln_b"], W, ys)
    os_, ls_ = zip(*[_dil_fwd_group(proj, W, gi, dil) for gi, (_, dil) in enumerate(DIL_PATTERNS)])
    od, lse, ys = _dil_combine(proj, os_, ls_, W, ys)
    yproj, merged = _branch_merge(ys, wl["w_br"], gates)

    def res_ep(acc, ex, outs):
        outs[0][...] = ex[0][...] + acc

    tm_o, tn_o = _pick(x.shape[0], 1024, SUBLANES), _pick(D, 1024)
    xn = _mm_nn("out_proj", merged, wl["w_out"], F32, epilogue=res_ep, extras=(x,),
                extra_specs=(pl.BlockSpec((tm_o, tn_o), lambda i, j, k: (i, j)),), tm=tm_o, tn=tn_o)
    saved = dict(x=x, h=h, proj=proj, gates=gates, oa=oa, sb_carries=sb_carries, cpre=cpre, od=od, lse=lse, ys=ys, yproj=yproj, merged=merged)
    return xn, saved


def _layer_bwd(dout, sv, wl, sp):
    W = sp["conv_b"].shape[-1]
    proj = sv["proj"]
    dmerged = _mm_nt("out_proj_bwd_dx", dout, wl["w_out"], F32)
    g_w_out = _mm_tn("out_proj_bwd_dw", sv["merged"], dout)
    dyp, dzg, db_gate = _merge_bwd(dmerged, sv["gates"], sv["yproj"])
    dy = _branch_bwd_dy(dyp, wl["w_br"])
    g_w_br = _branch_bwd_dw(sv["ys"], dyp)
    g_w_gate = _mm_tn("gate_bwd_dw", sv["h"], dzg, tk=4096)

    a_dq, a_dk, a_dv, a_dg = _sb_bwd(proj, sv["oa"], sv["sb_carries"], dy, W)
    a_qkv = _to_bf16("sb_bwd_cast", [a_dq, a_dk, a_dv])

    dc, b_dg, conv_stats = _conv_bwd_ln(proj, sv["cpre"], dy, sp["conv_ln_g"], sp["conv_ln_b"], W)
    b_da, b_db, g_conv_w = _conv_bwd_taps(proj, dc, wl["conv_w"], wl["taps"], W)

    c_du, c_dv, c_dg, g_sgu_w, g_sgu_bt, sgu_stats = _sgu_bwd(proj, dy, sp["sgu_w"], sp["sgu_bt"],
                                                              sp["sgu_ln_g"], sp["sgu_ln_b"], W)

    do, delta, d_dg = _dil_bwd_pre(proj, sv["od"], dy, W)
    dqs, dks, dvs = zip(*[_dil_bwd_group(proj, do, sv["lse"], delta, W, gi, dil)
                          for gi, (_, dil) in enumerate(DIL_PATTERNS)])
    d_qk = _to_bf16("dil_bwd_cast", [*dqs, *dks])
    d_dv = _sum3_bf16(*dvs)

    dproj = jnp.concatenate([a_qkv, a_dg, b_da, b_db, b_dg, c_du, c_dv, c_dg, d_qk, d_dv, d_dg], axis=1)
    g_w_in = _mm_tn("proj_bwd_dw", sv["h"], dproj, tn=768, tk=4096)
    dh = _mm_nt("gate_bwd_dh", dzg, wl["w_gate"], F32)
    dh = _mm_nt("proj_bwd_dh", dproj, wl["w_in"], F32, addend=dh)
    dx, dnorm = _rmsnorm_bwd(sv["x"], dh, dout, sp["norm_g"])

    K = wl["taps"]
    small = dict(norm_g=dnorm[0], conv_w=g_conv_w[:K], conv_b=conv_stats[2], conv_ln_g=conv_stats[0],
                 conv_ln_b=conv_stats[1], sgu_ln_g=sgu_stats[0], sgu_ln_b=sgu_stats[1], sgu_w=g_sgu_w,
                 sgu_b=g_sgu_bt.T, b_gate=db_gate[0])
    big = dict(w_in=g_w_in, w_gate=g_w_gate, w_br=g_w_br, w_out=g_w_out)
    return dx, big, small


HBM = pl.BlockSpec(memory_space=pl.ANY)


def _mesh_pos():
    return lax.axis_index("x"), lax.axis_index("y"), lax.axis_index("c")


def _other_chips(x, y):
    return [(1 - x, y), (x, 1 - y), (1 - x, 1 - y)]


def _shard_of(ref, axis, size, index):
    idx = [slice(None)] * len(ref.shape)
    idx[axis] = pl.ds(index * size, size)
    return ref.at[tuple(idx)]


def _all_gather(name, shards, axes):
    n = len(shards)
    out_shape = []
    for s, ax in zip(shards, axes):
        shp = list(s.shape)
        shp[ax] *= N_DEV
        out_shape.append(_sds(shp, s.dtype))

    def body(*refs):
        ins, outs = refs[:n], refs[n:2 * n]
        send, recv, lsem = refs[2 * n:]
        x, y, c = _mesh_pos()
        me, sib = (x, y, c), (x, y, 1 - c)
        chips = _other_chips(x, y)
        dev = lambda px, py, pc: 4 * px + 2 * py + pc

        def blk(a, d):
            return _shard_of(outs[a], axes[a], ins[a].shape[axes[a]], d)

        def cp(a, k, d, to, src=None):
            return pltpu.make_async_remote_copy(src_ref=blk(a, d) if src is None else src, dst_ref=blk(a, d),
                                                send_sem=send.at[a * 7 + k], recv_sem=recv.at[a * 7 + k],
                                                device_id=to, device_id_type=MESH)

        own = [pltpu.make_async_copy(ins[a], blk(a, dev(*me)), lsem.at[a]) for a in range(n)]
        for o in own:
            o.start()
        first = []
        for a in range(n):
            first.append(cp(a, 0, dev(*me), sib, src=ins[a]))
            first += [cp(a, 1 + j, dev(*me), (*ch, c), src=ins[a]) for j, ch in enumerate(chips)]
        for f in first:
            f.start()
        passed = []
        for j, ch in enumerate(chips):
            for a in range(n):
                cp(a, 1 + j, dev(*ch, c), me).wait_recv()
                p = cp(a, 4 + j, dev(*ch, c), sib)
                p.start()
                passed.append(p)
        for a in range(n):
            cp(a, 0, dev(*sib), me).wait_recv()
            for j, ch in enumerate(chips):
                cp(a, 4 + j, dev(*ch, 1 - c), me).wait_recv()
        for f in first + passed:
            f.wait_send()
        for o in own:
            o.wait()

    return pl.pallas_call(body, name=name, in_specs=[HBM] * n, out_specs=[HBM] * n, out_shape=out_shape,
                          scratch_shapes=[pltpu.SemaphoreType.DMA((7 * n,)), pltpu.SemaphoreType.DMA((7 * n,)),
                                          pltpu.SemaphoreType.DMA((n,))])(*shards)


HBM_ONLY = pl.BlockSpec(memory_space=pltpu.HBM)
SEM_SPEC = pl.BlockSpec(memory_space=pltpu.SEMAPHORE)
N_PEER = N_DEV - 1


def _peer(x, y, c, m):
    flip = lambda v, bit: 1 - v if bit else v
    return flip(x, (m >> 2) & 1), flip(y, (m >> 1) & 1), flip(c, m & 1)


def _exchange_copies(kind, srcs, lands, send, recv, axes):
    x, y, c = _mesh_pos()
    me = 4 * x + 2 * y + c
    n = len(srcs)
    remote, local = [], []
    for a in range(n):
        for m in range(1, N_DEV):
            px, py, pc = _peer(x, y, c, m)
            if kind == "gather":
                src = srcs[a]
                dst = _shard_of(lands[a], axes[a], srcs[a].shape[axes[a]], me)
            else:
                src = _shard_of(srcs[a], axes[a], srcs[a].shape[axes[a]] // N_DEV, 4 * px + 2 * py + pc)
                dst = lands[a].at[me]
            remote.append(pltpu.make_async_remote_copy(
                src_ref=src, dst_ref=dst, send_sem=send.at[a * N_PEER + m - 1], recv_sem=recv.at[a * N_PEER + m - 1],
                device_id=(px, py, pc), device_id_type=MESH))
        if kind == "gather":
            local.append(pltpu.make_async_copy(srcs[a], _shard_of(lands[a], axes[a], srcs[a].shape[axes[a]], me),
                                               send.at[n * N_PEER + a]))
    return remote, local


def _exchange_start(name, kind, srcs, lands, axes, carry):
    n = len(srcs)
    hbm = lambda a: pltpu.with_memory_space_constraint(a, pltpu.HBM)

    def body(*refs):
        send, recv = refs[2 * n + 1], refs[2 * n + 2]
        remote, local = _exchange_copies(kind, refs[:n], refs[n:2 * n], send, recv, axes)
        for cp in remote + local:
            cp.start()

    thru = [*srcs, *lands, carry]
    res = pl.pallas_call(
        body, name=name,
        out_shape=(pltpu.SemaphoreType.DMA((n * N_DEV,)), pltpu.SemaphoreType.DMA((n * N_PEER,)),
                   *[pltpu.HBM(a.shape, a.dtype) for a in thru]),
        in_specs=[HBM_ONLY] * len(thru), out_specs=(SEM_SPEC, SEM_SPEC, *[HBM_ONLY] * len(thru)),
        input_output_aliases={i: 2 + i for i in range(len(thru))},
        compiler_params=pltpu.CompilerParams(has_side_effects=pltpu.SideEffectType.DATAFLOW_SIDE_EFFECTING),
    )(*[hbm(a) for a in thru])
    return res[0], res[1], list(res[2:2 + n]), list(res[2 + n:2 + 2 * n]), res[2 + 2 * n]


def _exchange_wait(name, kind, send, recv, srcs, lands, axes, after):
    n = len(srcs)

    def body(*refs):
        send_ref, recv_ref = refs[2 * n], refs[2 * n + 1]
        remote, local = _exchange_copies(kind, refs[:n], refs[n:2 * n], send_ref, recv_ref, axes)
        for cp in remote:
            cp.wait_send()
            cp.wait_recv()
        for cp in local:
            cp.wait()

    thru = [*srcs, *lands]
    res = pl.pallas_call(
        body, name=name, out_shape=tuple(pltpu.HBM(a.shape, a.dtype) for a in thru),
        in_specs=[*[HBM_ONLY] * len(thru), SEM_SPEC, SEM_SPEC, HBM], out_specs=tuple([HBM_ONLY] * len(thru)),
        input_output_aliases={i: i for i in range(len(thru))},
        compiler_params=pltpu.CompilerParams(has_side_effects=pltpu.SideEffectType.DATAFLOW_SIDE_EFFECTING),
    )(*thru, send, recv, after)
    return list(res[:n]), list(res[n:])


def _adamw_scatter(name, land, grad, axis, me, w, m, v):
    P, R, C = land.shape
    tr = _pick(R, max(SUBLANES, (1 << 18) // C // 16 * 16), 16)
    nr = R // tr

    def body(me_ref, l_ref, o_ref, w_ref, m_ref, v_ref, g_ref, d_ref, mo_ref, vo_ref):
        own = o_ref[...].astype(F32)
        g = jnp.where(me_ref[0] == 0, own, l_ref[0].astype(F32))
        for k in range(1, P):
            g = g + jnp.where(me_ref[0] == k, own, l_ref[k].astype(F32))
        mn = ADAM_B1 * m_ref[...] + (1.0 - ADAM_B1) * g
        vn = ADAM_B2 * v_ref[...] + (1.0 - ADAM_B2) * (g * g)
        m_hat = mn / (1.0 - ADAM_B1 ** ADAM_STEP)
        v_hat = vn / (1.0 - ADAM_B2 ** ADAM_STEP)
        g_ref[...] = g
        d_ref[...] = -ADAM_LR * (m_hat / (jnp.sqrt(v_hat) + ADAM_EPS) + ADAM_WD * w_ref[...])
        mo_ref[...] = mn
        vo_ref[...] = vn

    if axis == 1:
        own_spec = pl.BlockSpec((tr, C), lambda i, me_ref: (i, me_ref[0]))
    else:
        own_spec = pl.BlockSpec((tr, C), lambda i, me_ref: (me_ref[0] * nr + i, 0))
    row = pl.BlockSpec((tr, C), lambda i, me_ref: (i, 0))
    return pl.pallas_call(
        body, name=name, out_shape=[_sds((R, C), F32)] * 4,
        grid_spec=pltpu.PrefetchScalarGridSpec(
            num_scalar_prefetch=1, grid=(nr,),
            in_specs=[pl.BlockSpec((P, tr, C), lambda i, me_ref: (0, i, 0)), own_spec, row, row, row],
            out_specs=[row] * 4),
        compiler_params=_cp("parallel"))(me, land, grad, w, m, v)


def _adamw_sum(name, parts, w, m, v):
    P, R, C = parts.shape
    tr = _pick(R, max(SUBLANES, (1 << 19) // C // SUBLANES * SUBLANES), SUBLANES)

    def body(p_ref, w_ref, m_ref, v_ref, g_ref, d_ref, mo_ref, vo_ref):
        g = p_ref[0].astype(F32)
        for k in range(1, P):
            g = g + p_ref[k].astype(F32)
        mn = ADAM_B1 * m_ref[...] + (1.0 - ADAM_B1) * g
        vn = ADAM_B2 * v_ref[...] + (1.0 - ADAM_B2) * (g * g)
        m_hat = mn / (1.0 - ADAM_B1 ** ADAM_STEP)
        v_hat = vn / (1.0 - ADAM_B2 ** ADAM_STEP)
        g_ref[...] = g
        d_ref[...] = -ADAM_LR * (m_hat / (jnp.sqrt(v_hat) + ADAM_EPS) + ADAM_WD * w_ref[...])
        mo_ref[...] = mn
        vo_ref[...] = vn

    row = pl.BlockSpec((tr, C), lambda i: (i, 0))
    return pl.pallas_call(body, name=name, grid=(R // tr,),
                          in_specs=[pl.BlockSpec((P, tr, C), lambda i: (0, i, 0)), row, row, row],
                          out_specs=[row] * 4, out_shape=[_sds((R, C), F32)] * 4,
                          compiler_params=_cp("parallel"))(parts, w, m, v)


def _rows128(a, pad_rows=SUBLANES):
    flat = a.reshape(-1, LANES)
    pad = (-flat.shape[0]) % pad_rows
    return jnp.pad(flat, ((0, pad), (0, 0))) if pad else flat


SMALL = ("norm_g", "conv_b", "conv_ln_g", "conv_ln_b", "sgu_ln_g", "sgu_ln_b", "sgu_w", "sgu_b", "b_gate", "final_g")


def kernel(x, norm_g, w_in, conv_w, conv_b, conv_ln_g, conv_ln_b, sgu_ln_g, sgu_ln_b, sgu_w, sgu_b, w_branch, w_gate, b_gate, w_out, final_g, loss_target, m_norm_g, m_w_in, m_conv_w, m_conv_b, m_conv_ln_g, m_conv_ln_b, m_sgu_ln_g, m_sgu_ln_b, m_sgu_w, m_sgu_b, m_w_branch, m_w_gate, m_b_gate, m_w_out, m_final_g, v_norm_g, v_w_in, v_conv_w, v_conv_b, v_conv_ln_g, v_conv_ln_b, v_sgu_ln_g, v_sgu_ln_b, v_sgu_w, v_sgu_b, v_w_branch, v_w_gate, v_b_gate, v_w_out, v_final_g):
    L, D = norm_g.shape
    W = conv_b.shape[1]
    taps = conv_w.shape[1]
    weights = dict(norm_g=norm_g, w_in=w_in, conv_w=conv_w, conv_b=conv_b, conv_ln_g=conv_ln_g, conv_ln_b=conv_ln_b,
                   sgu_ln_g=sgu_ln_g, sgu_ln_b=sgu_ln_b, sgu_w=sgu_w, sgu_b=sgu_b, w_branch=w_branch, w_gate=w_gate,
                   b_gate=b_gate, w_out=w_out, final_g=final_g)
    mom_m = dict(norm_g=m_norm_g, w_in=m_w_in, conv_w=m_conv_w, conv_b=m_conv_b, conv_ln_g=m_conv_ln_g,
                 conv_ln_b=m_conv_ln_b, sgu_ln_g=m_sgu_ln_g, sgu_ln_b=m_sgu_ln_b, sgu_w=m_sgu_w, sgu_b=m_sgu_b,
                 w_branch=m_w_branch, w_gate=m_w_gate, b_gate=m_b_gate, w_out=m_w_out, final_g=m_final_g)
    mom_v = dict(norm_g=v_norm_g, w_in=v_w_in, conv_w=v_conv_w, conv_b=v_conv_b, conv_ln_g=v_conv_ln_g,
                 conv_ln_b=v_conv_ln_b, sgu_ln_g=v_sgu_ln_g, sgu_ln_b=v_sgu_ln_b, sgu_w=v_sgu_w, sgu_b=v_sgu_b,
                 w_branch=v_w_branch, w_gate=v_w_gate, b_gate=v_b_gate, w_out=v_w_out, final_g=v_final_g)
    me = 4 * lax.axis_index("x") + 2 * lax.axis_index("y") + lax.axis_index("c")

    gather_names = ("w_in", "w_gate", "w_br", "w_out", "conv_w")
    gather_axes = (1, 1, 2, 0, 1)

    def gather_start(tag, l, which, carry):
        shards = [w_in[l].astype(BF16), w_gate[l].astype(BF16), w_branch[l].astype(BF16), w_out[l].astype(BF16),
                  jnp.pad(conv_w[l], ((0, CONV_HALO - taps), (0, 0)))]
        shards = [shards[i] for i in which]
        axes = [gather_axes[i] for i in which]
        lands = []
        for s, ax in zip(shards, axes):
            full = list(s.shape)
            full[ax] *= N_DEV
            lands.append(lax.empty(tuple(full), s.dtype))
        return _exchange_start(f"gather_start_{tag}", "gather", shards, lands, axes, carry) + (which,)

    def gather_wait(tag, started, after):
        send, recv, shards, lands, _, which = started
        _, full = _exchange_wait(f"gather_wait_{tag}", "gather", send, recv, shards, lands,
                                 [gather_axes[i] for i in which], after)
        return {gather_names[i]: f for i, f in zip(which, full)}

    xs = x[0]
    saved, gathered, smalls = [], [], []
    first = gather_start("0a", 0, (0,), xs)
    rest = gather_start("0b", 0, (1, 2, 3, 4), first[4])
    xs = rest[4]
    wl = gather_wait("0a", first, xs)
    wl["late"] = functools.partial(gather_wait, "0b", rest)
    for l in range(L):
        wl["taps"] = taps
        if l + 1 < L:
            started = gather_start(l + 1, l + 1, (0, 1, 2, 3, 4), xs)
            xs = started[4]
        sp = dict(norm_g=norm_g[l][None], conv_b=conv_b[l][None], conv_ln_g=conv_ln_g[l][None],
                  conv_ln_b=conv_ln_b[l][None], sgu_ln_g=sgu_ln_g[l][None], sgu_ln_b=sgu_ln_b[l][None],
                  sgu_w=sgu_w[l], sgu_bt=sgu_b[l].T, b_gate=b_gate[l][None])
        xs, sv = _layer_fwd(xs, wl, sp)
        saved.append(sv)
        gathered.append(wl)
        smalls.append(sp)
        if l + 1 < L:
            wl = gather_wait(l + 1, started, xs)

    dx, d_final, loss_part = _loss_head(xs, loss_target[0], final_g[None])
    loss = lax.psum(loss_part[0, 0], ("x", "y", "c"))

    big_names = ("w_in", "w_gate", "w_branch", "w_out")
    big_axes = (1, 1, 1, 0)
    me1 = me.astype(jnp.int32).reshape(1)
    outs = {}
    small_grads = []

    def scatter_start(l, g2d, carry):
        lands = []
        for g, ax in zip(g2d, big_axes):
            blk = list(g.shape)
            blk[ax] //= N_DEV
            lands.append(lax.empty((N_DEV, *blk), g.dtype))
        return _exchange_start(f"scatter_start_{l}", "scatter", g2d, lands, big_axes, carry)

    def scatter_finish(l, started, after):
        send, recv, g2d, lands, _ = started
        g2d, lands = _exchange_wait(f"scatter_wait_{l}", "scatter", send, recv, g2d, lands, big_axes, after)
        for nm, land, g, ax in zip(big_names, lands, g2d, big_axes):
            shard2d = land.shape[1:]
            outs.setdefault(nm, {})[l] = _adamw_scatter(
                "adamw_" + nm, land, g, ax, me1, weights[nm][l].reshape(shard2d), mom_m[nm][l].reshape(shard2d),
                mom_v[nm][l].reshape(shard2d))

    pending = None
    for l in reversed(range(L)):
        dx, big, small = _layer_bwd(dx, saved[l], gathered[l], smalls[l])
        small_grads.append(small)
        if pending is not None:
            scatter_finish(l + 1, pending, dx)
        g2d = [big["w_in"], big["w_gate"], big["w_br"].reshape(-1, D), big["w_out"]]
        if l > 0:
            pending = scatter_start(l, g2d, dx)
            dx = pending[4]
        else:
            pending = scatter_start(l, g2d, d_final)
            d_final = pending[4]
    small_grads.reverse()

    sg = {nm: jnp.stack([small_grads[l][nm] for l in range(L)]) for nm in SMALL[:-1]}
    sg["final_g"] = d_final[0]
    conv_w_full = jnp.stack([small_grads[l]["conv_w"] for l in range(L)])
    segs = [_rows128(sg[nm]) for nm in SMALL] + [_rows128(conv_w_full)]
    offs = [0]
    for s in segs:
        offs.append(offs[-1] + s.shape[0])
    pack = jnp.concatenate(segs, axis=0)
    (allp,) = _all_gather("gather_small_grads", [pack[None]], [0])

    def packed(src):
        return jnp.concatenate([_rows128(src[nm]) for nm in SMALL] + [jnp.zeros_like(segs[-1])], axis=0)

    s_g, s_d, s_m, s_v = _adamw_sum("adamw_small", allp, packed(weights), packed(mom_m), packed(mom_v))
    scatter_finish(0, pending, s_g)
    big_out = {nm: [jnp.stack([outs[nm][l][q] for l in range(L)]).reshape(weights[nm].shape) for q in range(4)]
               for nm in big_names}

    def unpack(buf, i, like):
        n = like.size // LANES
        return buf[offs[i]:offs[i] + n].reshape(like.shape)

    small_out = {nm: [unpack(b, i, weights[nm]) for b in (s_g, s_d, s_m, s_v)] for i, nm in enumerate(SMALL)}
    Wc = conv_w.shape[2]
    cw_sum = lax.dynamic_slice_in_dim(unpack(s_g, len(SMALL), conv_w_full), me * Wc, Wc, axis=2)
    cshape = (L * conv_w.shape[1], Wc)
    conv_out = [o.reshape(conv_w.shape) for o in _adamw_sum(
        "adamw_conv_w", cw_sum.reshape((1,) + cshape), conv_w.reshape(cshape), m_conv_w.reshape(cshape),
        v_conv_w.reshape(cshape))]

    order = ["norm_g", "w_in", "conv_w", "conv_b", "conv_ln_g", "conv_ln_b", "sgu_ln_g", "sgu_ln_b", "sgu_w", "sgu_b",
             "w_branch", "w_gate", "b_gate", "w_out", "final_g"]
    table = dict(small_out)
    table.update(big_out)
    table["conv_w"] = conv_out
    result = [loss, dx[None]]
    for q in range(4):
        result += [table[nm][q] for nm in order]
    return tuple(result)
```

```python
import functools

import jax
import jax.numpy as jnp
from jax import lax
from jax.experimental import pallas as pl
from jax.experimental.pallas import tpu as pltpu

F32 = jnp.float32
BF16 = jnp.bfloat16
LANES = 128
SUBLANES = 8
CONV_HALO = 32
SB_QUERY_BLOCK = 512
SB_QUERY_BLOCK_BWD = 512
SB_KEY_BLOCK = 1024
DIL_STEP_ROWS = 2048
DIL_UNROLL = 16
NORM_EPS = 1e-6
NEG = -1e30
N_DEV = 8
DIL_PATTERNS = ((128, 1), (512, 4), (2048, 16))

ADAM_LR = 0.001
ADAM_B1 = 0.9
ADAM_B2 = 0.999
ADAM_EPS = 1e-08
ADAM_WD = 0.01
ADAM_STEP = 10

MESH = pl.DeviceIdType.MESH
NN = (((1,), (0,)), ((), ()))
NT = (((1,), (1,)), ((), ()))
TN = (((0,), (0,)), ((), ()))
VMEM_LIMIT = 52 << 20


def _sds(shape, dtype):
    return jax.ShapeDtypeStruct(tuple(shape), dtype)


def _cp(*sem):
    return pltpu.CompilerParams(dimension_semantics=tuple(sem), vmem_limit_bytes=VMEM_LIMIT)


def _pick(n, target, quantum=LANES):
    if n <= target:
        return n
    t = (target // quantum) * quantum
    while t >= quantum:
        if n % t == 0:
            return t
        t -= quantum
    return n


def _sigmoid(x):
    return 1.0 / (1.0 + jnp.exp(-x))


def _silu(x):
    return x * _sigmoid(x)


def _dsilu(x):
    s = _sigmoid(x)
    return s * (1.0 + x * (1.0 - s))


_GELU_K = 0.7978845608028654
_GELU_A = 0.044715


def _gelu(x):
    return 0.5 * x * (1.0 + jnp.tanh(_GELU_K * (x + _GELU_A * x * x * x)))


def _dgelu(x):
    t = jnp.tanh(_GELU_K * (x + _GELU_A * x * x * x))
    return 0.5 * (1.0 + t) + 0.5 * x * (1.0 - t * t) * _GELU_K * (1.0 + 3.0 * _GELU_A * x * x)


def _ln_stats(v):
    mu = jnp.mean(v, axis=-1, keepdims=True)
    d = v - mu
    var = jnp.mean(d * d, axis=-1, keepdims=True)
    r = lax.rsqrt(var + NORM_EPS)
    return d * r, r


def _ln_bwd(dxh, xh, r):
    return r * (dxh - jnp.mean(dxh, axis=-1, keepdims=True) - xh * jnp.mean(dxh * xh, axis=-1, keepdims=True))


def _acc_rows(ref, row, val, first):
    @pl.when(first)
    def _():
        ref[...] = jnp.zeros_like(ref)
    ref[row:row + 1, :] += val


def _mm(name, a, b, *, grid, kaxis, dims, a_spec, b_spec, acc_shape, out_shape, out_specs,
        epilogue, extras=(), extra_specs=(), sem):
    nk = grid[kaxis]
    ne = len(extras)

    def body(*refs):
        a_ref, b_ref = refs[0], refs[1]
        ex = refs[2:2 + ne]
        outs = refs[2 + ne:-1]
        acc = refs[-1]
        k = pl.program_id(kaxis)

        @pl.when(k == 0)
        def _():
            acc[...] = jnp.zeros_like(acc)

        acc[...] += lax.dot_general(a_ref[...].astype(BF16), b_ref[...].astype(BF16), dims,
                                    preferred_element_type=F32)

        @pl.when(k == nk - 1)
        def _():
            epilogue(acc[...], ex, outs)

    return pl.pallas_call(
        body, name=name, grid=grid, in_specs=[a_spec, b_spec, *extra_specs], out_specs=out_specs,
        out_shape=out_shape, scratch_shapes=[pltpu.VMEM(acc_shape, F32)], compiler_params=_cp(*sem),
    )(a, b, *extras)


def _store(dtype):
    def ep(acc, ex, outs):
        outs[0][...] = acc.astype(dtype)
    return ep


def _mm_nn(name, a, b, out_dtype, epilogue=None, extras=(), extra_specs=(), tm=1024, tn=1024):
    M, K = a.shape
    N = b.shape[1]
    tm, tn = _pick(M, tm, SUBLANES), _pick(N, tn)
    return _mm(name, a, b, grid=(M // tm, N // tn, 1), kaxis=2, dims=NN,
               a_spec=pl.BlockSpec((tm, K), lambda i, j, k: (i, 0)),
               b_spec=pl.BlockSpec((K, tn), lambda i, j, k: (0, j)),
               acc_shape=(tm, tn), out_shape=_sds((M, N), out_dtype),
               out_specs=pl.BlockSpec((tm, tn), lambda i, j, k: (i, j)),
               epilogue=epilogue or _store(out_dtype), extras=extras, extra_specs=extra_specs,
               sem=("parallel", "parallel", "arbitrary"))


def _mm_nt(name, a, b, out_dtype, addend=None, tm=1024, tn=1024, tk=2048):
    M, K = a.shape
    N = b.shape[0]
    tm, tn, tk = _pick(M, tm, SUBLANES), _pick(N, tn), _pick(K, tk)
    extras, extra_specs = (), ()
    if addend is not None:
        extras = (addend,)
        extra_specs = (pl.BlockSpec((tm, tn), lambda i, j, k: (i, j)),)

    def ep(acc, ex, outs):
        if ex:
            acc = acc + ex[0][...]
        outs[0][...] = acc.astype(out_dtype)

    return _mm(name, a, b, grid=(M // tm, N // tn, K // tk), kaxis=2, dims=NT,
               a_spec=pl.BlockSpec((tm, tk), lambda i, j, k: (i, k)),
               b_spec=pl.BlockSpec((tn, tk), lambda i, j, k: (j, k)),
               acc_shape=(tm, tn), out_shape=_sds((M, N), out_dtype),
               out_specs=pl.BlockSpec((tm, tn), lambda i, j, k: (i, j)),
               epilogue=ep, extras=extras, extra_specs=extra_specs,
               sem=("parallel", "parallel", "arbitrary"))


def _mm_tn(name, a, b, tm=1024, tn=1024, tk=2048, out_dtype=BF16):
    K, M = a.shape
    N = b.shape[1]
    tm, tn, tk = _pick(M, tm), _pick(N, tn), _pick(K, tk, SUBLANES)
    return _mm(name, a, b, grid=(M // tm, N // tn, K // tk), kaxis=2, dims=TN,
               a_spec=pl.BlockSpec((tk, tm), lambda i, j, k: (k, i)),
               b_spec=pl.BlockSpec((tk, tn), lambda i, j, k: (k, j)),
               acc_shape=(tm, tn), out_shape=_sds((M, N), out_dtype),
               out_specs=pl.BlockSpec((tm, tn), lambda i, j, k: (i, j)),
               epilogue=_store(out_dtype), sem=("parallel", "parallel", "arbitrary"))


def _rmsnorm_fwd(x, g_row):
    T, D = x.shape
    tm = _pick(T, 512, SUBLANES)

    def body(x_ref, g_ref, h_ref):
        xv = x_ref[...]
        r = lax.rsqrt(jnp.mean(xv * xv, axis=-1, keepdims=True) + NORM_EPS)
        h_ref[...] = (xv * r * g_ref[...]).astype(BF16)

    row = pl.BlockSpec((tm, D), lambda i: (i, 0))
    return pl.pallas_call(body, name="rmsnorm_fwd", grid=(T // tm,),
                          in_specs=[row, pl.BlockSpec((1, D), lambda i: (0, 0))], out_specs=row,
                          out_shape=_sds((T, D), BF16), compiler_params=_cp("parallel"))(x, g_row)


def _rmsnorm_bwd(x, dh, dout, g_row):
    T, D = x.shape
    tm = _pick(T, 256, SUBLANES)

    def body(x_ref, dh_ref, do_ref, g_ref, dx_ref, dg_ref):
        xv = x_ref[...]
        r = lax.rsqrt(jnp.mean(xv * xv, axis=-1, keepdims=True) + NORM_EPS)
        xh = xv * r
        dhv = dh_ref[...]
        dxh = dhv * g_ref[...]
        dx_ref[...] = do_ref[...] + r * (dxh - xh * jnp.mean(dxh * xh, axis=-1, keepdims=True))
        _acc_rows(dg_ref, 0, jnp.sum(dhv * xh, axis=0, keepdims=True), pl.program_id(0) == 0)

    row = pl.BlockSpec((tm, D), lambda i: (i, 0))
    return pl.pallas_call(body, name="rmsnorm_bwd", grid=(T // tm,),
                          in_specs=[row, row, row, pl.BlockSpec((1, D), lambda i: (0, 0))],
                          out_specs=[row, pl.BlockSpec((SUBLANES, D), lambda i: (0, 0))],
                          out_shape=[_sds((T, D), F32), _sds((SUBLANES, D), F32)],
                          compiler_params=_cp("arbitrary"))(x, dh, dout, g_row)


def _loss_head(x, target, g_row):
    T, D = x.shape
    tm = _pick(T, 256, SUBLANES)

    def body(x_ref, t_ref, g_ref, dx_ref, dg_ref, loss_ref):
        first = pl.program_id(0) == 0
        xv = x_ref[...]
        g = g_ref[...]
        r = lax.rsqrt(jnp.mean(xv * xv, axis=-1, keepdims=True) + NORM_EPS)
        xh = xv * r
        err = xh * g - t_ref[...]
        part = 0.5 * jnp.sum(jnp.mean(err * err, axis=-1, keepdims=True), axis=0, keepdims=True)

        @pl.when(first)
        def _():
            loss_ref[...] = jnp.zeros_like(loss_ref)

        loss_ref[...] += jnp.broadcast_to(part, loss_ref.shape)
        dy = err / D
        _acc_rows(dg_ref, 0, jnp.sum(dy * xh, axis=0, keepdims=True), first)
        dxh = dy * g
        dx_ref[...] = r * (dxh - xh * jnp.mean(dxh * xh, axis=-1, keepdims=True))

    row = pl.BlockSpec((tm, D), lambda i: (i, 0))
    return pl.pallas_call(body, name="loss_head", grid=(T // tm,),
                          in_specs=[row, row, pl.BlockSpec((1, D), lambda i: (0, 0))],
                          out_specs=[row, pl.BlockSpec((SUBLANES, D), lambda i: (0, 0)),
                                     pl.BlockSpec((SUBLANES, LANES), lambda i: (0, 0))],
                          out_shape=[_sds((T, D), F32), _sds((SUBLANES, D), F32), _sds((SUBLANES, LANES), F32)],
                          compiler_params=_cp("arbitrary"))(x, target, g_row)


MXU_DIM = 256


def _tri(cmp):
    r = lax.broadcasted_iota(jnp.int32, (MXU_DIM, MXU_DIM), 0)
    c = lax.broadcasted_iota(jnp.int32, (MXU_DIM, MXU_DIM), 1)
    same_tile = (r // LANES) == (c // LANES)
    return (same_tile & cmp(r % LANES, c % LANES)).astype(BF16)


def _tile_scans(x, tri):
    out = []
    for p in range(x.shape[1] // MXU_DIM):
        sc = jnp.dot(x[:, p * MXU_DIM:(p + 1) * MXU_DIM].astype(BF16), tri, preferred_element_type=F32)
        out += [sc[:, :LANES], sc[:, LANES:]]
    return out


def _sb_scores(qs, kb, causal):
    z = lax.dot_general(qs, kb, NT, preferred_element_type=F32)
    lb = jnp.minimum(z, 0.0) - jnp.log(1.0 + jnp.exp(-jnp.abs(z)))
    l1 = lb - z
    return lb, (l1 if causal is None else jnp.where(causal, l1, 0.0))


def _masked(causal, x):
    return x if causal is None else jnp.where(causal, x, 0.0)


def _lanes(x, s):
    return x[:, s * LANES:(s + 1) * LANES]


def _sb_fwd(proj, W, ys):
    T = proj.shape[0]
    H = W // LANES
    assert T // LANES <= LANES
    tq = _pick(T, SB_QUERY_BLOCK, LANES)
    kblk = _pick(T, SB_KEY_BLOCK, LANES)
    assert kblk % tq == 0
    nsub = kblk // LANES
    half = kblk // 2 if (kblk // 2) % MXU_DIM == 0 else kblk
    scale = LANES ** -0.5

    def body(q_ref, k_ref, v_ref, g_ref, ys_in, o_ref, y_ref, c_ref, run):
        i = pl.program_id(1)
        qs = (q_ref[...] * scale).astype(BF16)
        row = i * tq + lax.broadcasted_iota(jnp.int32, (tq, kblk), 0)
        key = lax.broadcasted_iota(jnp.int32, (tq, kblk), 1)
        col = lax.broadcasted_iota(jnp.int32, (tq, LANES), 1)
        tri = _tri(lambda r, c: r > c)
        nkb = ((i + 1) * tq + kblk - 1) // kblk
        o_ref[...] = jnp.zeros_like(o_ref)
        c_ref[...] = jnp.zeros_like(c_ref)
        run[...] = jnp.zeros_like(run)

        def make_step(diagonal, width=kblk):
            def step(jj, carry):
                j = nkb - 1 - jj
                off = pl.multiple_of(j * kblk, kblk)
                kb = k_ref[pl.ds(off, width), :].astype(BF16)
                vb = v_ref[pl.ds(off, width), :].astype(BF16)
                causal = (key[:, :width] + off < row[:, :width]) if diagonal else None
                lb, l1m = _sb_scores(qs, kb, causal)
                c_after = run[...]
                cs = c_ref[...]
                after = _tile_scans(l1m, tri)
                for s in reversed(range(width // LANES)):
                    after[s] = c_after + after[s]
                    cs = jnp.where(col == j * nsub + s, c_after, cs)
                    c_after = c_after + jnp.sum(_lanes(l1m, s), axis=1, keepdims=True)
                run[...] = c_after
                c_ref[...] = cs
                w = _masked(causal, jnp.exp(lb + jnp.concatenate(after, axis=1)))
                o_ref[...] += jnp.dot(w.astype(BF16), vb, preferred_element_type=F32)
                return carry
            return step

        reach = (i + 1) * tq - (nkb - 1) * kblk

        @pl.when(reach <= half)
        def _():
            make_step(True, half)(0, 0)

        @pl.when(reach > half)
        def _():
            make_step(True)(0, 0)

        lax.fori_loop(1, nkb, make_step(False), 0)
        y_ref[...] = (o_ref[...] * _silu(g_ref[...])).astype(BF16)

    qspec = lambda c0: pl.BlockSpec((tq, LANES), lambda h, i: (i, c0 + h))
    kvspec = lambda c0: pl.BlockSpec((T, LANES), lambda h, i: (0, c0 + h))
    out = pl.BlockSpec((tq, LANES), lambda h, i: (i, h))
    return pl.pallas_call(body, name="sb_fwd", grid=(H, T // tq),
                          in_specs=[qspec(0), kvspec(H), kvspec(2 * H), qspec(3 * H), HBM],
                          out_specs=[out, pl.BlockSpec((None, tq, LANES), lambda h, i: (0, i, h)), out],
                          out_shape=[_sds((T, W), F32), _sds(ys.shape, BF16), _sds((T, W), F32)],
                          input_output_aliases={4: 1},
                          scratch_shapes=[pltpu.VMEM((tq, LANES), F32)],
                          compiler_params=_cp("parallel", "arbitrary"))(proj, proj, proj, proj, ys)


def _sb_bwd(proj, o, carries, dy, W):
    T = proj.shape[0]
    H = W // LANES
    tq = _pick(T, SB_QUERY_BLOCK_BWD, LANES)
    kblk = _pick(T, SB_KEY_BLOCK, LANES)
    assert kblk % tq == 0
    nsub = kblk // LANES
    half = kblk // 2 if (kblk // 2) % MXU_DIM == 0 else kblk
    scale = LANES ** -0.5

    def body(q_ref, k_ref, v_ref, g_ref, o_ref, c_ref, dy_ref, dq_ref, dk_ref, dv_ref, dg_ref, run):
        i = pl.program_id(1)

        @pl.when(i == 0)
        def _():
            dk_ref[...] = jnp.zeros_like(dk_ref)
            dv_ref[...] = jnp.zeros_like(dv_ref)

        gv = g_ref[...]
        dyv = dy_ref[...]
        dg_ref[...] = (dyv * o_ref[...] * _dsilu(gv)).astype(BF16)
        dob = (dyv * _silu(gv)).astype(BF16)
        qs = (q_ref[...] * scale).astype(BF16)
        row = i * tq + lax.broadcasted_iota(jnp.int32, (tq, kblk), 0)
        key = lax.broadcasted_iota(jnp.int32, (tq, kblk), 1)
        col = lax.broadcasted_iota(jnp.int32, (tq, LANES), 1)
        tri_after = _tri(lambda r, c: r > c)
        tri_before = _tri(lambda r, c: r < c)
        dq_ref[...] = jnp.zeros_like(dq_ref)
        run[...] = jnp.zeros_like(run)

        def make_step(diagonal, width=kblk):
            def step(j, carry):
                off = pl.multiple_of(j * kblk, kblk)
                kb = k_ref[pl.ds(off, width), :].astype(BF16)
                vb = v_ref[pl.ds(off, width), :].astype(BF16)
                causal = (key[:, :width] + off < row[:, :width]) if diagonal else None
                lb, l1m = _sb_scores(qs, kb, causal)
                cs = c_ref[...]
                after = [jnp.sum(jnp.where(col == j * nsub + s, cs, 0.0), axis=1, keepdims=True) + sc
                         for s, sc in enumerate(_tile_scans(l1m, tri_after))]
                w = _masked(causal, jnp.exp(lb + jnp.concatenate(after, axis=1)))
                gw = w * lax.dot_general(dob, vb, NT, preferred_element_type=F32)
                gpre = run[...]
                before = _tile_scans(gw, tri_before)
                for s in range(width // LANES):
                    before[s] = gpre + before[s]
                    gpre = gpre + jnp.sum(_lanes(gw, s), axis=1, keepdims=True)
                run[...] = gpre
                beta = jnp.exp(lb)
                dz = _masked(causal, gw * (1.0 - beta) - jnp.concatenate(before, axis=1) * beta).astype(BF16)
                dq_ref[...] += jnp.dot(dz, kb, preferred_element_type=F32)
                dk_ref[pl.ds(off, width), :] += lax.dot_general(dz, qs, TN, preferred_element_type=F32)
                dv_ref[pl.ds(off, width), :] += lax.dot_general(w.astype(BF16), dob, TN, preferred_element_type=F32)
                return carry
            return step

        last = ((i + 1) * tq + kblk - 1) // kblk - 1
        reach = (i + 1) * tq - last * kblk
        lax.fori_loop(0, last, make_step(False), 0)

        @pl.when(reach <= half)
        def _():
            make_step(True, half)(last, 0)

        @pl.when(reach > half)
        def _():
            make_step(True)(last, 0)

        dq_ref[...] = dq_ref[...] * scale

    qspec = lambda c0: pl.BlockSpec((tq, LANES), lambda h, i: (i, c0 + h))
    kvspec = lambda c0: pl.BlockSpec((T, LANES), lambda h, i: (0, c0 + h), pipeline_mode=pl.Buffered(1))
    blk = pl.BlockSpec((tq, LANES), lambda h, i: (i, h))
    full = pl.BlockSpec((T, LANES), lambda h, i: (0, h))
    return pl.pallas_call(body, name="sb_bwd", grid=(H, T // tq),
                          in_specs=[qspec(0), kvspec(H), kvspec(2 * H), qspec(3 * H), blk, blk,
                                    pl.BlockSpec((None, tq, LANES), lambda h, i: (0, i, h))],
                          out_specs=[blk, full, full, blk],
                          out_shape=[_sds((T, W), F32), _sds((T, W), F32), _sds((T, W), F32), _sds((T, W), BF16)],
                          scratch_shapes=[pltpu.VMEM((tq, LANES), F32)],
                          compiler_params=_cp("parallel", "arbitrary"))(proj, proj, proj, proj, o, carries, dy)


def _conv_specs(W, tm):
    per = tm // CONV_HALO
    cur = lambda c: pl.BlockSpec((tm, W), lambda i: (i, c))
    prev = lambda c: pl.BlockSpec((CONV_HALO, W), lambda i: (jnp.maximum(i * per - 1, 0), c))
    return cur, prev


def _conv_fwd(proj, conv_w, K, conv_b, ln_g, ln_b, W, ys):
    T = proj.shape[0]
    tm = _pick(T, 256, CONV_HALO)
    lead = CONV_HALO - (K - 1)
    cur, prev = _conv_specs(W, tm)

    def body(a_ref, b_ref, ah_ref, bh_ref, g_ref, w_ref, cb_ref, lg_ref, lb_ref, ys_in, c_ref, y_ref, glu):
        i = pl.program_id(0)
        glu[0:CONV_HALO, :] = jnp.where(i > 0, ah_ref[...] * _sigmoid(bh_ref[...]), 0.0)
        glu[CONV_HALO:, :] = a_ref[...] * _sigmoid(b_ref[...])
        c = jnp.broadcast_to(cb_ref[...], (tm, W))
        for k in range(K):
            c = c + w_ref[k:k + 1, :] * glu[lead + k:lead + k + tm, :]
        c_ref[...] = c
        xh, _ = _ln_stats(c)
        y_ref[...] = (_silu(xh * lg_ref[...] + lb_ref[...]) * _silu(g_ref[...])).astype(BF16)

    vec = pl.BlockSpec((1, W), lambda i: (0, 0))
    row = pl.BlockSpec((tm, W), lambda i: (i, 0))
    return pl.pallas_call(body, name="conv_fwd", grid=(T // tm,),
                          in_specs=[cur(4), cur(5), prev(4), prev(5), cur(6),
                                    pl.BlockSpec((CONV_HALO, W), lambda i: (0, 0)), vec, vec, vec, HBM],
                          out_specs=[row, pl.BlockSpec((None, tm, W), lambda i: (1, i, 0))],
                          out_shape=[_sds((T, W), F32), _sds(ys.shape, BF16)], input_output_aliases={9: 1},
                          scratch_shapes=[pltpu.VMEM((tm + CONV_HALO, W), F32)],
                          compiler_params=_cp("parallel"))(proj, proj, proj, proj, proj, conv_w, conv_b, ln_g, ln_b, ys)


def _conv_bwd_ln(proj, c, dy, ln_g, ln_b, W):
    T = proj.shape[0]
    tm = _pick(T, 256, SUBLANES)

    def body(g_ref, c_ref, dy_ref, lg_ref, lb_ref, dc_ref, dg_ref, st_ref):
        first = pl.program_id(0) == 0
        gv = g_ref[...]
        dyv = dy_ref[...]
        xh, r = _ln_stats(c_ref[...])
        lg = lg_ref[...]
        ln = xh * lg + lb_ref[...]
        dg_ref[...] = (dyv * _silu(ln) * _dsilu(gv)).astype(BF16)
        dln = dyv * _silu(gv) * _dsilu(ln)
        dc = _ln_bwd(dln * lg, xh, r)
        dc_ref[...] = dc
        _acc_rows(st_ref, 0, jnp.sum(dln * xh, axis=0, keepdims=True), first)
        st_ref[1:2, :] += jnp.sum(dln, axis=0, keepdims=True)
        st_ref[2:3, :] += jnp.sum(dc, axis=0, keepdims=True)

    row = pl.BlockSpec((tm, W), lambda i: (i, 0))
    vec = pl.BlockSpec((1, W), lambda i: (0, 0))
    return pl.pallas_call(body, name="conv_bwd_ln", grid=(T // tm,),
                          in_specs=[pl.BlockSpec((tm, W), lambda i: (i, 6)), row,
                                    pl.BlockSpec((None, tm, W), lambda i: (1, i, 0)), vec, vec],
                          out_specs=[row, row, pl.BlockSpec((SUBLANES, W), lambda i: (0, 0))],
                          out_shape=[_sds((T, W), F32), _sds((T, W), BF16), _sds((SUBLANES, W), F32)],
                          compiler_params=_cp("arbitrary"))(proj, c, dy, ln_g, ln_b)


def _conv_bwd_taps(proj, dc, conv_w, K, W):
    T = proj.shape[0]
    tm = _pick(T, 256, CONV_HALO)
    lead = CONV_HALO - (K - 1)
    per = tm // CONV_HALO
    nblk = T // tm
    cur, prev = _conv_specs(W, tm)

    def body(a_ref, b_ref, ah_ref, bh_ref, dc_ref, dcn_ref, w_ref, da_ref, db_ref, dw_ref, glu, dcs):
        i = pl.program_id(0)
        av = a_ref[...]
        sb = _sigmoid(b_ref[...])
        glu[0:CONV_HALO, :] = jnp.where(i > 0, ah_ref[...] * _sigmoid(bh_ref[...]), 0.0)
        glu[CONV_HALO:, :] = av * sb
        dcv = dc_ref[...]
        dcs[0:tm, :] = dcv
        dcs[tm:, :] = jnp.where(i < nblk - 1, dcn_ref[...], 0.0)

        @pl.when(i == 0)
        def _():
            dw_ref[...] = jnp.zeros_like(dw_ref)

        dglu = jnp.zeros((tm, W), F32)
        for k in range(K):
            dglu = dglu + w_ref[k:k + 1, :] * dcs[K - 1 - k:K - 1 - k + tm, :]
            dw_ref[k:k + 1, :] += jnp.sum(dcv * glu[lead + k:lead + k + tm, :], axis=0, keepdims=True)
        da_ref[...] = (dglu * sb).astype(BF16)
        db_ref[...] = (dglu * av * sb * (1.0 - sb)).astype(BF16)

    row = pl.BlockSpec((tm, W), lambda i: (i, 0))
    nxt = pl.BlockSpec((CONV_HALO, W), lambda i: (jnp.minimum((i + 1) * per, T // CONV_HALO - 1), 0))
    return pl.pallas_call(body, name="conv_bwd_taps", grid=(nblk,),
                          in_specs=[cur(4), cur(5), prev(4), prev(5), row, nxt,
                                    pl.BlockSpec((CONV_HALO, W), lambda i: (0, 0))],
                          out_specs=[row, row, pl.BlockSpec((CONV_HALO, W), lambda i: (0, 0))],
                          out_shape=[_sds((T, W), BF16), _sds((T, W), BF16), _sds((CONV_HALO, W), F32)],
                          scratch_shapes=[pltpu.VMEM((tm + CONV_HALO, W), F32), pltpu.VMEM((tm + CONV_HALO, W), F32)],
                          compiler_params=_cp("arbitrary"))(proj, proj, proj, proj, dc, dc, conv_w)


def _sgu_common(cu, cv, lg, lb, w_ref, bt_ref, z_scr, G, nch):
    u = _gelu(cu)
    xh, r = _ln_stats(_gelu(cv))
    vn = (xh * lg + lb).astype(BF16)
    rr = lax.broadcasted_iota(jnp.int32, (LANES, LANES), 0)
    cc = lax.broadcasted_iota(jnp.int32, (LANES, LANES), 1)
    tril = rr >= cc
    wts = [jnp.where(tril, w_ref[g], 0.0).astype(BF16) for g in range(G)]
    for ch in range(nch):
        rs = slice(ch * LANES, (ch + 1) * LANES)
        for g in range(G):
            cs = slice(g * LANES, (g + 1) * LANES)
            z_scr[rs, cs] = jnp.dot(wts[g], vn[rs, cs], preferred_element_type=F32) + bt_ref[:, g:g + 1]
    return u, xh, r, vn, wts, tril


def _sgu_fwd(proj, sgu_w, sgu_bt, ln_g, ln_b, W, ys):
    T = proj.shape[0]
    G = W // LANES
    tm = _pick(T, 256, LANES)
    nch = tm // LANES

    def body(u_ref, v_ref, g_ref, w_ref, bt_ref, lg_ref, lb_ref, ys_in, y_ref, z_scr):
        u, *_ = _sgu_common(u_ref[...], v_ref[...], lg_ref[...], lb_ref[...], w_ref, bt_ref, z_scr, G, nch)
        y_ref[...] = (u * z_scr[...] * _silu(g_ref[...])).astype(BF16)

    cur = lambda c: pl.BlockSpec((tm, W), lambda i: (i, c))
    vec = pl.BlockSpec((1, W), lambda i: (0, 0))
    return pl.pallas_call(body, name="sgu_fwd", grid=(T // tm,),
                          in_specs=[cur(7), cur(8), cur(9), pl.BlockSpec((G, LANES, LANES), lambda i: (0, 0, 0)),
                                    pl.BlockSpec((LANES, G), lambda i: (0, 0)), vec, vec, HBM],
                          out_specs=pl.BlockSpec((None, tm, W), lambda i: (2, i, 0)), out_shape=_sds(ys.shape, BF16),
                          input_output_aliases={7: 0}, scratch_shapes=[pltpu.VMEM((tm, W), F32)],
                          compiler_params=_cp("parallel"))(proj, proj, proj, sgu_w, sgu_bt, ln_g, ln_b, ys)


def _sgu_bwd(proj, dy, sgu_w, sgu_bt, ln_g, ln_b, W):
    T = proj.shape[0]
    G = W // LANES
    tm = _pick(T, 256, LANES)
    nch = tm // LANES

    def body(u_ref, v_ref, g_ref, dy_ref, w_ref, bt_ref, lg_ref, lb_ref,
             du_ref, dv_ref, dg_ref, dw_ref, dbt_ref, st_ref, z_scr, dvn_scr):
        first = pl.program_id(0) == 0
        cu, cv, gv, dyv = u_ref[...], v_ref[...], g_ref[...], dy_ref[...]
        lg = lg_ref[...]
        u, xh, r, vn, wts, tril = _sgu_common(cu, cv, lg, lb_ref[...], w_ref, bt_ref, z_scr, G, nch)
        z = z_scr[...]
        sg = _silu(gv)
        dg_ref[...] = (dyv * u * z * _dsilu(gv)).astype(BF16)
        du_ref[...] = (dyv * z * sg * _dgelu(cu)).astype(BF16)
        dz = dyv * u * sg
        dzb = dz.astype(BF16)

        @pl.when(first)
        def _():
            dw_ref[...] = jnp.zeros_like(dw_ref)
            dbt_ref[...] = jnp.zeros_like(dbt_ref)

        for g in range(G):
            cs = slice(g * LANES, (g + 1) * LANES)
            dwg = jnp.zeros((LANES, LANES), F32)
            dbg = jnp.zeros((LANES, 1), F32)
            for ch in range(nch):
                rs = slice(ch * LANES, (ch + 1) * LANES)
                dwg = dwg + lax.dot_general(dzb[rs, cs], vn[rs, cs], NT, preferred_element_type=F32)
                dbg = dbg + jnp.sum(dz[rs, cs], axis=1, keepdims=True)
                dvn_scr[rs, cs] = lax.dot_general(wts[g], dzb[rs, cs], TN, preferred_element_type=F32)
            dw_ref[g] += jnp.where(tril, dwg, 0.0)
            dbt_ref[:, g:g + 1] += dbg
        dvn = dvn_scr[...]
        _acc_rows(st_ref, 0, jnp.sum(dvn * xh, axis=0, keepdims=True), first)
        st_ref[1:2, :] += jnp.sum(dvn, axis=0, keepdims=True)
        dv_ref[...] = (_ln_bwd(dvn * lg, xh, r) * _dgelu(cv)).astype(BF16)

    cur = lambda c: pl.BlockSpec((tm, W), lambda i: (i, c))
    row = pl.BlockSpec((tm, W), lambda i: (i, 0))
    vec = pl.BlockSpec((1, W), lambda i: (0, 0))
    wspec = pl.BlockSpec((G, LANES, LANES), lambda i: (0, 0, 0))
    bspec = pl.BlockSpec((LANES, G), lambda i: (0, 0))
    return pl.pallas_call(body, name="sgu_bwd", grid=(T // tm,),
                          in_specs=[cur(7), cur(8), cur(9), pl.BlockSpec((None, tm, W), lambda i: (2, i, 0)),
                                    wspec, bspec, vec, vec],
                          out_specs=[row, row, row, wspec, bspec, pl.BlockSpec((SUBLANES, W), lambda i: (0, 0))],
                          out_shape=[_sds((T, W), BF16)] * 3 + [_sds((G, LANES, LANES), F32), _sds((LANES, G), F32),
                                                                 _sds((SUBLANES, W), F32)],
                          scratch_shapes=[pltpu.VMEM((tm, W), F32), pltpu.VMEM((tm, W), F32)],
                          compiler_params=_cp("arbitrary"))(proj, proj, proj, dy, sgu_w, sgu_bt, ln_g, ln_b)


def _rows(start, dil):
    return pl.ds(start, LANES, stride=dil) if dil > 1 else pl.ds(start, LANES)


def _dil_masks():
    a = lax.broadcasted_iota(jnp.int32, (LANES, LANES), 0)
    c = lax.broadcasted_iota(jnp.int32, (LANES, LANES), 1)
    return c <= a, c >= a


def _dil_geometry(T, dil):
    sbr = LANES * dil
    nb = max(1, min(DIL_STEP_ROWS, T) // sbr)
    return sbr, nb, T // (sbr * nb)


def _for_units(nb, dil, unit):
    for blk in range(nb):
        if dil <= DIL_UNROLL:
            for r in range(dil):
                unit(blk, r)
        else:
            def chunk(it, carry):
                for u in range(DIL_UNROLL):
                    unit(blk, it * DIL_UNROLL + u)
                return carry
            lax.fori_loop(0, dil // DIL_UNROLL, chunk, 0)


def _dil_fwd_group(proj, W, gi, dil):
    T = proj.shape[0]
    H = W // LANES
    sbr, nb, nsteps = _dil_geometry(T, dil)
    scale = LANES ** -0.5
    cq, ck, cv = (10 + gi) * H, (13 + gi) * H, 16 * H

    def body(q_ref, kc_ref, kp_ref, vc_ref, vp_ref, o_ref, l_ref):
        b = pl.program_id(1)
        m_cur, m_prev = _dil_masks()
        m_first = m_prev & (b > 0)

        def unit(blk, r):
            sl = _rows(blk * sbr + r, dil)
            if blk == 0:
                kp, vp, mp = kp_ref[_rows(r, dil), :], vp_ref[_rows(r, dil), :], m_first
            else:
                sp_ = _rows((blk - 1) * sbr + r, dil)
                kp, vp, mp = kc_ref[sp_, :], vc_ref[sp_, :], m_prev
            q = (q_ref[sl, :] * scale).astype(BF16)
            sc = lax.dot_general(q, kc_ref[sl, :].astype(BF16), NT, preferred_element_type=F32)
            sp = lax.dot_general(q, kp.astype(BF16), NT, preferred_element_type=F32)
            sc = jnp.where(m_cur, sc, NEG)
            sp = jnp.where(mp, sp, NEG)
            m = jnp.maximum(jnp.max(sc, axis=1, keepdims=True), jnp.max(sp, axis=1, keepdims=True))
            pc = jnp.exp(sc - m)
            pp = jnp.exp(sp - m)
            den = jnp.sum(pc, axis=1, keepdims=True) + jnp.sum(pp, axis=1, keepdims=True)
            pv = (jnp.dot(pc.astype(BF16), vc_ref[sl, :].astype(BF16), preferred_element_type=F32)
                  + jnp.dot(pp.astype(BF16), vp.astype(BF16), preferred_element_type=F32))
            o_ref[sl, :] = pv / den
            l_ref[sl, :] = jnp.broadcast_to(m + jnp.log(den), (LANES, LANES))

        _for_units(nb, dil, unit)

    cur = lambda c0: pl.BlockSpec((sbr * nb, LANES), lambda h, b: (b, c0 + h))
    prv = lambda c0: pl.BlockSpec((sbr, LANES), lambda h, b: (jnp.maximum(b * nb - 1, 0), c0 + h))
    out = pl.BlockSpec((sbr * nb, LANES), lambda h, b: (b, h))
    return pl.pallas_call(body, name=f"dil_fwd_g{gi}", grid=(H, nsteps),
                          in_specs=[cur(cq), cur(ck), prv(ck), cur(cv), prv(cv)], out_specs=[out, out],
                          out_shape=[_sds((T, W), F32), _sds((T, W), F32)],
                          compiler_params=_cp("parallel", "parallel"))(proj, proj, proj, proj, proj)


def _dil_combine(proj, os_, ls_, W, ys):
    T = proj.shape[0]
    tm = _pick(T, 256, SUBLANES)

    def body(g_ref, o0, o1, o2, l0, l1, l2, ys_in, od_ref, lse_ref, y_ref):
        a0, a1, a2 = l0[...], l1[...], l2[...]
        m = jnp.maximum(jnp.maximum(a0, a1), a2)
        e0, e1, e2 = jnp.exp(a0 - m), jnp.exp(a1 - m), jnp.exp(a2 - m)
        s = e0 + e1 + e2
        od = (e0 / s) * o0[...] + (e1 / s) * o1[...] + (e2 / s) * o2[...]
        od_ref[...] = od
        lse_ref[...] = m + jnp.log(s)
        y_ref[...] = (od * _silu(g_ref[...])).astype(BF16)

    row = pl.BlockSpec((tm, W), lambda i: (i, 0))
    return pl.pallas_call(body, name="dil_combine", grid=(T // tm,),
                          in_specs=[pl.BlockSpec((tm, W), lambda i: (i, 17))] + [row] * 6 + [HBM],
                          out_specs=[row, row, pl.BlockSpec((None, tm, W), lambda i: (3, i, 0))],
                          out_shape=[_sds((T, W), F32), _sds((T, W), F32), _sds(ys.shape, BF16)],
                          input_output_aliases={7: 2},
                          compiler_params=_cp("parallel"))(proj, *os_, *ls_, ys)


def _dil_bwd_pre(proj, od, dy, W):
    T = proj.shape[0]
    H = W // LANES
    tm = _pick(T, 512, SUBLANES)

    def body(g_ref, od_ref, dy_ref, do_ref, dl_ref, dg_ref):
        gv, odv, dyv = g_ref[...], od_ref[...], dy_ref[...]
        do = dyv * _silu(gv)
        do_ref[...] = do
        dl_ref[...] = jnp.broadcast_to(jnp.sum(do * odv, axis=1, keepdims=True), (tm, LANES))
        dg_ref[...] = (dyv * odv * _dsilu(gv)).astype(BF16)

    blk = pl.BlockSpec((tm, LANES), lambda i, h: (i, h))
    return pl.pallas_call(body, name="dil_bwd_pre", grid=(T // tm, H),
                          in_specs=[pl.BlockSpec((tm, LANES), lambda i, h: (i, 17 * H + h)), blk,
                                    pl.BlockSpec((None, tm, LANES), lambda i, h: (3, i, h))],
                          out_specs=[blk, blk, blk],
                          out_shape=[_sds((T, W), F32), _sds((T, W), F32), _sds((T, W), BF16)],
                          compiler_params=_cp("parallel", "parallel"))(proj, od, dy)


def _dil_bwd_group(proj, do, lse, delta, W, gi, dil):
    T = proj.shape[0]
    H = W // LANES
    sbr, nb, nsteps = _dil_geometry(T, dil)
    scale = LANES ** -0.5
    cq, ck, cv = (10 + gi) * H, (13 + gi) * H, 16 * H

    def body(qc_ref, qn_ref, kc_ref, kp_ref, vc_ref, vp_ref, doc_ref, don_ref, lc_ref, ln_ref, dc_ref, dn_ref,
             dq_ref, dk_ref, dv_ref):
        b = pl.program_id(1)
        m_cur, m_prev = _dil_masks()
        m_first = m_prev & (b > 0)
        m_last = m_prev & (b < nsteps - 1)

        def probs(q, k, mask, l):
            s = lax.dot_general(q, k, NT, preferred_element_type=F32)
            return jnp.exp(jnp.where(mask, s - l, NEG))

        def unit(blk, r):
            sl = _rows(blk * sbr + r, dil)
            if blk == 0:
                edge = _rows(r, dil)
                k_p, v_p, m_cp = kp_ref[edge, :], vp_ref[edge, :], m_first
            else:
                sp_ = _rows((blk - 1) * sbr + r, dil)
                k_p, v_p, m_cp = kc_ref[sp_, :], vc_ref[sp_, :], m_prev
            if blk == nb - 1:
                edge = _rows(r, dil)
                q_n, do_n, l_n, d_n, m_nc = qn_ref[edge, :], don_ref[edge, :], ln_ref[edge, :], dn_ref[edge, :], m_last
            else:
                sn_ = _rows((blk + 1) * sbr + r, dil)
                q_n, do_n, l_n, d_n, m_nc = qc_ref[sn_, :], doc_ref[sn_, :], lc_ref[sn_, :], dc_ref[sn_, :], m_prev
            q_c = (qc_ref[sl, :] * scale).astype(BF16)
            q_n = (q_n * scale).astype(BF16)
            k_c, k_p = kc_ref[sl, :].astype(BF16), k_p.astype(BF16)
            v_c, v_p = vc_ref[sl, :].astype(BF16), v_p.astype(BF16)
            do_c, do_n = doc_ref[sl, :].astype(BF16), do_n.astype(BF16)
            l_c, d_c = lc_ref[sl, :], dc_ref[sl, :]
            p_cc = probs(q_c, k_c, m_cur, l_c)
            p_cp = probs(q_c, k_p, m_cp, l_c)
            p_nc = probs(q_n, k_c, m_nc, l_n)
            ds_cc = (p_cc * (lax.dot_general(do_c, v_c, NT, preferred_element_type=F32) - d_c)).astype(BF16)
            ds_cp = (p_cp * (lax.dot_general(do_c, v_p, NT, preferred_element_type=F32) - d_c)).astype(BF16)
            ds_nc = (p_nc * (lax.dot_general(do_n, v_c, NT, preferred_element_type=F32) - d_n)).astype(BF16)
            dq = (jnp.dot(ds_cc, k_c, preferred_element_type=F32) + jnp.dot(ds_cp, k_p, preferred_element_type=F32))
            dk = (lax.dot_general(ds_cc, q_c, TN, preferred_element_type=F32)
                  + lax.dot_general(ds_nc, q_n, TN, preferred_element_type=F32))
            dv = (lax.dot_general(p_cc.astype(BF16), do_c, TN, preferred_element_type=F32)
                  + lax.dot_general(p_nc.astype(BF16), do_n, TN, preferred_element_type=F32))
            dq_ref[sl, :] = dq * scale
            dk_ref[sl, :] = dk
            dv_ref[sl, :] = dv

        _for_units(nb, dil, unit)

    cur = lambda c0: pl.BlockSpec((sbr * nb, LANES), lambda h, b: (b, c0 + h))
    prv = lambda c0: pl.BlockSpec((sbr, LANES), lambda h, b: (jnp.maximum(b * nb - 1, 0), c0 + h))
    nxt = lambda c0: pl.BlockSpec((sbr, LANES), lambda h, b: (jnp.minimum((b + 1) * nb, T // sbr - 1), c0 + h))
    out = pl.BlockSpec((sbr * nb, LANES), lambda h, b: (b, h))
    return pl.pallas_call(body, name=f"dil_bwd_g{gi}", grid=(H, nsteps),
                          in_specs=[cur(cq), nxt(cq), cur(ck), prv(ck), cur(cv), prv(cv),
                                    cur(0), nxt(0), cur(0), nxt(0), cur(0), nxt(0)],
                          out_specs=[out, out, out], out_shape=[_sds((T, W), F32)] * 3,
                          compiler_params=_cp("parallel", "parallel"))(
        proj, proj, proj, proj, proj, proj, do, do, lse, lse, delta, delta)


def _sum3_bf16(a, b, c):
    T, W = a.shape
    tm = _pick(T, 512, SUBLANES)

    def body(a_ref, b_ref, c_ref, o_ref):
        o_ref[...] = (a_ref[...] + b_ref[...] + c_ref[...]).astype(BF16)

    row = pl.BlockSpec((tm, W), lambda i: (i, 0))
    return pl.pallas_call(body, name="dil_dv_sum", grid=(T // tm,), in_specs=[row, row, row], out_specs=row,
                          out_shape=_sds((T, W), BF16), compiler_params=_cp("parallel"))(a, b, c)


def _to_bf16(name, parts):
    T, W = parts[0].shape
    n = len(parts)
    tm = _pick(T, 512, SUBLANES)

    def body(*refs):
        o_ref = refs[n]
        for p in range(n):
            o_ref[:, p * W:(p + 1) * W] = refs[p][...].astype(BF16)

    row = pl.BlockSpec((tm, W), lambda i: (i, 0))
    return pl.pallas_call(body, name=name, grid=(T // tm,), in_specs=[row] * n,
                          out_specs=pl.BlockSpec((tm, n * W), lambda i: (i, 0)),
                          out_shape=_sds((T, n * W), BF16), compiler_params=_cp("parallel"))(*parts)


def _branch_merge(ys, w_br, gates):
    NB, T, W = ys.shape
    D = w_br.shape[2]
    tm, tn = _pick(T, 1024, SUBLANES), _pick(D, 1024)
    nj = D // tn

    def body(y_ref, w_ref, g_ref, yp_ref, m_ref, acc):
        n = pl.program_id(2)
        yp = jnp.dot(y_ref[...], w_ref[...], preferred_element_type=F32)
        yp_ref[...] = yp.astype(BF16)

        @pl.when(n == 0)
        def _():
            acc[...] = jnp.zeros_like(acc)

        acc[...] += g_ref[...].astype(F32) * yp

        @pl.when(n == NB - 1)
        def _():
            m_ref[...] = acc[...].astype(BF16)

    return pl.pallas_call(
        body, name="branch_merge", grid=(T // tm, nj, NB),
        in_specs=[pl.BlockSpec((None, tm, W), lambda i, j, n: (n, i, 0)),
                  pl.BlockSpec((None, W, tn), lambda i, j, n: (n, 0, j)),
                  pl.BlockSpec((tm, tn), lambda i, j, n: (i, n * nj + j))],
        out_specs=[pl.BlockSpec((None, tm, tn), lambda i, j, n: (n, i, j)),
                   pl.BlockSpec((tm, tn), lambda i, j, n: (i, j))],
        out_shape=[_sds((NB, T, D), BF16), _sds((T, D), BF16)],
        scratch_shapes=[pltpu.VMEM((tm, tn), F32)],
        compiler_params=_cp("parallel", "parallel", "arbitrary"))(ys, w_br, gates)


def _merge_bwd(dmerged, gates, yproj):
    NB, T, D = yproj.shape
    tm = _pick(T, 256, SUBLANES)

    def body(dm_ref, g_ref, yp_ref, dyp_ref, dz_ref, db_ref):
        dm, g = dm_ref[...], g_ref[...].astype(F32)
        dyp_ref[...] = (dm * g).astype(BF16)
        dz = dm * yp_ref[...].astype(F32) * g * (1.0 - g)
        dz_ref[...] = dz.astype(BF16)
        _acc_rows(db_ref, 0, jnp.sum(dz, axis=0, keepdims=True), pl.program_id(1) == 0)

    return pl.pallas_call(
        body, name="merge_bwd", grid=(NB, T // tm),
        in_specs=[pl.BlockSpec((tm, D), lambda n, i: (i, 0)), pl.BlockSpec((tm, D), lambda n, i: (i, n)),
                  pl.BlockSpec((None, tm, D), lambda n, i: (n, i, 0))],
        out_specs=[pl.BlockSpec((None, tm, D), lambda n, i: (n, i, 0)), pl.BlockSpec((tm, D), lambda n, i: (i, n)),
                   pl.BlockSpec((SUBLANES, D), lambda n, i: (0, n))],
        out_shape=[_sds((NB, T, D), BF16), _sds((T, NB * D), BF16), _sds((SUBLANES, NB * D), F32)],
        compiler_params=_cp("parallel", "arbitrary"))(dmerged, gates, yproj)


def _branch_bwd_dy(dyp, w_br):
    NB, T, D = dyp.shape
    W = w_br.shape[1]
    tm = _pick(T, 1024, SUBLANES)

    def body(a_ref, w_ref, o_ref):
        o_ref[...] = lax.dot_general(a_ref[...], w_ref[...], NT, preferred_element_type=F32)

    return pl.pallas_call(body, name="branch_bwd_dy", grid=(NB, T // tm),
                          in_specs=[pl.BlockSpec((None, tm, D), lambda n, i: (n, i, 0)),
                                    pl.BlockSpec((None, W, D), lambda n, i: (n, 0, 0))],
                          out_specs=pl.BlockSpec((None, tm, W), lambda n, i: (n, i, 0)),
                          out_shape=_sds((NB, T, W), F32), compiler_params=_cp("parallel", "parallel"))(dyp, w_br)


def _branch_bwd_dw(ys, dyp):
    NB, T, W = ys.shape
    D = dyp.shape[2]
    tm, tn, tk = _pick(W, 1024), _pick(D, 1024), _pick(T, 2048, SUBLANES)
    return _mm("branch_bwd_dw", ys, dyp, grid=(NB, W // tm, D // tn, T // tk), kaxis=3, dims=TN,
               a_spec=pl.BlockSpec((None, tk, tm), lambda n, i, j, k: (n, k, i)),
               b_spec=pl.BlockSpec((None, tk, tn), lambda n, i, j, k: (n, k, j)),
               acc_shape=(tm, tn), out_shape=_sds((NB, W, D), BF16),
               out_specs=pl.BlockSpec((None, tm, tn), lambda n, i, j, k: (n, i, j)),
               epilogue=_store(BF16), sem=("parallel", "parallel", "parallel", "arbitrary"))


def _layer_fwd(x, wl, sp):
    W = sp["conv_b"].shape[-1]
    D = x.shape[1]
    h = _rmsnorm_fwd(x, sp["norm_g"])
    proj = _mm_nn("proj", h, wl["w_in"], F32, tn=768)
    if "late" in wl:
        wl.update(wl.pop("late")(proj))

    def gate_ep(acc, ex, outs):
        outs[0][...] = _sigmoid(acc + ex[0][...]).astype(BF16)

    tn_g = _pick(4 * D, 1024)
    gates = _mm_nn("gates", h, wl["w_gate"], BF16, epilogue=gate_ep, extras=(sp["b_gate"],),
                   extra_specs=(pl.BlockSpec((1, tn_g), lambda i, j, k: (0, j)),), tn=tn_g)
    ys = lax.empty((4, x.shape[0], W), BF16)
    oa, ys, sb_carries = _sb_fwd(proj, W, ys)
    cpre, ys = _conv_fwd(proj, wl["conv_w"], wl["taps"], sp["conv_b"], sp["conv_ln_g"], sp["conv_ln_b"], W, ys)
    ys = _sgu_fwd(proj, sp["sgu_w"], sp["sgu_bt"], sp["sgu_ln_g"], sp["sgu_ln_b"], W, ys)
    os_, ls_ = zip(*[_dil_fwd_group(proj, W, gi, dil) for gi, (_, dil) in enumerate(DIL_PATTERNS)])
    od, lse, ys = _dil_combine(proj, os_, ls_, W, ys)
    yproj, merged = _branch_merge(ys, wl["w_br"], gates)

    def res_ep(acc, ex, outs):
        outs[0][...] = ex[0][...] + acc

    tm_o, tn_o = _pick(x.shape[0], 1024, SUBLANES), _pick(D, 1024)
    xn = _mm_nn("out_proj", merged, wl["w_out"], F32, epilogue=res_ep, extras=(x,),
                extra_specs=(pl.BlockSpec((tm_o, tn_o), lambda i, j, k: (i, j)),), tm=tm_o, tn=tn_o)
    saved = dict(x=x, h=h, proj=proj, gates=gates, oa=oa, sb_carries=sb_carries, cpre=cpre, od=od, lse=lse, ys=ys, yproj=yproj, merged=merged)
    return xn, saved


def _layer_bwd(dout, sv, wl, sp):
    W = sp["conv_b"].shape[-1]
    proj = sv["proj"]
    dmerged = _mm_nt("out_proj_bwd_dx", dout, wl["w_out"], F32)
    g_w_out = _mm_tn("out_proj_bwd_dw", sv["merged"], dout)
    dyp, dzg, db_gate = _merge_bwd(dmerged, sv["gates"], sv["yproj"])
    dy = _branch_bwd_dy(dyp, wl["w_br"])
    g_w_br = _branch_bwd_dw(sv["ys"], dyp)
    g_w_gate = _mm_tn("gate_bwd_dw", sv["h"], dzg, tk=4096)

    a_dq, a_dk, a_dv, a_dg = _sb_bwd(proj, sv["oa"], sv["sb_carries"], dy, W)
    a_qkv = _to_bf16("sb_bwd_cast", [a_dq, a_dk, a_dv])

    dc, b_dg, conv_stats = _conv_bwd_ln(proj, sv["cpre"], dy, sp["conv_ln_g"], sp["conv_ln_b"], W)
    b_da, b_db, g_conv_w = _conv_bwd_taps(proj, dc, wl["conv_w"], wl["taps"], W)

    c_du, c_dv, c_dg, g_sgu_w, g_sgu_bt, sgu_stats = _sgu_bwd(proj, dy, sp["sgu_w"], sp["sgu_bt"],
                                                              sp["sgu_ln_g"], sp["sgu_ln_b"], W)

    do, delta, d_dg = _dil_bwd_pre(proj, sv["od"], dy, W)
    dqs, dks, dvs = zip(*[_dil_bwd_group(proj, do, sv["lse"], delta, W, gi, dil)
                          for gi, (_, dil) in enumerate(DIL_PATTERNS)])
    d_qk = _to_bf16("dil_bwd_cast", [*dqs, *dks])
    d_dv = _sum3_bf16(*dvs)

    dproj = jnp.concatenate([a_qkv, a_dg, b_da, b_db, b_dg, c_du, c_dv, c_dg, d_qk, d_dv, d_dg], axis=1)
    g_w_in = _mm_tn("proj_bwd_dw", sv["h"], dproj, tn=768, tk=4096)
    dh = _mm_nt("gate_bwd_dh", dzg, wl["w_gate"], F32)
    dh = _mm_nt("proj_bwd_dh", dproj, wl["w_in"], F32, addend=dh)
    dx, dnorm = _rmsnorm_bwd(sv["x"], dh, dout, sp["norm_g"])

    K = wl["taps"]
    small = dict(norm_g=dnorm[0], conv_w=g_conv_w[:K], conv_b=conv_stats[2], conv_ln_g=conv_stats[0],
                 conv_ln_b=conv_stats[1], sgu_ln_g=sgu_stats[0], sgu_ln_b=sgu_stats[1], sgu_w=g_sgu_w,
                 sgu_b=g_sgu_bt.T, b_gate=db_gate[0])
    big = dict(w_in=g_w_in, w_gate=g_w_gate, w_br=g_w_br, w_out=g_w_out)
    return dx, big, small


HBM = pl.BlockSpec(memory_space=pl.ANY)


def _mesh_pos():
    return lax.axis_index("x"), lax.axis_index("y"), lax.axis_index("c")


def _other_chips(x, y):
    return [(1 - x, y), (x, 1 - y), (1 - x, 1 - y)]


def _shard_of(ref, axis, size, index):
    idx = [slice(None)] * len(ref.shape)
    idx[axis] = pl.ds(index * size, size)
    return ref.at[tuple(idx)]


def _all_gather(name, shards, axes):
    n = len(shards)
    out_shape = []
    for s, ax in zip(shards, axes):
        shp = list(s.shape)
        shp[ax] *= N_DEV
        out_shape.append(_sds(shp, s.dtype))

    def body(*refs):
        ins, outs = refs[:n], refs[n:2 * n]
        send, recv, lsem = refs[2 * n:]
        x, y, c = _mesh_pos()
        me, sib = (x, y, c), (x, y, 1 - c)
        chips = _other_chips(x, y)
        dev = lambda px, py, pc: 4 * px + 2 * py + pc

        def blk(a, d):
            return _shard_of(outs[a], axes[a], ins[a].shape[axes[a]], d)

        def cp(a, k, d, to, src=None):
            return pltpu.make_async_remote_copy(src_ref=blk(a, d) if src is None else src, dst_ref=blk(a, d),
                                                send_sem=send.at[a * 7 + k], recv_sem=recv.at[a * 7 + k],
                                                device_id=to, device_id_type=MESH)

        own = [pltpu.make_async_copy(ins[a], blk(a, dev(*me)), lsem.at[a]) for a in range(n)]
        for o in own:
            o.start()
        first = []
        for a in range(n):
            first.append(cp(a, 0, dev(*me), sib, src=ins[a]))
            first += [cp(a, 1 + j, dev(*me), (*ch, c), src=ins[a]) for j, ch in enumerate(chips)]
        for f in first:
            f.start()
        passed = []
        for j, ch in enumerate(chips):
            for a in range(n):
                cp(a, 1 + j, dev(*ch, c), me).wait_recv()
                p = cp(a, 4 + j, dev(*ch, c), sib)
                p.start()
                passed.append(p)
        for a in range(n):
            cp(a, 0, dev(*sib), me).wait_recv()
            for j, ch in enumerate(chips):
                cp(a, 4 + j, dev(*ch, 1 - c), me).wait_recv()
        for f in first + passed:
            f.wait_send()
        for o in own:
            o.wait()

    return pl.pallas_call(body, name=name, in_specs=[HBM] * n, out_specs=[HBM] * n, out_shape=out_shape,
                          scratch_shapes=[pltpu.SemaphoreType.DMA((7 * n,)), pltpu.SemaphoreType.DMA((7 * n,)),
                                          pltpu.SemaphoreType.DMA((n,))])(*shards)


HBM_ONLY = pl.BlockSpec(memory_space=pltpu.HBM)
SEM_SPEC = pl.BlockSpec(memory_space=pltpu.SEMAPHORE)
N_PEER = N_DEV - 1


def _peer(x, y, c, m):
    flip = lambda v, bit: 1 - v if bit else v
    return flip(x, (m >> 2) & 1), flip(y, (m >> 1) & 1), flip(c, m & 1)


def _exchange_copies(kind, srcs, lands, send, recv, axes):
    x, y, c = _mesh_pos()
    me = 4 * x + 2 * y + c
    n = len(srcs)
    remote, local = [], []
    for a in range(n):
        for m in range(1, N_DEV):
            px, py, pc = _peer(x, y, c, m)
            if kind == "gather":
                src = srcs[a]
                dst = _shard_of(lands[a], axes[a], srcs[a].shape[axes[a]], me)
            else:
                src = _shard_of(srcs[a], axes[a], srcs[a].shape[axes[a]] // N_DEV, 4 * px + 2 * py + pc)
                dst = lands[a].at[me]
            remote.append(pltpu.make_async_remote_copy(
                src_ref=src, dst_ref=dst, send_sem=send.at[a * N_PEER + m - 1], recv_sem=recv.at[a * N_PEER + m - 1],
                device_id=(px, py, pc), device_id_type=MESH))
        if kind == "gather":
            local.append(pltpu.make_async_copy(srcs[a], _shard_of(lands[a], axes[a], srcs[a].shape[axes[a]], me),
                                               send.at[n * N_PEER + a]))
    return remote, local


def _exchange_start(name, kind, srcs, lands, axes, carry):
    n = len(srcs)
    hbm = lambda a: pltpu.with_memory_space_constraint(a, pltpu.HBM)

    def body(*refs):
        send, recv = refs[2 * n + 1], refs[2 * n + 2]
        remote, local = _exchange_copies(kind, refs[:n], refs[n:2 * n], send, recv, axes)
        for cp in remote + local:
            cp.start()

    thru = [*srcs, *lands, carry]
    res = pl.pallas_call(
        body, name=name,
        out_shape=(pltpu.SemaphoreType.DMA((n * N_DEV,)), pltpu.SemaphoreType.DMA((n * N_PEER,)),
                   *[pltpu.HBM(a.shape, a.dtype) for a in thru]),
        in_specs=[HBM_ONLY] * len(thru), out_specs=(SEM_SPEC, SEM_SPEC, *[HBM_ONLY] * len(thru)),
        input_output_aliases={i: 2 + i for i in range(len(thru))},
        compiler_params=pltpu.CompilerParams(has_side_effects=pltpu.SideEffectType.DATAFLOW_SIDE_EFFECTING),
    )(*[hbm(a) for a in thru])
    return res[0], res[1], list(res[2:2 + n]), list(res[2 + n:2 + 2 * n]), res[2 + 2 * n]


def _exchange_wait(name, kind, send, recv, srcs, lands, axes, after):
    n = len(srcs)

    def body(*refs):
        send_ref, recv_ref = refs[2 * n], refs[2 * n + 1]
        remote, local = _exchange_copies(kind, refs[:n], refs[n:2 * n], send_ref, recv_ref, axes)
        for cp in remote:
            cp.wait_send()
            cp.wait_recv()
        for cp in local:
            cp.wait()

    thru = [*srcs, *lands]
    res = pl.pallas_call(
        body, name=name, out_shape=tuple(pltpu.HBM(a.shape, a.dtype) for a in thru),
        in_specs=[*[HBM_ONLY] * len(thru), SEM_SPEC, SEM_SPEC, HBM], out_specs=tuple([HBM_ONLY] * len(thru)),
        input_output_aliases={i: i for i in range(len(thru))},
        compiler_params=pltpu.CompilerParams(has_side_effects=pltpu.SideEffectType.DATAFLOW_SIDE_EFFECTING),
    )(*thru, send, recv, after)
    return list(res[:n]), list(res[n:])


def _adamw_scatter(name, land, grad, axis, me, w, m, v):
    P, R, C = land.shape
    tr = _pick(R, max(SUBLANES, (1 << 18) // C // 16 * 16), 16)
    nr = R // tr

    def body(me_ref, l_ref, o_ref, w_ref, m_ref, v_ref, g_ref, d_ref, mo_ref, vo_ref):
        own = o_ref[...].astype(F32)
        g = jnp.where(me_ref[0] == 0, own, l_ref[0].astype(F32))
        for k in range(1, P):
            g = g + jnp.where(me_ref[0] == k, own, l_ref[k].astype(F32))
        mn = ADAM_B1 * m_ref[...] + (1.0 - ADAM_B1) * g
        vn = ADAM_B2 * v_ref[...] + (1.0 - ADAM_B2) * (g * g)
        m_hat = mn / (1.0 - ADAM_B1 ** ADAM_STEP)
        v_hat = vn / (1.0 - ADAM_B2 ** ADAM_STEP)
        g_ref[...] = g
        d_ref[...] = -ADAM_LR * (m_hat / (jnp.sqrt(v_hat) + ADAM_EPS) + ADAM_WD * w_ref[...])
        mo_ref[...] = mn
        vo_ref[...] = vn

    if axis == 1:
        own_spec = pl.BlockSpec((tr, C), lambda i, me_ref: (i, me_ref[0]))
    else:
        own_spec = pl.BlockSpec((tr, C), lambda i, me_ref: (me_ref[0] * nr + i, 0))
    row = pl.BlockSpec((tr, C), lambda i, me_ref: (i, 0))
    return pl.pallas_call(
        body, name=name, out_shape=[_sds((R, C), F32)] * 4,
        grid_spec=pltpu.PrefetchScalarGridSpec(
            num_scalar_prefetch=1, grid=(nr,),
            in_specs=[pl.BlockSpec((P, tr, C), lambda i, me_ref: (0, i, 0)), own_spec, row, row, row],
            out_specs=[row] * 4),
        compiler_params=_cp("parallel"))(me, land, grad, w, m, v)


def _adamw_sum(name, parts, w, m, v):
    P, R, C = parts.shape
    tr = _pick(R, max(SUBLANES, (1 << 19) // C // SUBLANES * SUBLANES), SUBLANES)

    def body(p_ref, w_ref, m_ref, v_ref, g_ref, d_ref, mo_ref, vo_ref):
        g = p_ref[0].astype(F32)
        for k in range(1, P):
            g = g + p_ref[k].astype(F32)
        mn = ADAM_B1 * m_ref[...] + (1.0 - ADAM_B1) * g
        vn = ADAM_B2 * v_ref[...] + (1.0 - ADAM_B2) * (g * g)
        m_hat = mn / (1.0 - ADAM_B1 ** ADAM_STEP)
        v_hat = vn / (1.0 - ADAM_B2 ** ADAM_STEP)
        g_ref[...] = g
        d_ref[...] = -ADAM_LR * (m_hat / (jnp.sqrt(v_hat) + ADAM_EPS) + ADAM_WD * w_ref[...])
        mo_ref[...] = mn
        vo_ref[...] = vn

    row = pl.BlockSpec((tr, C), lambda i: (i, 0))
    return pl.pallas_call(body, name=name, grid=(R // tr,),
                          in_specs=[pl.BlockSpec((P, tr, C), lambda i: (0, i, 0)), row, row, row],
                          out_specs=[row] * 4, out_shape=[_sds((R, C), F32)] * 4,
                          compiler_params=_cp("parallel"))(parts, w, m, v)


def _rows128(a, pad_rows=SUBLANES):
    flat = a.reshape(-1, LANES)
    pad = (-flat.shape[0]) % pad_rows
    return jnp.pad(flat, ((0, pad), (0, 0))) if pad else flat


SMALL = ("norm_g", "conv_b", "conv_ln_g", "conv_ln_b", "sgu_ln_g", "sgu_ln_b", "sgu_w", "sgu_b", "b_gate", "final_g")


def kernel(x, norm_g, w_in, conv_w, conv_b, conv_ln_g, conv_ln_b, sgu_ln_g, sgu_ln_b, sgu_w, sgu_b, w_branch, w_gate, b_gate, w_out, final_g, loss_target, m_norm_g, m_w_in, m_conv_w, m_conv_b, m_conv_ln_g, m_conv_ln_b, m_sgu_ln_g, m_sgu_ln_b, m_sgu_w, m_sgu_b, m_w_branch, m_w_gate, m_b_gate, m_w_out, m_final_g, v_norm_g, v_w_in, v_conv_w, v_conv_b, v_conv_ln_g, v_conv_ln_b, v_sgu_ln_g, v_sgu_ln_b, v_sgu_w, v_sgu_b, v_w_branch, v_w_gate, v_b_gate, v_w_out, v_final_g):
    L, D = norm_g.shape
    W = conv_b.shape[1]
    taps = conv_w.shape[1]
    weights = dict(norm_g=norm_g, w_in=w_in, conv_w=conv_w, conv_b=conv_b, conv_ln_g=conv_ln_g, conv_ln_b=conv_ln_b,
                   sgu_ln_g=sgu_ln_g, sgu_ln_b=sgu_ln_b, sgu_w=sgu_w, sgu_b=sgu_b, w_branch=w_branch, w_gate=w_gate,
                   b_gate=b_gate, w_out=w_out, final_g=final_g)
    mom_m = dict(norm_g=m_norm_g, w_in=m_w_in, conv_w=m_conv_w, conv_b=m_conv_b, conv_ln_g=m_conv_ln_g,
                 conv_ln_b=m_conv_ln_b, sgu_ln_g=m_sgu_ln_g, sgu_ln_b=m_sgu_ln_b, sgu_w=m_sgu_w, sgu_b=m_sgu_b,
                 w_branch=m_w_branch, w_gate=m_w_gate, b_gate=m_b_gate, w_out=m_w_out, final_g=m_final_g)
    mom_v = dict(norm_g=v_norm_g, w_in=v_w_in, conv_w=v_conv_w, conv_b=v_conv_b, conv_ln_g=v_conv_ln_g,
                 conv_ln_b=v_conv_ln_b, sgu_ln_g=v_sgu_ln_g, sgu_ln_b=v_sgu_ln_b, sgu_w=v_sgu_w, sgu_b=v_sgu_b,
                 w_branch=v_w_branch, w_gate=v_w_gate, b_gate=v_b_gate, w_out=v_w_out, final_g=v_final_g)
    me = 4 * lax.axis_index("x") + 2 * lax.axis_index("y") + lax.axis_index("c")

    gather_names = ("w_in", "w_gate", "w_br", "w_out", "conv_w")
    gather_axes = (1, 1, 2, 0, 1)

    def gather_start(tag, l, which, carry):
        shards = [w_in[l].astype(BF16), w_gate[l].astype(BF16), w_branch[l].astype(BF16), w_out[l].astype(BF16),
                  jnp.pad(conv_w[l], ((0, CONV_HALO - taps), (0, 0)))]
        shards = [shards[i] for i in which]
        axes = [gather_axes[i] for i in which]
        lands = []
        for s, ax in zip(shards, axes):
            full = list(s.shape)
            full[ax] *= N_DEV
            lands.append(lax.empty(tuple(full), s.dtype))
        return _exchange_start(f"gather_start_{tag}", "gather", shards, lands, axes, carry) + (which,)

    def gather_wait(tag, started, after):
        send, recv, shards, lands, _, which = started
        _, full = _exchange_wait(f"gather_wait_{tag}", "gather", send, recv, shards, lands,
                                 [gather_axes[i] for i in which], after)
        return {gather_names[i]: f for i, f in zip(which, full)}

    xs = x[0]
    saved, gathered, smalls = [], [], []
    first = gather_start("0a", 0, (0,), xs)
    rest = gather_start("0b", 0, (1, 2, 3, 4), first[4])
    xs = rest[4]
    wl = gather_wait("0a", first, xs)
    wl["late"] = functools.partial(gather_wait, "0b", rest)
    for l in range(L):
        wl["taps"] = taps
        if l + 1 < L:
            started = gather_start(l + 1, l + 1, (0, 1, 2, 3, 4), xs)
            xs = started[4]
        sp = dict(norm_g=norm_g[l][None], conv_b=conv_b[l][None], conv_ln_g=conv_ln_g[l][None],
                  conv_ln_b=conv_ln_b[l][None], sgu_ln_g=sgu_ln_g[l][None], sgu_ln_b=sgu_ln_b[l][None],
                  sgu_w=sgu_w[l], sgu_bt=sgu_b[l].T, b_gate=b_gate[l][None])
        xs, sv = _layer_fwd(xs, wl, sp)
        saved.append(sv)
        gathered.append(wl)
        smalls.append(sp)
        if l + 1 < L:
            wl = gather_wait(l + 1, started, xs)

    dx, d_final, loss_part = _loss_head(xs, loss_target[0], final_g[None])
    loss = lax.psum(loss_part[0, 0], ("x", "y", "c"))

    big_names = ("w_in", "w_gate", "w_branch", "w_out")
    big_axes = (1, 1, 1, 0)
    me1 = me.astype(jnp.int32).reshape(1)
    outs = {}
    small_grads = []

    def scatter_start(l, g2d, carry):
        lands = []
        for g, ax in zip(g2d, big_axes):
            blk = list(g.shape)
            blk[ax] //= N_DEV
            lands.append(lax.empty((N_DEV, *blk), g.dtype))
        return _exchange_start(f"scatter_start_{l}", "scatter", g2d, lands, big_axes, carry)

    def scatter_finish(l, started, after):
        send, recv, g2d, lands, _ = started
        g2d, lands = _exchange_wait(f"scatter_wait_{l}", "scatter", send, recv, g2d, lands, big_axes, after)
        for nm, land, g, ax in zip(big_names, lands, g2d, big_axes):
            shard2d = land.shape[1:]
            outs.setdefault(nm, {})[l] = _adamw_scatter(
                "adamw_" + nm, land, g, ax, me1, weights[nm][l].reshape(shard2d), mom_m[nm][l].reshape(shard2d),
                mom_v[nm][l].reshape(shard2d))

    pending = None
    for l in reversed(range(L)):
        dx, big, small = _layer_bwd(dx, saved[l], gathered[l], smalls[l])
        small_grads.append(small)
        if pending is not None:
            scatter_finish(l + 1, pending, dx)
        g2d = [big["w_in"], big["w_gate"], big["w_br"].reshape(-1, D), big["w_out"]]
        if l > 0:
            pending = scatter_start(l, g2d, dx)
            dx = pending[4]
        else:
            pending = scatter_start(l, g2d, d_final)
            d_final = pending[4]
    small_grads.reverse()

    sg = {nm: jnp.stack([small_grads[l][nm] for l in range(L)]) for nm in SMALL[:-1]}
    sg["final_g"] = d_final[0]
    conv_w_full = jnp.stack([small_grads[l]["conv_w"] for l in range(L)])
    segs = [_rows128(sg[nm]) for nm in SMALL] + [_rows128(conv_w_full)]
    offs = [0]
    for s in segs:
        offs.append(offs[-1] + s.shape[0])
    pack = jnp.concatenate(segs, axis=0)
    (allp,) = _all_gather("gather_small_grads", [pack[None]], [0])

    def packed(src):
        return jnp.concatenate([_rows128(src[nm]) for nm in SMALL] + [jnp.zeros_like(segs[-1])], axis=0)

    s_g, s_d, s_m, s_v = _adamw_sum("adamw_small", allp, packed(weights), packed(mom_m), packed(mom_v))
    scatter_finish(0, pending, s_g)
    big_out = {nm: [jnp.stack([outs[nm][l][q] for l in range(L)]).reshape(weights[nm].shape) for q in range(4)]
               for nm in big_names}

    def unpack(buf, i, like):
        n = like.size // LANES
        return buf[offs[i]:offs[i] + n].reshape(like.shape)

    small_out = {nm: [unpack(b, i, weights[nm]) for b in (s_g, s_d, s_m, s_v)] for i, nm in enumerate(SMALL)}
    Wc = conv_w.shape[2]
    cw_sum = lax.dynamic_slice_in_dim(unpack(s_g, len(SMALL), conv_w_full), me * Wc, Wc, axis=2)
    cshape = (L * conv_w.shape[1], Wc)
    conv_out = [o.reshape(conv_w.shape) for o in _adamw_sum(
        "adamw_conv_w", cw_sum.reshape((1,) + cshape), conv_w.reshape(cshape), m_conv_w.reshape(cshape),
        v_conv_w.reshape(cshape))]

    order = ["norm_g", "w_in", "conv_w", "conv_b", "conv_ln_g", "conv_ln_b", "sgu_ln_g", "sgu_ln_b", "sgu_w", "sgu_b",
             "w_branch", "w_gate", "b_gate", "w_out", "final_g"]
    table = dict(small_out)
    table.update(big_out)
    table["conv_w"] = conv_out
    result = [loss, dx[None]]
    for q in range(4):
        result += [table[nm][q] for nm in order]
    return tuple(result)
```

```python
import functools

import jax
import jax.numpy as jnp
from jax import lax
from jax.experimental import pallas as pl
from jax.experimental.pallas import tpu as pltpu

F32 = jnp.float32
BF16 = jnp.bfloat16
LANES = 128
SUBLANES = 8
CONV_HALO = 32
SB_QUERY_BLOCK = 512
SB_QUERY_BLOCK_BWD = 512
SB_KEY_BLOCK = 1024
DIL_STEP_ROWS = 2048
DIL_UNROLL = 16
NORM_EPS = 1e-6
NEG = -1e30
N_DEV = 8
DIL_PATTERNS = ((128, 1), (512, 4), (2048, 16))

ADAM_LR = 0.001
ADAM_B1 = 0.9
ADAM_B2 = 0.999
ADAM_EPS = 1e-08
ADAM_WD = 0.01
ADAM_STEP = 10

MESH = pl.DeviceIdType.MESH
NN = (((1,), (0,)), ((), ()))
NT = (((1,), (1,)), ((), ()))
TN = (((0,), (0,)), ((), ()))
VMEM_LIMIT = 52 << 20


def _sds(shape, dtype):
    return jax.ShapeDtypeStruct(tuple(shape), dtype)


def _cp(*sem):
    return pltpu.CompilerParams(dimension_semantics=tuple(sem), vmem_limit_bytes=VMEM_LIMIT)


def _pick(n, target, quantum=LANES):
    if n <= target:
        return n
    t = (target // quantum) * quantum
    while t >= quantum:
        if n % t == 0:
            return t
        t -= quantum
    return n


def _sigmoid(x):
    return 1.0 / (1.0 + jnp.exp(-x))


def _silu(x):
    return x * _sigmoid(x)


def _dsilu(x):
    s = _sigmoid(x)
    return s * (1.0 + x * (1.0 - s))


_GELU_K = 0.7978845608028654
_GELU_A = 0.044715


def _gelu(x):
    return 0.5 * x * (1.0 + jnp.tanh(_GELU_K * (x + _GELU_A * x * x * x)))


def _dgelu(x):
    t = jnp.tanh(_GELU_K * (x + _GELU_A * x * x * x))
    return 0.5 * (1.0 + t) + 0.5 * x * (1.0 - t * t) * _GELU_K * (1.0 + 3.0 * _GELU_A * x * x)


def _ln_stats(v):
    mu = jnp.mean(v, axis=-1, keepdims=True)
    d = v - mu
    var = jnp.mean(d * d, axis=-1, keepdims=True)
    r = lax.rsqrt(var + NORM_EPS)
    return d * r, r


def _ln_bwd(dxh, xh, r):
    return r * (dxh - jnp.mean(dxh, axis=-1, keepdims=True) - xh * jnp.mean(dxh * xh, axis=-1, keepdims=True))


def _acc_rows(ref, row, val, first):
    @pl.when(first)
    def _():
        ref[...] = jnp.zeros_like(ref)
    ref[row:row + 1, :] += val


def _mm(name, a, b, *, grid, kaxis, dims, a_spec, b_spec, acc_shape, out_shape, out_specs,
        epilogue, extras=(), extra_specs=(), sem):
    nk = grid[kaxis]
    ne = len(extras)

    def body(*refs):
        a_ref, b_ref = refs[0], refs[1]
        ex = refs[2:2 + ne]
        outs = refs[2 + ne:-1]
        acc = refs[-1]
        k = pl.program_id(kaxis)

        @pl.when(k == 0)
        def _():
            acc[...] = jnp.zeros_like(acc)

        acc[...] += lax.dot_general(a_ref[...].astype(BF16), b_ref[...].astype(BF16), dims,
                                    preferred_element_type=F32)

        @pl.when(k == nk - 1)
        def _():
            epilogue(acc[...], ex, outs)

    return pl.pallas_call(
        body, name=name, grid=grid, in_specs=[a_spec, b_spec, *extra_specs], out_specs=out_specs,
        out_shape=out_shape, scratch_shapes=[pltpu.VMEM(acc_shape, F32)], compiler_params=_cp(*sem),
    )(a, b, *extras)


def _store(dtype):
    def ep(acc, ex, outs):
        outs[0][...] = acc.astype(dtype)
    return ep


def _mm_nn(name, a, b, out_dtype, epilogue=None, extras=(), extra_specs=(), tm=1024, tn=1024):
    M, K = a.shape
    N = b.shape[1]
    tm, tn = _pick(M, tm, SUBLANES), _pick(N, tn)
    return _mm(name, a, b, grid=(M // tm, N // tn, 1), kaxis=2, dims=NN,
               a_spec=pl.BlockSpec((tm, K), lambda i, j, k: (i, 0)),
               b_spec=pl.BlockSpec((K, tn), lambda i, j, k: (0, j)),
               acc_shape=(tm, tn), out_shape=_sds((M, N), out_dtype),
               out_specs=pl.BlockSpec((tm, tn), lambda i, j, k: (i, j)),
               epilogue=epilogue or _store(out_dtype), extras=extras, extra_specs=extra_specs,
               sem=("parallel", "parallel", "arbitrary"))


def _mm_nt(name, a, b, out_dtype, addend=None, tm=1024, tn=1024, tk=2048):
    M, K = a.shape
    N = b.shape[0]
    tm, tn, tk = _pick(M, tm, SUBLANES), _pick(N, tn), _pick(K, tk)
    extras, extra_specs = (), ()
    if addend is not None:
        extras = (addend,)
        extra_specs = (pl.BlockSpec((tm, tn), lambda i, j, k: (i, j)),)

    def ep(acc, ex, outs):
        if ex:
            acc = acc + ex[0][...]
        outs[0][...] = acc.astype(out_dtype)

    return _mm(name, a, b, grid=(M // tm, N // tn, K // tk), kaxis=2, dims=NT,
               a_spec=pl.BlockSpec((tm, tk), lambda i, j, k: (i, k)),
               b_spec=pl.BlockSpec((tn, tk), lambda i, j, k: (j, k)),
               acc_shape=(tm, tn), out_shape=_sds((M, N), out_dtype),
               out_specs=pl.BlockSpec((tm, tn), lambda i, j, k: (i, j)),
               epilogue=ep, extras=extras, extra_specs=extra_specs,
               sem=("parallel", "parallel", "arbitrary"))


def _mm_tn(name, a, b, tm=1024, tn=1024, tk=2048, out_dtype=BF16):
    K, M = a.shape
    N = b.shape[1]
    tm, tn, tk = _pick(M, tm), _pick(N, tn), _pick(K, tk, SUBLANES)
    return _mm(name, a, b, grid=(M // tm, N // tn, K // tk), kaxis=2, dims=TN,
               a_spec=pl.BlockSpec((tk, tm), lambda i, j, k: (k, i)),
               b_spec=pl.BlockSpec((tk, tn), lambda i, j, k: (k, j)),
               acc_shape=(tm, tn), out_shape=_sds((M, N), out_dtype),
               out_specs=pl.BlockSpec((tm, tn), lambda i, j, k: (i, j)),
               epilogue=_store(out_dtype), sem=("parallel", "parallel", "arbitrary"))


def _rmsnorm_fwd(x, g_row):
    T, D = x.shape
    tm = _pick(T, 512, SUBLANES)

    def body(x_ref, g_ref, h_ref):
        xv = x_ref[...]
        r = lax.rsqrt(jnp.mean(xv * xv, axis=-1, keepdims=True) + NORM_EPS)
        h_ref[...] = (xv * r * g_ref[...]).astype(BF16)

    row = pl.BlockSpec((tm, D), lambda i: (i, 0))
    return pl.pallas_call(body, name="rmsnorm_fwd", grid=(T // tm,),
                          in_specs=[row, pl.BlockSpec((1, D), lambda i: (0, 0))], out_specs=row,
                          out_shape=_sds((T, D), BF16), compiler_params=_cp("parallel"))(x, g_row)


def _rmsnorm_bwd(x, dh, dout, g_row):
    T, D = x.shape
    tm = _pick(T, 256, SUBLANES)

    def body(x_ref, dh_ref, do_ref, g_ref, dx_ref, dg_ref):
        xv = x_ref[...]
        r = lax.rsqrt(jnp.mean(xv * xv, axis=-1, keepdims=True) + NORM_EPS)
        xh = xv * r
        dhv = dh_ref[...]
        dxh = dhv * g_ref[...]
        dx_ref[...] = do_ref[...] + r * (dxh - xh * jnp.mean(dxh * xh, axis=-1, keepdims=True))
        _acc_rows(dg_ref, 0, jnp.sum(dhv * xh, axis=0, keepdims=True), pl.program_id(0) == 0)

    row = pl.BlockSpec((tm, D), lambda i: (i, 0))
    return pl.pallas_call(body, name="rmsnorm_bwd", grid=(T // tm,),
                          in_specs=[row, row, row, pl.BlockSpec((1, D), lambda i: (0, 0))],
                          out_specs=[row, pl.BlockSpec((SUBLANES, D), lambda i: (0, 0))],
                          out_shape=[_sds((T, D), F32), _sds((SUBLANES, D), F32)],
                          compiler_params=_cp("arbitrary"))(x, dh, dout, g_row)


def _loss_head(x, target, g_row):
    T, D = x.shape
    tm = _pick(T, 256, SUBLANES)

    def body(x_ref, t_ref, g_ref, dx_ref, dg_ref, loss_ref):
        first = pl.program_id(0) == 0
        xv = x_ref[...]
        g = g_ref[...]
        r = lax.rsqrt(jnp.mean(xv * xv, axis=-1, keepdims=True) + NORM_EPS)
        xh = xv * r
        err = xh * g - t_ref[...]
        part = 0.5 * jnp.sum(jnp.mean(err * err, axis=-1, keepdims=True), axis=0, keepdims=True)

        @pl.when(first)
        def _():
            loss_ref[...] = jnp.zeros_like(loss_ref)

        loss_ref[...] += jnp.broadcast_to(part, loss_ref.shape)
        dy = err / D
        _acc_rows(dg_ref, 0, jnp.sum(dy * xh, axis=0, keepdims=True), first)
        dxh = dy * g
        dx_ref[...] = r * (dxh - xh * jnp.mean(dxh * xh, axis=-1, keepdims=True))

    row = pl.BlockSpec((tm, D), lambda i: (i, 0))
    return pl.pallas_call(body, name="loss_head", grid=(T // tm,),
                          in_specs=[row, row, pl.BlockSpec((1, D), lambda i: (0, 0))],
                          out_specs=[row, pl.BlockSpec((SUBLANES, D), lambda i: (0, 0)),
                                     pl.BlockSpec((SUBLANES, LANES), lambda i: (0, 0))],
                          out_shape=[_sds((T, D), F32), _sds((SUBLANES, D), F32), _sds((SUBLANES, LANES), F32)],
                          compiler_params=_cp("arbitrary"))(x, target, g_row)


MXU_DIM = 256


def _tri(cmp):
    r = lax.broadcasted_iota(jnp.int32, (MXU_DIM, MXU_DIM), 0)
    c = lax.broadcasted_iota(jnp.int32, (MXU_DIM, MXU_DIM), 1)
    same_tile = (r // LANES) == (c // LANES)
    return (same_tile & cmp(r % LANES, c % LANES)).astype(BF16)


def _tile_scans(x, tri):
    out = []
    for p in range(x.shape[1] // MXU_DIM):
        sc = jnp.dot(x[:, p * MXU_DIM:(p + 1) * MXU_DIM].astype(BF16), tri, preferred_element_type=F32)
        out += [sc[:, :LANES], sc[:, LANES:]]
    return out


def _sb_scores(qs, kb, causal):
    z = lax.dot_general(qs, kb, NT, preferred_element_type=F32)
    lb = jnp.minimum(z, 0.0) - jnp.log(1.0 + jnp.exp(-jnp.abs(z)))
    l1 = lb - z
    return lb, (l1 if causal is None else jnp.where(causal, l1, 0.0))


def _masked(causal, x):
    return x if causal is None else jnp.where(causal, x, 0.0)


def _lanes(x, s):
    return x[:, s * LANES:(s + 1) * LANES]


def _sb_fwd(proj, W, ys):
    T = proj.shape[0]
    H = W // LANES
    assert T // LANES <= LANES
    tq = _pick(T, SB_QUERY_BLOCK, LANES)
    kblk = _pick(T, SB_KEY_BLOCK, LANES)
    assert kblk % tq == 0
    nsub = kblk // LANES
    half = kblk // 2 if (kblk // 2) % MXU_DIM == 0 else kblk
    scale = LANES ** -0.5

    def body(q_ref, k_ref, v_ref, g_ref, ys_in, o_ref, y_ref, c_ref, run):
        i = pl.program_id(1)
        qs = (q_ref[...] * scale).astype(BF16)
        row = i * tq + lax.broadcasted_iota(jnp.int32, (tq, kblk), 0)
        key = lax.broadcasted_iota(jnp.int32, (tq, kblk), 1)
        col = lax.broadcasted_iota(jnp.int32, (tq, LANES), 1)
        tri = _tri(lambda r, c: r > c)
        nkb = ((i + 1) * tq + kblk - 1) // kblk
        o_ref[...] = jnp.zeros_like(o_ref)
        c_ref[...] = jnp.zeros_like(c_ref)
        run[...] = jnp.zeros_like(run)

        def make_step(diagonal, width=kblk):
            def step(jj, carry):
                j = nkb - 1 - jj
                off = pl.multiple_of(j * kblk, kblk)
                kb = k_ref[pl.ds(off, width), :].astype(BF16)
                vb = v_ref[pl.ds(off, width), :].astype(BF16)
                causal = (key[:, :width] + off < row[:, :width]) if diagonal else None
                lb, l1m = _sb_scores(qs, kb, causal)
                c_after = run[...]
                cs = c_ref[...]
                after = _tile_scans(l1m, tri)
                for s in reversed(range(width // LANES)):
                    after[s] = c_after + after[s]
                    cs = jnp.where(col == j * nsub + s, c_after, cs)
                    c_after = c_after + jnp.sum(_lanes(l1m, s), axis=1, keepdims=True)
                run[...] = c_after
                c_ref[...] = cs
                w = _masked(causal, jnp.exp(lb + jnp.concatenate(after, axis=1)))
                o_ref[...] += jnp.dot(w.astype(BF16), vb, preferred_element_type=F32)
                return carry
            return step

        reach = (i + 1) * tq - (nkb - 1) * kblk

        @pl.when(reach <= half)
        def _():
            make_step(True, half)(0, 0)

        @pl.when(reach > half)
        def _():
            make_step(True)(0, 0)

        lax.fori_loop(1, nkb, make_step(False), 0)
        y_ref[...] = (o_ref[...] * _silu(g_ref[...])).astype(BF16)

    qspec = lambda c0: pl.BlockSpec((tq, LANES), lambda h, i: (i, c0 + h))
    kvspec = lambda c0: pl.BlockSpec((T, LANES), lambda h, i: (0, c0 + h))
    out = pl.BlockSpec((tq, LANES), lambda h, i: (i, h))
    return pl.pallas_call(body, name="sb_fwd", grid=(H, T // tq),
                          in_specs=[qspec(0), kvspec(H), kvspec(2 * H), qspec(3 * H), HBM],
                          out_specs=[out, pl.BlockSpec((None, tq, LANES), lambda h, i: (0, i, h)), out],
                          out_shape=[_sds((T, W), F32), _sds(ys.shape, BF16), _sds((T, W), F32)],
                          input_output_aliases={4: 1},
                          scratch_shapes=[pltpu.VMEM((tq, LANES), F32)],
                          compiler_params=_cp("parallel", "arbitrary"))(proj, proj, proj, proj, ys)


def _sb_bwd(proj, o, carries, dy, W):
    T = proj.shape[0]
    H = W // LANES
    tq = _pick(T, SB_QUERY_BLOCK_BWD, LANES)
    kblk = _pick(T, SB_KEY_BLOCK, LANES)
    assert kblk % tq == 0
    nsub = kblk // LANES
    half = kblk // 2 if (kblk // 2) % MXU_DIM == 0 else kblk
    scale = LANES ** -0.5

    def body(q_ref, k_ref, v_ref, g_ref, o_ref, c_ref, dy_ref, dq_ref, dk_ref, dv_ref, dg_ref, run):
        i = pl.program_id(1)

        @pl.when(i == 0)
        def _():
            dk_ref[...] = jnp.zeros_like(dk_ref)
            dv_ref[...] = jnp.zeros_like(dv_ref)

        gv = g_ref[...]
        dyv = dy_ref[...]
        dg_ref[...] = (dyv * o_ref[...] * _dsilu(gv)).astype(BF16)
        dob = (dyv * _silu(gv)).astype(BF16)
        qs = (q_ref[...] * scale).astype(BF16)
        row = i * tq + lax.broadcasted_iota(jnp.int32, (tq, kblk), 0)
        key = lax.broadcasted_iota(jnp.int32, (tq, kblk), 1)
        col = lax.broadcasted_iota(jnp.int32, (tq, LANES), 1)
        tri_after = _tri(lambda r, c: r > c)
        tri_before = _tri(lambda r, c: r < c)
        dq_ref[...] = jnp.zeros_like(dq_ref)
        run[...] = jnp.zeros_like(run)

        def make_step(diagonal, width=kblk):
            def step(j, carry):
                off = pl.multiple_of(j * kblk, kblk)
                kb = k_ref[pl.ds(off, width), :].astype(BF16)
                vb = v_ref[pl.ds(off, width), :].astype(BF16)
                causal = (key[:, :width] + off < row[:, :width]) if diagonal else None
                lb, l1m = _sb_scores(qs, kb, causal)
                cs = c_ref[...]
                after = [jnp.sum(jnp.where(col == j * nsub + s, cs, 0.0), axis=1, keepdims=True) + sc
                         for s, sc in enumerate(_tile_scans(l1m, tri_after))]
                w = _masked(causal, jnp.exp(lb + jnp.concatenate(after, axis=1)))
                gw = w * lax.dot_general(dob, vb, NT, preferred_element_type=F32)
                gpre = run[...]
                before = _tile_scans(gw, tri_before)
                for s in range(width // LANES):
                    before[s] = gpre + before[s]
                    gpre = gpre + jnp.sum(_lanes(gw, s), axis=1, keepdims=True)
                run[...] = gpre
                beta = jnp.exp(lb)
                dz = _masked(causal, gw * (1.0 - beta) - jnp.concatenate(before, axis=1) * beta).astype(BF16)
                dq_ref[...] += jnp.dot(dz, kb, preferred_element_type=F32)
                dk_ref[pl.ds(off, width), :] += lax.dot_general(dz, qs, TN, preferred_element_type=F32)
                dv_ref[pl.ds(off, width), :] += lax.dot_general(w.astype(BF16), dob, TN, preferred_element_type=F32)
                return carry
            return step

        last = ((i + 1) * tq + kblk - 1) // kblk - 1
        reach = (i + 1) * tq - last * kblk
        lax.fori_loop(0, last, make_step(False), 0)

        @pl.when(reach <= half)
        def _():
            make_step(True, half)(last, 0)

        @pl.when(reach > half)
        def _():
            make_step(True)(last, 0)

        dq_ref[...] = dq_ref[...] * scale

    qspec = lambda c0: pl.BlockSpec((tq, LANES), lambda h, i: (i, c0 + h))
    kvspec = lambda c0: pl.BlockSpec((T, LANES), lambda h, i: (0, c0 + h), pipeline_mode=pl.Buffered(1))
    blk = pl.BlockSpec((tq, LANES), lambda h, i: (i, h))
    full = pl.BlockSpec((T, LANES), lambda h, i: (0, h))
    return pl.pallas_call(body, name="sb_bwd", grid=(H, T // tq),
                          in_specs=[qspec(0), kvspec(H), kvspec(2 * H), qspec(3 * H), blk, blk,
                                    pl.BlockSpec((None, tq, LANES), lambda h, i: (0, i, h))],
                          out_specs=[blk, full, full, blk],
                          out_shape=[_sds((T, W), F32), _sds((T, W), F32), _sds((T, W), F32), _sds((T, W), BF16)],
                          scratch_shapes=[pltpu.VMEM((tq, LANES), F32)],
                          compiler_params=_cp("parallel", "arbitrary"))(proj, proj, proj, proj, o, carries, dy)


def _conv_specs(W, tm):
    per = tm // CONV_HALO
    cur = lambda c: pl.BlockSpec((tm, W), lambda i: (i, c))
    prev = lambda c: pl.BlockSpec((CONV_HALO, W), lambda i: (jnp.maximum(i * per - 1, 0), c))
    return cur, prev


def _conv_fwd(proj, conv_w, K, conv_b, ln_g, ln_b, W, ys):
    T = proj.shape[0]
    tm = _pick(T, 256, CONV_HALO)
    lead = CONV_HALO - (K - 1)
    cur, prev = _conv_specs(W, tm)

    def body(a_ref, b_ref, ah_ref, bh_ref, g_ref, w_ref, cb_ref, lg_ref, lb_ref, ys_in, c_ref, y_ref, glu):
        i = pl.program_id(0)
        glu[0:CONV_HALO, :] = jnp.where(i > 0, ah_ref[...] * _sigmoid(bh_ref[...]), 0.0)
        glu[CONV_HALO:, :] = a_ref[...] * _sigmoid(b_ref[...])
        c = jnp.broadcast_to(cb_ref[...], (tm, W))
        for k in range(K):
            c = c + w_ref[k:k + 1, :] * glu[lead + k:lead + k + tm, :]
        c_ref[...] = c
        xh, _ = _ln_stats(c)
        y_ref[...] = (_silu(xh * lg_ref[...] + lb_ref[...]) * _silu(g_ref[...])).astype(BF16)

    vec = pl.BlockSpec((1, W), lambda i: (0, 0))
    row = pl.BlockSpec((tm, W), lambda i: (i, 0))
    return pl.pallas_call(body, name="conv_fwd", grid=(T // tm,),
                          in_specs=[cur(4), cur(5), prev(4), prev(5), cur(6),
                                    pl.BlockSpec((CONV_HALO, W), lambda i: (0, 0)), vec, vec, vec, HBM],
                          out_specs=[row, pl.BlockSpec((None, tm, W), lambda i: (1, i, 0))],
                          out_shape=[_sds((T, W), F32), _sds(ys.shape, BF16)], input_output_aliases={9: 1},
                          scratch_shapes=[pltpu.VMEM((tm + CONV_HALO, W), F32)],
                          compiler_params=_cp("parallel"))(proj, proj, proj, proj, proj, conv_w, conv_b, ln_g, ln_b, ys)


def _conv_bwd_ln(proj, c, dy, ln_g, ln_b, W):
    T = proj.shape[0]
    tm = _pick(T, 256, SUBLANES)

    def body(g_ref, c_ref, dy_ref, lg_ref, lb_ref, dc_ref, dg_ref, st_ref):
        first = pl.program_id(0) == 0
        gv = g_ref[...]
        dyv = dy_ref[...]
        xh, r = _ln_stats(c_ref[...])
        lg = lg_ref[...]
        ln = xh * lg + lb_ref[...]
        dg_ref[...] = (dyv * _silu(ln) * _dsilu(gv)).astype(BF16)
        dln = dyv * _silu(gv) * _dsilu(ln)
        dc = _ln_bwd(dln * lg, xh, r)
        dc_ref[...] = dc
        _acc_rows(st_ref, 0, jnp.sum(dln * xh, axis=0, keepdims=True), first)
        st_ref[1:2, :] += jnp.sum(dln, axis=0, keepdims=True)
        st_ref[2:3, :] += jnp.sum(dc, axis=0, keepdims=True)

    row = pl.BlockSpec((tm, W), lambda i: (i, 0))
    vec = pl.BlockSpec((1, W), lambda i: (0, 0))
    return pl.pallas_call(body, name="conv_bwd_ln", grid=(T // tm,),
                          in_specs=[pl.BlockSpec((tm, W), lambda i: (i, 6)), row,
                                    pl.BlockSpec((None, tm, W), lambda i: (1, i, 0)), vec, vec],
                          out_specs=[row, row, pl.BlockSpec((SUBLANES, W), lambda i: (0, 0))],
                          out_shape=[_sds((T, W), F32), _sds((T, W), BF16), _sds((SUBLANES, W), F32)],
                          compiler_params=_cp("arbitrary"))(proj, c, dy, ln_g, ln_b)


def _conv_bwd_taps(proj, dc, conv_w, K, W):
    T = proj.shape[0]
    tm = _pick(T, 256, CONV_HALO)
    lead = CONV_HALO - (K - 1)
    per = tm // CONV_HALO
    nblk = T // tm
    cur, prev = _conv_specs(W, tm)

    def body(a_ref, b_ref, ah_ref, bh_ref, dc_ref, dcn_ref, w_ref, da_ref, db_ref, dw_ref, glu, dcs):
        i = pl.program_id(0)
        av = a_ref[...]
        sb = _sigmoid(b_ref[...])
        glu[0:CONV_HALO, :] = jnp.where(i > 0, ah_ref[...] * _sigmoid(bh_ref[...]), 0.0)
        glu[CONV_HALO:, :] = av * sb
        dcv = dc_ref[...]
        dcs[0:tm, :] = dcv
        dcs[tm:, :] = jnp.where(i < nblk - 1, dcn_ref[...], 0.0)

        @pl.when(i == 0)
        def _():
            dw_ref[...] = jnp.zeros_like(dw_ref)

        dglu = jnp.zeros((tm, W), F32)
        for k in range(K):
            dglu = dglu + w_ref[k:k + 1, :] * dcs[K - 1 - k:K - 1 - k + tm, :]
            dw_ref[k:k + 1, :] += jnp.sum(dcv * glu[lead + k:lead + k + tm, :], axis=0, keepdims=True)
        da_ref[...] = (dglu * sb).astype(BF16)
        db_ref[...] = (dglu * av * sb * (1.0 - sb)).astype(BF16)

    row = pl.BlockSpec((tm, W), lambda i: (i, 0))
    nxt = pl.BlockSpec((CONV_HALO, W), lambda i: (jnp.minimum((i + 1) * per, T // CONV_HALO - 1), 0))
    return pl.pallas_call(body, name="conv_bwd_taps", grid=(nblk,),
                          in_specs=[cur(4), cur(5), prev(4), prev(5), row, nxt,
                                    pl.BlockSpec((CONV_HALO, W), lambda i: (0, 0))],
                          out_specs=[row, row, pl.BlockSpec((CONV_HALO, W), lambda i: (0, 0))],
                          out_shape=[_sds((T, W), BF16), _sds((T, W), BF16), _sds((CONV_HALO, W), F32)],
                          scratch_shapes=[pltpu.VMEM((tm + CONV_HALO, W), F32), pltpu.VMEM((tm + CONV_HALO, W), F32)],
                          compiler_params=_cp("arbitrary"))(proj, proj, proj, proj, dc, dc, conv_w)


def _sgu_common(cu, cv, lg, lb, w_ref, bt_ref, z_scr, G, nch):
    u = _gelu(cu)
    xh, r = _ln_stats(_gelu(cv))
    vn = (xh * lg + lb).astype(BF16)
    rr = lax.broadcasted_iota(jnp.int32, (LANES, LANES), 0)
    cc = lax.broadcasted_iota(jnp.int32, (LANES, LANES), 1)
    tril = rr >= cc
    wts = [jnp.where(tril, w_ref[g], 0.0).astype(BF16) for g in range(G)]
    for ch in range(nch):
        rs = slice(ch * LANES, (ch + 1) * LANES)
        for g in range(G):
            cs = slice(g * LANES, (g + 1) * LANES)
            z_scr[rs, cs] = jnp.dot(wts[g], vn[rs, cs], preferred_element_type=F32) + bt_ref[:, g:g + 1]
    return u, xh, r, vn, wts, tril


def _sgu_fwd(proj, sgu_w, sgu_bt, ln_g, ln_b, W, ys):
    T = proj.shape[0]
    G = W // LANES
    tm = _pick(T, 256, LANES)
    nch = tm // LANES

    def body(u_ref, v_ref, g_ref, w_ref, bt_ref, lg_ref, lb_ref, ys_in, y_ref, z_scr):
        u, *_ = _sgu_common(u_ref[...], v_ref[...], lg_ref[...], lb_ref[...], w_ref, bt_ref, z_scr, G, nch)
        y_ref[...] = (u * z_scr[...] * _silu(g_ref[...])).astype(BF16)

    cur = lambda c: pl.BlockSpec((tm, W), lambda i: (i, c))
    vec = pl.BlockSpec((1, W), lambda i: (0, 0))
    return pl.pallas_call(body, name="sgu_fwd", grid=(T // tm,),
                          in_specs=[cur(7), cur(8), cur(9), pl.BlockSpec((G, LANES, LANES), lambda i: (0, 0, 0)),
                                    pl.BlockSpec((LANES, G), lambda i: (0, 0)), vec, vec, HBM],
                          out_specs=pl.BlockSpec((None, tm, W), lambda i: (2, i, 0)), out_shape=_sds(ys.shape, BF16),
                          input_output_aliases={7: 0}, scratch_shapes=[pltpu.VMEM((tm, W), F32)],
                          compiler_params=_cp("parallel"))(proj, proj, proj, sgu_w, sgu_bt, ln_g, ln_b, ys)


def _sgu_bwd(proj, dy, sgu_w, sgu_bt, ln_g, ln_b, W):
    T = proj.shape[0]
    G = W // LANES
    tm = _pick(T, 256, LANES)
    nch = tm // LANES

    def body(u_ref, v_ref, g_ref, dy_ref, w_ref, bt_ref, lg_ref, lb_ref,
             du_ref, dv_ref, dg_ref, dw_ref, dbt_ref, st_ref, z_scr, dvn_scr):
        first = pl.program_id(0) == 0
        cu, cv, gv, dyv = u_ref[...], v_ref[...], g_ref[...], dy_ref[...]
        lg = lg_ref[...]
        u, xh, r, vn, wts, tril = _sgu_common(cu, cv, lg, lb_ref[...], w_ref, bt_ref, z_scr, G, nch)
        z = z_scr[...]
        sg = _silu(gv)
        dg_ref[...] = (dyv * u * z * _dsilu(gv)).astype(BF16)
        du_ref[...] = (dyv * z * sg * _dgelu(cu)).astype(BF16)
        dz = dyv * u * sg
        dzb = dz.astype(BF16)

        @pl.when(first)
        def _():
            dw_ref[...] = jnp.zeros_like(dw_ref)
            dbt_ref[...] = jnp.zeros_like(dbt_ref)

        for g in range(G):
            cs = slice(g * LANES, (g + 1) * LANES)
            dwg = jnp.zeros((LANES, LANES), F32)
            dbg = jnp.zeros((LANES, 1), F32)
            for ch in range(nch):
                rs = slice(ch * LANES, (ch + 1) * LANES)
                dwg = dwg + lax.dot_general(dzb[rs, cs], vn[rs, cs], NT, preferred_element_type=F32)
                dbg = dbg + jnp.sum(dz[rs, cs], axis=1, keepdims=True)
                dvn_scr[rs, cs] = lax.dot_general(wts[g], dzb[rs, cs], TN, preferred_element_type=F32)
            dw_ref[g] += jnp.where(tril, dwg, 0.0)
            dbt_ref[:, g:g + 1] += dbg
        dvn = dvn_scr[...]
        _acc_rows(st_ref, 0, jnp.sum(dvn * xh, axis=0, keepdims=True), first)
        st_ref[1:2, :] += jnp.sum(dvn, axis=0, keepdims=True)
        dv_ref[...] = (_ln_bwd(dvn * lg, xh, r) * _dgelu(cv)).astype(BF16)

    cur = lambda c: pl.BlockSpec((tm, W), lambda i: (i, c))
    row = pl.BlockSpec((tm, W), lambda i: (i, 0))
    vec = pl.BlockSpec((1, W), lambda i: (0, 0))
    wspec = pl.BlockSpec((G, LANES, LANES), lambda i: (0, 0, 0))
    bspec = pl.BlockSpec((LANES, G), lambda i: (0, 0))
    return pl.pallas_call(body, name="sgu_bwd", grid=(T // tm,),
                          in_specs=[cur(7), cur(8), cur(9), pl.BlockSpec((None, tm, W), lambda i: (2, i, 0)),
                                    wspec, bspec, vec, vec],
                          out_specs=[row, row, row, wspec, bspec, pl.BlockSpec((SUBLANES, W), lambda i: (0, 0))],
                          out_shape=[_sds((T, W), BF16)] * 3 + [_sds((G, LANES, LANES), F32), _sds((LANES, G), F32),
                                                                 _sds((SUBLANES, W), F32)],
                          scratch_shapes=[pltpu.VMEM((tm, W), F32), pltpu.VMEM((tm, W), F32)],
                          compiler_params=_cp("arbitrary"))(proj, proj, proj, dy, sgu_w, sgu_bt, ln_g, ln_b)


def _rows(start, dil):
    return pl.ds(start, LANES, stride=dil) if dil > 1 else pl.ds(start, LANES)


def _dil_masks():
    a = lax.broadcasted_iota(jnp.int32, (LANES, LANES), 0)
    c = lax.broadcasted_iota(jnp.int32, (LANES, LANES), 1)
    return c <= a, c >= a


def _dil_geometry(T, dil):
    sbr = LANES * dil
    nb = max(1, min(DIL_STEP_ROWS, T) // sbr)
    return sbr, nb, T // (sbr * nb)


def _for_units(nb, dil, unit):
    for blk in range(nb):
        if dil <= DIL_UNROLL:
            for r in range(dil):
                unit(blk, r)
        else:
            def chunk(it, carry):
                for u in range(DIL_UNROLL):
                    unit(blk, it * DIL_UNROLL + u)
                return carry
            lax.fori_loop(0, dil // DIL_UNROLL, chunk, 0)


def _dil_fwd_group(proj, W, gi, dil):
    T = proj.shape[0]
    H = W // LANES
    sbr, nb, nsteps = _dil_geometry(T, dil)
    scale = LANES ** -0.5
    cq, ck, cv = (10 + gi) * H, (13 + gi) * H, 16 * H

    def body(q_ref, kc_ref, kp_ref, vc_ref, vp_ref, o_ref, l_ref):
        b = pl.program_id(1)
        m_cur, m_prev = _dil_masks()
        m_first = m_prev & (b > 0)

        def unit(blk, r):
            sl = _rows(blk * sbr + r, dil)
            if blk == 0:
                kp, vp, mp = kp_ref[_rows(r, dil), :], vp_ref[_rows(r, dil), :], m_first
            else:
                sp_ = _rows((blk - 1) * sbr + r, dil)
                kp, vp, mp = kc_ref[sp_, :], vc_ref[sp_, :], m_prev
            q = (q_ref[sl, :] * scale).astype(BF16)
            sc = lax.dot_general(q, kc_ref[sl, :].astype(BF16), NT, preferred_element_type=F32)
            sp = lax.dot_general(q, kp.astype(BF16), NT, preferred_element_type=F32)
            sc = jnp.where(m_cur, sc, NEG)
            sp = jnp.where(mp, sp, NEG)
            m = jnp.maximum(jnp.max(sc, axis=1, keepdims=True), jnp.max(sp, axis=1, keepdims=True))
            pc = jnp.exp(sc - m)
            pp = jnp.exp(sp - m)
            den = jnp.sum(pc, axis=1, keepdims=True) + jnp.sum(pp, axis=1, keepdims=True)
            pv = (jnp.dot(pc.astype(BF16), vc_ref[sl, :].astype(BF16), preferred_element_type=F32)
                  + jnp.dot(pp.astype(BF16), vp.astype(BF16), preferred_element_type=F32))
            o_ref[sl, :] = pv / den
            l_ref[sl, :] = jnp.broadcast_to(m + jnp.log(den), (LANES, LANES))

        _for_units(nb, dil, unit)

    cur = lambda c0: pl.BlockSpec((sbr * nb, LANES), lambda h, b: (b, c0 + h))
    prv = lambda c0: pl.BlockSpec((sbr, LANES), lambda h, b: (jnp.maximum(b * nb - 1, 0), c0 + h))
    out = pl.BlockSpec((sbr * nb, LANES), lambda h, b: (b, h))
    return pl.pallas_call(body, name=f"dil_fwd_g{gi}", grid=(H, nsteps),
                          in_specs=[cur(cq), cur(ck), prv(ck), cur(cv), prv(cv)], out_specs=[out, out],
                          out_shape=[_sds((T, W), F32), _sds((T, W), F32)],
                          compiler_params=_cp("parallel", "parallel"))(proj, proj, proj, proj, proj)


def _dil_combine(proj, os_, ls_, W, ys):
    T = proj.shape[0]
    tm = _pick(T, 256, SUBLANES)

    def body(g_ref, o0, o1, o2, l0, l1, l2, ys_in, od_ref, lse_ref, y_ref):
        a0, a1, a2 = l0[...], l1[...], l2[...]
        m = jnp.maximum(jnp.maximum(a0, a1), a2)
        e0, e1, e2 = jnp.exp(a0 - m), jnp.exp(a1 - m), jnp.exp(a2 - m)
        s = e0 + e1 + e2
        od = (e0 / s) * o0[...] + (e1 / s) * o1[...] + (e2 / s) * o2[...]
        od_ref[...] = od
        lse_ref[...] = m + jnp.log(s)
        y_ref[...] = (od * _silu(g_ref[...])).astype(BF16)

    row = pl.BlockSpec((tm, W), lambda i: (i, 0))
    return pl.pallas_call(body, name="dil_combine", grid=(T // tm,),
                          in_specs=[pl.BlockSpec((tm, W), lambda i: (i, 17))] + [row] * 6 + [HBM],
                          out_specs=[row, row, pl.BlockSpec((None, tm, W), lambda i: (3, i, 0))],
                          out_shape=[_sds((T, W), F32), _sds((T, W), F32), _sds(ys.shape, BF16)],
                          input_output_aliases={7: 2},
                          compiler_params=_cp("parallel"))(proj, *os_, *ls_, ys)


def _dil_bwd_pre(proj, od, dy, W):
    T = proj.shape[0]
    H = W // LANES
    tm = _pick(T, 512, SUBLANES)

    def body(g_ref, od_ref, dy_ref, do_ref, dl_ref, dg_ref):
        gv, odv, dyv = g_ref[...], od_ref[...], dy_ref[...]
        do = dyv * _silu(gv)
        do_ref[...] = do
        dl_ref[...] = jnp.broadcast_to(jnp.sum(do * odv, axis=1, keepdims=True), (tm, LANES))
        dg_ref[...] = (dyv * odv * _dsilu(gv)).astype(BF16)

    blk = pl.BlockSpec((tm, LANES), lambda i, h: (i, h))
    return pl.pallas_call(body, name="dil_bwd_pre", grid=(T // tm, H),
                          in_specs=[pl.BlockSpec((tm, LANES), lambda i, h: (i, 17 * H + h)), blk,
                                    pl.BlockSpec((None, tm, LANES), lambda i, h: (3, i, h))],
                          out_specs=[blk, blk, blk],
                          out_shape=[_sds((T, W), F32), _sds((T, W), F32), _sds((T, W), BF16)],
                          compiler_params=_cp("parallel", "parallel"))(proj, od, dy)


def _dil_bwd_group(proj, do, lse, delta, W, gi, dil):
    T = proj.shape[0]
    H = W // LANES
    sbr, nb, nsteps = _dil_geometry(T, dil)
    scale = LANES ** -0.5
    cq, ck, cv = (10 + gi) * H, (13 + gi) * H, 16 * H

    def body(qc_ref, qn_ref, kc_ref, kp_ref, vc_ref, vp_ref, doc_ref, don_ref, lc_ref, ln_ref, dc_ref, dn_ref,
             dq_ref, dk_ref, dv_ref):
        b = pl.program_id(1)
        m_cur, m_prev = _dil_masks()
        m_first = m_prev & (b > 0)
        m_last = m_prev & (b < nsteps - 1)

        def probs(q, k, mask, l):
            s = lax.dot_general(q, k, NT, preferred_element_type=F32)
            return jnp.exp(jnp.where(mask, s - l, NEG))

        def unit(blk, r):
            sl = _rows(blk * sbr + r, dil)
            if blk == 0:
                edge = _rows(r, dil)
                k_p, v_p, m_cp = kp_ref[edge, :], vp_ref[edge, :], m_first
            else:
                sp_ = _rows((blk - 1) * sbr + r, dil)
                k_p, v_p, m_cp = kc_ref[sp_, :], vc_ref[sp_, :], m_prev
            if blk == nb - 1:
                edge = _rows(r, dil)
                q_n, do_n, l_n, d_n, m_nc = qn_ref[edge, :], don_ref[edge, :], ln_ref[edge, :], dn_ref[edge, :], m_last
            else:
                sn_ = _rows((blk + 1) * sbr + r, dil)
                q_n, do_n, l_n, d_n, m_nc = qc_ref[sn_, :], doc_ref[sn_, :], lc_ref[sn_, :], dc_ref[sn_, :], m_prev
            q_c = (qc_ref[sl, :] * scale).astype(BF16)
            q_n = (q_n * scale).astype(BF16)
            k_c, k_p = kc_ref[sl, :].astype(BF16), k_p.astype(BF16)
            v_c, v_p = vc_ref[sl, :].astype(BF16), v_p.astype(BF16)
            do_c, do_n = doc_ref[sl, :].astype(BF16), do_n.astype(BF16)
            l_c, d_c = lc_ref[sl, :], dc_ref[sl, :]
            p_cc = probs(q_c, k_c, m_cur, l_c)
            p_cp = probs(q_c, k_p, m_cp, l_c)
            p_nc = probs(q_n, k_c, m_nc, l_n)
            ds_cc = (p_cc * (lax.dot_general(do_c, v_c, NT, preferred_element_type=F32) - d_c)).astype(BF16)
            ds_cp = (p_cp * (lax.dot_general(do_c, v_p, NT, preferred_element_type=F32) - d_c)).astype(BF16)
            ds_nc = (p_nc * (lax.dot_general(do_n, v_c, NT, preferred_element_type=F32) - d_n)).astype(BF16)
            dq = (jnp.dot(ds_cc, k_c, preferred_element_type=F32) + jnp.dot(ds_cp, k_p, preferred_element_type=F32))
            dk = (lax.dot_general(ds_cc, q_c, TN, preferred_element_type=F32)
                  + lax.dot_general(ds_nc, q_n, TN, preferred_element_type=F32))
            dv = (lax.dot_general(p_cc.astype(BF16), do_c, TN, preferred_element_type=F32)
                  + lax.dot_general(p_nc.astype(BF16), do_n, TN, preferred_element_type=F32))
            dq_ref[sl, :] = dq * scale
            dk_ref[sl, :] = dk
            dv_ref[sl, :] = dv

        _for_units(nb, dil, unit)

    cur = lambda c0: pl.BlockSpec((sbr * nb, LANES), lambda h, b: (b, c0 + h))
    prv = lambda c0: pl.BlockSpec((sbr, LANES), lambda h, b: (jnp.maximum(b * nb - 1, 0), c0 + h))
    nxt = lambda c0: pl.BlockSpec((sbr, LANES), lambda h, b: (jnp.minimum((b + 1) * nb, T // sbr - 1), c0 + h))
    out = pl.BlockSpec((sbr * nb, LANES), lambda h, b: (b, h))
    return pl.pallas_call(body, name=f"dil_bwd_g{gi}", grid=(H, nsteps),
                          in_specs=[cur(cq), nxt(cq), cur(ck), prv(ck), cur(cv), prv(cv),
                                    cur(0), nxt(0), cur(0), nxt(0), cur(0), nxt(0)],
                          out_specs=[out, out, out], out_shape=[_sds((T, W), F32)] * 3,
                          compiler_params=_cp("parallel", "parallel"))(
        proj, proj, proj, proj, proj, proj, do, do, lse, lse, delta, delta)


def _sum3_bf16(a, b, c):
    T, W = a.shape
    tm = _pick(T, 512, SUBLANES)

    def body(a_ref, b_ref, c_ref, o_ref):
        o_ref[...] = (a_ref[...] + b_ref[...] + c_ref[...]).astype(BF16)

    row = pl.BlockSpec((tm, W), lambda i: (i, 0))
    return pl.pallas_call(body, name="dil_dv_sum", grid=(T // tm,), in_specs=[row, row, row], out_specs=row,
                          out_shape=_sds((T, W), BF16), compiler_params=_cp("parallel"))(a, b, c)


def _to_bf16(name, parts):
    T, W = parts[0].shape
    n = len(parts)
    tm = _pick(T, 512, SUBLANES)

    def body(*refs):
        o_ref = refs[n]
        for p in range(n):
            o_ref[:, p * W:(p + 1) * W] = refs[p][...].astype(BF16)

    row = pl.BlockSpec((tm, W), lambda i: (i, 0))
    return pl.pallas_call(body, name=name, grid=(T // tm,), in_specs=[row] * n,
                          out_specs=pl.BlockSpec((tm, n * W), lambda i: (i, 0)),
                          out_shape=_sds((T, n * W), BF16), compiler_params=_cp("parallel"))(*parts)


def _branch_merge(ys, w_br, gates):
    NB, T, W = ys.shape
    D = w_br.shape[2]
    tm, tn = _pick(T, 1024, SUBLANES), _pick(D, 1024)
    nj = D // tn

    def body(y_ref, w_ref, g_ref, yp_ref, m_ref, acc):
        n = pl.program_id(2)
        yp = jnp.dot(y_ref[...], w_ref[...], preferred_element_type=F32)
        yp_ref[...] = yp.astype(BF16)

        @pl.when(n == 0)
        def _():
            acc[...] = jnp.zeros_like(acc)

        acc[...] += g_ref[...].astype(F32) * yp

        @pl.when(n == NB - 1)
        def _():
            m_ref[...] = acc[...].astype(BF16)

    return pl.pallas_call(
        body, name="branch_merge", grid=(T // tm, nj, NB),
        in_specs=[pl.BlockSpec((None, tm, W), lambda i, j, n: (n, i, 0)),
                  pl.BlockSpec((None, W, tn), lambda i, j, n: (n, 0, j)),
                  pl.BlockSpec((tm, tn), lambda i, j, n: (i, n * nj + j))],
        out_specs=[pl.BlockSpec((None, tm, tn), lambda i, j, n: (n, i, j)),
                   pl.BlockSpec((tm, tn), lambda i, j, n: (i, j))],
        out_shape=[_sds((NB, T, D), BF16), _sds((T, D), BF16)],
        scratch_shapes=[pltpu.VMEM((tm, tn), F32)],
        compiler_params=_cp("parallel", "parallel", "arbitrary"))(ys, w_br, gates)


def _merge_bwd(dmerged, gates, yproj):
    NB, T, D = yproj.shape
    tm = _pick(T, 256, SUBLANES)

    def body(dm_ref, g_ref, yp_ref, dyp_ref, dz_ref, db_ref):
        dm, g = dm_ref[...], g_ref[...].astype(F32)
        dyp_ref[...] = (dm * g).astype(BF16)
        dz = dm * yp_ref[...].astype(F32) * g * (1.0 - g)
        dz_ref[...] = dz.astype(BF16)
        _acc_rows(db_ref, 0, jnp.sum(dz, axis=0, keepdims=True), pl.program_id(1) == 0)

    return pl.pallas_call(
        body, name="merge_bwd", grid=(NB, T // tm),
        in_specs=[pl.BlockSpec((tm, D), lambda n, i: (i, 0)), pl.BlockSpec((tm, D), lambda n, i: (i, n)),
                  pl.BlockSpec((None, tm, D), lambda n, i: (n, i, 0))],
        out_specs=[pl.BlockSpec((None, tm, D), lambda n, i: (n, i, 0)), pl.BlockSpec((tm, D), lambda n, i: (i, n)),
                   pl.BlockSpec((SUBLANES, D), lambda n, i: (0, n))],
        out_shape=[_sds((NB, T, D), BF16), _sds((T, NB * D), BF16), _sds((SUBLANES, NB * D), F32)],
        compiler_params=_cp("parallel", "arbitrary"))(dmerged, gates, yproj)


def _branch_bwd_dy(dyp, w_br):
    NB, T, D = dyp.shape
    W = w_br.shape[1]
    tm = _pick(T, 1024, SUBLANES)

    def body(a_ref, w_ref, o_ref):
        o_ref[...] = lax.dot_general(a_ref[...], w_ref[...], NT, preferred_element_type=F32)

    return pl.pallas_call(body, name="branch_bwd_dy", grid=(NB, T // tm),
                          in_specs=[pl.BlockSpec((None, tm, D), lambda n, i: (n, i, 0)),
                                    pl.BlockSpec((None, W, D), lambda n, i: (n, 0, 0))],
                          out_specs=pl.BlockSpec((None, tm, W), lambda n, i: (n, i, 0)),
                          out_shape=_sds((NB, T, W), F32), compiler_params=_cp("parallel", "parallel"))(dyp, w_br)


def _branch_bwd_dw(ys, dyp):
    NB, T, W = ys.shape
    D = dyp.shape[2]
    tm, tn, tk = _pick(W, 1024), _pick(D, 1024), _pick(T, 2048, SUBLANES)
    return _mm("branch_bwd_dw", ys, dyp, grid=(NB, W // tm, D // tn, T // tk), kaxis=3, dims=TN,
               a_spec=pl.BlockSpec((None, tk, tm), lambda n, i, j, k: (n, k, i)),
               b_spec=pl.BlockSpec((None, tk, tn), lambda n, i, j, k: (n, k, j)),
               acc_shape=(tm, tn), out_shape=_sds((NB, W, D), BF16),
               out_specs=pl.BlockSpec((None, tm, tn), lambda n, i, j, k: (n, i, j)),
               epilogue=_store(BF16), sem=("parallel", "parallel", "parallel", "arbitrary"))


def _layer_fwd(x, wl, sp):
    W = sp["conv_b"].shape[-1]
    D = x.shape[1]
    h = _rmsnorm_fwd(x, sp["norm_g"])
    proj = _mm_nn("proj", h, wl["w_in"], F32, tn=768)
    if "late" in wl:
        wl.update(wl.pop("late")(proj))

    def gate_ep(acc, ex, outs):
        outs[0][...] = _sigmoid(acc + ex[0][...]).astype(BF16)

    tn_g = _pick(4 * D, 1024)
    gates = _mm_nn("gates", h, wl["w_gate"], BF16, epilogue=gate_ep, extras=(sp["b_gate"],),
                   extra_specs=(pl.BlockSpec((1, tn_g), lambda i, j, k: (0, j)),), tn=tn_g)
    ys = lax.empty((4, x.shape[0], W), BF16)
    oa, ys, sb_carries = _sb_fwd(proj, W, ys)
    cpre, ys = _conv_fwd(proj, wl["conv_w"], wl["taps"], sp["conv_b"], sp["conv_ln_g"], sp["conv_ln_b"], W, ys)
    ys = _sgu_fwd(proj, sp["sgu_w"], sp["sgu_bt"], sp["sgu_ln_g"], sp["sgu_ln_b"], W, ys)
    os_, ls_ = zip(*[_dil_fwd_group(proj, W, gi, dil) for gi, (_, dil) in enumerate(DIL_PATTERNS)])
    od, lse, ys = _dil_combine(proj, os_, ls_, W, ys)
    yproj, merged = _branch_merge(ys, wl["w_br"], gates)

    def res_ep(acc, ex, outs):
        outs[0][...] = ex[0][...] + acc

    tm_o, tn_o = _pick(x.shape[0], 1024, SUBLANES), _pick(D, 1024)
    xn = _mm_nn("out_proj", merged, wl["w_out"], F32, epilogue=res_ep, extras=(x,),
                extra_specs=(pl.BlockSpec((tm_o, tn_o), lambda i, j, k: (i, j)),), tm=tm_o, tn=tn_o)
    saved = dict(x=x, h=h, proj=proj, gates=gates, oa=oa, sb_carries=sb_carries, cpre=cpre, od=od, lse=lse, ys=ys, yproj=yproj, merged=merged)
    return xn, saved


def _layer_bwd(dout, sv, wl, sp):
    W = sp["conv_b"].shape[-1]
    proj = sv["proj"]
    dmerged = _mm_nt("out_proj_bwd_dx", dout, wl["w_out"], F32)
    g_w_out = _mm_tn("out_proj_bwd_dw", sv["merged"], dout)
    dyp, dzg, db_gate = _merge_bwd(dmerged, sv["gates"], sv["yproj"])
    dy = _branch_bwd_dy(dyp, wl["w_br"])
    g_w_br = _branch_bwd_dw(sv["ys"], dyp)
    g_w_gate = _mm_tn("gate_bwd_dw", sv["h"], dzg, tk=4096)

    a_dq, a_dk, a_dv, a_dg = _sb_bwd(proj, sv["oa"], sv["sb_carries"], dy, W)
    a_qkv = _to_bf16("sb_bwd_cast", [a_dq, a_dk, a_dv])

    dc, b_dg, conv_stats = _conv_bwd_ln(proj, sv["cpre"], dy, sp["conv_ln_g"], sp["conv_ln_b"], W)
    b_da, b_db, g_conv_w = _conv_bwd_taps(proj, dc, wl["conv_w"], wl["taps"], W)

    c_du, c_dv, c_dg, g_sgu_w, g_sgu_bt, sgu_stats = _sgu_bwd(proj, dy, sp["sgu_w"], sp["sgu_bt"],
                                                              sp["sgu_ln_g"], sp["sgu_ln_b"], W)

    do, delta, d_dg = _dil_bwd_pre(proj, sv["od"], dy, W)
    dqs, dks, dvs = zip(*[_dil_bwd_group(proj, do, sv["lse"], delta, W, gi, dil)
                          for gi, (_, dil) in enumerate(DIL_PATTERNS)])
    d_qk = _to_bf16("dil_bwd_cast", [*dqs, *dks])
    d_dv = _sum3_bf16(*dvs)

    dproj = jnp.concatenate([a_qkv, a_dg, b_da, b_db, b_dg, c_du, c_dv, c_dg, d_qk, d_dv, d_dg], axis=1)
    g_w_in = _mm_tn("proj_bwd_dw", sv["h"], dproj, tn=768, tk=4096)
    dh = _mm_nt("gate_bwd_dh", dzg, wl["w_gate"], F32)
    dh = _mm_nt("proj_bwd_dh", dproj, wl["w_in"], F32, addend=dh)
    dx, dnorm = _rmsnorm_bwd(sv["x"], dh, dout, sp["norm_g"])

    K = wl["taps"]
    small = dict(norm_g=dnorm[0], conv_w=g_conv_w[:K], conv_b=conv_stats[2], conv_ln_g=conv_stats[0],
                 conv_ln_b=conv_stats[1], sgu_ln_g=sgu_stats[0], sgu_ln_b=sgu_stats[1], sgu_w=g_sgu_w,
                 sgu_b=g_sgu_bt.T, b_gate=db_gate[0])
    big = dict(w_in=g_w_in, w_gate=g_w_gate, w_br=g_w_br, w_out=g_w_out)
    return dx, big, small


HBM = pl.BlockSpec(memory_space=pl.ANY)


def _mesh_pos():
    return lax.axis_index("x"), lax.axis_index("y"), lax.axis_index("c")


def _shard_of(ref, axis, size, index):
    idx = [slice(None)] * len(ref.shape)
    idx[axis] = pl.ds(index * size, size)
    return ref.at[tuple(idx)]


HBM_ONLY = pl.BlockSpec(memory_space=pltpu.HBM)
SEM_SPEC = pl.BlockSpec(memory_space=pltpu.SEMAPHORE)
N_PEER = N_DEV - 1


def _peer(x, y, c, m):
    flip = lambda v, bit: 1 - v if bit else v
    return flip(x, (m >> 2) & 1), flip(y, (m >> 1) & 1), flip(c, m & 1)


def _exchange_copies(kind, srcs, lands, send, recv, axes):
    x, y, c = _mesh_pos()
    me = 4 * x + 2 * y + c
    n = len(srcs)
    remote, local = [], []
    for a in range(n):
        for m in range(1, N_DEV):
            px, py, pc = _peer(x, y, c, m)
            if kind == "gather":
                src = srcs[a]
                dst = _shard_of(lands[a], axes[a], srcs[a].shape[axes[a]], me)
            else:
                src = _shard_of(srcs[a], axes[a], srcs[a].shape[axes[a]] // N_DEV, 4 * px + 2 * py + pc)
                dst = lands[a].at[me]
            remote.append(pltpu.make_async_remote_copy(
                src_ref=src, dst_ref=dst, send_sem=send.at[a * N_PEER + m - 1], recv_sem=recv.at[a * N_PEER + m - 1],
                device_id=(px, py, pc), device_id_type=MESH))
        if kind == "gather":
            local.append(pltpu.make_async_copy(srcs[a], _shard_of(lands[a], axes[a], srcs[a].shape[axes[a]], me),
                                               send.at[n * N_PEER + a]))
    return remote, local


def _exchange_start(name, kind, srcs, lands, axes, carry):
    n = len(srcs)
    hbm = lambda a: pltpu.with_memory_space_constraint(a, pltpu.HBM)

    def body(*refs):
        send, recv = refs[2 * n + 1], refs[2 * n + 2]
        remote, local = _exchange_copies(kind, refs[:n], refs[n:2 * n], send, recv, axes)
        for cp in remote + local:
            cp.start()

    thru = [*srcs, *lands, carry]
    res = pl.pallas_call(
        body, name=name,
        out_shape=(pltpu.SemaphoreType.DMA((n * N_DEV,)), pltpu.SemaphoreType.DMA((n * N_PEER,)),
                   *[pltpu.HBM(a.shape, a.dtype) for a in thru]),
        in_specs=[HBM_ONLY] * len(thru), out_specs=(SEM_SPEC, SEM_SPEC, *[HBM_ONLY] * len(thru)),
        input_output_aliases={i: 2 + i for i in range(len(thru))},
        compiler_params=pltpu.CompilerParams(has_side_effects=pltpu.SideEffectType.DATAFLOW_SIDE_EFFECTING),
    )(*[hbm(a) for a in thru])
    return res[0], res[1], list(res[2:2 + n]), list(res[2 + n:2 + 2 * n]), res[2 + 2 * n]


def _exchange_wait(name, kind, send, recv, srcs, lands, axes, after):
    n = len(srcs)

    def body(*refs):
        send_ref, recv_ref = refs[2 * n], refs[2 * n + 1]
        remote, local = _exchange_copies(kind, refs[:n], refs[n:2 * n], send_ref, recv_ref, axes)
        for cp in remote:
            cp.wait_send()
            cp.wait_recv()
        for cp in local:
            cp.wait()

    thru = [*srcs, *lands]
    res = pl.pallas_call(
        body, name=name, out_shape=tuple(pltpu.HBM(a.shape, a.dtype) for a in thru),
        in_specs=[*[HBM_ONLY] * len(thru), SEM_SPEC, SEM_SPEC, HBM], out_specs=tuple([HBM_ONLY] * len(thru)),
        input_output_aliases={i: i for i in range(len(thru))},
        compiler_params=pltpu.CompilerParams(has_side_effects=pltpu.SideEffectType.DATAFLOW_SIDE_EFFECTING),
    )(*thru, send, recv, after)
    return list(res[:n]), list(res[n:])


def _adamw_scatter(name, land, grad, axis, me, w, m, v):
    P, R, C = land.shape
    tr = _pick(R, max(SUBLANES, (1 << 18) // C // 16 * 16), 16)
    nr = R // tr

    def body(me_ref, l_ref, o_ref, w_ref, m_ref, v_ref, g_ref, d_ref, mo_ref, vo_ref):
        own = o_ref[...].astype(F32)
        g = jnp.where(me_ref[0] == 0, own, l_ref[0].astype(F32))
        for k in range(1, P):
            g = g + jnp.where(me_ref[0] == k, own, l_ref[k].astype(F32))
        mn = ADAM_B1 * m_ref[...] + (1.0 - ADAM_B1) * g
        vn = ADAM_B2 * v_ref[...] + (1.0 - ADAM_B2) * (g * g)
        m_hat = mn / (1.0 - ADAM_B1 ** ADAM_STEP)
        v_hat = vn / (1.0 - ADAM_B2 ** ADAM_STEP)
        g_ref[...] = g
        d_ref[...] = -ADAM_LR * (m_hat / (jnp.sqrt(v_hat) + ADAM_EPS) + ADAM_WD * w_ref[...])
        mo_ref[...] = mn
        vo_ref[...] = vn

    if axis == 1:
        own_spec = pl.BlockSpec((tr, C), lambda i, me_ref: (i, me_ref[0]))
    else:
        own_spec = pl.BlockSpec((tr, C), lambda i, me_ref: (me_ref[0] * nr + i, 0))
    row = pl.BlockSpec((tr, C), lambda i, me_ref: (i, 0))
    return pl.pallas_call(
        body, name=name, out_shape=[_sds((R, C), F32)] * 4,
        grid_spec=pltpu.PrefetchScalarGridSpec(
            num_scalar_prefetch=1, grid=(nr,),
            in_specs=[pl.BlockSpec((P, tr, C), lambda i, me_ref: (0, i, 0)), own_spec, row, row, row],
            out_specs=[row] * 4),
        compiler_params=_cp("parallel"))(me, land, grad, w, m, v)


def _adamw_sum(name, parts, w, m, v):
    P, R, C = parts.shape
    tr = _pick(R, max(SUBLANES, (1 << 19) // C // SUBLANES * SUBLANES), SUBLANES)

    def body(p_ref, w_ref, m_ref, v_ref, g_ref, d_ref, mo_ref, vo_ref):
        g = p_ref[0].astype(F32)
        for k in range(1, P):
            g = g + p_ref[k].astype(F32)
        mn = ADAM_B1 * m_ref[...] + (1.0 - ADAM_B1) * g
        vn = ADAM_B2 * v_ref[...] + (1.0 - ADAM_B2) * (g * g)
        m_hat = mn / (1.0 - ADAM_B1 ** ADAM_STEP)
        v_hat = vn / (1.0 - ADAM_B2 ** ADAM_STEP)
        g_ref[...] = g
        d_ref[...] = -ADAM_LR * (m_hat / (jnp.sqrt(v_hat) + ADAM_EPS) + ADAM_WD * w_ref[...])
        mo_ref[...] = mn
        vo_ref[...] = vn

    row = pl.BlockSpec((tr, C), lambda i: (i, 0))
    return pl.pallas_call(body, name=name, grid=(R // tr,),
                          in_specs=[pl.BlockSpec((P, tr, C), lambda i: (0, i, 0)), row, row, row],
                          out_specs=[row] * 4, out_shape=[_sds((R, C), F32)] * 4,
                          compiler_params=_cp("parallel"))(parts, w, m, v)


def _rows128(a, pad_rows=SUBLANES):
    flat = a.reshape(-1, LANES)
    pad = (-flat.shape[0]) % pad_rows
    return jnp.pad(flat, ((0, pad), (0, 0))) if pad else flat


SMALL = ("norm_g", "conv_b", "conv_ln_g", "conv_ln_b", "sgu_ln_g", "sgu_ln_b", "sgu_w", "sgu_b", "b_gate", "final_g")


def kernel(x, norm_g, w_in, conv_w, conv_b, conv_ln_g, conv_ln_b, sgu_ln_g, sgu_ln_b, sgu_w, sgu_b, w_branch, w_gate, b_gate, w_out, final_g, loss_target, m_norm_g, m_w_in, m_conv_w, m_conv_b, m_conv_ln_g, m_conv_ln_b, m_sgu_ln_g, m_sgu_ln_b, m_sgu_w, m_sgu_b, m_w_branch, m_w_gate, m_b_gate, m_w_out, m_final_g, v_norm_g, v_w_in, v_conv_w, v_conv_b, v_conv_ln_g, v_conv_ln_b, v_sgu_ln_g, v_sgu_ln_b, v_sgu_w, v_sgu_b, v_w_branch, v_w_gate, v_b_gate, v_w_out, v_final_g):
    L, D = norm_g.shape
    W = conv_b.shape[1]
    taps = conv_w.shape[1]
    weights = dict(norm_g=norm_g, w_in=w_in, conv_w=conv_w, conv_b=conv_b, conv_ln_g=conv_ln_g, conv_ln_b=conv_ln_b,
                   sgu_ln_g=sgu_ln_g, sgu_ln_b=sgu_ln_b, sgu_w=sgu_w, sgu_b=sgu_b, w_branch=w_branch, w_gate=w_gate,
                   b_gate=b_gate, w_out=w_out, final_g=final_g)
    mom_m = dict(norm_g=m_norm_g, w_in=m_w_in, conv_w=m_conv_w, conv_b=m_conv_b, conv_ln_g=m_conv_ln_g,
                 conv_ln_b=m_conv_ln_b, sgu_ln_g=m_sgu_ln_g, sgu_ln_b=m_sgu_ln_b, sgu_w=m_sgu_w, sgu_b=m_sgu_b,
                 w_branch=m_w_branch, w_gate=m_w_gate, b_gate=m_b_gate, w_out=m_w_out, final_g=m_final_g)
    mom_v = dict(norm_g=v_norm_g, w_in=v_w_in, conv_w=v_conv_w, conv_b=v_conv_b, conv_ln_g=v_conv_ln_g,
                 conv_ln_b=v_conv_ln_b, sgu_ln_g=v_sgu_ln_g, sgu_ln_b=v_sgu_ln_b, sgu_w=v_sgu_w, sgu_b=v_sgu_b,
                 w_branch=v_w_branch, w_gate=v_w_gate, b_gate=v_b_gate, w_out=v_w_out, final_g=v_final_g)
    me = 4 * lax.axis_index("x") + 2 * lax.axis_index("y") + lax.axis_index("c")

    gather_names = ("w_in", "w_gate", "w_br", "w_out", "conv_w")
    gather_axes = (1, 1, 2, 0, 1)

    def gather_start(tag, l, which, carry):
        shards = [w_in[l].astype(BF16), w_gate[l].astype(BF16), w_branch[l].astype(BF16), w_out[l].astype(BF16),
                  jnp.pad(conv_w[l], ((0, CONV_HALO - taps), (0, 0)))]
        shards = [shards[i] for i in which]
        axes = [gather_axes[i] for i in which]
        lands = []
        for s, ax in zip(shards, axes):
            full = list(s.shape)
            full[ax] *= N_DEV
            lands.append(lax.empty(tuple(full), s.dtype))
        return _exchange_start(f"gather_start_{tag}", "gather", shards, lands, axes, carry) + (which,)

    def gather_wait(tag, started, after):
        send, recv, shards, lands, _, which = started
        _, full = _exchange_wait(f"gather_wait_{tag}", "gather", send, recv, shards, lands,
                                 [gather_axes[i] for i in which], after)
        return {gather_names[i]: f for i, f in zip(which, full)}

    xs = x[0]
    saved, gathered, smalls = [], [], []
    first = gather_start("0a", 0, (0,), xs)
    rest = gather_start("0b", 0, (1, 2, 3, 4), first[4])
    xs = rest[4]
    wl = gather_wait("0a", first, xs)
    wl["late"] = functools.partial(gather_wait, "0b", rest)
    for l in range(L):
        wl["taps"] = taps
        if l + 1 < L:
            started = gather_start(l + 1, l + 1, (0, 1, 2, 3, 4), xs)
            xs = started[4]
        sp = dict(norm_g=norm_g[l][None], conv_b=conv_b[l][None], conv_ln_g=conv_ln_g[l][None],
                  conv_ln_b=conv_ln_b[l][None], sgu_ln_g=sgu_ln_g[l][None], sgu_ln_b=sgu_ln_b[l][None],
                  sgu_w=sgu_w[l], sgu_bt=sgu_b[l].T, b_gate=b_gate[l][None])
        xs, sv = _layer_fwd(xs, wl, sp)
        saved.append(sv)
        gathered.append(wl)
        smalls.append(sp)
        if l + 1 < L:
            wl = gather_wait(l + 1, started, xs)

    dx, d_final, loss_part = _loss_head(xs, loss_target[0], final_g[None])
    loss = lax.psum(loss_part[0, 0], ("x", "y", "c"))

    big_names = ("w_in", "w_gate", "w_branch", "w_out")
    big_axes = (1, 1, 1, 0)
    me1 = me.astype(jnp.int32).reshape(1)
    outs = {}
    small_grads = []

    def scatter_start(l, g2d, carry):
        lands = []
        for g, ax in zip(g2d, big_axes):
            blk = list(g.shape)
            blk[ax] //= N_DEV
            lands.append(lax.empty((N_DEV, *blk), g.dtype))
        return _exchange_start(f"scatter_start_{l}", "scatter", g2d, lands, big_axes, carry)

    def scatter_finish(l, started, after):
        send, recv, g2d, lands, _ = started
        g2d, lands = _exchange_wait(f"scatter_wait_{l}", "scatter", send, recv, g2d, lands, big_axes, after)
        for nm, land, g, ax in zip(big_names, lands, g2d, big_axes):
            shard2d = land.shape[1:]
            outs.setdefault(nm, {})[l] = _adamw_scatter(
                "adamw_" + nm, land, g, ax, me1, weights[nm][l].reshape(shard2d), mom_m[nm][l].reshape(shard2d),
                mom_v[nm][l].reshape(shard2d))

    pending = None
    for l in reversed(range(L)):
        dx, big, small = _layer_bwd(dx, saved[l], gathered[l], smalls[l])
        small_grads.append(small)
        if pending is not None:
            scatter_finish(l + 1, pending, dx)
        g2d = [big["w_in"], big["w_gate"], big["w_br"].reshape(-1, D), big["w_out"]]
        if l > 0:
            pending = scatter_start(l, g2d, dx)
            dx = pending[4]
        else:
            pending = scatter_start(l, g2d, d_final)
            d_final = pending[4]
    small_grads.reverse()

    sg = {nm: jnp.stack([small_grads[l][nm] for l in range(L)]) for nm in SMALL[:-1]}
    sg["final_g"] = d_final[0]
    conv_w_full = jnp.stack([small_grads[l]["conv_w"] for l in range(L)])
    segs = [_rows128(sg[nm]) for nm in SMALL] + [_rows128(conv_w_full)]
    offs = [0]
    for s in segs:
        offs.append(offs[-1] + s.shape[0])
    pack = jnp.concatenate(segs, axis=0)
    small = _exchange_start("small_start", "gather", [pack[None]], [lax.empty((N_DEV, *pack.shape), F32)], [0], d_final)
    scatter_finish(0, pending, small[4])
    _, (allp,) = _exchange_wait("small_wait", "gather", small[0], small[1], small[2], small[3], [0],
                                outs["w_in"][0][0])

    def packed(src):
        return jnp.concatenate([_rows128(src[nm]) for nm in SMALL] + [jnp.zeros_like(segs[-1])], axis=0)

    s_g, s_d, s_m, s_v = _adamw_sum("adamw_small", allp, packed(weights), packed(mom_m), packed(mom_v))
    big_out = {nm: [jnp.stack([outs[nm][l][q] for l in range(L)]).reshape(weights[nm].shape) for q in range(4)]
               for nm in big_names}

    def unpack(buf, i, like):
        n = like.size // LANES
        return buf[offs[i]:offs[i] + n].reshape(like.shape)

    small_out = {nm: [unpack(b, i, weights[nm]) for b in (s_g, s_d, s_m, s_v)] for i, nm in enumerate(SMALL)}
    Wc = conv_w.shape[2]
    cw_sum = lax.dynamic_slice_in_dim(unpack(s_g, len(SMALL), conv_w_full), me * Wc, Wc, axis=2)
    cshape = (L * conv_w.shape[1], Wc)
    conv_out = [o.reshape(conv_w.shape) for o in _adamw_sum(
        "adamw_conv_w", cw_sum.reshape((1,) + cshape), conv_w.reshape(cshape), m_conv_w.reshape(cshape),
        v_conv_w.reshape(cshape))]

    order = ["norm_g", "w_in", "conv_w", "conv_b", "conv_ln_g", "conv_ln_b", "sgu_ln_g", "sgu_ln_b", "sgu_w", "sgu_b",
             "w_branch", "w_gate", "b_gate", "w_out", "final_g"]
    table = dict(small_out)
    table.update(big_out)
    table["conv_w"] = conv_out
    result = [loss, dx[None]]
    for q in range(4):
        result += [table[nm][q] for nm in order]
    return tuple(result)
```
